```python
import math
import jax, jax.numpy as jnp
from jax import lax
import numpy as np

D_MODEL = 2048
BATCH = 4
SEQ = 2048
DEPTH = 4

D_CONV = D_MODEL // 2
CONV_HEAD_DIM = 64
N_CONV_HEADS = D_CONV // CONV_HEAD_DIM
CONV_WIDTH = 3
D_SSM = D_MODEL - D_CONV
SSM_GROUP = 16
N_SSM_GROUPS = D_SSM // SSM_GROUP
SSM_STATE = 64
D_IN = 3 * D_CONV + D_SSM
DT_MIN = 1e-3
DT_MAX = 1e-1
D_FF = ((8 * D_MODEL // 3 + 255) // 256) * 256
N_EXPERTS = 8
TOP_K = 2
MOE_BLOCK = 128
N_DENSE = (DEPTH + 1) // 2
N_MOE = DEPTH // 2
EPS = 1e-6

kernel_name = "hybrid_conv_s5_moe_adaln_trunk"


def rmsnorm(x, g):
    xf = x.astype(jnp.float32)
    y = xf * lax.rsqrt(jnp.mean(xf * xf, axis=-1, keepdims=True) + EPS)
    return (y * g.astype(jnp.float32)).astype(x.dtype)


def short_conv(gate_b, gate_c, v, conv_w):
    z = gate_c * v
    L = z.shape[1]
    zp = jnp.pad(z, ((0, 0), (CONV_WIDTH - 1, 0), (0, 0)))
    out = zp[:, 0:L, :] * conv_w[:, 0]
    for k in range(1, CONV_WIDTH):
        out = out + zp[:, k:k + L, :] * conv_w[:, k]
    return gate_b * out


def _complex_affine_combine(left, right):
    ar1, ai1, br1, bi1 = left
    ar2, ai2, br2, bi2 = right
    ar = ar1 * ar2 - ai1 * ai2
    ai = ar1 * ai2 + ai1 * ar2
    br = ar2 * br1 - ai2 * bi1 + br2
    bi = ar2 * bi1 + ai2 * br1 + bi2
    return (ar, ai, br, bi)


def s5_mixer(u, a_re, a_im, log_dt, b_re, b_im, c_re, c_im, d_skip, w_glu, b_glu):
    f32 = jnp.float32
    Bsz, L, _ = u.shape
    uf = u.astype(f32).reshape(Bsz, L, N_SSM_GROUPS, SSM_GROUP)
    lam_re = a_re.astype(f32)
    lam_im = a_im.astype(f32)
    dt = jnp.exp(log_dt.astype(f32))[:, None]
    mag = jnp.exp(lam_re * dt)
    ang = lam_im * dt
    ab_re = mag * jnp.cos(ang)
    ab_im = mag * jnp.sin(ang)
    den = lam_re * lam_re + lam_im * lam_im
    nr = ab_re - 1.0
    q_re = (nr * lam_re + ab_im * lam_im) / den
    q_im = (ab_im * lam_re - nr * lam_im) / den
    br = b_re.astype(f32)
    bi = b_im.astype(f32)
    bb_re = q_re[..., None] * br - q_im[..., None] * bi
    bb_im = q_re[..., None] * bi + q_im[..., None] * br
    bu_re = jnp.einsum('blgh,gph->blgp', uf, bb_re)
    bu_im = jnp.einsum('blgh,gph->blgp', uf, bb_im)
    a_seq_re = jnp.broadcast_to(ab_re, bu_re.shape)
    a_seq_im = jnp.broadcast_to(ab_im, bu_im.shape)
    _, _, s_re, s_im = lax.associative_scan(
        _complex_affine_combine, (a_seq_re, a_seq_im, bu_re, bu_im), axis=1)
    y = (jnp.einsum('ghp,blgp->blgh', c_re.astype(f32), s_re)
         - jnp.einsum('ghp,blgp->blgh', c_im.astype(f32), s_im)
         + d_skip.astype(f32) * uf)
    y = jax.nn.gelu(y.reshape(Bsz, L, D_SSM))
    y = y * jax.nn.sigmoid(y @ w_glu.astype(f32) + b_glu.astype(f32))
    return y.astype(u.dtype)


def hybrid_mixer(h, w_in, conv_w, a_re, a_im, log_dt, b_re, b_im, c_re, c_im,
                 d_skip, w_glu, b_glu, w_out):
    z = h @ w_in
    gate_b, gate_c, v, u = jnp.split(z, [D_CONV, 2 * D_CONV, 3 * D_CONV], axis=-1)
    y_conv = short_conv(gate_b, gate_c, v, conv_w)
    y_ssm = s5_mixer(u, a_re, a_im, log_dt, b_re, b_im, c_re, c_im,
                     d_skip, w_glu, b_glu)
    return jnp.concatenate([y_conv, y_ssm], axis=-1) @ w_out


def swiglu(h, w_gate, w_up, w_down):
    return (jax.nn.silu(h @ w_gate) * (h @ w_up)) @ w_down


def moe_swiglu(h, router_w, router_b, w_gate, w_up, w_down):
    Bsz, L, D = h.shape
    hf = h.reshape(-1, D)
    T = hf.shape[0]
    logits = hf.astype(jnp.float32) @ router_w.astype(jnp.float32) + router_b.astype(jnp.float32)
    top_logit, top_e = lax.top_k(logits, TOP_K)
    top_w = jax.nn.softmax(top_logit, axis=-1)
    e_flat = top_e.reshape(-1)
    w_flat = top_w.reshape(-1)
    tok_flat = jnp.repeat(jnp.arange(T, dtype=jnp.int32), TOP_K)
    n_assign = T * TOP_K
    order = jnp.argsort(e_flat)
    e_sorted = e_flat[order]
    counts = jnp.bincount(e_flat, length=N_EXPERTS)
    start = jnp.cumsum(counts) - counts
    padded = ((counts + MOE_BLOCK - 1) // MOE_BLOCK) * MOE_BLOCK
    pad_end = jnp.cumsum(padded)
    pad_start = pad_end - padded
    dest = pad_start[e_sorted] + (jnp.arange(n_assign) - start[e_sorted])
    n_blocks = -(-n_assign // MOE_BLOCK) + N_EXPERTS
    n_rows = n_blocks * MOE_BLOCK
    row_tok = jnp.zeros((n_rows,), jnp.int32).at[dest].set(tok_flat[order])
    row_w = jnp.zeros((n_rows,), jnp.float32).at[dest].set(w_flat[order])
    block_e = jnp.clip(jnp.searchsorted(pad_end, jnp.arange(n_blocks) * MOE_BLOCK, side='right'),
                       0, N_EXPERTS - 1)
    xs = hf[row_tok].reshape(n_blocks, MOE_BLOCK, D)

    def expert_block(args):
        xb, e = args
        return (jax.nn.silu(xb @ w_gate[e]) * (xb @ w_up[e])) @ w_down[e]

    ys = lax.map(expert_block, (xs, block_e)).reshape(n_rows, D)
    out = jnp.zeros_like(hf).at[row_tok].add(ys * row_w[:, None].astype(ys.dtype))
    return out.reshape(Bsz, L, D)


def setup_inputs(seed: int = 0) -> dict:
    key = jax.random.key(seed)
    ks = iter(jax.random.split(key, 40))
    f32 = jnp.float32

    def nrm(shape, scale):
        return jax.random.normal(next(ks), shape, f32) * scale

    G, P, H = N_SSM_GROUPS, SSM_STATE, SSM_GROUP
    x = nrm((BATCH, SEQ, D_MODEL), 1.0)
    c = nrm((BATCH, D_MODEL), 1.0)
    w_ada = nrm((DEPTH, D_MODEL, 6 * D_MODEL), 0.5 * D_MODEL ** -0.5)
    b_ada = nrm((DEPTH, 6 * D_MODEL), 0.02)
    g_mix_pre = 1.0 + nrm((DEPTH, D_MODEL), 0.05)
    g_mix_post = 1.0 + nrm((DEPTH, D_MODEL), 0.05)
    g_ffn_pre = 1.0 + nrm((DEPTH, D_MODEL), 0.05)
    g_ffn_post = 1.0 + nrm((DEPTH, D_MODEL), 0.05)
    w_in = nrm((DEPTH, D_MODEL, D_IN), D_MODEL ** -0.5)
    conv_w = nrm((DEPTH, D_CONV, CONV_WIDTH), CONV_WIDTH ** -0.5)
    ssm_a_re = -0.5 + nrm((DEPTH, G, P), 0.02)
    ssm_a_im = jnp.pi * jnp.arange(P, dtype=f32) + nrm((DEPTH, G, P), 0.02)
    ssm_log_dt = jax.random.uniform(next(ks), (DEPTH, G), f32,
                                    math.log(DT_MIN), math.log(DT_MAX))
    ssm_b_re = nrm((DEPTH, G, P, H), (2 * H) ** -0.5)
    ssm_b_im = nrm((DEPTH, G, P, H), (2 * H) ** -0.5)
    ssm_c_re = nrm((DEPTH, G, H, P), (2 * P) ** -0.5)
    ssm_c_im = nrm((DEPTH, G, H, P), (2 * P) ** -0.5)
    ssm_d = nrm((DEPTH, G, H), 1.0)
    w_glu = nrm((DEPTH, D_SSM, D_SSM), D_SSM ** -0.5)
    b_glu = nrm((DEPTH, D_SSM), 0.02)
    w_out = nrm((DEPTH, D_MODEL, D_MODEL), D_MODEL ** -0.5)
    ffn_w_gate = nrm((N_DENSE, D_MODEL, D_FF), D_MODEL ** -0.5)
    ffn_w_up = nrm((N_DENSE, D_MODEL, D_FF), D_MODEL ** -0.5)
    ffn_w_down = nrm((N_DENSE, D_FF, D_MODEL), D_FF ** -0.5)
    router_w = nrm((N_MOE, D_MODEL, N_EXPERTS), D_MODEL ** -0.5)
    router_b = nrm((N_MOE, N_EXPERTS), 0.01)
    moe_w_gate = nrm((N_MOE, N_EXPERTS, D_MODEL, D_FF), D_MODEL ** -0.5)
    moe_w_up = nrm((N_MOE, N_EXPERTS, D_MODEL, D_FF), D_MODEL ** -0.5)
    moe_w_down = nrm((N_MOE, N_EXPERTS, D_FF, D_MODEL), D_FF ** -0.5)
    return {
        "x": x, "c": c, "w_ada": w_ada, "b_ada": b_ada,
        "g_mix_pre": g_mix_pre, "g_mix_post": g_mix_post,
        "g_ffn_pre": g_ffn_pre, "g_ffn_post": g_ffn_post,
        "w_in": w_in, "conv_w": conv_w,
        "ssm_a_re": ssm_a_re, "ssm_a_im": ssm_a_im, "ssm_log_dt": ssm_log_dt,
        "ssm_b_re": ssm_b_re, "ssm_b_im": ssm_b_im,
        "ssm_c_re": ssm_c_re, "ssm_c_im": ssm_c_im, "ssm_d": ssm_d,
        "w_glu": w_glu, "b_glu": b_glu, "w_out": w_out,
        "ffn_w_gate": ffn_w_gate, "ffn_w_up": ffn_w_up, "ffn_w_down": ffn_w_down,
        "router_w": router_w, "router_b": router_b,
        "moe_w_gate": moe_w_gate, "moe_w_up": moe_w_up, "moe_w_down": moe_w_down,
    }


def reference(x, c, w_ada, b_ada, g_mix_pre, g_mix_post, g_ffn_pre, g_ffn_post,
              w_in, conv_w, ssm_a_re, ssm_a_im, ssm_log_dt, ssm_b_re, ssm_b_im,
              ssm_c_re, ssm_c_im, ssm_d, w_glu, b_glu, w_out,
              ffn_w_gate, ffn_w_up, ffn_w_down, router_w, router_b,
              moe_w_gate, moe_w_up, moe_w_down):
    c_act = jax.nn.silu(c)
    for l in range(DEPTH):
        mod = (c_act @ w_ada[l] + b_ada[l])[:, None, :]
        sh_m, sc_m, gt_m, sh_f, sc_f, gt_f = jnp.split(mod, 6, axis=-1)
        h = rmsnorm(x, g_mix_pre[l]) * (1.0 + sc_m) + sh_m
        y = hybrid_mixer(h, w_in[l], conv_w[l], ssm_a_re[l], ssm_a_im[l], ssm_log_dt[l],
                         ssm_b_re[l], ssm_b_im[l], ssm_c_re[l], ssm_c_im[l], ssm_d[l],
                         w_glu[l], b_glu[l], w_out[l])
        x = x + gt_m * rmsnorm(y, g_mix_post[l])
        h = rmsnorm(x, g_ffn_pre[l]) * (1.0 + sc_f) + sh_f
        if l % 2 == 0:
            i = l // 2
            y = swiglu(h, ffn_w_gate[i], ffn_w_up[i], ffn_w_down[i])
        else:
            i = l // 2
            y = moe_swiglu(h, router_w[i], router_b[i], moe_w_gate[i], moe_w_up[i], moe_w_down[i])
        x = x + gt_f * rmsnorm(y, g_ffn_post[l])
    return x
```

```python
import functools

import jax
import jax.numpy as jnp
from jax import lax
from jax.experimental import pallas as pl
from jax.experimental.pallas import tpu as pltpu

F32 = jnp.float32
BF16 = jnp.bfloat16
NORM_EPS = 1e-6
LANES = 128
SUBLANES = 8
MXU_DIM = 256
MIB = 1024 * 1024


def _params(semantics, vmem_mib):
    return pltpu.CompilerParams(dimension_semantics=semantics,
                                vmem_limit_bytes=vmem_mib * MIB)


def _dot(a, b):
    return jnp.dot(a, b, preferred_element_type=F32)


def _rms(x):
    return x * lax.rsqrt(jnp.mean(x * x, axis=-1, keepdims=True) + NORM_EPS)


def _tile(n, want):
    t = min(n, want)
    while n % t:
        t -= 1
    return t


def _adaln_kernel(c_ref, w_ref, b_ref, o_ref):
    c = c_ref[...]
    c_act = c * jax.nn.sigmoid(c)
    o_ref[...] = jnp.dot(c_act, w_ref[...], preferred_element_type=F32,
                         precision=lax.Precision.HIGHEST) + b_ref[...]


def adaln_mod(c_pad, w_ada, b_ada):
    depth, d, n = w_ada.shape
    rows = c_pad.shape[0]
    tn = _tile(n, 1024)
    return pl.pallas_call(
        _adaln_kernel,
        grid=(depth, n // tn),
        in_specs=[
            pl.BlockSpec((rows, d), lambda l, j: (0, 0)),
            pl.BlockSpec((None, d, tn), lambda l, j: (l, 0, j)),
            pl.BlockSpec((None, 1, tn), lambda l, j: (l, 0, j)),
        ],
        out_specs=pl.BlockSpec((None, rows, tn), lambda l, j: (l, 0, j)),
        out_shape=jax.ShapeDtypeStruct((depth, rows, n), F32),
        compiler_params=_params(("arbitrary", "arbitrary"), 32),
        name="adaln_mod",
    )(c_pad, w_ada, b_ada.reshape(depth, 1, n))


def _mixer_in_kernel(x_ref, sc_ref, sh_ref, g_ref, wb_ref, wc_ref, wv_ref, wu_ref,
                     cw_ref, yconv_ref, u_ref, h_scr, halo_scr, *, n_conv_tiles,
                     tiles_per_seq):
    i = pl.program_id(0)
    j = pl.program_id(1)
    tm = x_ref.shape[0]

    @pl.when(j == 0)
    def _():
        h = _rms(x_ref[...]) * g_ref[...] * (1.0 + sc_ref[...]) + sh_ref[...]
        h_scr[...] = h.astype(BF16)

    @pl.when(j < n_conv_tiles)
    def _():
        h = h_scr[...]
        gate_b = _dot(h, wb_ref[...])
        z = _dot(h, wc_ref[...]) * _dot(h, wv_ref[...])

        @pl.when(i % tiles_per_seq == 0)
        def _():
            halo_scr[j] = jnp.zeros(halo_scr.shape[1:], F32)

        prev = halo_scr[j]
        halo_scr[j] = z[tm - SUBLANES:, :]
        w0 = cw_ref[0:1, :]
        w1 = cw_ref[1:2, :]
        w2 = cw_ref[2:3, :]
        z1 = pltpu.roll(z, 1, 0)
        z2 = pltpu.roll(z, 2, 0)
        yconv_ref[...] = (gate_b * (z * w2 + z1 * w1 + z2 * w0)).astype(yconv_ref.dtype)
        row = lax.broadcasted_iota(jnp.int32, (SUBLANES, z.shape[1]), 0)
        z1t = jnp.where(row < 1, pltpu.roll(prev, 1, 0), z1[:SUBLANES, :])
        z2t = jnp.where(row < 2, pltpu.roll(prev, 2, 0), z2[:SUBLANES, :])
        top = gate_b[:SUBLANES, :] * (z[:SUBLANES, :] * w2 + z1t * w1 + z2t * w0)
        yconv_ref[0:SUBLANES, :] = top.astype(yconv_ref.dtype)

    @pl.when(j >= n_conv_tiles)
    def _():
        u_ref[...] = _dot(h_scr[...], wu_ref[...])


def mixer_in(x, sc, sh, g_pre, w_in16, conv_wt, *, seq_len, d_conv):
    t, d = x.shape
    d_ssm = w_in16.shape[1] - 3 * d_conv
    tm = _tile(seq_len, 512)
    tn = _tile(min(d_conv, d_ssm), 512)
    nc = d_conv // tn
    nu = d_ssm // tn
    tiles_per_seq = seq_len // tm
    last_c = nc - 1

    def conv_col(j):
        return jnp.minimum(j, last_c)

    def mod_spec():
        return pl.BlockSpec((None, 1, d), lambda i, j: (i // tiles_per_seq, 0, 0))

    kern = functools.partial(_mixer_in_kernel, n_conv_tiles=nc, tiles_per_seq=tiles_per_seq)
    return pl.pallas_call(
        kern,
        grid=(t // tm, nc + nu),
        in_specs=[
            pl.BlockSpec((tm, d), lambda i, j: (i, 0)),
            mod_spec(), mod_spec(),
            pl.BlockSpec((1, d), lambda i, j: (0, 0)),
            pl.BlockSpec((d, tn), lambda i, j: (0, conv_col(j))),
            pl.BlockSpec((d, tn), lambda i, j: (0, nc + conv_col(j))),
            pl.BlockSpec((d, tn), lambda i, j: (0, 2 * nc + conv_col(j))),
            pl.BlockSpec((d, tn), lambda i, j: (0, 3 * nc + jnp.maximum(j - nc, 0))),
            pl.BlockSpec((3, tn), lambda i, j: (0, conv_col(j))),
        ],
        out_specs=[
            pl.BlockSpec((tm, tn), lambda i, j: (i, conv_col(j))),
            pl.BlockSpec((tm, tn), lambda i, j: (i, jnp.maximum(j - nc, 0))),
        ],
        out_shape=[jax.ShapeDtypeStruct((t, d_conv), BF16),
                   jax.ShapeDtypeStruct((t, d_ssm), F32)],
        scratch_shapes=[pltpu.VMEM((tm, d), BF16),
                        pltpu.VMEM((nc, SUBLANES, tn), F32)],
        compiler_params=_params(("arbitrary", "arbitrary"), 48),
        name="mixer_in",
    )(x, sc, sh, g_pre, w_in16, w_in16, w_in16, w_in16, conv_wt)


def _ssm_prep_kernel(are_ref, aim_ref, ldt_ref, bre_ref, bim_ref,
                     bbre_ref, bbim_ref, abre_ref, abim_ref):
    lam_re = are_ref[...]
    lam_im = aim_ref[...]
    dt = jnp.exp(ldt_ref[...])
    mag = jnp.exp(lam_re * dt)
    ang = lam_im * dt
    ab_re = mag * jnp.cos(ang)
    ab_im = mag * jnp.sin(ang)
    den = lam_re * lam_re + lam_im * lam_im
    nr = ab_re - 1.0
    q_re = (nr * lam_re + ab_im * lam_im) / den
    q_im = (ab_im * lam_re - nr * lam_im) / den
    b_re = bre_ref[...]
    b_im = bim_ref[...]
    bbre_ref[...] = q_re * b_re - q_im * b_im
    bbim_ref[...] = q_re * b_im + q_im * b_re
    abre_ref[...] = ab_re
    abim_ref[...] = ab_im


def ssm_prep(a_re, a_im, log_dt, b_re, b_im):
    g, p, h = b_re.shape
    ldt = jnp.broadcast_to(log_dt.reshape(g, 1, 1), (g, p, 1))
    return pl.pallas_call(
        _ssm_prep_kernel,
        out_shape=[jax.ShapeDtypeStruct((g, p, h), F32),
                   jax.ShapeDtypeStruct((g, p, h), F32),
                   jax.ShapeDtypeStruct((g, p, 1), F32),
                   jax.ShapeDtypeStruct((g, p, 1), F32)],
        compiler_params=pltpu.CompilerParams(vmem_limit_bytes=40 * MIB),
        name="ssm_prep",
    )(a_re.reshape(g, p, 1), a_im.reshape(g, p, 1), ldt, b_re, b_im)


def _block_diag_slabs(w, groups_per_slab):
    g, r, c = w.shape
    ns = g // groups_per_slab
    w = w.reshape(ns, groups_per_slab, r, c)
    eye = jnp.eye(groups_per_slab, dtype=w.dtype)
    out = w[:, :, :, None, :] * eye[None, :, None, :, None]
    return out.reshape(ns, groups_per_slab * r, groups_per_slab * c)


def _gelu_tanh(x):
    return 0.5 * x * (1.0 + jnp.tanh(0.7978845608028654 * (x + 0.044715 * x * x * x)))


def _ssm_kernel(u_ref, bbd_ref, cre_ref, cim_ref, are_ref, aim_ref, d_ref, wglu_ref,
                bglu_ref, o_ref, sre_scr, sim_scr, stre_scr, stim_scr, y_scr, *,
                pitch, groups_per_step):
    nb, tc, d_ssm = u_ref.shape
    n_slabs, k_slab, two_sw = bbd_ref.shape
    sw = two_sw // 2
    tiles_per_slab = sw // LANES
    pack = SUBLANES // nb
    n_groups = sre_scr.shape[0]

    def slab_rows(lane_tile, b):
        q, h = divmod(lane_tile, pack)
        r0 = (h * nb + b) * pitch
        return q, slice(r0, r0 + tc)

    @pl.when(pl.program_id(0) == 0)
    def _():
        stre_scr[...] = jnp.zeros(stre_scr.shape, F32)
        stim_scr[...] = jnp.zeros(stim_scr.shape, F32)

    for b in range(nb):
        for s in range(n_slabs):
            ub = u_ref[b, :, s * k_slab:(s + 1) * k_slab].astype(BF16)
            r = _dot(ub, bbd_ref[s])
            for j in range(tiles_per_slab):
                q, rows = slab_rows(s * tiles_per_slab + j, b)
                sre_scr[q, rows, :] = r[:, j * LANES:(j + 1) * LANES]
                sim_scr[q, rows, :] = r[:, sw + j * LANES:sw + (j + 1) * LANES]

    for q0 in range(0, n_groups, groups_per_step):
        qs = list(range(q0, min(q0 + groups_per_step, n_groups)))
        a_re = [are_ref[q] for q in qs]
        a_im = [aim_ref[q] for q in qs]

        def step(t, carry, qs=qs, a_re=a_re, a_im=a_im):
            rows = pl.ds(t, SUBLANES, stride=pitch)
            out = []
            for n, q in enumerate(qs):
                s_re, s_im = carry[2 * n], carry[2 * n + 1]
                n_re = a_re[n] * s_re - a_im[n] * s_im + sre_scr[q, rows, :]
                n_im = a_re[n] * s_im + a_im[n] * s_re + sim_scr[q, rows, :]
                sre_scr[q, rows, :] = n_re
                sim_scr[q, rows, :] = n_im
                out += [n_re, n_im]
            return tuple(out)

        init = []
        for q in qs:
            init += [stre_scr[q], stim_scr[q]]
        fin = lax.fori_loop(0, tc, step, tuple(init), unroll=2)
        for n, q in enumerate(qs):
            stre_scr[q] = fin[2 * n]
            stim_scr[q] = fin[2 * n + 1]

    for b in range(nb):
        for s in range(n_slabs):
            cols = slice(s * k_slab, (s + 1) * k_slab)
            pieces_re, pieces_im = [], []
            for j in range(tiles_per_slab):
                q, rows = slab_rows(s * tiles_per_slab + j, b)
                pieces_re.append(sre_scr[q, rows, :].astype(BF16))
                pieces_im.append(sim_scr[q, rows, :].astype(BF16))
            y = (_dot(jnp.concatenate(pieces_re, axis=-1), cre_ref[s])
                 - _dot(jnp.concatenate(pieces_im, axis=-1), cim_ref[s]))
            y_scr[:, cols] = y + d_ref[:, cols] * u_ref[b, :, cols]
        y = _gelu_tanh(y_scr[...])
        gate = _dot(y.astype(BF16), wglu_ref[...]) + bglu_ref[...]
        o_ref[b] = (y * jax.nn.sigmoid(gate)).astype(o_ref.dtype)


def _pack_state_rows(a, nb):
    pack = SUBLANES // nb
    n_groups = a.shape[1] // (LANES * pack)
    a = a.reshape(n_groups, pack, 1, LANES)
    return jnp.broadcast_to(a, (n_groups, pack, nb, LANES)).reshape(n_groups, SUBLANES, LANES)


def ssm(u3, bbd, cre, cim, a_re, a_im, d_skip, w_glu16, b_glu):
    nb, seq_len, d_ssm = u3.shape
    n_state = a_re.shape[1]
    assert SUBLANES % nb == 0 and n_state % (LANES * (SUBLANES // nb)) == 0
    tc = _tile(seq_len, 128)
    pitch = tc + SUBLANES
    n_groups = n_state // (LANES * (SUBLANES // nb))
    a_re = _pack_state_rows(a_re, nb)
    a_im = _pack_state_rows(a_im, nb)
    kern = functools.partial(_ssm_kernel, pitch=pitch, groups_per_step=min(4, n_groups))

    def full(a):
        nd = a.ndim
        return pl.BlockSpec(a.shape, lambda c, nd=nd: (0,) * nd)

    return pl.pallas_call(
        kern,
        grid=(seq_len // tc,),
        in_specs=[pl.BlockSpec((nb, tc, d_ssm), lambda c: (0, c, 0)),
                  full(bbd), full(cre), full(cim), full(a_re), full(a_im),
                  full(d_skip), full(w_glu16), full(b_glu)],
        out_specs=pl.BlockSpec((nb, tc, d_ssm), lambda c: (0, c, 0)),
        out_shape=jax.ShapeDtypeStruct((nb, seq_len, d_ssm), BF16),
        scratch_shapes=[pltpu.VMEM((n_groups, SUBLANES * pitch, LANES), F32),
                        pltpu.VMEM((n_groups, SUBLANES * pitch, LANES), F32),
                        pltpu.VMEM((n_groups, SUBLANES, LANES), F32),
                        pltpu.VMEM((n_groups, SUBLANES, LANES), F32),
                        pltpu.VMEM((tc, d_ssm), F32)],
        compiler_params=_params(("arbitrary",), 56),
        name="ssm_scan",
    )(u3, bbd, cre, cim, a_re, a_im, d_skip, w_glu16, b_glu)


def _mixer_out_kernel(*refs, routed):
    if routed:
        (yc_ref, ys_ref, wo1_ref, wo2_ref, x_ref, gt_ref, gpost_ref, gpre_ref, sc_ref,
         sh_ref, rw_ref, rb_ref, xo_ref, h_ref, lg_ref) = refs
    else:
        (yc_ref, ys_ref, wo1_ref, wo2_ref, x_ref, gt_ref, gpost_ref, gpre_ref, sc_ref,
         sh_ref, xo_ref, h_ref) = refs
    y = _dot(yc_ref[...], wo1_ref[...]) + _dot(ys_ref[...], wo2_ref[...])
    x_new = x_ref[...] + gt_ref[...] * (_rms(y) * gpost_ref[...])
    xo_ref[...] = x_new
    h = _rms(x_new) * gpre_ref[...] * (1.0 + sc_ref[...]) + sh_ref[...]
    h_ref[...] = h.astype(h_ref.dtype)
    if routed:
        lg_ref[...] = jnp.dot(h, rw_ref[...], preferred_element_type=F32,
                              precision=lax.Precision.HIGHEST) + rb_ref[...]


def mixer_out(y_conv, y_ssm, w_out16, x, gt, g_post, g_pre, sc, sh, router=None, *,
              seq_len):
    t, d = x.shape
    d_conv = y_conv.shape[1]
    d_ssm = y_ssm.shape[1]
    tm = _tile(seq_len, 512)
    tiles_per_seq = seq_len // tm
    routed = router is not None

    def mod_spec():
        return pl.BlockSpec((None, 1, d), lambda i: (i // tiles_per_seq, 0, 0))

    def row_spec(n):
        return pl.BlockSpec((tm, n), lambda i: (i, 0))

    def const_spec(shape):
        return pl.BlockSpec(shape, lambda i: (0, 0))

    in_specs = [row_spec(d_conv), row_spec(d_ssm),
                const_spec((d_conv, d)), pl.BlockSpec((d_ssm, d), lambda i: (0, 0)),
                row_spec(d), mod_spec(), const_spec((1, d)), const_spec((1, d)),
                mod_spec(), mod_spec()]
    args = [y_conv, y_ssm, w_out16[:d_conv], w_out16[d_conv:], x, gt, g_post, g_pre, sc, sh]
    out_specs = [row_spec(d), row_spec(d)]
    out_shape = [jax.ShapeDtypeStruct((t, d), F32),
                 jax.ShapeDtypeStruct((t, d), F32 if routed else BF16)]
    if routed:
        rw_pad, rb_pad = router
        in_specs += [const_spec(rw_pad.shape), const_spec(rb_pad.shape)]
        args += [rw_pad, rb_pad]
        out_specs.append(row_spec(LANES))
        out_shape.append(jax.ShapeDtypeStruct((t, LANES), F32))
    return pl.pallas_call(
        functools.partial(_mixer_out_kernel, routed=routed),
        grid=(t // tm,),
        in_specs=in_specs, out_specs=out_specs, out_shape=out_shape,
        compiler_params=_params(("arbitrary",), 56),
        name="mixer_out_routed" if routed else "mixer_out",
    )(*args)


def _ffn_kernel(h_ref, wg_ref, wu_ref, wd_ref, x_ref, gt_ref, gpost_ref, o_ref, acc_ref):
    k = pl.program_id(1)

    @pl.when(k == 0)
    def _():
        acc_ref[...] = jnp.zeros(acc_ref.shape, F32)

    h = h_ref[...]
    gate = _dot(h, wg_ref[...])
    act = (gate * jax.nn.sigmoid(gate)) * _dot(h, wu_ref[...])
    acc_ref[...] += _dot(act.astype(BF16), wd_ref[...])

    @pl.when(k == pl.num_programs(1) - 1)
    def _():
        o_ref[...] = x_ref[...] + gt_ref[...] * (_rms(acc_ref[...]) * gpost_ref[...])


def ffn_dense(h16, wg16, wu16, wd16, x, gt, g_post, *, seq_len):
    t, d = x.shape
    d_ff = wg16.shape[1]
    tm = _tile(seq_len, 512)
    tk = _tile(d_ff, 512)
    tiles_per_seq = seq_len // tm
    return pl.pallas_call(
        _ffn_kernel,
        grid=(t // tm, d_ff // tk),
        in_specs=[
            pl.BlockSpec((tm, d), lambda i, k: (i, 0)),
            pl.BlockSpec((d, tk), lambda i, k: (0, k)),
            pl.BlockSpec((d, tk), lambda i, k: (0, k)),
            pl.BlockSpec((tk, d), lambda i, k: (k, 0)),
            pl.BlockSpec((tm, d), lambda i, k: (i, 0)),
            pl.BlockSpec((None, 1, d), lambda i, k: (i // tiles_per_seq, 0, 0)),
            pl.BlockSpec((1, d), lambda i, k: (0, 0)),
        ],
        out_specs=pl.BlockSpec((tm, d), lambda i, k: (i, 0)),
        out_shape=jax.ShapeDtypeStruct((t, d), F32),
        scratch_shapes=[pltpu.VMEM((tm, d), F32)],
        compiler_params=_params(("arbitrary", "arbitrary"), 48),
        name="ffn_dense",
    )(h16, wg16, wu16, wd16, x, gt, g_post)


def _router_kernel(lg_ref, ri_ref, rw_ref, cnt_ref, carry_scr, *, n_experts):
    @pl.when(pl.program_id(0) == 0)
    def _():
        carry_scr[...] = jnp.zeros(carry_scr.shape, F32)

    tm = lg_ref.shape[0]
    lane = lax.broadcasted_iota(jnp.int32, (tm, LANES), 1)
    neg = jnp.float32(-jnp.inf)
    logit = jnp.where(lane < n_experts, lg_ref[...], neg)
    m1 = jnp.max(logit, axis=-1, keepdims=True)
    i1 = jnp.min(jnp.where(logit == m1, lane, LANES), axis=-1, keepdims=True)
    rest = jnp.where(lane == i1, neg, logit)
    m2 = jnp.max(rest, axis=-1, keepdims=True)
    i2 = jnp.min(jnp.where(rest == m2, lane, LANES), axis=-1, keepdims=True)
    e2 = jnp.exp(m2 - m1)
    w1 = 1.0 / (1.0 + e2)
    w2 = e2 / (1.0 + e2)
    hit1 = lane == i1
    hit2 = lane == i2
    onehot = jnp.where(hit1 | hit2, 1.0, 0.0)
    r = lax.broadcasted_iota(jnp.int32, (tm, tm), 0)
    c = lax.broadcasted_iota(jnp.int32, (tm, tm), 1)
    earlier = jnp.where(c < r, 1.0, 0.0).astype(BF16)
    carry = carry_scr[0:1, :]
    before = _dot(earlier, onehot.astype(BF16)) + carry
    pos1 = jnp.sum(jnp.where(hit1, before, 0.0), axis=-1, keepdims=True).astype(jnp.int32)
    pos2 = jnp.sum(jnp.where(hit2, before, 0.0), axis=-1, keepdims=True).astype(jnp.int32)
    total = carry + jnp.sum(onehot, axis=0, keepdims=True)
    carry_scr[...] = jnp.broadcast_to(total, carry_scr.shape)
    cnt_ref[...] = jnp.broadcast_to(total, cnt_ref.shape).astype(jnp.int32)
    zero_i = jnp.zeros((tm, LANES), jnp.int32)
    ri_ref[...] = jnp.where(lane == 0, i1, jnp.where(lane == 1, i2,
                            jnp.where(lane == 2, pos1, jnp.where(lane == 3, pos2, zero_i))))
    rw_ref[...] = jnp.where(lane == 0, w1, jnp.where(lane == 1, w2, 0.0))


def router(logits, n_experts):
    t = logits.shape[0]
    tm = _tile(t, 256)
    return pl.pallas_call(
        functools.partial(_router_kernel, n_experts=n_experts),
        grid=(t // tm,),
        in_specs=[pl.BlockSpec((tm, LANES), lambda i: (i, 0))],
        out_specs=[pl.BlockSpec((tm, LANES), lambda i: (i, 0)),
                   pl.BlockSpec((tm, LANES), lambda i: (i, 0)),
                   pl.BlockSpec((SUBLANES, LANES), lambda i: (0, 0))],
        out_shape=[jax.ShapeDtypeStruct((t, LANES), jnp.int32),
                   jax.ShapeDtypeStruct((t, LANES), F32),
                   jax.ShapeDtypeStruct((SUBLANES, LANES), jnp.int32)],
        scratch_shapes=[pltpu.VMEM((SUBLANES, LANES), F32)],
        compiler_params=_params(("arbitrary",), 32),
        name="router_top2",
    )(logits)


DMA_WINDOW = 16


def _gather_rows_kernel(nrows_ref, tok_ref, src_ref, dst_ref, sem):
    n = nrows_ref[0]

    def row_copy(src_row, dst_row):
        return pltpu.make_async_copy(src_ref.at[pl.ds(src_row, 1)],
                                     dst_ref.at[pl.ds(dst_row, 1)], sem)

    def body(r, carry):
        @pl.when(r >= DMA_WINDOW)
        def _():
            row_copy(0, 0).wait()
        row_copy(tok_ref[r], r).start()
        return carry

    lax.fori_loop(0, n, body, 0)

    def drain(r, carry):
        row_copy(0, 0).wait()
        return carry

    lax.fori_loop(0, jnp.minimum(n, DMA_WINDOW), drain, 0)


def gather_rows(n_rows_used, row_tok, src, n_rows):
    d = src.shape[1]
    return pl.pallas_call(
        _gather_rows_kernel,
        grid_spec=pltpu.PrefetchScalarGridSpec(
            num_scalar_prefetch=2,
            grid=(1,),
            in_specs=[pl.BlockSpec(memory_space=pl.ANY)],
            out_specs=pl.BlockSpec(memory_space=pl.ANY),
            scratch_shapes=[pltpu.SemaphoreType.DMA(())],
        ),
        out_shape=jax.ShapeDtypeStruct((n_rows, d), src.dtype),
        compiler_params=_params(("arbitrary",), 16),
        name="gather_rows",
    )(n_rows_used, row_tok, src)


def _moe_ffn_kernel(te_ref, nu_ref, xs_ref, wg_ref, wu_ref, wd_ref, o_ref, acc_ref, h_scr):
    r = pl.program_id(0)
    k = pl.program_id(1)

    @pl.when(r < nu_ref[0])
    def _():
        @pl.when(k == 0)
        def _():
            acc_ref[...] = jnp.zeros(acc_ref.shape, F32)
            h_scr[...] = xs_ref[...].astype(BF16)

        h = h_scr[...]
        gate = _dot(h, wg_ref[...])
        act = (gate * jax.nn.sigmoid(gate)) * _dot(h, wu_ref[...])
        acc_ref[...] += _dot(act.astype(BF16), wd_ref[...])

        @pl.when(k == pl.num_programs(1) - 1)
        def _():
            o_ref[...] = acc_ref[...]


def moe_ffn(tile_expert, n_used, xs, wg16, wu16, wd16, tm):
    n_rows, d = xs.shape
    d_ff = wg16.shape[2]
    tk = _tile(d_ff, 512)
    nk = d_ff // tk

    def row_map(r, k, te, nu):
        return (jnp.minimum(r, nu[0] - 1), 0)

    def ff_idx(r, k, nu):
        return jnp.where(r < nu[0], k, nk - 1)

    return pl.pallas_call(
        _moe_ffn_kernel,
        grid_spec=pltpu.PrefetchScalarGridSpec(
            num_scalar_prefetch=2,
            grid=(n_rows // tm, nk),
            in_specs=[
                pl.BlockSpec((tm, d), row_map),
                pl.BlockSpec((None, d, tk), lambda r, k, te, nu: (te[r], 0, ff_idx(r, k, nu))),
                pl.BlockSpec((None, d, tk), lambda r, k, te, nu: (te[r], 0, ff_idx(r, k, nu))),
                pl.BlockSpec((None, tk, d), lambda r, k, te, nu: (te[r], ff_idx(r, k, nu), 0)),
            ],
            out_specs=pl.BlockSpec((tm, d), row_map),
            scratch_shapes=[pltpu.VMEM((tm, d), F32), pltpu.VMEM((tm, d), BF16)],
        ),
        out_shape=jax.ShapeDtypeStruct((n_rows, d), F32),
        compiler_params=_params(("arbitrary", "arbitrary"), 48),
        name="moe_ffn",
    )(tile_expert, n_used, xs, wg16, wu16, wd16)


def _moe_combine_kernel(d0_ref, d1_ref, ys_ref, rw_ref, x_ref, gt_ref, gpost_ref, o_ref,
                        buf, sem):
    tm = x_ref.shape[0]
    base = pl.program_id(0) * tm

    def row_copy(src_row, slot, t):
        return pltpu.make_async_copy(ys_ref.at[pl.ds(src_row, 1)],
                                     buf.at[slot, pl.ds(t, 1)], sem)

    def issue(t, carry):
        row_copy(d0_ref[base + t], 0, t).start()
        row_copy(d1_ref[base + t], 1, t).start()
        return carry

    lax.fori_loop(0, tm, issue, 0)

    def drain(t, carry):
        row_copy(0, 0, t).wait()
        row_copy(0, 1, t).wait()
        return carry

    lax.fori_loop(0, tm, drain, 0)
    y = buf[0] * rw_ref[:, 0:1] + buf[1] * rw_ref[:, 1:2]
    o_ref[...] = x_ref[...] + gt_ref[...] * (_rms(y) * gpost_ref[...])


def moe_combine(dest0, dest1, ys, route_w, x, gt, g_post, *, seq_len):
    t, d = x.shape
    tm = _tile(seq_len, 256)
    tiles_per_seq = seq_len // tm
    return pl.pallas_call(
        _moe_combine_kernel,
        grid_spec=pltpu.PrefetchScalarGridSpec(
            num_scalar_prefetch=2,
            grid=(t // tm,),
            in_specs=[
                pl.BlockSpec(memory_space=pl.ANY),
                pl.BlockSpec((tm, LANES), lambda i, d0, d1: (i, 0)),
                pl.BlockSpec((tm, d), lambda i, d0, d1: (i, 0)),
                pl.BlockSpec((None, 1, d), lambda i, d0, d1: (i // tiles_per_seq, 0, 0)),
                pl.BlockSpec((1, d), lambda i, d0, d1: (0, 0)),
            ],
            out_specs=pl.BlockSpec((tm, d), lambda i, d0, d1: (i, 0)),
            scratch_shapes=[pltpu.VMEM((2, tm, d), F32), pltpu.SemaphoreType.DMA(())],
        ),
        out_shape=jax.ShapeDtypeStruct((t, d), F32),
        compiler_params=_params(("arbitrary",), 32),
        name="moe_combine",
    )(dest0, dest1, ys, route_w, x, gt, g_post)


MOE_TILE_ROWS = 512


def moe_layer(h, logits, x, gt, g_post, wg16, wu16, wd16, *, seq_len):
    t, d = h.shape
    n_experts = wg16.shape[0]
    tm = min(MOE_TILE_ROWS, t)
    route_i, route_w, counts = router(logits, n_experts)
    e0, e1, pos0, pos1 = (route_i[:, k] for k in range(4))
    counts = counts[0, :n_experts]
    padded = ((counts + tm - 1) // tm) * tm
    pad_end = jnp.cumsum(padded)
    pad_start = pad_end - padded
    dest0 = pad_start[e0] + pos0
    dest1 = pad_start[e1] + pos1
    n_tiles = (2 * t) // tm + n_experts
    n_rows = n_tiles * tm
    tok = jnp.arange(t, dtype=jnp.int32)
    row_tok = jnp.zeros((n_rows,), jnp.int32).at[dest0].set(tok).at[dest1].set(tok)
    n_used = (pad_end[-1] // tm).astype(jnp.int32)
    tile_e = jnp.clip(jnp.searchsorted(pad_end, jnp.arange(n_tiles, dtype=jnp.int32) * tm,
                                       side='right'), 0, n_experts - 1).astype(jnp.int32)
    tile_e = jnp.where(jnp.arange(n_tiles) < n_used, tile_e, tile_e[n_used - 1])
    n_used1 = n_used.reshape(1)
    xs = gather_rows(n_used1 * tm, row_tok, h, n_rows)
    ys = moe_ffn(tile_e, n_used1, xs, wg16, wu16, wd16, tm)
    return moe_combine(dest0, dest1, ys, route_w, x, gt, g_post, seq_len=seq_len)


def kernel(x, c, w_ada, b_ada, g_mix_pre, g_mix_post, g_ffn_pre, g_ffn_post, w_in, conv_w,
           ssm_a_re, ssm_a_im, ssm_log_dt, ssm_b_re, ssm_b_im, ssm_c_re, ssm_c_im, ssm_d,
           w_glu, b_glu, w_out, ffn_w_gate, ffn_w_up, ffn_w_down, router_w, router_b,
           moe_w_gate, moe_w_up, moe_w_down):
    nb, seq_len, d = x.shape
    depth = w_ada.shape[0]
    d_conv = conv_w.shape[1]
    g, p, h_dim = ssm_b_re.shape[1:]
    d_ssm = g * h_dim
    n_experts = router_w.shape[2]
    t = nb * seq_len
    groups_per_slab = max(1, min(g, MXU_DIM // h_dim))

    c_pad = jnp.zeros((SUBLANES, d), F32).at[:nb].set(c)
    mod = adaln_mod(c_pad, w_ada, b_ada)
    xf = x.reshape(t, d)

    for l in range(depth):
        sh_m, sc_m, gt_m, sh_f, sc_f, gt_f = (
            mod[l, :nb, k * d:(k + 1) * d].reshape(nb, 1, d) for k in range(6))
        y_conv, u = mixer_in(xf, sc_m, sh_m, g_mix_pre[l].reshape(1, d),
                             w_in[l].astype(BF16), conv_w[l].T,
                             seq_len=seq_len, d_conv=d_conv)
        bb_re, bb_im, ab_re, ab_im = ssm_prep(ssm_a_re[l], ssm_a_im[l], ssm_log_dt[l],
                                              ssm_b_re[l], ssm_b_im[l])
        bbd = jnp.concatenate(
            [_block_diag_slabs(jnp.swapaxes(bb_re, 1, 2), groups_per_slab),
             _block_diag_slabs(jnp.swapaxes(bb_im, 1, 2), groups_per_slab)],
            axis=-1).astype(BF16)
        cre = _block_diag_slabs(jnp.swapaxes(ssm_c_re[l], 1, 2), groups_per_slab).astype(BF16)
        cim = _block_diag_slabs(jnp.swapaxes(ssm_c_im[l], 1, 2), groups_per_slab).astype(BF16)
        y_ssm = ssm(u.reshape(nb, seq_len, d_ssm), bbd, cre, cim,
                    ab_re.reshape(1, g * p), ab_im.reshape(1, g * p),
                    ssm_d[l].reshape(1, d_ssm), w_glu[l].astype(BF16),
                    b_glu[l].reshape(1, d_ssm)).reshape(t, d_ssm)
        routed = l % 2 == 1
        i = l // 2
        rt = None
        if routed:
            rw_pad = jnp.zeros((d, LANES), F32).at[:, :n_experts].set(router_w[i])
            rb_pad = jnp.zeros((1, LANES), F32).at[0, :n_experts].set(router_b[i])
            rt = (rw_pad, rb_pad)
        outs = mixer_out(y_conv, y_ssm, w_out[l].astype(BF16), xf, gt_m,
                         g_mix_post[l].reshape(1, d), g_ffn_pre[l].reshape(1, d),
                         sc_f, sh_f, rt, seq_len=seq_len)
        g_post = g_ffn_post[l].reshape(1, d)
        if routed:
            xf, h, logits = outs
            xf = moe_layer(h, logits, xf, gt_f, g_post, moe_w_gate[i].astype(BF16),
                           moe_w_up[i].astype(BF16), moe_w_down[i].astype(BF16),
                           seq_len=seq_len)
        else:
            xf, h = outs
            xf = ffn_dense(h, ffn_w_gate[i].astype(BF16), ffn_w_up[i].astype(BF16),
                           ffn_w_down[i].astype(BF16), xf, gt_f, g_post, seq_len=seq_len)
    return xf.reshape(nb, seq_len, d)
```

```python
import functools

import jax
import jax.numpy as jnp
from jax import lax
from jax.experimental import pallas as pl
from jax.experimental.pallas import tpu as pltpu

F32 = jnp.float32
BF16 = jnp.bfloat16
NORM_EPS = 1e-6
LANES = 128
SUBLANES = 8
MXU_DIM = 256
MIB = 1024 * 1024


def _params(semantics, vmem_mib):
    return pltpu.CompilerParams(dimension_semantics=semantics,
                                vmem_limit_bytes=vmem_mib * MIB)


def _dot(a, b):
    return jnp.dot(a, b, preferred_element_type=F32)


def _rms(x):
    return x * lax.rsqrt(jnp.mean(x * x, axis=-1, keepdims=True) + NORM_EPS)


def _tile(n, want):
    t = min(n, want)
    while n % t:
        t -= 1
    return t


def _adaln_kernel(c_ref, w_ref, b_ref, o_ref):
    c = c_ref[...]
    c_act = c * jax.nn.sigmoid(c)
    o_ref[...] = jnp.dot(c_act, w_ref[...], preferred_element_type=F32,
                         precision=lax.Precision.HIGHEST) + b_ref[...]


def adaln_mod(c_pad, w_ada, b_ada):
    depth, d, n = w_ada.shape
    rows = c_pad.shape[0]
    tn = _tile(n, 1024)
    return pl.pallas_call(
        _adaln_kernel,
        grid=(depth, n // tn),
        in_specs=[
            pl.BlockSpec((rows, d), lambda l, j: (0, 0)),
            pl.BlockSpec((None, d, tn), lambda l, j: (l, 0, j)),
            pl.BlockSpec((None, 1, tn), lambda l, j: (l, 0, j)),
        ],
        out_specs=pl.BlockSpec((None, rows, tn), lambda l, j: (l, 0, j)),
        out_shape=jax.ShapeDtypeStruct((depth, rows, n), F32),
        compiler_params=_params(("arbitrary", "arbitrary"), 32),
        name="adaln_mod",
    )(c_pad, w_ada, b_ada.reshape(depth, 1, n))


def _mixer_in_kernel(x_ref, sc_ref, sh_ref, g_ref, wb_ref, wc_ref, wv_ref, wu_ref,
                     cw_ref, yconv_ref, u_ref, h_scr, halo_scr, *, n_conv_tiles,
                     tiles_per_seq):
    i = pl.program_id(0)
    j = pl.program_id(1)
    tm = x_ref.shape[0]

    @pl.when(j == 0)
    def _():
        h = _rms(x_ref[...]) * g_ref[...] * (1.0 + sc_ref[...]) + sh_ref[...]
        h_scr[...] = h.astype(BF16)

    @pl.when(j < n_conv_tiles)
    def _():
        h = h_scr[...]
        gate_b = _dot(h, wb_ref[...])
        z = _dot(h, wc_ref[...]) * _dot(h, wv_ref[...])

        @pl.when(i % tiles_per_seq == 0)
        def _():
            halo_scr[j] = jnp.zeros(halo_scr.shape[1:], F32)

        prev = halo_scr[j]
        halo_scr[j] = z[tm - SUBLANES:, :]
        w0 = cw_ref[0:1, :]
        w1 = cw_ref[1:2, :]
        w2 = cw_ref[2:3, :]
        z1 = pltpu.roll(z, 1, 0)
        z2 = pltpu.roll(z, 2, 0)
        yconv_ref[...] = (gate_b * (z * w2 + z1 * w1 + z2 * w0)).astype(yconv_ref.dtype)
        row = lax.broadcasted_iota(jnp.int32, (SUBLANES, z.shape[1]), 0)
        z1t = jnp.where(row < 1, pltpu.roll(prev, 1, 0), z1[:SUBLANES, :])
        z2t = jnp.where(row < 2, pltpu.roll(prev, 2, 0), z2[:SUBLANES, :])
        top = gate_b[:SUBLANES, :] * (z[:SUBLANES, :] * w2 + z1t * w1 + z2t * w0)
        yconv_ref[0:SUBLANES, :] = top.astype(yconv_ref.dtype)

    @pl.when(j >= n_conv_tiles)
    def _():
        u_ref[...] = _dot(h_scr[...], wu_ref[...])


def mixer_in(x, sc, sh, g_pre, w_in16, conv_wt, *, seq_len, d_conv):
    t, d = x.shape
    d_ssm = w_in16.shape[1] - 3 * d_conv
    tm = _tile(seq_len, 512)
    tn = _tile(min(d_conv, d_ssm), 512)
    nc = d_conv // tn
    nu = d_ssm // tn
    tiles_per_seq = seq_len // tm
    last_c = nc - 1

    def conv_col(j):
        return jnp.minimum(j, last_c)

    def mod_spec():
        return pl.BlockSpec((None, 1, d), lambda i, j: (i // tiles_per_seq, 0, 0))

    kern = functools.partial(_mixer_in_kernel, n_conv_tiles=nc, tiles_per_seq=tiles_per_seq)
    return pl.pallas_call(
        kern,
        grid=(t // tm, nc + nu),
        in_specs=[
            pl.BlockSpec((tm, d), lambda i, j: (i, 0)),
            mod_spec(), mod_spec(),
            pl.BlockSpec((1, d), lambda i, j: (0, 0)),
            pl.BlockSpec((d, tn), lambda i, j: (0, conv_col(j))),
            pl.BlockSpec((d, tn), lambda i, j: (0, nc + conv_col(j))),
            pl.BlockSpec((d, tn), lambda i, j: (0, 2 * nc + conv_col(j))),
            pl.BlockSpec((d, tn), lambda i, j: (0, 3 * nc + jnp.maximum(j - nc, 0))),
            pl.BlockSpec((3, tn), lambda i, j: (0, conv_col(j))),
        ],
        out_specs=[
            pl.BlockSpec((tm, tn), lambda i, j: (i, conv_col(j))),
            pl.BlockSpec((tm, tn), lambda i, j: (i, jnp.maximum(j - nc, 0))),
        ],
        out_shape=[jax.ShapeDtypeStruct((t, d_conv), BF16),
                   jax.ShapeDtypeStruct((t, d_ssm), F32)],
        scratch_shapes=[pltpu.VMEM((tm, d), BF16),
                        pltpu.VMEM((nc, SUBLANES, tn), F32)],
        compiler_params=_params(("arbitrary", "arbitrary"), 48),
        name="mixer_in",
    )(x, sc, sh, g_pre, w_in16, w_in16, w_in16, w_in16, conv_wt)


def _ssm_prep_kernel(are_ref, aim_ref, ldt_ref, bre_ref, bim_ref,
                     bbre_ref, bbim_ref, abre_ref, abim_ref):
    lam_re = are_ref[...]
    lam_im = aim_ref[...]
    dt = jnp.exp(ldt_ref[...])
    mag = jnp.exp(lam_re * dt)
    ang = lam_im * dt
    ab_re = mag * jnp.cos(ang)
    ab_im = mag * jnp.sin(ang)
    den = lam_re * lam_re + lam_im * lam_im
    nr = ab_re - 1.0
    q_re = (nr * lam_re + ab_im * lam_im) / den
    q_im = (ab_im * lam_re - nr * lam_im) / den
    b_re = bre_ref[...]
    b_im = bim_ref[...]
    bbre_ref[...] = q_re * b_re - q_im * b_im
    bbim_ref[...] = q_re * b_im + q_im * b_re
    abre_ref[...] = ab_re
    abim_ref[...] = ab_im


def ssm_prep(a_re, a_im, log_dt, b_re, b_im):
    g, p, h = b_re.shape
    ldt = jnp.broadcast_to(log_dt.reshape(g, 1, 1), (g, p, 1))
    return pl.pallas_call(
        _ssm_prep_kernel,
        out_shape=[jax.ShapeDtypeStruct((g, p, h), F32),
                   jax.ShapeDtypeStruct((g, p, h), F32),
                   jax.ShapeDtypeStruct((g, p, 1), F32),
                   jax.ShapeDtypeStruct((g, p, 1), F32)],
        compiler_params=pltpu.CompilerParams(vmem_limit_bytes=40 * MIB),
        name="ssm_prep",
    )(a_re.reshape(g, p, 1), a_im.reshape(g, p, 1), ldt, b_re, b_im)


def _block_diag_slabs(w, groups_per_slab):
    g, r, c = w.shape
    ns = g // groups_per_slab
    w = w.reshape(ns, groups_per_slab, r, c)
    eye = jnp.eye(groups_per_slab, dtype=w.dtype)
    out = w[:, :, :, None, :] * eye[None, :, None, :, None]
    return out.reshape(ns, groups_per_slab * r, groups_per_slab * c)


def _gelu_tanh(x):
    return 0.5 * x * (1.0 + jnp.tanh(0.7978845608028654 * (x + 0.044715 * x * x * x)))


def _ssm_kernel(u_ref, bbd_ref, cre_ref, cim_ref, are_ref, aim_ref, d_ref, wglu_ref,
                bglu_ref, o_ref, sre_scr, sim_scr, stre_scr, stim_scr, y_scr, *,
                pitch, groups_per_step):
    nb, tc, d_ssm = u_ref.shape
    n_slabs, k_slab, two_sw = bbd_ref.shape
    sw = two_sw // 2
    tiles_per_slab = sw // LANES
    pack = SUBLANES // nb
    n_groups = sre_scr.shape[0]

    def slab_rows(lane_tile, b):
        q, h = divmod(lane_tile, pack)
        r0 = (h * nb + b) * pitch
        return q, slice(r0, r0 + tc)

    @pl.when(pl.program_id(0) == 0)
    def _():
        stre_scr[...] = jnp.zeros(stre_scr.shape, F32)
        stim_scr[...] = jnp.zeros(stim_scr.shape, F32)

    for b in range(nb):
        for s in range(n_slabs):
            ub = u_ref[b, :, s * k_slab:(s + 1) * k_slab].astype(BF16)
            r = _dot(ub, bbd_ref[s])
            for j in range(tiles_per_slab):
                q, rows = slab_rows(s * tiles_per_slab + j, b)
                sre_scr[q, rows, :] = r[:, j * LANES:(j + 1) * LANES]
                sim_scr[q, rows, :] = r[:, sw + j * LANES:sw + (j + 1) * LANES]

    for q0 in range(0, n_groups, groups_per_step):
        qs = list(range(q0, min(q0 + groups_per_step, n_groups)))
        a_re = [are_ref[q] for q in qs]
        a_im = [aim_ref[q] for q in qs]

        def step(t, carry, qs=qs, a_re=a_re, a_im=a_im):
            rows = pl.ds(t, SUBLANES, stride=pitch)
            out = []
            for n, q in enumerate(qs):
                s_re, s_im = carry[2 * n], carry[2 * n + 1]
                n_re = a_re[n] * s_re - a_im[n] * s_im + sre_scr[q, rows, :]
                n_im = a_re[n] * s_im + a_im[n] * s_re + sim_scr[q, rows, :]
                sre_scr[q, rows, :] = n_re
                sim_scr[q, rows, :] = n_im
                out += [n_re, n_im]
            return tuple(out)

        init = []
        for q in qs:
            init += [stre_scr[q], stim_scr[q]]
        fin = lax.fori_loop(0, tc, step, tuple(init), unroll=2)
        for n, q in enumerate(qs):
            stre_scr[q] = fin[2 * n]
            stim_scr[q] = fin[2 * n + 1]

    for b in range(nb):
        for s in range(n_slabs):
            cols = slice(s * k_slab, (s + 1) * k_slab)
            pieces_re, pieces_im = [], []
            for j in range(tiles_per_slab):
                q, rows = slab_rows(s * tiles_per_slab + j, b)
                pieces_re.append(sre_scr[q, rows, :].astype(BF16))
                pieces_im.append(sim_scr[q, rows, :].astype(BF16))
            y = (_dot(jnp.concatenate(pieces_re, axis=-1), cre_ref[s])
                 - _dot(jnp.concatenate(pieces_im, axis=-1), cim_ref[s]))
            y_scr[:, cols] = y + d_ref[:, cols] * u_ref[b, :, cols]
        y = _gelu_tanh(y_scr[...])
        gate = _dot(y.astype(BF16), wglu_ref[...]) + bglu_ref[...]
        o_ref[b] = (y * jax.nn.sigmoid(gate)).astype(o_ref.dtype)


def _pack_state_rows(a, nb):
    pack = SUBLANES // nb
    n_groups = a.shape[1] // (LANES * pack)
    a = a.reshape(n_groups, pack, 1, LANES)
    return jnp.broadcast_to(a, (n_groups, pack, nb, LANES)).reshape(n_groups, SUBLANES, LANES)


def ssm(u3, bbd, cre, cim, a_re, a_im, d_skip, w_glu16, b_glu):
    nb, seq_len, d_ssm = u3.shape
    n_state = a_re.shape[1]
    assert SUBLANES % nb == 0 and n_state % (LANES * (SUBLANES // nb)) == 0
    tc = _tile(seq_len, 128)
    pitch = tc + SUBLANES
    n_groups = n_state // (LANES * (SUBLANES // nb))
    a_re = _pack_state_rows(a_re, nb)
    a_im = _pack_state_rows(a_im, nb)
    kern = functools.partial(_ssm_kernel, pitch=pitch, groups_per_step=min(4, n_groups))

    def full(a):
        nd = a.ndim
        return pl.BlockSpec(a.shape, lambda c, nd=nd: (0,) * nd)

    return pl.pallas_call(
        kern,
        grid=(seq_len // tc,),
        in_specs=[pl.BlockSpec((nb, tc, d_ssm), lambda c: (0, c, 0)),
                  full(bbd), full(cre), full(cim), full(a_re), full(a_im),
                  full(d_skip), full(w_glu16), full(b_glu)],
        out_specs=pl.BlockSpec((nb, tc, d_ssm), lambda c: (0, c, 0)),
        out_shape=jax.ShapeDtypeStruct((nb, seq_len, d_ssm), BF16),
        scratch_shapes=[pltpu.VMEM((n_groups, SUBLANES * pitch, LANES), F32),
                        pltpu.VMEM((n_groups, SUBLANES * pitch, LANES), F32),
                        pltpu.VMEM((n_groups, SUBLANES, LANES), F32),
                        pltpu.VMEM((n_groups, SUBLANES, LANES), F32),
                        pltpu.VMEM((tc, d_ssm), F32)],
        compiler_params=_params(("arbitrary",), 56),
        name="ssm_scan",
    )(u3, bbd, cre, cim, a_re, a_im, d_skip, w_glu16, b_glu)


def _mixer_out_kernel(*refs, routed):
    if routed:
        (yc_ref, ys_ref, wo1_ref, wo2_ref, x_ref, gt_ref, gpost_ref, gpre_ref, sc_ref,
         sh_ref, rw_ref, rb_ref, xo_ref, h_ref, lg_ref) = refs
    else:
        (yc_ref, ys_ref, wo1_ref, wo2_ref, x_ref, gt_ref, gpost_ref, gpre_ref, sc_ref,
         sh_ref, xo_ref, h_ref) = refs
    y = _dot(yc_ref[...], wo1_ref[...]) + _dot(ys_ref[...], wo2_ref[...])
    x_new = x_ref[...] + gt_ref[...] * (_rms(y) * gpost_ref[...])
    xo_ref[...] = x_new
    h = _rms(x_new) * gpre_ref[...] * (1.0 + sc_ref[...]) + sh_ref[...]
    h_ref[...] = h.astype(h_ref.dtype)
    if routed:
        w = rw_ref[...]
        h_hi = h.astype(BF16)
        w_hi = w.astype(BF16)
        h_lo = (h - h_hi.astype(F32)).astype(BF16)
        w_lo = (w - w_hi.astype(F32)).astype(BF16)
        lg_ref[...] = (_dot(h_hi, w_hi) + (_dot(h_hi, w_lo) + _dot(h_lo, w_hi))
                       + rb_ref[...])


def mixer_out(y_conv, y_ssm, w_out16, x, gt, g_post, g_pre, sc, sh, router=None, *,
              seq_len):
    t, d = x.shape
    d_conv = y_conv.shape[1]
    d_ssm = y_ssm.shape[1]
    tm = _tile(seq_len, 512)
    tiles_per_seq = seq_len // tm
    routed = router is not None

    def mod_spec():
        return pl.BlockSpec((None, 1, d), lambda i: (i // tiles_per_seq, 0, 0))

    def row_spec(n):
        return pl.BlockSpec((tm, n), lambda i: (i, 0))

    def const_spec(shape):
        return pl.BlockSpec(shape, lambda i: (0, 0))

    in_specs = [row_spec(d_conv), row_spec(d_ssm),
                const_spec((d_conv, d)), pl.BlockSpec((d_ssm, d), lambda i: (0, 0)),
                row_spec(d), mod_spec(), const_spec((1, d)), const_spec((1, d)),
                mod_spec(), mod_spec()]
    args = [y_conv, y_ssm, w_out16[:d_conv], w_out16[d_conv:], x, gt, g_post, g_pre, sc, sh]
    out_specs = [row_spec(d), row_spec(d)]
    out_shape = [jax.ShapeDtypeStruct((t, d), F32),
                 jax.ShapeDtypeStruct((t, d), F32 if routed else BF16)]
    if routed:
        rw_pad, rb_pad = router
        in_specs += [const_spec(rw_pad.shape), const_spec(rb_pad.shape)]
        args += [rw_pad, rb_pad]
        out_specs.append(row_spec(LANES))
        out_shape.append(jax.ShapeDtypeStruct((t, LANES), F32))
    return pl.pallas_call(
        functools.partial(_mixer_out_kernel, routed=routed),
        grid=(t // tm,),
        in_specs=in_specs, out_specs=out_specs, out_shape=out_shape,
        compiler_params=_params(("arbitrary",), 56),
        name="mixer_out_routed" if routed else "mixer_out",
    )(*args)


def _ffn_kernel(h_ref, wg_ref, wu_ref, wd_ref, x_ref, gt_ref, gpost_ref, o_ref, acc_ref):
    k = pl.program_id(1)

    @pl.when(k == 0)
    def _():
        acc_ref[...] = jnp.zeros(acc_ref.shape, F32)

    h = h_ref[...]
    gate = _dot(h, wg_ref[...])
    act = (gate * jax.nn.sigmoid(gate)) * _dot(h, wu_ref[...])
    acc_ref[...] += _dot(act.astype(BF16), wd_ref[...])

    @pl.when(k == pl.num_programs(1) - 1)
    def _():
        o_ref[...] = x_ref[...] + gt_ref[...] * (_rms(acc_ref[...]) * gpost_ref[...])


def ffn_dense(h16, wg16, wu16, wd16, x, gt, g_post, *, seq_len):
    t, d = x.shape
    d_ff = wg16.shape[1]
    tm = _tile(seq_len, 512)
    tk = _tile(d_ff, 512)
    tiles_per_seq = seq_len // tm
    return pl.pallas_call(
        _ffn_kernel,
        grid=(t // tm, d_ff // tk),
        in_specs=[
            pl.BlockSpec((tm, d), lambda i, k: (i, 0)),
            pl.BlockSpec((d, tk), lambda i, k: (0, k)),
            pl.BlockSpec((d, tk), lambda i, k: (0, k)),
            pl.BlockSpec((tk, d), lambda i, k: (k, 0)),
            pl.BlockSpec((tm, d), lambda i, k: (i, 0)),
            pl.BlockSpec((None, 1, d), lambda i, k: (i // tiles_per_seq, 0, 0)),
            pl.BlockSpec((1, d), lambda i, k: (0, 0)),
        ],
        out_specs=pl.BlockSpec((tm, d), lambda i, k: (i, 0)),
        out_shape=jax.ShapeDtypeStruct((t, d), F32),
        scratch_shapes=[pltpu.VMEM((tm, d), F32)],
        compiler_params=_params(("arbitrary", "arbitrary"), 48),
        name="ffn_dense",
    )(h16, wg16, wu16, wd16, x, gt, g_post)


def _router_kernel(lg_ref, ri_ref, rw_ref, cnt_ref, carry_scr, *, n_experts):
    @pl.when(pl.program_id(0) == 0)
    def _():
        carry_scr[...] = jnp.zeros(carry_scr.shape, F32)

    tm = lg_ref.shape[0]
    lane = lax.broadcasted_iota(jnp.int32, (tm, LANES), 1)
    neg = jnp.float32(-jnp.inf)
    logit = jnp.where(lane < n_experts, lg_ref[...], neg)
    m1 = jnp.max(logit, axis=-1, keepdims=True)
    i1 = jnp.min(jnp.where(logit == m1, lane, LANES), axis=-1, keepdims=True)
    rest = jnp.where(lane == i1, neg, logit)
    m2 = jnp.max(rest, axis=-1, keepdims=True)
    i2 = jnp.min(jnp.where(rest == m2, lane, LANES), axis=-1, keepdims=True)
    e2 = jnp.exp(m2 - m1)
    w1 = 1.0 / (1.0 + e2)
    w2 = e2 / (1.0 + e2)
    hit1 = lane == i1
    hit2 = lane == i2
    onehot = jnp.where(hit1 | hit2, 1.0, 0.0)
    r = lax.broadcasted_iota(jnp.int32, (tm, tm), 0)
    c = lax.broadcasted_iota(jnp.int32, (tm, tm), 1)
    earlier = jnp.where(c < r, 1.0, 0.0).astype(BF16)
    carry = carry_scr[0:1, :]
    before = _dot(earlier, onehot.astype(BF16)) + carry
    pos1 = jnp.sum(jnp.where(hit1, before, 0.0), axis=-1, keepdims=True).astype(jnp.int32)
    pos2 = jnp.sum(jnp.where(hit2, before, 0.0), axis=-1, keepdims=True).astype(jnp.int32)
    total = carry + jnp.sum(onehot, axis=0, keepdims=True)
    carry_scr[...] = jnp.broadcast_to(total, carry_scr.shape)
    cnt_ref[...] = jnp.broadcast_to(total, cnt_ref.shape).astype(jnp.int32)
    zero_i = jnp.zeros((tm, LANES), jnp.int32)
    ri_ref[...] = jnp.where(lane == 0, i1, jnp.where(lane == 1, i2,
                            jnp.where(lane == 2, pos1, jnp.where(lane == 3, pos2, zero_i))))
    rw_ref[...] = jnp.where(lane == 0, w1, jnp.where(lane == 1, w2, 0.0))


def router(logits, n_experts):
    t = logits.shape[0]
    tm = _tile(t, 256)
    return pl.pallas_call(
        functools.partial(_router_kernel, n_experts=n_experts),
        grid=(t // tm,),
        in_specs=[pl.BlockSpec((tm, LANES), lambda i: (i, 0))],
        out_specs=[pl.BlockSpec((tm, LANES), lambda i: (i, 0)),
                   pl.BlockSpec((tm, LANES), lambda i: (i, 0)),
                   pl.BlockSpec((SUBLANES, LANES), lambda i: (0, 0))],
        out_shape=[jax.ShapeDtypeStruct((t, LANES), jnp.int32),
                   jax.ShapeDtypeStruct((t, LANES), F32),
                   jax.ShapeDtypeStruct((SUBLANES, LANES), jnp.int32)],
        scratch_shapes=[pltpu.VMEM((SUBLANES, LANES), F32)],
        compiler_params=_params(("arbitrary",), 32),
        name="router_top2",
    )(logits)


def _moe_ffn_kernel(te_ref, nu_ref, tok_ref, h_hbm, wg_ref, wu_ref, wd_ref, o_ref,
                    acc_ref, h_scr, gbuf, sems):
    r = pl.program_id(0)
    k = pl.program_id(1)
    n_used = nu_ref[0]
    tm = acc_ref.shape[0]

    def row_copy(src_row, slot, i):
        return pltpu.make_async_copy(h_hbm.at[pl.ds(src_row, 1)],
                                     gbuf.at[slot, pl.ds(i, 1)], sems.at[slot])

    def start_tile(tile, slot):
        def body(i, carry):
            row_copy(tok_ref[tile * tm + i], slot, i).start()
            return carry
        lax.fori_loop(0, tm, body, 0, unroll=8)

    def wait_tile(slot):
        def body(i, carry):
            row_copy(0, slot, i).wait()
            return carry
        lax.fori_loop(0, tm, body, 0, unroll=8)

    @pl.when(r < n_used)
    def _():
        @pl.when(k == 0)
        def _():
            slot = r % 2

            @pl.when(r == 0)
            def _():
                start_tile(0, 0)

            wait_tile(slot)

            @pl.when(r + 1 < n_used)
            def _():
                start_tile(r + 1, 1 - slot)

            acc_ref[...] = jnp.zeros(acc_ref.shape, F32)
            h_scr[...] = gbuf[slot].astype(BF16)

        h = h_scr[...]
        gate = _dot(h, wg_ref[...])
        act = (gate * jax.nn.sigmoid(gate)) * _dot(h, wu_ref[...])
        acc_ref[...] += _dot(act.astype(BF16), wd_ref[...])

        @pl.when(k == pl.num_programs(1) - 1)
        def _():
            o_ref[...] = acc_ref[...]

    @pl.when((r >= n_used) & (k == 0))
    def _():
        o_ref[...] = jnp.zeros(o_ref.shape, o_ref.dtype)


def moe_ffn(tile_expert, n_used, row_tok, h, wg16, wu16, wd16, tm):
    n_rows = row_tok.shape[0]
    d = h.shape[1]
    d_ff = wg16.shape[2]
    tk = _tile(d_ff, 512)
    nk = d_ff // tk

    def ff_idx(r, k, nu):
        return jnp.where(r < nu[0], k, nk - 1)

    return pl.pallas_call(
        _moe_ffn_kernel,
        grid_spec=pltpu.PrefetchScalarGridSpec(
            num_scalar_prefetch=3,
            grid=(n_rows // tm, nk),
            in_specs=[
                pl.BlockSpec(memory_space=pl.ANY),
                pl.BlockSpec((None, d, tk),
                             lambda r, k, te, nu, tok: (te[r], 0, ff_idx(r, k, nu))),
                pl.BlockSpec((None, d, tk),
                             lambda r, k, te, nu, tok: (te[r], 0, ff_idx(r, k, nu))),
                pl.BlockSpec((None, tk, d),
                             lambda r, k, te, nu, tok: (te[r], ff_idx(r, k, nu), 0)),
            ],
            out_specs=pl.BlockSpec((tm, d), lambda r, k, te, nu, tok: (r, 0)),
            scratch_shapes=[pltpu.VMEM((tm, d), F32), pltpu.VMEM((tm, d), BF16),
                            pltpu.VMEM((2, tm, d), F32), pltpu.SemaphoreType.DMA((2,))],
        ),
        out_shape=jax.ShapeDtypeStruct((n_rows, d), F32),
        compiler_params=_params(("arbitrary", "arbitrary"), 48),
        name="moe_ffn",
    )(tile_expert, n_used, row_tok, h, wg16, wu16, wd16)


def _moe_combine_kernel(d0_ref, d1_ref, ys_ref, rw_ref, x_ref, gt_ref, gpost_ref, o_ref,
                        buf, sem):
    tm = x_ref.shape[0]
    base = pl.program_id(0) * tm

    def row_copy(src_row, slot, t):
        return pltpu.make_async_copy(ys_ref.at[pl.ds(src_row, 1)],
                                     buf.at[slot, pl.ds(t, 1)], sem)

    def issue(t, carry):
        row_copy(d0_ref[base + t], 0, t).start()
        row_copy(d1_ref[base + t], 1, t).start()
        return carry

    lax.fori_loop(0, tm, issue, 0)

    def drain(t, carry):
        row_copy(0, 0, t).wait()
        row_copy(0, 1, t).wait()
        return carry

    lax.fori_loop(0, tm, drain, 0)
    y = buf[0] * rw_ref[:, 0:1] + buf[1] * rw_ref[:, 1:2]
    o_ref[...] = x_ref[...] + gt_ref[...] * (_rms(y) * gpost_ref[...])


def moe_combine(dest0, dest1, ys, route_w, x, gt, g_post, *, seq_len):
    t, d = x.shape
    tm = _tile(seq_len, 256)
    tiles_per_seq = seq_len // tm
    return pl.pallas_call(
        _moe_combine_kernel,
        grid_spec=pltpu.PrefetchScalarGridSpec(
            num_scalar_prefetch=2,
            grid=(t // tm,),
            in_specs=[
                pl.BlockSpec(memory_space=pl.ANY),
                pl.BlockSpec((tm, LANES), lambda i, d0, d1: (i, 0)),
                pl.BlockSpec((tm, d), lambda i, d0, d1: (i, 0)),
                pl.BlockSpec((None, 1, d), lambda i, d0, d1: (i // tiles_per_seq, 0, 0)),
                pl.BlockSpec((1, d), lambda i, d0, d1: (0, 0)),
            ],
            out_specs=pl.BlockSpec((tm, d), lambda i, d0, d1: (i, 0)),
            scratch_shapes=[pltpu.VMEM((2, tm, d), F32), pltpu.SemaphoreType.DMA(())],
        ),
        out_shape=jax.ShapeDtypeStruct((t, d), F32),
        compiler_params=_params(("arbitrary",), 32),
        name="moe_combine",
    )(dest0, dest1, ys, route_w, x, gt, g_post)


MOE_TILE_ROWS = 512


def moe_layer(h, logits, x, gt, g_post, wg16, wu16, wd16, *, seq_len):
    t, d = h.shape
    n_experts = wg16.shape[0]
    tm = min(MOE_TILE_ROWS, t)
    route_i, route_w, counts = router(logits, n_experts)
    e0, e1, pos0, pos1 = (route_i[:, k] for k in range(4))
    counts = counts[0, :n_experts]
    padded = ((counts + tm - 1) // tm) * tm
    pad_end = jnp.cumsum(padded)
    pad_start = pad_end - padded
    dest0 = pad_start[e0] + pos0
    dest1 = pad_start[e1] + pos1
    n_tiles = (2 * t) // tm + n_experts
    n_rows = n_tiles * tm
    tok = jnp.arange(t, dtype=jnp.int32)
    row_tok = jnp.zeros((n_rows,), jnp.int32).at[dest0].set(tok).at[dest1].set(tok)
    n_used = (pad_end[-1] // tm).astype(jnp.int32)
    tile_e = jnp.clip(jnp.searchsorted(pad_end, jnp.arange(n_tiles, dtype=jnp.int32) * tm,
                                       side='right'), 0, n_experts - 1).astype(jnp.int32)
    tile_e = jnp.where(jnp.arange(n_tiles) < n_used, tile_e, tile_e[n_used - 1])
    ys = moe_ffn(tile_e, n_used.reshape(1), row_tok, h, wg16, wu16, wd16, tm)
    return moe_combine(dest0, dest1, ys, route_w, x, gt, g_post, seq_len=seq_len)


def kernel(x, c, w_ada, b_ada, g_mix_pre, g_mix_post, g_ffn_pre, g_ffn_post, w_in, conv_w,
           ssm_a_re, ssm_a_im, ssm_log_dt, ssm_b_re, ssm_b_im, ssm_c_re, ssm_c_im, ssm_d,
           w_glu, b_glu, w_out, ffn_w_gate, ffn_w_up, ffn_w_down, router_w, router_b,
           moe_w_gate, moe_w_up, moe_w_down):
    nb, seq_len, d = x.shape
    depth = w_ada.shape[0]
    d_conv = conv_w.shape[1]
    g, p, h_dim = ssm_b_re.shape[1:]
    d_ssm = g * h_dim
    n_experts = router_w.shape[2]
    t = nb * seq_len
    groups_per_slab = max(1, min(g, MXU_DIM // h_dim))

    c_pad = jnp.zeros((SUBLANES, d), F32).at[:nb].set(c)
    mod = adaln_mod(c_pad, w_ada, b_ada)
    xf = x.reshape(t, d)

    for l in range(depth):
        sh_m, sc_m, gt_m, sh_f, sc_f, gt_f = (
            mod[l, :nb, k * d:(k + 1) * d].reshape(nb, 1, d) for k in range(6))
        y_conv, u = mixer_in(xf, sc_m, sh_m, g_mix_pre[l].reshape(1, d),
                             w_in[l].astype(BF16), conv_w[l].T,
                             seq_len=seq_len, d_conv=d_conv)
        bb_re, bb_im, ab_re, ab_im = ssm_prep(ssm_a_re[l], ssm_a_im[l], ssm_log_dt[l],
                                              ssm_b_re[l], ssm_b_im[l])
        bbd = jnp.concatenate(
            [_block_diag_slabs(jnp.swapaxes(bb_re, 1, 2), groups_per_slab),
             _block_diag_slabs(jnp.swapaxes(bb_im, 1, 2), groups_per_slab)],
            axis=-1).astype(BF16)
        cre = _block_diag_slabs(jnp.swapaxes(ssm_c_re[l], 1, 2), groups_per_slab).astype(BF16)
        cim = _block_diag_slabs(jnp.swapaxes(ssm_c_im[l], 1, 2), groups_per_slab).astype(BF16)
        y_ssm = ssm(u.reshape(nb, seq_len, d_ssm), bbd, cre, cim,
                    ab_re.reshape(1, g * p), ab_im.reshape(1, g * p),
                    ssm_d[l].reshape(1, d_ssm), w_glu[l].astype(BF16),
                    b_glu[l].reshape(1, d_ssm)).reshape(t, d_ssm)
        routed = l % 2 == 1
        i = l // 2
        rt = None
        if routed:
            rw_pad = jnp.zeros((d, LANES), F32).at[:, :n_experts].set(router_w[i])
            rb_pad = jnp.zeros((1, LANES), F32).at[0, :n_experts].set(router_b[i])
            rt = (rw_pad, rb_pad)
        outs = mixer_out(y_conv, y_ssm, w_out[l].astype(BF16), xf, gt_m,
                         g_mix_post[l].reshape(1, d), g_ffn_pre[l].reshape(1, d),
                         sc_f, sh_f, rt, seq_len=seq_len)
        g_post = g_ffn_post[l].reshape(1, d)
        if routed:
            xf, h, logits = outs
            xf = moe_layer(h, logits, xf, gt_f, g_post, moe_w_gate[i].astype(BF16),
                           moe_w_up[i].astype(BF16), moe_w_down[i].astype(BF16),
                           seq_len=seq_len)
        else:
            xf, h = outs
            xf = ffn_dense(h, ffn_w_gate[i].astype(BF16), ffn_w_up[i].astype(BF16),
                           ffn_w_down[i].astype(BF16), xf, gt_f, g_post, seq_len=seq_len)
    return xf.reshape(nb, seq_len, d)
```

```python
import functools

import jax
import jax.numpy as jnp
from jax import lax
from jax.experimental import pallas as pl
from jax.experimental.pallas import tpu as pltpu

F32 = jnp.float32
BF16 = jnp.bfloat16
NORM_EPS = 1e-6
LANES = 128
SUBLANES = 8
MXU_DIM = 256
MIB = 1024 * 1024


def _params(semantics, vmem_mib):
    return pltpu.CompilerParams(dimension_semantics=semantics,
                                vmem_limit_bytes=vmem_mib * MIB)


def _dot(a, b):
    return jnp.dot(a, b, preferred_element_type=F32)


def _rms(x):
    return x * lax.rsqrt(jnp.mean(x * x, axis=-1, keepdims=True) + NORM_EPS)


def _tile(n, want):
    t = min(n, want)
    while n % t:
        t -= 1
    return t


def _adaln_kernel(c_ref, w_ref, b_ref, o_ref):
    c = c_ref[...]
    c_act = c * jax.nn.sigmoid(c)
    o_ref[...] = jnp.dot(c_act, w_ref[...], preferred_element_type=F32,
                         precision=lax.Precision.HIGHEST) + b_ref[...]


def adaln_mod(c_pad, w_ada, b_ada):
    depth, d, n = w_ada.shape
    rows = c_pad.shape[0]
    tn = _tile(n, 1024)
    return pl.pallas_call(
        _adaln_kernel,
        grid=(depth, n // tn),
        in_specs=[
            pl.BlockSpec((rows, d), lambda l, j: (0, 0)),
            pl.BlockSpec((None, d, tn), lambda l, j: (l, 0, j)),
            pl.BlockSpec((None, 1, tn), lambda l, j: (l, 0, j)),
        ],
        out_specs=pl.BlockSpec((None, rows, tn), lambda l, j: (l, 0, j)),
        out_shape=jax.ShapeDtypeStruct((depth, rows, n), F32),
        compiler_params=_params(("arbitrary", "arbitrary"), 32),
        name="adaln_mod",
    )(c_pad, w_ada, b_ada.reshape(depth, 1, n))


def _mixer_in_kernel(x_ref, sc_ref, sh_ref, g_ref, w_ref, cw_ref, yconv_ref, u_ref,
                     halo_scr, *, tiles_per_seq, col_chunk):
    tm = x_ref.shape[0]
    d_conv = yconv_ref.shape[1]
    d_ssm = u_ref.shape[1]
    h = (_rms(x_ref[...]) * g_ref[...] * (1.0 + sc_ref[...]) + sh_ref[...]).astype(BF16)

    @pl.when(pl.program_id(0) % tiles_per_seq == 0)
    def _():
        halo_scr[...] = jnp.zeros(halo_scr.shape, F32)

    for c0 in range(0, d_conv, col_chunk):
        cols = slice(c0, c0 + col_chunk)
        gate_b = _dot(h, w_ref[:, c0:c0 + col_chunk])
        z = (_dot(h, w_ref[:, d_conv + c0:d_conv + c0 + col_chunk])
             * _dot(h, w_ref[:, 2 * d_conv + c0:2 * d_conv + c0 + col_chunk]))
        prev = halo_scr[:, cols]
        halo_scr[:, cols] = z[tm - SUBLANES:, :]
        w0 = cw_ref[0:1, cols]
        w1 = cw_ref[1:2, cols]
        w2 = cw_ref[2:3, cols]
        z1 = pltpu.roll(z, 1, 0)
        z2 = pltpu.roll(z, 2, 0)
        yconv_ref[:, cols] = (gate_b * (z * w2 + z1 * w1 + z2 * w0)).astype(yconv_ref.dtype)
        row = lax.broadcasted_iota(jnp.int32, (SUBLANES, col_chunk), 0)
        z1t = jnp.where(row < 1, pltpu.roll(prev, 1, 0), z1[:SUBLANES, :])
        z2t = jnp.where(row < 2, pltpu.roll(prev, 2, 0), z2[:SUBLANES, :])
        top = gate_b[:SUBLANES, :] * (z[:SUBLANES, :] * w2 + z1t * w1 + z2t * w0)
        yconv_ref[0:SUBLANES, cols] = top.astype(yconv_ref.dtype)

    for c0 in range(0, d_ssm, col_chunk):
        u_ref[:, c0:c0 + col_chunk] = _dot(
            h, w_ref[:, 3 * d_conv + c0:3 * d_conv + c0 + col_chunk])


def mixer_in(x, sc, sh, g_pre, w_in16, conv_wt, *, seq_len, d_conv):
    t, d = x.shape
    d_in = w_in16.shape[1]
    d_ssm = d_in - 3 * d_conv
    tm = _tile(seq_len, 512)
    col_chunk = _tile(min(d_conv, d_ssm), 512)
    assert d_conv % col_chunk == 0 and d_ssm % col_chunk == 0
    tiles_per_seq = seq_len // tm

    def mod_spec():
        return pl.BlockSpec((None, 1, d), lambda i: (i // tiles_per_seq, 0, 0))

    kern = functools.partial(_mixer_in_kernel, tiles_per_seq=tiles_per_seq,
                             col_chunk=col_chunk)
    return pl.pallas_call(
        kern,
        grid=(t // tm,),
        in_specs=[
            pl.BlockSpec((tm, d), lambda i: (i, 0)),
            mod_spec(), mod_spec(),
            pl.BlockSpec((1, d), lambda i: (0, 0)),
            pl.BlockSpec((d, d_in), lambda i: (0, 0), pipeline_mode=pl.Buffered(1)),
            pl.BlockSpec((3, d_conv), lambda i: (0, 0)),
        ],
        out_specs=[
            pl.BlockSpec((tm, d_conv), lambda i: (i, 0)),
            pl.BlockSpec((tm, d_ssm), lambda i: (i, 0)),
        ],
        out_shape=[jax.ShapeDtypeStruct((t, d_conv), BF16),
                   jax.ShapeDtypeStruct((t, d_ssm), F32)],
        scratch_shapes=[pltpu.VMEM((SUBLANES, d_conv), F32)],
        compiler_params=_params(("arbitrary",), 48),
        name="mixer_in",
    )(x, sc, sh, g_pre, w_in16, conv_wt)


def _ssm_prep_kernel(are_ref, aim_ref, ldt_ref, bre_ref, bim_ref,
                     bbre_ref, bbim_ref, abre_ref, abim_ref):
    lam_re = are_ref[...]
    lam_im = aim_ref[...]
    dt = jnp.exp(ldt_ref[...])
    mag = jnp.exp(lam_re * dt)
    ang = lam_im * dt
    ab_re = mag * jnp.cos(ang)
    ab_im = mag * jnp.sin(ang)
    den = lam_re * lam_re + lam_im * lam_im
    nr = ab_re - 1.0
    q_re = (nr * lam_re + ab_im * lam_im) / den
    q_im = (ab_im * lam_re - nr * lam_im) / den
    b_re = bre_ref[...]
    b_im = bim_ref[...]
    bbre_ref[...] = q_re * b_re - q_im * b_im
    bbim_ref[...] = q_re * b_im + q_im * b_re
    abre_ref[...] = ab_re
    abim_ref[...] = ab_im


def ssm_prep(a_re, a_im, log_dt, b_re, b_im):
    g, p, h = b_re.shape
    ldt = jnp.broadcast_to(log_dt.reshape(g, 1, 1), (g, p, 1))
    return pl.pallas_call(
        _ssm_prep_kernel,
        out_shape=[jax.ShapeDtypeStruct((g, p, h), F32),
                   jax.ShapeDtypeStruct((g, p, h), F32),
                   jax.ShapeDtypeStruct((g, p, 1), F32),
                   jax.ShapeDtypeStruct((g, p, 1), F32)],
        compiler_params=pltpu.CompilerParams(vmem_limit_bytes=40 * MIB),
        name="ssm_prep",
    )(a_re.reshape(g, p, 1), a_im.reshape(g, p, 1), ldt, b_re, b_im)


def _block_diag_slabs(w, groups_per_slab):
    g, r, c = w.shape
    ns = g // groups_per_slab
    w = w.reshape(ns, groups_per_slab, r, c)
    eye = jnp.eye(groups_per_slab, dtype=w.dtype)
    out = w[:, :, :, None, :] * eye[None, :, None, :, None]
    return out.reshape(ns, groups_per_slab * r, groups_per_slab * c)


def _gelu_tanh(x):
    return 0.5 * x * (1.0 + jnp.tanh(0.7978845608028654 * (x + 0.044715 * x * x * x)))


def _ssm_kernel(u_ref, bbd_ref, cre_ref, cim_ref, are_ref, aim_ref, d_ref, wglu_ref,
                bglu_ref, o_ref, sre_scr, sim_scr, stre_scr, stim_scr, y_scr, *,
                pitch, groups_per_step):
    nb, tc, d_ssm = u_ref.shape
    n_slabs, k_slab, two_sw = bbd_ref.shape
    sw = two_sw // 2
    tiles_per_slab = sw // LANES
    pack = SUBLANES // nb
    n_groups = sre_scr.shape[0]

    def slab_rows(lane_tile, b):
        q, h = divmod(lane_tile, pack)
        r0 = (h * nb + b) * pitch
        return q, slice(r0, r0 + tc)

    @pl.when(pl.program_id(0) == 0)
    def _():
        stre_scr[...] = jnp.zeros(stre_scr.shape, F32)
        stim_scr[...] = jnp.zeros(stim_scr.shape, F32)

    for b in range(nb):
        for s in range(n_slabs):
            ub = u_ref[b, :, s * k_slab:(s + 1) * k_slab].astype(BF16)
            r = _dot(ub, bbd_ref[s])
            for j in range(tiles_per_slab):
                q, rows = slab_rows(s * tiles_per_slab + j, b)
                sre_scr[q, rows, :] = r[:, j * LANES:(j + 1) * LANES]
                sim_scr[q, rows, :] = r[:, sw + j * LANES:sw + (j + 1) * LANES]

    for q0 in range(0, n_groups, groups_per_step):
        qs = list(range(q0, min(q0 + groups_per_step, n_groups)))
        a_re = [are_ref[q] for q in qs]
        a_im = [aim_ref[q] for q in qs]

        def step(t, carry, qs=qs, a_re=a_re, a_im=a_im):
            rows = pl.ds(t, SUBLANES, stride=pitch)
            out = []
            for n, q in enumerate(qs):
                s_re, s_im = carry[2 * n], carry[2 * n + 1]
                n_re = a_re[n] * s_re - a_im[n] * s_im + sre_scr[q, rows, :]
                n_im = a_re[n] * s_im + a_im[n] * s_re + sim_scr[q, rows, :]
                sre_scr[q, rows, :] = n_re
                sim_scr[q, rows, :] = n_im
                out += [n_re, n_im]
            return tuple(out)

        init = []
        for q in qs:
            init += [stre_scr[q], stim_scr[q]]
        fin = lax.fori_loop(0, tc, step, tuple(init), unroll=2)
        for n, q in enumerate(qs):
            stre_scr[q] = fin[2 * n]
            stim_scr[q] = fin[2 * n + 1]

    for b in range(nb):
        for s in range(n_slabs):
            cols = slice(s * k_slab, (s + 1) * k_slab)
            pieces_re, pieces_im = [], []
            for j in range(tiles_per_slab):
                q, rows = slab_rows(s * tiles_per_slab + j, b)
                pieces_re.append(sre_scr[q, rows, :].astype(BF16))
                pieces_im.append(sim_scr[q, rows, :].astype(BF16))
            y = (_dot(jnp.concatenate(pieces_re, axis=-1), cre_ref[s])
                 - _dot(jnp.concatenate(pieces_im, axis=-1), cim_ref[s]))
            y_scr[:, cols] = y + d_ref[:, cols] * u_ref[b, :, cols]
        y = _gelu_tanh(y_scr[...])
        gate = _dot(y.astype(BF16), wglu_ref[...]) + bglu_ref[...]
        o_ref[b] = (y * jax.nn.sigmoid(gate)).astype(o_ref.dtype)


def _pack_state_rows(a, nb):
    pack = SUBLANES // nb
    n_groups = a.shape[1] // (LANES * pack)
    a = a.reshape(n_groups, pack, 1, LANES)
    return jnp.broadcast_to(a, (n_groups, pack, nb, LANES)).reshape(n_groups, SUBLANES, LANES)


def ssm(u3, bbd, cre, cim, a_re, a_im, d_skip, w_glu16, b_glu):
    nb, seq_len, d_ssm = u3.shape
    n_state = a_re.shape[1]
    assert SUBLANES % nb == 0 and n_state % (LANES * (SUBLANES // nb)) == 0
    tc = _tile(seq_len, 128)
    pitch = tc + SUBLANES
    n_groups = n_state // (LANES * (SUBLANES // nb))
    a_re = _pack_state_rows(a_re, nb)
    a_im = _pack_state_rows(a_im, nb)
    kern = functools.partial(_ssm_kernel, pitch=pitch, groups_per_step=min(4, n_groups))

    def full(a):
        nd = a.ndim
        return pl.BlockSpec(a.shape, lambda c, nd=nd: (0,) * nd)

    return pl.pallas_call(
        kern,
        grid=(seq_len // tc,),
        in_specs=[pl.BlockSpec((nb, tc, d_ssm), lambda c: (0, c, 0)),
                  full(bbd), full(cre), full(cim), full(a_re), full(a_im),
                  full(d_skip), full(w_glu16), full(b_glu)],
        out_specs=pl.BlockSpec((nb, tc, d_ssm), lambda c: (0, c, 0)),
        out_shape=jax.ShapeDtypeStruct((nb, seq_len, d_ssm), BF16),
        scratch_shapes=[pltpu.VMEM((n_groups, SUBLANES * pitch, LANES), F32),
                        pltpu.VMEM((n_groups, SUBLANES * pitch, LANES), F32),
                        pltpu.VMEM((n_groups, SUBLANES, LANES), F32),
                        pltpu.VMEM((n_groups, SUBLANES, LANES), F32),
                        pltpu.VMEM((tc, d_ssm), F32)],
        compiler_params=_params(("arbitrary",), 56),
        name="ssm_scan",
    )(u3, bbd, cre, cim, a_re, a_im, d_skip, w_glu16, b_glu)


def _mixer_out_kernel(*refs, routed):
    if routed:
        (yc_ref, ys_ref, wo_ref, x_ref, gt_ref, gpost_ref, gpre_ref, sc_ref,
         sh_ref, rw_ref, rb_ref, xo_ref, h_ref, lg_ref) = refs
    else:
        (yc_ref, ys_ref, wo_ref, x_ref, gt_ref, gpost_ref, gpre_ref, sc_ref,
         sh_ref, xo_ref, h_ref) = refs
    d_conv = yc_ref.shape[1]
    y = _dot(yc_ref[...], wo_ref[:d_conv, :]) + _dot(ys_ref[...], wo_ref[d_conv:, :])
    x_new = x_ref[...] + gt_ref[...] * (_rms(y) * gpost_ref[...])
    xo_ref[...] = x_new
    h = _rms(x_new) * gpre_ref[...] * (1.0 + sc_ref[...]) + sh_ref[...]
    h_ref[...] = h.astype(h_ref.dtype)
    if routed:
        w = rw_ref[...]
        h_hi = h.astype(BF16)
        w_hi = w.astype(BF16)
        h_lo = (h - h_hi.astype(F32)).astype(BF16)
        w_lo = (w - w_hi.astype(F32)).astype(BF16)
        lg_ref[...] = (_dot(h_hi, w_hi) + (_dot(h_hi, w_lo) + _dot(h_lo, w_hi))
                       + rb_ref[...])


def mixer_out(y_conv, y_ssm, w_out16, x, gt, g_post, g_pre, sc, sh, router=None, *,
              seq_len):
    t, d = x.shape
    d_conv = y_conv.shape[1]
    d_ssm = y_ssm.shape[1]
    tm = _tile(seq_len, 512)
    tiles_per_seq = seq_len // tm
    routed = router is not None

    def mod_spec():
        return pl.BlockSpec((None, 1, d), lambda i: (i // tiles_per_seq, 0, 0))

    def row_spec(n):
        return pl.BlockSpec((tm, n), lambda i: (i, 0))

    def const_spec(shape):
        return pl.BlockSpec(shape, lambda i: (0, 0))

    in_specs = [row_spec(d_conv), row_spec(d_ssm),
                pl.BlockSpec((d_conv + d_ssm, d), lambda i: (0, 0),
                             pipeline_mode=pl.Buffered(1)),
                row_spec(d), mod_spec(), const_spec((1, d)), const_spec((1, d)),
                mod_spec(), mod_spec()]
    args = [y_conv, y_ssm, w_out16, x, gt, g_post, g_pre, sc, sh]
    out_specs = [row_spec(d), row_spec(d)]
    out_shape = [jax.ShapeDtypeStruct((t, d), F32),
                 jax.ShapeDtypeStruct((t, d), F32 if routed else BF16)]
    if routed:
        rw_pad, rb_pad = router
        in_specs += [const_spec(rw_pad.shape), const_spec(rb_pad.shape)]
        args += [rw_pad, rb_pad]
        out_specs.append(row_spec(LANES))
        out_shape.append(jax.ShapeDtypeStruct((t, LANES), F32))
    return pl.pallas_call(
        functools.partial(_mixer_out_kernel, routed=routed),
        grid=(t // tm,),
        in_specs=in_specs, out_specs=out_specs, out_shape=out_shape,
        compiler_params=_params(("arbitrary",), 56),
        name="mixer_out_routed" if routed else "mixer_out",
    )(*args)


def _ffn_kernel(h_ref, wg_ref, wu_ref, wd_ref, x_ref, gt_ref, gpost_ref, o_ref, acc_ref):
    k = pl.program_id(1)

    @pl.when(k == 0)
    def _():
        acc_ref[...] = jnp.zeros(acc_ref.shape, F32)

    h = h_ref[...]
    gate = _dot(h, wg_ref[...])
    act = (gate * jax.nn.sigmoid(gate)) * _dot(h, wu_ref[...])
    acc_ref[...] += _dot(act.astype(BF16), wd_ref[...])

    @pl.when(k == pl.num_programs(1) - 1)
    def _():
        o_ref[...] = x_ref[...] + gt_ref[...] * (_rms(acc_ref[...]) * gpost_ref[...])


def ffn_dense(h16, wg16, wu16, wd16, x, gt, g_post, *, seq_len):
    t, d = x.shape
    d_ff = wg16.shape[1]
    tm = _tile(seq_len, 512)
    tk = _tile(d_ff, 512)
    tiles_per_seq = seq_len // tm
    return pl.pallas_call(
        _ffn_kernel,
        grid=(t // tm, d_ff // tk),
        in_specs=[
            pl.BlockSpec((tm, d), lambda i, k: (i, 0)),
            pl.BlockSpec((d, tk), lambda i, k: (0, k)),
            pl.BlockSpec((d, tk), lambda i, k: (0, k)),
            pl.BlockSpec((tk, d), lambda i, k: (k, 0)),
            pl.BlockSpec((tm, d), lambda i, k: (i, 0)),
            pl.BlockSpec((None, 1, d), lambda i, k: (i // tiles_per_seq, 0, 0)),
            pl.BlockSpec((1, d), lambda i, k: (0, 0)),
        ],
        out_specs=pl.BlockSpec((tm, d), lambda i, k: (i, 0)),
        out_shape=jax.ShapeDtypeStruct((t, d), F32),
        scratch_shapes=[pltpu.VMEM((tm, d), F32)],
        compiler_params=_params(("arbitrary", "arbitrary"), 48),
        name="ffn_dense",
    )(h16, wg16, wu16, wd16, x, gt, g_post)


def _router_kernel(lg_ref, ri_ref, rw_ref, cnt_ref, carry_scr, *, n_experts):
    @pl.when(pl.program_id(0) == 0)
    def _():
        carry_scr[...] = jnp.zeros(carry_scr.shape, F32)

    tm = lg_ref.shape[0]
    lane = lax.broadcasted_iota(jnp.int32, (tm, LANES), 1)
    neg = jnp.float32(-jnp.inf)
    logit = jnp.where(lane < n_experts, lg_ref[...], neg)
    m1 = jnp.max(logit, axis=-1, keepdims=True)
    i1 = jnp.min(jnp.where(logit == m1, lane, LANES), axis=-1, keepdims=True)
    rest = jnp.where(lane == i1, neg, logit)
    m2 = jnp.max(rest, axis=-1, keepdims=True)
    i2 = jnp.min(jnp.where(rest == m2, lane, LANES), axis=-1, keepdims=True)
    e2 = jnp.exp(m2 - m1)
    w1 = 1.0 / (1.0 + e2)
    w2 = e2 / (1.0 + e2)
    hit1 = lane == i1
    hit2 = lane == i2
    onehot = jnp.where(hit1 | hit2, 1.0, 0.0)
    r = lax.broadcasted_iota(jnp.int32, (tm, tm), 0)
    c = lax.broadcasted_iota(jnp.int32, (tm, tm), 1)
    earlier = jnp.where(c < r, 1.0, 0.0).astype(BF16)
    carry = carry_scr[0:1, :]
    before = _dot(earlier, onehot.astype(BF16)) + carry
    pos1 = jnp.sum(jnp.where(hit1, before, 0.0), axis=-1, keepdims=True).astype(jnp.int32)
    pos2 = jnp.sum(jnp.where(hit2, before, 0.0), axis=-1, keepdims=True).astype(jnp.int32)
    total = carry + jnp.sum(onehot, axis=0, keepdims=True)
    carry_scr[...] = jnp.broadcast_to(total, carry_scr.shape)
    cnt_ref[...] = jnp.broadcast_to(total, cnt_ref.shape).astype(jnp.int32)
    zero_i = jnp.zeros((tm, LANES), jnp.int32)
    ri_ref[...] = jnp.where(lane == 0, i1, jnp.where(lane == 1, i2,
                            jnp.where(lane == 2, pos1, jnp.where(lane == 3, pos2, zero_i))))
    rw_ref[...] = jnp.where(lane == 0, w1, jnp.where(lane == 1, w2, 0.0))


def router(logits, n_experts):
    t = logits.shape[0]
    tm = _tile(t, 256)
    return pl.pallas_call(
        functools.partial(_router_kernel, n_experts=n_experts),
        grid=(t // tm,),
        in_specs=[pl.BlockSpec((tm, LANES), lambda i: (i, 0))],
        out_specs=[pl.BlockSpec((tm, LANES), lambda i: (i, 0)),
                   pl.BlockSpec((tm, LANES), lambda i: (i, 0)),
                   pl.BlockSpec((SUBLANES, LANES), lambda i: (0, 0))],
        out_shape=[jax.ShapeDtypeStruct((t, LANES), jnp.int32),
                   jax.ShapeDtypeStruct((t, LANES), F32),
                   jax.ShapeDtypeStruct((SUBLANES, LANES), jnp.int32)],
        scratch_shapes=[pltpu.VMEM((SUBLANES, LANES), F32)],
        compiler_params=_params(("arbitrary",), 32),
        name="router_top2",
    )(logits)


def _moe_ffn_kernel(te_ref, nu_ref, tok_ref, h_hbm, wg_ref, wu_ref, wd_ref, o_ref,
                    h_scr, gbuf, sem):
    r = pl.program_id(0)
    k = pl.program_id(1)
    n_used = nu_ref[0]
    tm = o_ref.shape[0]

    def row_copy(src_row, i):
        return pltpu.make_async_copy(h_hbm.at[pl.ds(src_row, 1)], gbuf.at[pl.ds(i, 1)], sem)

    def start_tile(tile):
        def body(i, carry):
            row_copy(tok_ref[tile * tm + i], i).start()
            return carry
        lax.fori_loop(0, tm, body, 0, unroll=8)

    def wait_tile():
        def body(i, carry):
            row_copy(0, i).wait()
            return carry
        lax.fori_loop(0, tm, body, 0, unroll=8)

    @pl.when(k == 0)
    def _():
        o_ref[...] = jnp.zeros(o_ref.shape, o_ref.dtype)

    @pl.when(r < n_used)
    def _():
        @pl.when(k == 0)
        def _():
            @pl.when(r == 0)
            def _():
                start_tile(0)

            wait_tile()
            h_scr[...] = gbuf[...].astype(BF16)

            @pl.when(r + 1 < n_used)
            def _():
                start_tile(r + 1)

        h = h_scr[...]
        gate = _dot(h, wg_ref[...].astype(BF16))
        act = (gate * jax.nn.sigmoid(gate)) * _dot(h, wu_ref[...].astype(BF16))
        o_ref[...] += _dot(act.astype(BF16), wd_ref[...].astype(BF16))


def moe_ffn(tile_expert, n_used, row_tok, h, w_gate, w_up, w_down, layer, tm):
    n_rows = row_tok.shape[0]
    d = h.shape[1]
    d_ff = w_gate.shape[3]
    tk = _tile(d_ff, 512)
    nk = d_ff // tk

    def ff_idx(r, k, nu):
        return jnp.where(r < nu[0], k, nk - 1)

    return pl.pallas_call(
        _moe_ffn_kernel,
        grid_spec=pltpu.PrefetchScalarGridSpec(
            num_scalar_prefetch=3,
            grid=(n_rows // tm, nk),
            in_specs=[
                pl.BlockSpec(memory_space=pl.ANY),
                pl.BlockSpec((None, None, d, tk),
                             lambda r, k, te, nu, tok: (layer, te[r], 0, ff_idx(r, k, nu))),
                pl.BlockSpec((None, None, d, tk),
                             lambda r, k, te, nu, tok: (layer, te[r], 0, ff_idx(r, k, nu))),
                pl.BlockSpec((None, None, tk, d),
                             lambda r, k, te, nu, tok: (layer, te[r], ff_idx(r, k, nu), 0)),
            ],
            out_specs=pl.BlockSpec((tm, d), lambda r, k, te, nu, tok: (r, 0)),
            scratch_shapes=[pltpu.VMEM((tm, d), BF16), pltpu.VMEM((tm, d), F32),
                            pltpu.SemaphoreType.DMA(())],
        ),
        out_shape=jax.ShapeDtypeStruct((n_rows, d), F32),
        compiler_params=_params(("arbitrary", "arbitrary"), 56),
        name="moe_ffn",
    )(tile_expert, n_used, row_tok, h, w_gate, w_up, w_down)


def _moe_combine_kernel(d0_ref, d1_ref, ys_ref, rw_ref, x_ref, gt_ref, gpost_ref, o_ref,
                        buf, sem):
    tm = x_ref.shape[0]
    base = pl.program_id(0) * tm

    def row_copy(src_row, slot, t):
        return pltpu.make_async_copy(ys_ref.at[pl.ds(src_row, 1)],
                                     buf.at[slot, pl.ds(t, 1)], sem)

    def issue(t, carry):
        row_copy(d0_ref[base + t], 0, t).start()
        row_copy(d1_ref[base + t], 1, t).start()
        return carry

    lax.fori_loop(0, tm, issue, 0)

    def drain(t, carry):
        row_copy(0, 0, t).wait()
        row_copy(0, 1, t).wait()
        return carry

    lax.fori_loop(0, tm, drain, 0)
    y = buf[0] * rw_ref[:, 0:1] + buf[1] * rw_ref[:, 1:2]
    o_ref[...] = x_ref[...] + gt_ref[...] * (_rms(y) * gpost_ref[...])


def moe_combine(dest0, dest1, ys, route_w, x, gt, g_post, *, seq_len):
    t, d = x.shape
    tm = _tile(seq_len, 256)
    tiles_per_seq = seq_len // tm
    return pl.pallas_call(
        _moe_combine_kernel,
        grid_spec=pltpu.PrefetchScalarGridSpec(
            num_scalar_prefetch=2,
            grid=(t // tm,),
            in_specs=[
                pl.BlockSpec(memory_space=pl.ANY),
                pl.BlockSpec((tm, LANES), lambda i, d0, d1: (i, 0)),
                pl.BlockSpec((tm, d), lambda i, d0, d1: (i, 0)),
                pl.BlockSpec((None, 1, d), lambda i, d0, d1: (i // tiles_per_seq, 0, 0)),
                pl.BlockSpec((1, d), lambda i, d0, d1: (0, 0)),
            ],
            out_specs=pl.BlockSpec((tm, d), lambda i, d0, d1: (i, 0)),
            scratch_shapes=[pltpu.VMEM((2, tm, d), F32), pltpu.SemaphoreType.DMA(())],
        ),
        out_shape=jax.ShapeDtypeStruct((t, d), F32),
        compiler_params=_params(("arbitrary",), 32),
        name="moe_combine",
    )(dest0, dest1, ys, route_w, x, gt, g_post)


MOE_TILE_ROWS = 512


def moe_layer(h, logits, x, gt, g_post, w_gate, w_up, w_down, layer, *, seq_len):
    t, d = h.shape
    n_experts = w_gate.shape[1]
    tm = min(MOE_TILE_ROWS, t)
    route_i, route_w, counts = router(logits, n_experts)
    e0, e1, pos0, pos1 = (route_i[:, k] for k in range(4))
    counts = counts[0, :n_experts]
    padded = ((counts + tm - 1) // tm) * tm
    pad_end = jnp.cumsum(padded)
    pad_start = pad_end - padded
    dest0 = pad_start[e0] + pos0
    dest1 = pad_start[e1] + pos1
    n_tiles = (2 * t) // tm + n_experts
    n_rows = n_tiles * tm
    tok = jnp.arange(t, dtype=jnp.int32)
    row_tok = jnp.zeros((n_rows,), jnp.int32).at[dest0].set(tok).at[dest1].set(tok)
    n_used = (pad_end[-1] // tm).astype(jnp.int32)
    tile_e = jnp.clip(jnp.searchsorted(pad_end, jnp.arange(n_tiles, dtype=jnp.int32) * tm,
                                       side='right'), 0, n_experts - 1).astype(jnp.int32)
    tile_e = jnp.where(jnp.arange(n_tiles) < n_used, tile_e, tile_e[n_used - 1])
    ys = moe_ffn(tile_e, n_used.reshape(1), row_tok, h, w_gate, w_up, w_down, layer, tm)
    return moe_combine(dest0, dest1, ys, route_w, x, gt, g_post, seq_len=seq_len)


def kernel(x, c, w_ada, b_ada, g_mix_pre, g_mix_post, g_ffn_pre, g_ffn_post, w_in, conv_w,
           ssm_a_re, ssm_a_im, ssm_log_dt, ssm_b_re, ssm_b_im, ssm_c_re, ssm_c_im, ssm_d,
           w_glu, b_glu, w_out, ffn_w_gate, ffn_w_up, ffn_w_down, router_w, router_b,
           moe_w_gate, moe_w_up, moe_w_down):
    nb, seq_len, d = x.shape
    depth = w_ada.shape[0]
    d_conv = conv_w.shape[1]
    g, p, h_dim = ssm_b_re.shape[1:]
    d_ssm = g * h_dim
    n_experts = router_w.shape[2]
    t = nb * seq_len
    groups_per_slab = max(1, min(g, MXU_DIM // h_dim))

    c_pad = jnp.zeros((SUBLANES, d), F32).at[:nb].set(c)
    mod = adaln_mod(c_pad, w_ada, b_ada)
    xf = x.reshape(t, d)

    for l in range(depth):
        sh_m, sc_m, gt_m, sh_f, sc_f, gt_f = (
            mod[l, :nb, k * d:(k + 1) * d].reshape(nb, 1, d) for k in range(6))
        y_conv, u = mixer_in(xf, sc_m, sh_m, g_mix_pre[l].reshape(1, d),
                             w_in[l].astype(BF16), conv_w[l].T,
                             seq_len=seq_len, d_conv=d_conv)
        bb_re, bb_im, ab_re, ab_im = ssm_prep(ssm_a_re[l], ssm_a_im[l], ssm_log_dt[l],
                                              ssm_b_re[l], ssm_b_im[l])
        bbd = jnp.concatenate(
            [_block_diag_slabs(jnp.swapaxes(bb_re, 1, 2), groups_per_slab),
             _block_diag_slabs(jnp.swapaxes(bb_im, 1, 2), groups_per_slab)],
            axis=-1).astype(BF16)
        cre = _block_diag_slabs(jnp.swapaxes(ssm_c_re[l], 1, 2), groups_per_slab).astype(BF16)
        cim = _block_diag_slabs(jnp.swapaxes(ssm_c_im[l], 1, 2), groups_per_slab).astype(BF16)
        y_ssm = ssm(u.reshape(nb, seq_len, d_ssm), bbd, cre, cim,
                    ab_re.reshape(1, g * p), ab_im.reshape(1, g * p),
                    ssm_d[l].reshape(1, d_ssm), w_glu[l].astype(BF16),
                    b_glu[l].reshape(1, d_ssm)).reshape(t, d_ssm)
        routed = l % 2 == 1
        i = l // 2
        rt = None
        if routed:
            rw_pad = jnp.zeros((d, LANES), F32).at[:, :n_experts].set(router_w[i])
            rb_pad = jnp.zeros((1, LANES), F32).at[0, :n_experts].set(router_b[i])
            rt = (rw_pad, rb_pad)
        outs = mixer_out(y_conv, y_ssm, w_out[l].astype(BF16), xf, gt_m,
                         g_mix_post[l].reshape(1, d), g_ffn_pre[l].reshape(1, d),
                         sc_f, sh_f, rt, seq_len=seq_len)
        g_post = g_ffn_post[l].reshape(1, d)
        if routed:
            xf, h, logits = outs
            xf = moe_layer(h, logits, xf, gt_f, g_post, moe_w_gate, moe_w_up, moe_w_down, i,
                           seq_len=seq_len)
        else:
            xf, h = outs
            xf = ffn_dense(h, ffn_w_gate[i].astype(BF16), ffn_w_up[i].astype(BF16),
                           ffn_w_down[i].astype(BF16), xf, gt_f, g_post, seq_len=seq_len)
    return xf.reshape(nb, seq_len, d)
```

```python
import functools

import jax
import jax.numpy as jnp
from jax import lax
from jax.experimental import pallas as pl
from jax.experimental.pallas import tpu as pltpu

F32 = jnp.float32
BF16 = jnp.bfloat16
NORM_EPS = 1e-6
LANES = 128
SUBLANES = 8
MXU_DIM = 256
MIB = 1024 * 1024


def _params(semantics, vmem_mib):
    return pltpu.CompilerParams(dimension_semantics=semantics,
                                vmem_limit_bytes=vmem_mib * MIB)


def _dot(a, b):
    return jnp.dot(a, b, preferred_element_type=F32)


def _rms(x):
    return x * lax.rsqrt(jnp.mean(x * x, axis=-1, keepdims=True) + NORM_EPS)


def _tile(n, want):
    t = min(n, want)
    while n % t:
        t -= 1
    return t


def _adaln_kernel(c_ref, w_ref, b_ref, o_ref):
    c = c_ref[...]
    c_act = c * jax.nn.sigmoid(c)
    o_ref[...] = jnp.dot(c_act, w_ref[...], preferred_element_type=F32,
                         precision=lax.Precision.HIGHEST) + b_ref[...]


def adaln_mod(c_pad, w_ada, b_ada):
    depth, d, n = w_ada.shape
    rows = c_pad.shape[0]
    tn = _tile(n, 1024)
    return pl.pallas_call(
        _adaln_kernel,
        grid=(depth, n // tn),
        in_specs=[
            pl.BlockSpec((rows, d), lambda l, j: (0, 0)),
            pl.BlockSpec((None, d, tn), lambda l, j: (l, 0, j)),
            pl.BlockSpec((None, 1, tn), lambda l, j: (l, 0, j)),
        ],
        out_specs=pl.BlockSpec((None, rows, tn), lambda l, j: (l, 0, j)),
        out_shape=jax.ShapeDtypeStruct((depth, rows, n), F32),
        compiler_params=_params(("arbitrary", "arbitrary"), 32),
        name="adaln_mod",
    )(c_pad, w_ada, b_ada.reshape(depth, 1, n))


def _mixer_in_kernel(x_ref, sc_ref, sh_ref, g_ref, w_ref, cw_ref, yconv_ref, u_ref,
                     halo_scr, *, tiles_per_seq, col_chunk):
    tm = x_ref.shape[0]
    d_conv = yconv_ref.shape[1]
    d_ssm = u_ref.shape[1]
    h = (_rms(x_ref[...]) * g_ref[...] * (1.0 + sc_ref[...]) + sh_ref[...]).astype(BF16)

    @pl.when(pl.program_id(0) % tiles_per_seq == 0)
    def _():
        halo_scr[...] = jnp.zeros(halo_scr.shape, F32)

    for c0 in range(0, d_conv, col_chunk):
        cols = slice(c0, c0 + col_chunk)
        gate_b = _dot(h, w_ref[:, c0:c0 + col_chunk])
        z = (_dot(h, w_ref[:, d_conv + c0:d_conv + c0 + col_chunk])
             * _dot(h, w_ref[:, 2 * d_conv + c0:2 * d_conv + c0 + col_chunk]))
        prev = halo_scr[:, cols]
        halo_scr[:, cols] = z[tm - SUBLANES:, :]
        w0 = cw_ref[0:1, cols]
        w1 = cw_ref[1:2, cols]
        w2 = cw_ref[2:3, cols]
        z1 = pltpu.roll(z, 1, 0)
        z2 = pltpu.roll(z, 2, 0)
        yconv_ref[:, cols] = (gate_b * (z * w2 + z1 * w1 + z2 * w0)).astype(yconv_ref.dtype)
        row = lax.broadcasted_iota(jnp.int32, (SUBLANES, col_chunk), 0)
        z1t = jnp.where(row < 1, pltpu.roll(prev, 1, 0), z1[:SUBLANES, :])
        z2t = jnp.where(row < 2, pltpu.roll(prev, 2, 0), z2[:SUBLANES, :])
        top = gate_b[:SUBLANES, :] * (z[:SUBLANES, :] * w2 + z1t * w1 + z2t * w0)
        yconv_ref[0:SUBLANES, cols] = top.astype(yconv_ref.dtype)

    for c0 in range(0, d_ssm, col_chunk):
        u_ref[:, c0:c0 + col_chunk] = _dot(
            h, w_ref[:, 3 * d_conv + c0:3 * d_conv + c0 + col_chunk])


def mixer_in(x, sc, sh, g_pre, w_in16, conv_wt, *, seq_len, d_conv):
    t, d = x.shape
    d_in = w_in16.shape[1]
    d_ssm = d_in - 3 * d_conv
    tm = _tile(seq_len, 512)
    col_chunk = _tile(min(d_conv, d_ssm), 512)
    assert d_conv % col_chunk == 0 and d_ssm % col_chunk == 0
    tiles_per_seq = seq_len // tm

    def mod_spec():
        return pl.BlockSpec((None, 1, d), lambda i: (i // tiles_per_seq, 0, 0))

    kern = functools.partial(_mixer_in_kernel, tiles_per_seq=tiles_per_seq,
                             col_chunk=col_chunk)
    return pl.pallas_call(
        kern,
        grid=(t // tm,),
        in_specs=[
            pl.BlockSpec((tm, d), lambda i: (i, 0)),
            mod_spec(), mod_spec(),
            pl.BlockSpec((1, d), lambda i: (0, 0)),
            pl.BlockSpec((d, d_in), lambda i: (0, 0), pipeline_mode=pl.Buffered(1)),
            pl.BlockSpec((3, d_conv), lambda i: (0, 0)),
        ],
        out_specs=[
            pl.BlockSpec((tm, d_conv), lambda i: (i, 0)),
            pl.BlockSpec((tm, d_ssm), lambda i: (i, 0)),
        ],
        out_shape=[jax.ShapeDtypeStruct((t, d_conv), BF16),
                   jax.ShapeDtypeStruct((t, d_ssm), F32)],
        scratch_shapes=[pltpu.VMEM((SUBLANES, d_conv), F32)],
        compiler_params=_params(("arbitrary",), 48),
        name="mixer_in",
    )(x, sc, sh, g_pre, w_in16, conv_wt)


def _ssm_prep_kernel(are_ref, aim_ref, ldt_ref, bre_ref, bim_ref,
                     bbre_ref, bbim_ref, abre_ref, abim_ref):
    lam_re = are_ref[...]
    lam_im = aim_ref[...]
    dt = jnp.exp(ldt_ref[...])
    mag = jnp.exp(lam_re * dt)
    ang = lam_im * dt
    ab_re = mag * jnp.cos(ang)
    ab_im = mag * jnp.sin(ang)
    den = lam_re * lam_re + lam_im * lam_im
    nr = ab_re - 1.0
    q_re = (nr * lam_re + ab_im * lam_im) / den
    q_im = (ab_im * lam_re - nr * lam_im) / den
    b_re = bre_ref[...]
    b_im = bim_ref[...]
    bbre_ref[...] = q_re * b_re - q_im * b_im
    bbim_ref[...] = q_re * b_im + q_im * b_re
    abre_ref[...] = ab_re
    abim_ref[...] = ab_im


def ssm_prep(a_re, a_im, log_dt, b_re, b_im):
    g, p, h = b_re.shape
    ldt = jnp.broadcast_to(log_dt.reshape(g, 1, 1), (g, p, 1))
    return pl.pallas_call(
        _ssm_prep_kernel,
        out_shape=[jax.ShapeDtypeStruct((g, p, h), F32),
                   jax.ShapeDtypeStruct((g, p, h), F32),
                   jax.ShapeDtypeStruct((g, p, 1), F32),
                   jax.ShapeDtypeStruct((g, p, 1), F32)],
        compiler_params=pltpu.CompilerParams(vmem_limit_bytes=40 * MIB),
        name="ssm_prep",
    )(a_re.reshape(g, p, 1), a_im.reshape(g, p, 1), ldt, b_re, b_im)


def _block_diag_slabs(w, groups_per_slab):
    g, r, c = w.shape
    ns = g // groups_per_slab
    w = w.reshape(ns, groups_per_slab, r, c)
    eye = jnp.eye(groups_per_slab, dtype=w.dtype)
    out = w[:, :, :, None, :] * eye[None, :, None, :, None]
    return out.reshape(ns, groups_per_slab * r, groups_per_slab * c)


def _gelu_tanh(x):
    return 0.5 * x * (1.0 + jnp.tanh(0.7978845608028654 * (x + 0.044715 * x * x * x)))


def _ssm_kernel(u_ref, bbd_ref, cre_ref, cim_ref, are_ref, aim_ref, d_ref, wglu_ref,
                bglu_ref, o_ref, sre_scr, sim_scr, stre_scr, stim_scr, y_scr, *,
                pitch, groups_per_step):
    nb, tc, d_ssm = u_ref.shape
    n_slabs, k_slab, two_sw = bbd_ref.shape
    sw = two_sw // 2
    tiles_per_slab = sw // LANES
    pack = SUBLANES // nb
    n_groups = sre_scr.shape[0]

    def slab_rows(lane_tile, b):
        q, h = divmod(lane_tile, pack)
        r0 = (h * nb + b) * pitch
        return q, slice(r0, r0 + tc)

    @pl.when(pl.program_id(0) == 0)
    def _():
        stre_scr[...] = jnp.zeros(stre_scr.shape, F32)
        stim_scr[...] = jnp.zeros(stim_scr.shape, F32)

    for b in range(nb):
        for s in range(n_slabs):
            ub = u_ref[b, :, s * k_slab:(s + 1) * k_slab].astype(BF16)
            r = _dot(ub, bbd_ref[s])
            for j in range(tiles_per_slab):
                q, rows = slab_rows(s * tiles_per_slab + j, b)
                sre_scr[q, rows, :] = r[:, j * LANES:(j + 1) * LANES]
                sim_scr[q, rows, :] = r[:, sw + j * LANES:sw + (j + 1) * LANES]

    for q0 in range(0, n_groups, groups_per_step):
        qs = list(range(q0, min(q0 + groups_per_step, n_groups)))
        a_re = [are_ref[q] for q in qs]
        a_im = [aim_ref[q] for q in qs]

        def step(t, carry, qs=qs, a_re=a_re, a_im=a_im):
            rows = pl.ds(t, SUBLANES, stride=pitch)
            out = []
            for n, q in enumerate(qs):
                s_re, s_im = carry[2 * n], carry[2 * n + 1]
                n_re = a_re[n] * s_re - a_im[n] * s_im + sre_scr[q, rows, :]
                n_im = a_re[n] * s_im + a_im[n] * s_re + sim_scr[q, rows, :]
                sre_scr[q, rows, :] = n_re
                sim_scr[q, rows, :] = n_im
                out += [n_re, n_im]
            return tuple(out)

        init = []
        for q in qs:
            init += [stre_scr[q], stim_scr[q]]
        fin = lax.fori_loop(0, tc, step, tuple(init), unroll=2)
        for n, q in enumerate(qs):
            stre_scr[q] = fin[2 * n]
            stim_scr[q] = fin[2 * n + 1]

    for b in range(nb):
        for s in range(n_slabs):
            cols = slice(s * k_slab, (s + 1) * k_slab)
            pieces_re, pieces_im = [], []
            for j in range(tiles_per_slab):
                q, rows = slab_rows(s * tiles_per_slab + j, b)
                pieces_re.append(sre_scr[q, rows, :].astype(BF16))
                pieces_im.append(sim_scr[q, rows, :].astype(BF16))
            y = (_dot(jnp.concatenate(pieces_re, axis=-1), cre_ref[s])
                 - _dot(jnp.concatenate(pieces_im, axis=-1), cim_ref[s]))
            y_scr[:, cols] = y + d_ref[:, cols] * u_ref[b, :, cols]
        y = _gelu_tanh(y_scr[...])
        gate = _dot(y.astype(BF16), wglu_ref[...]) + bglu_ref[...]
        o_ref[b] = (y * jax.nn.sigmoid(gate)).astype(o_ref.dtype)


def _pack_state_rows(a, nb):
    pack = SUBLANES // nb
    n_groups = a.shape[1] // (LANES * pack)
    a = a.reshape(n_groups, pack, 1, LANES)
    return jnp.broadcast_to(a, (n_groups, pack, nb, LANES)).reshape(n_groups, SUBLANES, LANES)


def ssm(u3, bbd, cre, cim, a_re, a_im, d_skip, w_glu16, b_glu):
    nb, seq_len, d_ssm = u3.shape
    n_state = a_re.shape[1]
    assert SUBLANES % nb == 0 and n_state % (LANES * (SUBLANES // nb)) == 0
    tc = _tile(seq_len, 128)
    pitch = tc + SUBLANES
    n_groups = n_state // (LANES * (SUBLANES // nb))
    a_re = _pack_state_rows(a_re, nb)
    a_im = _pack_state_rows(a_im, nb)
    kern = functools.partial(_ssm_kernel, pitch=pitch, groups_per_step=min(4, n_groups))

    def full(a):
        nd = a.ndim
        return pl.BlockSpec(a.shape, lambda c, nd=nd: (0,) * nd)

    return pl.pallas_call(
        kern,
        grid=(seq_len // tc,),
        in_specs=[pl.BlockSpec((nb, tc, d_ssm), lambda c: (0, c, 0)),
                  full(bbd), full(cre), full(cim), full(a_re), full(a_im),
                  full(d_skip), full(w_glu16), full(b_glu)],
        out_specs=pl.BlockSpec((nb, tc, d_ssm), lambda c: (0, c, 0)),
        out_shape=jax.ShapeDtypeStruct((nb, seq_len, d_ssm), BF16),
        scratch_shapes=[pltpu.VMEM((n_groups, SUBLANES * pitch, LANES), F32),
                        pltpu.VMEM((n_groups, SUBLANES * pitch, LANES), F32),
                        pltpu.VMEM((n_groups, SUBLANES, LANES), F32),
                        pltpu.VMEM((n_groups, SUBLANES, LANES), F32),
                        pltpu.VMEM((tc, d_ssm), F32)],
        compiler_params=_params(("arbitrary",), 56),
        name="ssm_scan",
    )(u3, bbd, cre, cim, a_re, a_im, d_skip, w_glu16, b_glu)


def _mixer_out_kernel(*refs, routed):
    if routed:
        (yc_ref, ys_ref, wo_ref, x_ref, gt_ref, gpost_ref, gpre_ref, sc_ref,
         sh_ref, rw_ref, rb_ref, xo_ref, h_ref, lg_ref) = refs
    else:
        (yc_ref, ys_ref, wo_ref, x_ref, gt_ref, gpost_ref, gpre_ref, sc_ref,
         sh_ref, xo_ref, h_ref) = refs
    d_conv = yc_ref.shape[1]
    y = _dot(yc_ref[...], wo_ref[:d_conv, :]) + _dot(ys_ref[...], wo_ref[d_conv:, :])
    x_new = x_ref[...] + gt_ref[...] * (_rms(y) * gpost_ref[...])
    xo_ref[...] = x_new
    h = _rms(x_new) * gpre_ref[...] * (1.0 + sc_ref[...]) + sh_ref[...]
    h_ref[...] = h.astype(h_ref.dtype)
    if routed:
        w = rw_ref[...]
        h_hi = h.astype(BF16)
        w_hi = w.astype(BF16)
        h_lo = (h - h_hi.astype(F32)).astype(BF16)
        w_lo = (w - w_hi.astype(F32)).astype(BF16)
        lg_ref[...] = (_dot(h_hi, w_hi) + (_dot(h_hi, w_lo) + _dot(h_lo, w_hi))
                       + rb_ref[...])


def mixer_out(y_conv, y_ssm, w_out16, x, gt, g_post, g_pre, sc, sh, router=None, *,
              seq_len):
    t, d = x.shape
    d_conv = y_conv.shape[1]
    d_ssm = y_ssm.shape[1]
    tm = _tile(seq_len, 512)
    tiles_per_seq = seq_len // tm
    routed = router is not None

    def mod_spec():
        return pl.BlockSpec((None, 1, d), lambda i: (i // tiles_per_seq, 0, 0))

    def row_spec(n):
        return pl.BlockSpec((tm, n), lambda i: (i, 0))

    def const_spec(shape):
        return pl.BlockSpec(shape, lambda i: (0, 0))

    in_specs = [row_spec(d_conv), row_spec(d_ssm),
                pl.BlockSpec((d_conv + d_ssm, d), lambda i: (0, 0),
                             pipeline_mode=pl.Buffered(1)),
                row_spec(d), mod_spec(), const_spec((1, d)), const_spec((1, d)),
                mod_spec(), mod_spec()]
    args = [y_conv, y_ssm, w_out16, x, gt, g_post, g_pre, sc, sh]
    out_specs = [row_spec(d), row_spec(d)]
    out_shape = [jax.ShapeDtypeStruct((t, d), F32),
                 jax.ShapeDtypeStruct((t, d), F32 if routed else BF16)]
    if routed:
        rw_pad, rb_pad = router
        in_specs += [const_spec(rw_pad.shape), const_spec(rb_pad.shape)]
        args += [rw_pad, rb_pad]
        out_specs.append(row_spec(LANES))
        out_shape.append(jax.ShapeDtypeStruct((t, LANES), F32))
    return pl.pallas_call(
        functools.partial(_mixer_out_kernel, routed=routed),
        grid=(t // tm,),
        in_specs=in_specs, out_specs=out_specs, out_shape=out_shape,
        compiler_params=_params(("arbitrary",), 56),
        name="mixer_out_routed" if routed else "mixer_out",
    )(*args)


def _ffn_kernel(h_ref, wg_ref, wu_ref, wd_ref, x_ref, gt_ref, gpost_ref, o_ref, acc_ref):
    k = pl.program_id(1)

    @pl.when(k == 0)
    def _():
        acc_ref[...] = jnp.zeros(acc_ref.shape, F32)

    h = h_ref[...]
    gate = _dot(h, wg_ref[...])
    act = (gate * jax.nn.sigmoid(gate)) * _dot(h, wu_ref[...])
    acc_ref[...] += _dot(act.astype(BF16), wd_ref[...])

    @pl.when(k == pl.num_programs(1) - 1)
    def _():
        o_ref[...] = x_ref[...] + gt_ref[...] * (_rms(acc_ref[...]) * gpost_ref[...])


def ffn_dense(h16, wg16, wu16, wd16, x, gt, g_post, *, seq_len):
    t, d = x.shape
    d_ff = wg16.shape[1]
    tm = _tile(seq_len, 512)
    tk = _tile(d_ff, 512)
    tiles_per_seq = seq_len // tm
    return pl.pallas_call(
        _ffn_kernel,
        grid=(t // tm, d_ff // tk),
        in_specs=[
            pl.BlockSpec((tm, d), lambda i, k: (i, 0)),
            pl.BlockSpec((d, tk), lambda i, k: (0, k)),
            pl.BlockSpec((d, tk), lambda i, k: (0, k)),
            pl.BlockSpec((tk, d), lambda i, k: (k, 0)),
            pl.BlockSpec((tm, d), lambda i, k: (i, 0)),
            pl.BlockSpec((None, 1, d), lambda i, k: (i // tiles_per_seq, 0, 0)),
            pl.BlockSpec((1, d), lambda i, k: (0, 0)),
        ],
        out_specs=pl.BlockSpec((tm, d), lambda i, k: (i, 0)),
        out_shape=jax.ShapeDtypeStruct((t, d), F32),
        scratch_shapes=[pltpu.VMEM((tm, d), F32)],
        compiler_params=_params(("arbitrary", "arbitrary"), 48),
        name="ffn_dense",
    )(h16, wg16, wu16, wd16, x, gt, g_post)


def _router_kernel(lg_ref, ri_ref, rw_ref, cnt_ref, carry_scr, *, n_experts):
    @pl.when(pl.program_id(0) == 0)
    def _():
        carry_scr[...] = jnp.zeros(carry_scr.shape, F32)

    tm = lg_ref.shape[0]
    lane = lax.broadcasted_iota(jnp.int32, (tm, LANES), 1)
    neg = jnp.float32(-jnp.inf)
    logit = jnp.where(lane < n_experts, lg_ref[...], neg)
    m1 = jnp.max(logit, axis=-1, keepdims=True)
    i1 = jnp.min(jnp.where(logit == m1, lane, LANES), axis=-1, keepdims=True)
    rest = jnp.where(lane == i1, neg, logit)
    m2 = jnp.max(rest, axis=-1, keepdims=True)
    i2 = jnp.min(jnp.where(rest == m2, lane, LANES), axis=-1, keepdims=True)
    e2 = jnp.exp(m2 - m1)
    w1 = 1.0 / (1.0 + e2)
    w2 = e2 / (1.0 + e2)
    hit1 = lane == i1
    hit2 = lane == i2
    onehot = jnp.where(hit1 | hit2, 1.0, 0.0)
    r = lax.broadcasted_iota(jnp.int32, (tm, tm), 0)
    c = lax.broadcasted_iota(jnp.int32, (tm, tm), 1)
    earlier = jnp.where(c < r, 1.0, 0.0).astype(BF16)
    carry = carry_scr[0:1, :]
    before = _dot(earlier, onehot.astype(BF16)) + carry
    pos1 = jnp.sum(jnp.where(hit1, before, 0.0), axis=-1, keepdims=True).astype(jnp.int32)
    pos2 = jnp.sum(jnp.where(hit2, before, 0.0), axis=-1, keepdims=True).astype(jnp.int32)
    total = carry + jnp.sum(onehot, axis=0, keepdims=True)
    carry_scr[...] = jnp.broadcast_to(total, carry_scr.shape)
    cnt_ref[...] = jnp.broadcast_to(total, cnt_ref.shape).astype(jnp.int32)
    zero_i = jnp.zeros((tm, LANES), jnp.int32)
    ri_ref[...] = jnp.where(lane == 0, i1, jnp.where(lane == 1, i2,
                            jnp.where(lane == 2, pos1, jnp.where(lane == 3, pos2, zero_i))))
    rw_ref[...] = jnp.where(lane == 0, w1, jnp.where(lane == 1, w2, 0.0))


def router(logits, n_experts):
    t = logits.shape[0]
    tm = _tile(t, 256)
    return pl.pallas_call(
        functools.partial(_router_kernel, n_experts=n_experts),
        grid=(t // tm,),
        in_specs=[pl.BlockSpec((tm, LANES), lambda i: (i, 0))],
        out_specs=[pl.BlockSpec((tm, LANES), lambda i: (i, 0)),
                   pl.BlockSpec((tm, LANES), lambda i: (i, 0)),
                   pl.BlockSpec((SUBLANES, LANES), lambda i: (0, 0))],
        out_shape=[jax.ShapeDtypeStruct((t, LANES), jnp.int32),
                   jax.ShapeDtypeStruct((t, LANES), F32),
                   jax.ShapeDtypeStruct((SUBLANES, LANES), jnp.int32)],
        scratch_shapes=[pltpu.VMEM((SUBLANES, LANES), F32)],
        compiler_params=_params(("arbitrary",), 32),
        name="router_top2",
    )(logits)


def _moe_ffn_kernel(ge_ref, ns_ref, nu_ref, tok_ref, h_hbm, wg_ref, wu_ref, wd_ref, o_ref,
                    h_scr, gbuf, sems):
    g = pl.program_id(0)
    k = pl.program_id(1)
    n_used = nu_ref[0]
    n_sub_max, ts, _ = h_scr.shape
    group_rows = n_sub_max * ts

    def row_copy(src_row, j, i):
        return pltpu.make_async_copy(h_hbm.at[pl.ds(src_row, 1)],
                                     gbuf.at[j, pl.ds(i, 1)], sems.at[j])

    def for_each_sub_tile(group, fn):
        for j in range(n_sub_max):
            @pl.when(j < ns_ref[group])
            def _(j=j):
                fn(j)

    def start_group(group):
        def sub_tile(j):
            def body(i, carry):
                row_copy(tok_ref[group * group_rows + j * ts + i], j, i).start()
                return carry
            lax.fori_loop(0, ts, body, 0, unroll=8)
        for_each_sub_tile(group, sub_tile)

    def finish_group(group):
        def sub_tile(j):
            def body(i, carry):
                row_copy(0, j, i).wait()
                return carry
            lax.fori_loop(0, ts, body, 0, unroll=8)
            h_scr[j] = gbuf[j].astype(BF16)
        for_each_sub_tile(group, sub_tile)

    @pl.when(k == 0)
    def _():
        o_ref[...] = jnp.zeros(o_ref.shape, o_ref.dtype)

    @pl.when(g < n_used)
    def _():
        @pl.when(k == 0)
        def _():
            @pl.when(g == 0)
            def _():
                start_group(0)

            finish_group(g)

            @pl.when(g + 1 < n_used)
            def _():
                start_group(g + 1)

        w_gate = wg_ref[...].astype(BF16)
        w_up = wu_ref[...].astype(BF16)
        w_down = wd_ref[...].astype(BF16)

        def sub_tile(j):
            h = h_scr[j]
            gate = _dot(h, w_gate)
            act = (gate * jax.nn.sigmoid(gate)) * _dot(h, w_up)
            o_ref[j * ts:(j + 1) * ts, :] += _dot(act.astype(BF16), w_down)
        for_each_sub_tile(g, sub_tile)


def moe_ffn(group_expert, group_subs, n_used, row_tok, h, w_gate, w_up, w_down, layer, *,
            sub_rows, n_sub_max):
    n_rows = row_tok.shape[0]
    d = h.shape[1]
    d_ff = w_gate.shape[3]
    tk = _tile(d_ff, 256)
    nk = d_ff // tk
    group_rows = sub_rows * n_sub_max

    def ff_idx(g, k, nu):
        return jnp.where(g < nu[0], k, nk - 1)

    return pl.pallas_call(
        _moe_ffn_kernel,
        grid_spec=pltpu.PrefetchScalarGridSpec(
            num_scalar_prefetch=4,
            grid=(n_rows // group_rows, nk),
            in_specs=[
                pl.BlockSpec(memory_space=pl.ANY),
                pl.BlockSpec((None, None, d, tk),
                             lambda g, k, ge, ns, nu, tok: (layer, ge[g], 0, ff_idx(g, k, nu))),
                pl.BlockSpec((None, None, d, tk),
                             lambda g, k, ge, ns, nu, tok: (layer, ge[g], 0, ff_idx(g, k, nu))),
                pl.BlockSpec((None, None, tk, d),
                             lambda g, k, ge, ns, nu, tok: (layer, ge[g], ff_idx(g, k, nu), 0)),
            ],
            out_specs=pl.BlockSpec((group_rows, d), lambda g, k, ge, ns, nu, tok: (g, 0)),
            scratch_shapes=[pltpu.VMEM((n_sub_max, sub_rows, d), BF16),
                            pltpu.VMEM((n_sub_max, sub_rows, d), F32),
                            pltpu.SemaphoreType.DMA((n_sub_max,))],
        ),
        out_shape=jax.ShapeDtypeStruct((n_rows, d), F32),
        compiler_params=_params(("arbitrary", "arbitrary"), 56),
        name="moe_ffn",
    )(group_expert, group_subs, n_used, row_tok, h, w_gate, w_up, w_down)


def _moe_combine_kernel(d0_ref, d1_ref, ys_ref, rw_ref, x_ref, gt_ref, gpost_ref, o_ref,
                        buf, sem):
    tm = x_ref.shape[0]
    base = pl.program_id(0) * tm

    def row_copy(src_row, slot, t):
        return pltpu.make_async_copy(ys_ref.at[pl.ds(src_row, 1)],
                                     buf.at[slot, pl.ds(t, 1)], sem)

    def issue(t, carry):
        row_copy(d0_ref[base + t], 0, t).start()
        row_copy(d1_ref[base + t], 1, t).start()
        return carry

    lax.fori_loop(0, tm, issue, 0)

    def drain(t, carry):
        row_copy(0, 0, t).wait()
        row_copy(0, 1, t).wait()
        return carry

    lax.fori_loop(0, tm, drain, 0)
    y = buf[0] * rw_ref[:, 0:1] + buf[1] * rw_ref[:, 1:2]
    o_ref[...] = x_ref[...] + gt_ref[...] * (_rms(y) * gpost_ref[...])


def moe_combine(dest0, dest1, ys, route_w, x, gt, g_post, *, seq_len):
    t, d = x.shape
    tm = _tile(seq_len, 256)
    tiles_per_seq = seq_len // tm
    return pl.pallas_call(
        _moe_combine_kernel,
        grid_spec=pltpu.PrefetchScalarGridSpec(
            num_scalar_prefetch=2,
            grid=(t // tm,),
            in_specs=[
                pl.BlockSpec(memory_space=pl.ANY),
                pl.BlockSpec((tm, LANES), lambda i, d0, d1: (i, 0)),
                pl.BlockSpec((tm, d), lambda i, d0, d1: (i, 0)),
                pl.BlockSpec((None, 1, d), lambda i, d0, d1: (i // tiles_per_seq, 0, 0)),
                pl.BlockSpec((1, d), lambda i, d0, d1: (0, 0)),
            ],
            out_specs=pl.BlockSpec((tm, d), lambda i, d0, d1: (i, 0)),
            scratch_shapes=[pltpu.VMEM((2, tm, d), F32), pltpu.SemaphoreType.DMA(())],
        ),
        out_shape=jax.ShapeDtypeStruct((t, d), F32),
        compiler_params=_params(("arbitrary",), 32),
        name="moe_combine",
    )(dest0, dest1, ys, route_w, x, gt, g_post)


MOE_SUB_ROWS = 512
MOE_GROUP_SUBS = 2


def moe_layer(h, logits, x, gt, g_post, w_gate, w_up, w_down, layer, *, seq_len):
    t, d = h.shape
    n_experts = w_gate.shape[1]
    ts = min(MOE_SUB_ROWS, t)
    group_rows = ts * MOE_GROUP_SUBS
    route_i, route_w, counts = router(logits, n_experts)
    e0, e1, pos0, pos1 = (route_i[:, k] for k in range(4))
    counts = counts[0, :n_experts]
    groups_e = (counts + group_rows - 1) // group_rows
    g_end = jnp.cumsum(groups_e)
    g_start = g_end - groups_e
    dest0 = g_start[e0] * group_rows + pos0
    dest1 = g_start[e1] * group_rows + pos1
    n_groups = -(-(2 * t) // group_rows) + n_experts
    n_rows = n_groups * group_rows
    tok = jnp.arange(t, dtype=jnp.int32)
    row_tok = jnp.zeros((n_rows,), jnp.int32).at[dest0].set(tok).at[dest1].set(tok)
    n_used = g_end[-1].astype(jnp.int32)
    gid = jnp.arange(n_groups, dtype=jnp.int32)
    group_e = jnp.clip(jnp.searchsorted(g_end, gid, side='right'), 0, n_experts - 1)
    group_e = jnp.where(gid < n_used, group_e, group_e[n_used - 1]).astype(jnp.int32)
    rows_in_group = jnp.clip(counts[group_e] - (gid - g_start[group_e]) * group_rows,
                             0, group_rows)
    group_subs = jnp.where(gid < n_used, (rows_in_group + ts - 1) // ts, 0).astype(jnp.int32)
    ys = moe_ffn(group_e, group_subs, n_used.reshape(1), row_tok, h, w_gate, w_up, w_down,
                 layer, sub_rows=ts, n_sub_max=MOE_GROUP_SUBS)
    return moe_combine(dest0, dest1, ys, route_w, x, gt, g_post, seq_len=seq_len)


def kernel(x, c, w_ada, b_ada, g_mix_pre, g_mix_post, g_ffn_pre, g_ffn_post, w_in, conv_w,
           ssm_a_re, ssm_a_im, ssm_log_dt, ssm_b_re, ssm_b_im, ssm_c_re, ssm_c_im, ssm_d,
           w_glu, b_glu, w_out, ffn_w_gate, ffn_w_up, ffn_w_down, router_w, router_b,
           moe_w_gate, moe_w_up, moe_w_down):
    nb, seq_len, d = x.shape
    depth = w_ada.shape[0]
    d_conv = conv_w.shape[1]
    g, p, h_dim = ssm_b_re.shape[1:]
    d_ssm = g * h_dim
    n_experts = router_w.shape[2]
    t = nb * seq_len
    groups_per_slab = max(1, min(g, MXU_DIM // h_dim))

    c_pad = jnp.zeros((SUBLANES, d), F32).at[:nb].set(c)
    mod = adaln_mod(c_pad, w_ada, b_ada)
    xf = x.reshape(t, d)

    for l in range(depth):
        sh_m, sc_m, gt_m, sh_f, sc_f, gt_f = (
            mod[l, :nb, k * d:(k + 1) * d].reshape(nb, 1, d) for k in range(6))
        y_conv, u = mixer_in(xf, sc_m, sh_m, g_mix_pre[l].reshape(1, d),
                             w_in[l].astype(BF16), conv_w[l].T,
                             seq_len=seq_len, d_conv=d_conv)
        bb_re, bb_im, ab_re, ab_im = ssm_prep(ssm_a_re[l], ssm_a_im[l], ssm_log_dt[l],
                                              ssm_b_re[l], ssm_b_im[l])
        bbd = jnp.concatenate(
            [_block_diag_slabs(jnp.swapaxes(bb_re, 1, 2), groups_per_slab),
             _block_diag_slabs(jnp.swapaxes(bb_im, 1, 2), groups_per_slab)],
            axis=-1).astype(BF16)
        cre = _block_diag_slabs(jnp.swapaxes(ssm_c_re[l], 1, 2), groups_per_slab).astype(BF16)
        cim = _block_diag_slabs(jnp.swapaxes(ssm_c_im[l], 1, 2), groups_per_slab).astype(BF16)
        y_ssm = ssm(u.reshape(nb, seq_len, d_ssm), bbd, cre, cim,
                    ab_re.reshape(1, g * p), ab_im.reshape(1, g * p),
                    ssm_d[l].reshape(1, d_ssm), w_glu[l].astype(BF16),
                    b_glu[l].reshape(1, d_ssm)).reshape(t, d_ssm)
        routed = l % 2 == 1
        i = l // 2
        rt = None
        if routed:
            rw_pad = jnp.zeros((d, LANES), F32).at[:, :n_experts].set(router_w[i])
            rb_pad = jnp.zeros((1, LANES), F32).at[0, :n_experts].set(router_b[i])
            rt = (rw_pad, rb_pad)
        outs = mixer_out(y_conv, y_ssm, w_out[l].astype(BF16), xf, gt_m,
                         g_mix_post[l].reshape(1, d), g_ffn_pre[l].reshape(1, d),
                         sc_f, sh_f, rt, seq_len=seq_len)
        g_post = g_ffn_post[l].reshape(1, d)
        if routed:
            xf, h, logits = outs
            xf = moe_layer(h, logits, xf, gt_f, g_post, moe_w_gate, moe_w_up, moe_w_down, i,
                           seq_len=seq_len)
        else:
            xf, h = outs
            xf = ffn_dense(h, ffn_w_gate[i].astype(BF16), ffn_w_up[i].astype(BF16),
                           ffn_w_down[i].astype(BF16), xf, gt_f, g_post, seq_len=seq_len)
    return xf.reshape(nb, seq_len, d)
```

```python
import functools

import jax
import jax.numpy as jnp
from jax import lax
from jax.experimental import pallas as pl
from jax.experimental.pallas import tpu as pltpu

F32 = jnp.float32
BF16 = jnp.bfloat16
NORM_EPS = 1e-6
LANES = 128
SUBLANES = 8
MXU_DIM = 256
MIB = 1024 * 1024


def _params(semantics, vmem_mib):
    return pltpu.CompilerParams(dimension_semantics=semantics,
                                vmem_limit_bytes=vmem_mib * MIB)


def _dot(a, b):
    return jnp.dot(a, b, preferred_element_type=F32)


def _rms(x):
    return x * lax.rsqrt(jnp.mean(x * x, axis=-1, keepdims=True) + NORM_EPS)


def _tile(n, want):
    t = min(n, want)
    while n % t:
        t -= 1
    return t


def _adaln_kernel(c_ref, w_ref, b_ref, o_ref):
    c = c_ref[...]
    c_act = c * jax.nn.sigmoid(c)
    o_ref[...] = jnp.dot(c_act, w_ref[...], preferred_element_type=F32,
                         precision=lax.Precision.HIGHEST) + b_ref[...]


def adaln_mod(c_pad, w_ada, b_ada):
    depth, d, n = w_ada.shape
    rows = c_pad.shape[0]
    tn = _tile(n, 1024)
    return pl.pallas_call(
        _adaln_kernel,
        grid=(depth, n // tn),
        in_specs=[
            pl.BlockSpec((rows, d), lambda l, j: (0, 0)),
            pl.BlockSpec((None, d, tn), lambda l, j: (l, 0, j)),
            pl.BlockSpec((None, 1, tn), lambda l, j: (l, 0, j)),
        ],
        out_specs=pl.BlockSpec((None, rows, tn), lambda l, j: (l, 0, j)),
        out_shape=jax.ShapeDtypeStruct((depth, rows, n), F32),
        compiler_params=_params(("arbitrary", "arbitrary"), 32),
        name="adaln_mod",
    )(c_pad, w_ada, b_ada.reshape(depth, 1, n))


def _mixer_in_kernel(x_ref, sc_ref, sh_ref, g_ref, w_ref, cw_ref, yconv_ref, u_ref,
                     halo_scr, *, tiles_per_seq, col_chunk):
    tm = x_ref.shape[0]
    d_conv = yconv_ref.shape[1]
    d_ssm = u_ref.shape[1]
    h = (_rms(x_ref[...]) * g_ref[...] * (1.0 + sc_ref[...]) + sh_ref[...]).astype(BF16)

    @pl.when(pl.program_id(0) % tiles_per_seq == 0)
    def _():
        halo_scr[...] = jnp.zeros(halo_scr.shape, F32)

    for c0 in range(0, d_conv, col_chunk):
        cols = slice(c0, c0 + col_chunk)
        gate_b = _dot(h, w_ref[:, c0:c0 + col_chunk])
        z = (_dot(h, w_ref[:, d_conv + c0:d_conv + c0 + col_chunk])
             * _dot(h, w_ref[:, 2 * d_conv + c0:2 * d_conv + c0 + col_chunk]))
        prev = halo_scr[:, cols]
        halo_scr[:, cols] = z[tm - SUBLANES:, :]
        w0 = cw_ref[0:1, cols]
        w1 = cw_ref[1:2, cols]
        w2 = cw_ref[2:3, cols]
        z1 = pltpu.roll(z, 1, 0)
        z2 = pltpu.roll(z, 2, 0)
        yconv_ref[:, cols] = (gate_b * (z * w2 + z1 * w1 + z2 * w0)).astype(yconv_ref.dtype)
        row = lax.broadcasted_iota(jnp.int32, (SUBLANES, col_chunk), 0)
        z1t = jnp.where(row < 1, pltpu.roll(prev, 1, 0), z1[:SUBLANES, :])
        z2t = jnp.where(row < 2, pltpu.roll(prev, 2, 0), z2[:SUBLANES, :])
        top = gate_b[:SUBLANES, :] * (z[:SUBLANES, :] * w2 + z1t * w1 + z2t * w0)
        yconv_ref[0:SUBLANES, cols] = top.astype(yconv_ref.dtype)

    for c0 in range(0, d_ssm, col_chunk):
        u_ref[:, c0:c0 + col_chunk] = _dot(
            h, w_ref[:, 3 * d_conv + c0:3 * d_conv + c0 + col_chunk])


SH_MIX, SC_MIX, GT_MIX, SH_FFN, SC_FFN, GT_FFN = range(6)


def _mod_spec(layer, chunk, d, batch_of):
    return pl.BlockSpec((None, None, None, 1, d),
                        lambda i, *_: (layer, batch_of(i), chunk, 0, 0))


def _layer_spec(layer, shape, **kw):
    zeros = (0,) * len(shape)
    return pl.BlockSpec((None,) + tuple(shape), lambda *_: (layer,) + zeros, **kw)


def mixer_in(x, mod5, g_pre, w_in16, conv_wt, layer, *, seq_len):
    t, d = x.shape
    d_in = w_in16.shape[2]
    d_conv = conv_wt.shape[2]
    d_ssm = d_in - 3 * d_conv
    tm = _tile(seq_len, 512)
    col_chunk = _tile(min(d_conv, d_ssm), 512)
    assert d_conv % col_chunk == 0 and d_ssm % col_chunk == 0
    tiles_per_seq = seq_len // tm

    def batch_of(i):
        return i // tiles_per_seq

    kern = functools.partial(_mixer_in_kernel, tiles_per_seq=tiles_per_seq,
                             col_chunk=col_chunk)
    return pl.pallas_call(
        kern,
        grid=(t // tm,),
        in_specs=[
            pl.BlockSpec((tm, d), lambda i: (i, 0)),
            _mod_spec(layer, SC_MIX, d, batch_of), _mod_spec(layer, SH_MIX, d, batch_of),
            _layer_spec(layer, (1, d)),
            _layer_spec(layer, (d, d_in), pipeline_mode=pl.Buffered(1)),
            _layer_spec(layer, (3, d_conv)),
        ],
        out_specs=[
            pl.BlockSpec((tm, d_conv), lambda i: (i, 0)),
            pl.BlockSpec((tm, d_ssm), lambda i: (i, 0)),
        ],
        out_shape=[jax.ShapeDtypeStruct((t, d_conv), BF16),
                   jax.ShapeDtypeStruct((t, d_ssm), F32)],
        scratch_shapes=[pltpu.VMEM((SUBLANES, d_conv), F32)],
        compiler_params=_params(("arbitrary",), 48),
        name="mixer_in",
    )(x, mod5, mod5, g_pre, w_in16, conv_wt)


def _ssm_prep_kernel(are_ref, aim_ref, ldt_ref, bre_ref, bim_ref,
                     bbre_ref, bbim_ref, abre_ref, abim_ref):
    lam_re = are_ref[...]
    lam_im = aim_ref[...]
    dt = jnp.exp(ldt_ref[...])
    mag = jnp.exp(lam_re * dt)
    ang = lam_im * dt
    ab_re = mag * jnp.cos(ang)
    ab_im = mag * jnp.sin(ang)
    den = lam_re * lam_re + lam_im * lam_im
    nr = ab_re - 1.0
    q_re = (nr * lam_re + ab_im * lam_im) / den
    q_im = (ab_im * lam_re - nr * lam_im) / den
    b_re = bre_ref[...]
    b_im = bim_ref[...]
    bbre_ref[...] = q_re * b_re - q_im * b_im
    bbim_ref[...] = q_re * b_im + q_im * b_re
    abre_ref[...] = ab_re
    abim_ref[...] = ab_im


def ssm_prep(a_re, a_im, log_dt, b_re, b_im):
    depth, g, p, h = b_re.shape
    ldt = jnp.broadcast_to(log_dt.reshape(depth, g, 1, 1), (depth, g, p, 1))

    def spec(last):
        return pl.BlockSpec((None, g, p, last), lambda l: (l, 0, 0, 0))

    return pl.pallas_call(
        _ssm_prep_kernel,
        grid=(depth,),
        in_specs=[spec(1), spec(1), spec(1), spec(h), spec(h)],
        out_specs=[spec(h), spec(h), spec(1), spec(1)],
        out_shape=[jax.ShapeDtypeStruct((depth, g, p, h), F32),
                   jax.ShapeDtypeStruct((depth, g, p, h), F32),
                   jax.ShapeDtypeStruct((depth, g, p, 1), F32),
                   jax.ShapeDtypeStruct((depth, g, p, 1), F32)],
        compiler_params=_params(("arbitrary",), 56),
        name="ssm_prep",
    )(a_re.reshape(depth, g, p, 1), a_im.reshape(depth, g, p, 1), ldt, b_re, b_im)


def _block_diag_slabs(w, groups_per_slab):
    depth, g, r, c = w.shape
    ns = g // groups_per_slab
    w = w.reshape(depth, ns, groups_per_slab, r, c)
    eye = jnp.eye(groups_per_slab, dtype=w.dtype)
    out = w[:, :, :, :, None, :] * eye[None, None, :, None, :, None]
    return out.reshape(depth, ns, groups_per_slab * r, groups_per_slab * c)


def _gelu_tanh(x):
    return 0.5 * x * (1.0 + jnp.tanh(0.7978845608028654 * (x + 0.044715 * x * x * x)))


def _ssm_kernel(u_ref, bbd_ref, cre_ref, cim_ref, are_ref, aim_ref, d_ref, wglu_ref,
                bglu_ref, o_ref, sre_scr, sim_scr, stre_scr, stim_scr, y_scr, *,
                pitch, groups_per_step):
    nb, tc, d_ssm = u_ref.shape
    n_slabs, k_slab, two_sw = bbd_ref.shape
    sw = two_sw // 2
    tiles_per_slab = sw // LANES
    pack = SUBLANES // nb
    n_groups = sre_scr.shape[0]

    def slab_rows(lane_tile, b):
        q, h = divmod(lane_tile, pack)
        r0 = (h * nb + b) * pitch
        return q, slice(r0, r0 + tc)

    @pl.when(pl.program_id(0) == 0)
    def _():
        stre_scr[...] = jnp.zeros(stre_scr.shape, F32)
        stim_scr[...] = jnp.zeros(stim_scr.shape, F32)

    for b in range(nb):
        for s in range(n_slabs):
            ub = u_ref[b, :, s * k_slab:(s + 1) * k_slab].astype(BF16)
            r = _dot(ub, bbd_ref[s])
            for j in range(tiles_per_slab):
                q, rows = slab_rows(s * tiles_per_slab + j, b)
                sre_scr[q, rows, :] = r[:, j * LANES:(j + 1) * LANES]
                sim_scr[q, rows, :] = r[:, sw + j * LANES:sw + (j + 1) * LANES]

    for q0 in range(0, n_groups, groups_per_step):
        qs = list(range(q0, min(q0 + groups_per_step, n_groups)))
        a_re = [are_ref[q] for q in qs]
        a_im = [aim_ref[q] for q in qs]

        def step(t, carry, qs=qs, a_re=a_re, a_im=a_im):
            rows = pl.ds(t, SUBLANES, stride=pitch)
            out = []
            for n, q in enumerate(qs):
                s_re, s_im = carry[2 * n], carry[2 * n + 1]
                n_re = a_re[n] * s_re - a_im[n] * s_im + sre_scr[q, rows, :]
                n_im = a_re[n] * s_im + a_im[n] * s_re + sim_scr[q, rows, :]
                sre_scr[q, rows, :] = n_re
                sim_scr[q, rows, :] = n_im
                out += [n_re, n_im]
            return tuple(out)

        init = []
        for q in qs:
            init += [stre_scr[q], stim_scr[q]]
        fin = lax.fori_loop(0, tc, step, tuple(init), unroll=2)
        for n, q in enumerate(qs):
            stre_scr[q] = fin[2 * n]
            stim_scr[q] = fin[2 * n + 1]

    for b in range(nb):
        for s in range(n_slabs):
            cols = slice(s * k_slab, (s + 1) * k_slab)
            pieces_re, pieces_im = [], []
            for j in range(tiles_per_slab):
                q, rows = slab_rows(s * tiles_per_slab + j, b)
                pieces_re.append(sre_scr[q, rows, :].astype(BF16))
                pieces_im.append(sim_scr[q, rows, :].astype(BF16))
            y = (_dot(jnp.concatenate(pieces_re, axis=-1), cre_ref[s])
                 - _dot(jnp.concatenate(pieces_im, axis=-1), cim_ref[s]))
            y_scr[:, cols] = y + d_ref[:, cols] * u_ref[b, :, cols]
        y = _gelu_tanh(y_scr[...])
        gate = _dot(y.astype(BF16), wglu_ref[...]) + bglu_ref[...]
        o_ref[b] = (y * jax.nn.sigmoid(gate)).astype(o_ref.dtype)


def _pack_state_rows(a, nb):
    depth = a.shape[0]
    pack = SUBLANES // nb
    n_groups = a.shape[1] // (LANES * pack)
    a = a.reshape(depth, n_groups, pack, 1, LANES)
    return jnp.broadcast_to(a, (depth, n_groups, pack, nb, LANES)).reshape(
        depth, n_groups, SUBLANES, LANES)


def ssm(u3, bbd, cre, cim, a_re, a_im, d_skip, w_glu16, b_glu, layer):
    nb, seq_len, d_ssm = u3.shape
    n_groups = a_re.shape[1]
    assert SUBLANES % nb == 0
    tc = _tile(seq_len, 128)
    pitch = tc + SUBLANES
    kern = functools.partial(_ssm_kernel, pitch=pitch, groups_per_step=min(4, n_groups))

    def full(a):
        return _layer_spec(layer, a.shape[1:])

    return pl.pallas_call(
        kern,
        grid=(seq_len // tc,),
        in_specs=[pl.BlockSpec((nb, tc, d_ssm), lambda c: (0, c, 0)),
                  full(bbd), full(cre), full(cim), full(a_re), full(a_im),
                  full(d_skip), full(w_glu16), full(b_glu)],
        out_specs=pl.BlockSpec((nb, tc, d_ssm), lambda c: (0, c, 0)),
        out_shape=jax.ShapeDtypeStruct((nb, seq_len, d_ssm), BF16),
        scratch_shapes=[pltpu.VMEM((n_groups, SUBLANES * pitch, LANES), F32),
                        pltpu.VMEM((n_groups, SUBLANES * pitch, LANES), F32),
                        pltpu.VMEM((n_groups, SUBLANES, LANES), F32),
                        pltpu.VMEM((n_groups, SUBLANES, LANES), F32),
                        pltpu.VMEM((tc, d_ssm), F32)],
        compiler_params=_params(("arbitrary",), 56),
        name="ssm_scan",
    )(u3, bbd, cre, cim, a_re, a_im, d_skip, w_glu16, b_glu)


def _mixer_out_kernel(*refs, routed):
    if routed:
        (yc_ref, ys_ref, wo_ref, x_ref, gt_ref, gpost_ref, gpre_ref, sc_ref,
         sh_ref, rw_ref, rb_ref, xo_ref, h_ref, lg_ref) = refs
    else:
        (yc_ref, ys_ref, wo_ref, x_ref, gt_ref, gpost_ref, gpre_ref, sc_ref,
         sh_ref, xo_ref, h_ref) = refs
    d_conv = yc_ref.shape[1]
    y = _dot(yc_ref[...], wo_ref[:d_conv, :]) + _dot(ys_ref[...], wo_ref[d_conv:, :])
    x_new = x_ref[...] + gt_ref[...] * (_rms(y) * gpost_ref[...])
    xo_ref[...] = x_new
    h = _rms(x_new) * gpre_ref[...] * (1.0 + sc_ref[...]) + sh_ref[...]
    h_ref[...] = h.astype(h_ref.dtype)
    if routed:
        w = rw_ref[...]
        h_hi = h.astype(BF16)
        w_hi = w.astype(BF16)
        h_lo = (h - h_hi.astype(F32)).astype(BF16)
        w_lo = (w - w_hi.astype(F32)).astype(BF16)
        lg_ref[...] = (_dot(h_hi, w_hi) + (_dot(h_hi, w_lo) + _dot(h_lo, w_hi))
                       + rb_ref[...])


def mixer_out(y_conv, y_ssm, w_out16, x, mod5, g_post, g_pre, layer, router=None, *,
              seq_len):
    t, d = x.shape
    d_conv = y_conv.shape[1]
    d_ssm = y_ssm.shape[1]
    tm = _tile(seq_len, 512)
    tiles_per_seq = seq_len // tm
    routed = router is not None

    def batch_of(i):
        return i // tiles_per_seq

    def row_spec(n):
        return pl.BlockSpec((tm, n), lambda i: (i, 0))

    def const_spec(shape):
        return pl.BlockSpec(shape, lambda i: (0, 0))

    in_specs = [row_spec(d_conv), row_spec(d_ssm),
                _layer_spec(layer, (d_conv + d_ssm, d), pipeline_mode=pl.Buffered(1)),
                row_spec(d), _mod_spec(layer, GT_MIX, d, batch_of),
                _layer_spec(layer, (1, d)), _layer_spec(layer, (1, d)),
                _mod_spec(layer, SC_FFN, d, batch_of), _mod_spec(layer, SH_FFN, d, batch_of)]
    args = [y_conv, y_ssm, w_out16, x, mod5, g_post, g_pre, mod5, mod5]
    out_specs = [row_spec(d), row_spec(d)]
    out_shape = [jax.ShapeDtypeStruct((t, d), F32),
                 jax.ShapeDtypeStruct((t, d), F32 if routed else BF16)]
    if routed:
        rw_pad, rb_pad = router
        in_specs += [const_spec(rw_pad.shape), const_spec(rb_pad.shape)]
        args += [rw_pad, rb_pad]
        out_specs.append(row_spec(LANES))
        out_shape.append(jax.ShapeDtypeStruct((t, LANES), F32))
    return pl.pallas_call(
        functools.partial(_mixer_out_kernel, routed=routed),
        grid=(t // tm,),
        in_specs=in_specs, out_specs=out_specs, out_shape=out_shape,
        compiler_params=_params(("arbitrary",), 56),
        name="mixer_out_routed" if routed else "mixer_out",
    )(*args)


def _ffn_kernel(h_ref, wg_ref, wu_ref, wd_ref, x_ref, gt_ref, gpost_ref, o_ref, acc_ref):
    k = pl.program_id(1)

    @pl.when(k == 0)
    def _():
        acc_ref[...] = jnp.zeros(acc_ref.shape, F32)

    h = h_ref[...]
    gate = _dot(h, wg_ref[...])
    act = (gate * jax.nn.sigmoid(gate)) * _dot(h, wu_ref[...])
    acc_ref[...] += _dot(act.astype(BF16), wd_ref[...])

    @pl.when(k == pl.num_programs(1) - 1)
    def _():
        o_ref[...] = x_ref[...] + gt_ref[...] * (_rms(acc_ref[...]) * gpost_ref[...])


def ffn_dense(h16, wg16, wu16, wd16, x, mod5, g_post, layer, ffn_index, *, seq_len):
    t, d = x.shape
    d_ff = wg16.shape[2]
    tm = _tile(seq_len, 512)
    tk = _tile(d_ff, 512)
    tiles_per_seq = seq_len // tm
    return pl.pallas_call(
        _ffn_kernel,
        grid=(t // tm, d_ff // tk),
        in_specs=[
            pl.BlockSpec((tm, d), lambda i, k: (i, 0)),
            pl.BlockSpec((None, d, tk), lambda i, k: (ffn_index, 0, k)),
            pl.BlockSpec((None, d, tk), lambda i, k: (ffn_index, 0, k)),
            pl.BlockSpec((None, tk, d), lambda i, k: (ffn_index, k, 0)),
            pl.BlockSpec((tm, d), lambda i, k: (i, 0)),
            _mod_spec(layer, GT_FFN, d, lambda i: i // tiles_per_seq),
            _layer_spec(layer, (1, d)),
        ],
        out_specs=pl.BlockSpec((tm, d), lambda i, k: (i, 0)),
        out_shape=jax.ShapeDtypeStruct((t, d), F32),
        scratch_shapes=[pltpu.VMEM((tm, d), F32)],
        compiler_params=_params(("arbitrary", "arbitrary"), 48),
        name="ffn_dense",
    )(h16, wg16, wu16, wd16, x, mod5, g_post)


def _router_kernel(lg_ref, ri_ref, rw_ref, cnt_ref, carry_scr, *, n_experts):
    @pl.when(pl.program_id(0) == 0)
    def _():
        carry_scr[...] = jnp.zeros(carry_scr.shape, F32)

    tm = lg_ref.shape[0]
    lane = lax.broadcasted_iota(jnp.int32, (tm, LANES), 1)
    neg = jnp.float32(-jnp.inf)
    logit = jnp.where(lane < n_experts, lg_ref[...], neg)
    m1 = jnp.max(logit, axis=-1, keepdims=True)
    i1 = jnp.min(jnp.where(logit == m1, lane, LANES), axis=-1, keepdims=True)
    rest = jnp.where(lane == i1, neg, logit)
    m2 = jnp.max(rest, axis=-1, keepdims=True)
    i2 = jnp.min(jnp.where(rest == m2, lane, LANES), axis=-1, keepdims=True)
    e2 = jnp.exp(m2 - m1)
    w1 = 1.0 / (1.0 + e2)
    w2 = e2 / (1.0 + e2)
    hit1 = lane == i1
    hit2 = lane == i2
    onehot = jnp.where(hit1 | hit2, 1.0, 0.0)
    r = lax.broadcasted_iota(jnp.int32, (tm, tm), 0)
    c = lax.broadcasted_iota(jnp.int32, (tm, tm), 1)
    earlier = jnp.where(c < r, 1.0, 0.0).astype(BF16)
    carry = carry_scr[0:1, :]
    before = _dot(earlier, onehot.astype(BF16)) + carry
    pos1 = jnp.sum(jnp.where(hit1, before, 0.0), axis=-1, keepdims=True).astype(jnp.int32)
    pos2 = jnp.sum(jnp.where(hit2, before, 0.0), axis=-1, keepdims=True).astype(jnp.int32)
    total = carry + jnp.sum(onehot, axis=0, keepdims=True)
    carry_scr[...] = jnp.broadcast_to(total, carry_scr.shape)
    cnt_ref[...] = jnp.broadcast_to(total, cnt_ref.shape).astype(jnp.int32)
    zero_i = jnp.zeros((tm, LANES), jnp.int32)
    ri_ref[...] = jnp.where(lane == 0, i1, jnp.where(lane == 1, i2,
                            jnp.where(lane == 2, pos1, jnp.where(lane == 3, pos2, zero_i))))
    rw_ref[...] = jnp.where(lane == 0, w1, jnp.where(lane == 1, w2, 0.0))


def router(logits, n_experts):
    t = logits.shape[0]
    tm = _tile(t, 256)
    return pl.pallas_call(
        functools.partial(_router_kernel, n_experts=n_experts),
        grid=(t // tm,),
        in_specs=[pl.BlockSpec((tm, LANES), lambda i: (i, 0))],
        out_specs=[pl.BlockSpec((tm, LANES), lambda i: (i, 0)),
                   pl.BlockSpec((tm, LANES), lambda i: (i, 0)),
                   pl.BlockSpec((SUBLANES, LANES), lambda i: (0, 0))],
        out_shape=[jax.ShapeDtypeStruct((t, LANES), jnp.int32),
                   jax.ShapeDtypeStruct((t, LANES), F32),
                   jax.ShapeDtypeStruct((SUBLANES, LANES), jnp.int32)],
        scratch_shapes=[pltpu.VMEM((SUBLANES, LANES), F32)],
        compiler_params=_params(("arbitrary",), 32),
        name="router_top2",
    )(logits)


def _moe_ffn_kernel(ge_ref, ns_ref, nu_ref, tok_ref, h_hbm, wg_ref, wu_ref, wd_ref, o_ref,
                    h_scr, gbuf, wg16, wu16, wd16, sem, *, rows_per_step):
    g = pl.program_id(0)
    k = pl.program_id(1)
    n_used = nu_ref[0]
    n_sub_max, ts, _ = h_scr.shape
    group_rows = n_sub_max * ts
    nk = pl.num_programs(1)
    n_gather = gbuf.shape[0]

    def row_copy(src_row, dst_row):
        return pltpu.make_async_copy(h_hbm.at[pl.ds(src_row, 1)],
                                     gbuf.at[pl.ds(dst_row, 1)], sem)

    def for_each_sub_tile(group, fn):
        for j in range(n_sub_max):
            @pl.when(j < ns_ref[group])
            def _(j=j):
                fn(j)

    def wait_rows():
        def body(i, carry):
            row_copy(0, i).wait()
            return carry
        lax.fori_loop(0, n_gather, body, 0, unroll=8)

    @pl.when(k == 0)
    def _():
        o_ref[...] = jnp.zeros(o_ref.shape, o_ref.dtype)

    @pl.when(g < n_used)
    def _():
        @pl.when(k == 0)
        def _():
            @pl.when(g == 0)
            def _():
                def body(i, carry):
                    row_copy(tok_ref[i], i).start()
                    return carry
                lax.fori_loop(0, n_gather, body, 0, unroll=8)

            wait_rows()

            def cast(j):
                h_scr[j] = gbuf[j * ts:(j + 1) * ts, :].astype(BF16)
            for_each_sub_tile(g, cast)

        next_base = jnp.minimum(g + 1, n_used - 1) * group_rows + k * rows_per_step

        def sub_tile(j):
            if j == 0:
                wg16[...] = wg_ref[...].astype(BF16)
                wu16[...] = wu_ref[...].astype(BF16)
                wd16[...] = wd_ref[...].astype(BF16)
                for i in range(rows_per_step):
                    row_copy(tok_ref[next_base + i], k * rows_per_step + i).start()
            h = h_scr[j]
            gate = _dot(h, wg16[...])
            act = (gate * jax.nn.sigmoid(gate)) * _dot(h, wu16[...])
            o_ref[j * ts:(j + 1) * ts, :] += _dot(act.astype(BF16), wd16[...])
        for_each_sub_tile(g, sub_tile)

        @pl.when((g == n_used - 1) & (k == nk - 1))
        def _():
            wait_rows()


def moe_ffn(group_expert, group_subs, n_used, row_tok, h, w_gate, w_up, w_down, layer, *,
            sub_rows, n_sub_max):
    d = h.shape[1]
    d_ff = w_gate.shape[3]
    tk = _tile(d_ff, 256)
    nk = d_ff // tk
    group_rows = sub_rows * n_sub_max
    rows_per_step = -(-group_rows // nk)
    n_gather = rows_per_step * nk
    n_rows = group_expert.shape[0] * group_rows
    assert row_tok.shape[0] - n_rows >= n_gather - group_rows

    def ff_idx(g, k, nu):
        return jnp.where(g < nu[0], k, nk - 1)

    return pl.pallas_call(
        functools.partial(_moe_ffn_kernel, rows_per_step=rows_per_step),
        grid_spec=pltpu.PrefetchScalarGridSpec(
            num_scalar_prefetch=4,
            grid=(n_rows // group_rows, nk),
            in_specs=[
                pl.BlockSpec(memory_space=pl.ANY),
                pl.BlockSpec((None, None, d, tk),
                             lambda g, k, ge, ns, nu, tok: (layer, ge[g], 0, ff_idx(g, k, nu))),
                pl.BlockSpec((None, None, d, tk),
                             lambda g, k, ge, ns, nu, tok: (layer, ge[g], 0, ff_idx(g, k, nu))),
                pl.BlockSpec((None, None, tk, d),
                             lambda g, k, ge, ns, nu, tok: (layer, ge[g], ff_idx(g, k, nu), 0)),
            ],
            out_specs=pl.BlockSpec((group_rows, d), lambda g, k, ge, ns, nu, tok: (g, 0)),
            scratch_shapes=[pltpu.VMEM((n_sub_max, sub_rows, d), BF16),
                            pltpu.VMEM((n_gather, d), F32),
                            pltpu.VMEM((d, tk), BF16), pltpu.VMEM((d, tk), BF16),
                            pltpu.VMEM((tk, d), BF16),
                            pltpu.SemaphoreType.DMA(())],
        ),
        out_shape=jax.ShapeDtypeStruct((n_rows, d), F32),
        compiler_params=_params(("arbitrary", "arbitrary"), 56),
        name="moe_ffn",
    )(group_expert, group_subs, n_used, row_tok, h, w_gate, w_up, w_down)


def _moe_combine_kernel(d0_ref, d1_ref, ys_ref, rw_ref, x_ref, gt_ref, gpost_ref, o_ref,
                        buf, sem):
    tm = x_ref.shape[0]
    base = pl.program_id(0) * tm

    def row_copy(src_row, slot, t):
        return pltpu.make_async_copy(ys_ref.at[pl.ds(src_row, 1)],
                                     buf.at[slot, pl.ds(t, 1)], sem)

    def issue(t, carry):
        row_copy(d0_ref[base + t], 0, t).start()
        row_copy(d1_ref[base + t], 1, t).start()
        return carry

    lax.fori_loop(0, tm, issue, 0)

    def drain(t, carry):
        row_copy(0, 0, t).wait()
        row_copy(0, 1, t).wait()
        return carry

    lax.fori_loop(0, tm, drain, 0)
    y = buf[0] * rw_ref[:, 0:1] + buf[1] * rw_ref[:, 1:2]
    o_ref[...] = x_ref[...] + gt_ref[...] * (_rms(y) * gpost_ref[...])


def moe_combine(dest0, dest1, ys, route_w, x, mod5, g_post, layer, *, seq_len):
    t, d = x.shape
    tm = _tile(seq_len, 256)
    tiles_per_seq = seq_len // tm
    return pl.pallas_call(
        _moe_combine_kernel,
        grid_spec=pltpu.PrefetchScalarGridSpec(
            num_scalar_prefetch=2,
            grid=(t // tm,),
            in_specs=[
                pl.BlockSpec(memory_space=pl.ANY),
                pl.BlockSpec((tm, LANES), lambda i, d0, d1: (i, 0)),
                pl.BlockSpec((tm, d), lambda i, d0, d1: (i, 0)),
                _mod_spec(layer, GT_FFN, d, lambda i: i // tiles_per_seq),
                _layer_spec(layer, (1, d)),
            ],
            out_specs=pl.BlockSpec((tm, d), lambda i, d0, d1: (i, 0)),
            scratch_shapes=[pltpu.VMEM((2, tm, d), F32), pltpu.SemaphoreType.DMA(())],
        ),
        out_shape=jax.ShapeDtypeStruct((t, d), F32),
        compiler_params=_params(("arbitrary",), 32),
        name="moe_combine",
    )(dest0, dest1, ys, route_w, x, mod5, g_post)


MOE_SUB_ROWS = 512
MOE_GROUP_SUBS = 2


def moe_layer(h, logits, x, mod5, g_post, w_gate, w_up, w_down, layer, moe_index, *,
              seq_len):
    t, d = h.shape
    n_experts = w_gate.shape[1]
    ts = min(MOE_SUB_ROWS, t)
    group_rows = ts * MOE_GROUP_SUBS
    route_i, route_w, counts = router(logits, n_experts)
    e0, e1, pos0, pos1 = (route_i[:, k] for k in range(4))
    counts = counts[0, :n_experts]
    groups_e = (counts + group_rows - 1) // group_rows
    g_end = jnp.cumsum(groups_e)
    g_start = g_end - groups_e
    dest0 = g_start[e0] * group_rows + pos0
    dest1 = g_start[e1] * group_rows + pos1
    n_groups = -(-(2 * t) // group_rows) + n_experts
    n_rows = n_groups * group_rows
    tok = jnp.arange(t, dtype=jnp.int32)
    row_tok = jnp.zeros((n_rows + group_rows,), jnp.int32).at[
        jnp.concatenate([dest0, dest1])].set(jnp.concatenate([tok, tok]))
    n_used = g_end[-1].astype(jnp.int32)
    gid = jnp.arange(n_groups, dtype=jnp.int32)
    group_e = jnp.clip(jnp.searchsorted(g_end, gid, side='right'), 0, n_experts - 1)
    group_e = jnp.where(gid < n_used, group_e, group_e[n_used - 1]).astype(jnp.int32)
    rows_in_group = jnp.clip(counts[group_e] - (gid - g_start[group_e]) * group_rows,
                             0, group_rows)
    group_subs = jnp.where(gid < n_used, (rows_in_group + ts - 1) // ts, 0).astype(jnp.int32)
    ys = moe_ffn(group_e, group_subs, n_used.reshape(1), row_tok, h, w_gate, w_up, w_down,
                 moe_index, sub_rows=ts, n_sub_max=MOE_GROUP_SUBS)
    return moe_combine(dest0, dest1, ys, route_w, x, mod5, g_post, layer, seq_len=seq_len)


def kernel(x, c, w_ada, b_ada, g_mix_pre, g_mix_post, g_ffn_pre, g_ffn_post, w_in, conv_w,
           ssm_a_re, ssm_a_im, ssm_log_dt, ssm_b_re, ssm_b_im, ssm_c_re, ssm_c_im, ssm_d,
           w_glu, b_glu, w_out, ffn_w_gate, ffn_w_up, ffn_w_down, router_w, router_b,
           moe_w_gate, moe_w_up, moe_w_down):
    nb, seq_len, d = x.shape
    depth = w_ada.shape[0]
    d_conv = conv_w.shape[1]
    g, p, h_dim = ssm_b_re.shape[1:]
    d_ssm = g * h_dim
    n_experts = router_w.shape[2]
    t = nb * seq_len
    groups_per_slab = max(1, min(g, MXU_DIM // h_dim))

    c_pad = jnp.zeros((SUBLANES, d), F32).at[:nb].set(c)
    mod = adaln_mod(c_pad, w_ada, b_ada)
    mod5 = mod.reshape(depth, SUBLANES, 6, 1, d)
    xf = x.reshape(t, d)

    def rows(v):
        return v.reshape(v.shape[0], 1, v.shape[1])
    g_mix_pre, g_mix_post, g_ffn_pre, g_ffn_post = (
        rows(v) for v in (g_mix_pre, g_mix_post, g_ffn_pre, g_ffn_post))
    w_in16 = w_in.astype(BF16)
    w_out16 = w_out.astype(BF16)
    w_glu16 = w_glu.astype(BF16)
    ffn_wg16, ffn_wu16, ffn_wd16 = (w.astype(BF16) for w in (ffn_w_gate, ffn_w_up, ffn_w_down))
    conv_wt = jnp.swapaxes(conv_w, 1, 2)
    bb_re, bb_im, ab_re, ab_im = ssm_prep(ssm_a_re, ssm_a_im, ssm_log_dt, ssm_b_re, ssm_b_im)
    bbd = jnp.concatenate(
        [_block_diag_slabs(jnp.swapaxes(bb_re, 2, 3), groups_per_slab),
         _block_diag_slabs(jnp.swapaxes(bb_im, 2, 3), groups_per_slab)],
        axis=-1).astype(BF16)
    cre = _block_diag_slabs(jnp.swapaxes(ssm_c_re, 2, 3), groups_per_slab).astype(BF16)
    cim = _block_diag_slabs(jnp.swapaxes(ssm_c_im, 2, 3), groups_per_slab).astype(BF16)
    a_re = _pack_state_rows(ab_re.reshape(depth, g * p), nb)
    a_im = _pack_state_rows(ab_im.reshape(depth, g * p), nb)
    d_skip = ssm_d.reshape(depth, 1, d_ssm)
    b_glu = rows(b_glu)
    n_moe = router_w.shape[0]
    rw_pad = jnp.zeros((n_moe, d, LANES), F32).at[:, :, :n_experts].set(router_w)
    rb_pad = jnp.zeros((n_moe, 1, LANES), F32).at[:, 0, :n_experts].set(router_b)

    for l in range(depth):
        y_conv, u = mixer_in(xf, mod5, g_mix_pre, w_in16, conv_wt, l, seq_len=seq_len)
        y_ssm = ssm(u.reshape(nb, seq_len, d_ssm), bbd, cre, cim, a_re, a_im, d_skip,
                    w_glu16, b_glu, l).reshape(t, d_ssm)
        routed = l % 2 == 1
        i = l // 2
        rt = (rw_pad[i], rb_pad[i]) if routed else None
        outs = mixer_out(y_conv, y_ssm, w_out16, xf, mod5, g_mix_post, g_ffn_pre, l, rt,
                         seq_len=seq_len)
        if routed:
            xf, h, logits = outs
            xf = moe_layer(h, logits, xf, mod5, g_ffn_post, moe_w_gate, moe_w_up, moe_w_down,
                           l, i, seq_len=seq_len)
        else:
            xf, h = outs
            xf = ffn_dense(h, ffn_wg16, ffn_wu16, ffn_wd16, xf, mod5, g_ffn_post, l, i,
                           seq_len=seq_len)
    return xf.reshape(nb, seq_len, d)
```

```python
import functools

import jax
import jax.numpy as jnp
from jax import lax
from jax.experimental import pallas as pl
from jax.experimental.pallas import tpu as pltpu

F32 = jnp.float32
BF16 = jnp.bfloat16
NORM_EPS = 1e-6
LANES = 128
SUBLANES = 8
MXU_DIM = 256
EPILOGUE_ROWS = 64
MIB = 1024 * 1024


def _params(semantics, vmem_mib):
    return pltpu.CompilerParams(dimension_semantics=semantics,
                                vmem_limit_bytes=vmem_mib * MIB)


def _dot(a, b):
    return jnp.dot(a, b, preferred_element_type=F32)


def _rms(x):
    return x * lax.rsqrt(jnp.mean(x * x, axis=-1, keepdims=True) + NORM_EPS)


def _tile(n, want):
    t = min(n, want)
    while n % t:
        t -= 1
    return t


def _adaln_kernel(c_ref, w_ref, b_ref, o_ref):
    c = c_ref[...]
    c_act = c * jax.nn.sigmoid(c)
    c_hi = c_act.astype(BF16)
    c_lo = (c_act - c_hi.astype(F32)).astype(BF16)
    w16 = w_ref[...].astype(BF16)
    o_ref[...] = (_dot(c_hi, w16) + _dot(c_lo, w16)) + b_ref[...]


def adaln_mod(c_pad, w_ada, b_ada):
    depth, d, n = w_ada.shape
    rows = c_pad.shape[0]
    tn = _tile(n, 1024)
    return pl.pallas_call(
        _adaln_kernel,
        grid=(depth, n // tn),
        in_specs=[
            pl.BlockSpec((rows, d), lambda l, j: (0, 0)),
            pl.BlockSpec((None, d, tn), lambda l, j: (l, 0, j)),
            pl.BlockSpec((None, 1, tn), lambda l, j: (l, 0, j)),
        ],
        out_specs=pl.BlockSpec((None, rows, tn), lambda l, j: (l, 0, j)),
        out_shape=jax.ShapeDtypeStruct((depth, rows, n), F32),
        compiler_params=_params(("arbitrary", "arbitrary"), 32),
        name="adaln_mod",
    )(c_pad, w_ada, b_ada.reshape(depth, 1, n))


def _mixer_in_kernel(x_ref, sc_ref, sh_ref, g_ref, w_ref, cw_ref, yconv_ref, u_ref,
                     halo_scr, *, tiles_per_seq, col_chunk):
    tm = x_ref.shape[0]
    d_conv = yconv_ref.shape[1]
    d_ssm = u_ref.shape[1]
    h = (_rms(x_ref[...]) * g_ref[...] * (1.0 + sc_ref[...]) + sh_ref[...]).astype(BF16)

    @pl.when(pl.program_id(0) % tiles_per_seq == 0)
    def _():
        halo_scr[...] = jnp.zeros(halo_scr.shape, F32)

    for c0 in range(0, d_conv, col_chunk):
        cols = slice(c0, c0 + col_chunk)
        gate_b = _dot(h, w_ref[:, c0:c0 + col_chunk])
        z = (_dot(h, w_ref[:, d_conv + c0:d_conv + c0 + col_chunk])
             * _dot(h, w_ref[:, 2 * d_conv + c0:2 * d_conv + c0 + col_chunk]))
        prev = halo_scr[:, cols]
        halo_scr[:, cols] = z[tm - SUBLANES:, :]
        w0 = cw_ref[0:1, cols]
        w1 = cw_ref[1:2, cols]
        w2 = cw_ref[2:3, cols]
        z1 = pltpu.roll(z, 1, 0)
        z2 = pltpu.roll(z, 2, 0)
        yconv_ref[:, cols] = (gate_b * (z * w2 + z1 * w1 + z2 * w0)).astype(yconv_ref.dtype)
        row = lax.broadcasted_iota(jnp.int32, (SUBLANES, col_chunk), 0)
        z1t = jnp.where(row < 1, pltpu.roll(prev, 1, 0), z1[:SUBLANES, :])
        z2t = jnp.where(row < 2, pltpu.roll(prev, 2, 0), z2[:SUBLANES, :])
        top = gate_b[:SUBLANES, :] * (z[:SUBLANES, :] * w2 + z1t * w1 + z2t * w0)
        yconv_ref[0:SUBLANES, cols] = top.astype(yconv_ref.dtype)

    for c0 in range(0, d_ssm, col_chunk):
        u_ref[:, c0:c0 + col_chunk] = _dot(
            h, w_ref[:, 3 * d_conv + c0:3 * d_conv + c0 + col_chunk])


SH_MIX, SC_MIX, GT_MIX, SH_FFN, SC_FFN, GT_FFN = range(6)


def _mod_spec(layer, chunk, d, batch_of):
    return pl.BlockSpec((None, None, None, 1, d),
                        lambda i, *_: (layer, batch_of(i), chunk, 0, 0))


def _layer_spec(layer, shape, **kw):
    zeros = (0,) * len(shape)
    return pl.BlockSpec((None,) + tuple(shape), lambda *_: (layer,) + zeros, **kw)


def mixer_in(x, mod5, g_pre, w_in16, conv_wt, layer, *, seq_len):
    t, d = x.shape
    d_in = w_in16.shape[2]
    d_conv = conv_wt.shape[2]
    d_ssm = d_in - 3 * d_conv
    tm = _tile(seq_len, 512)
    col_chunk = _tile(min(d_conv, d_ssm), 512)
    assert d_conv % col_chunk == 0 and d_ssm % col_chunk == 0
    tiles_per_seq = seq_len // tm

    def batch_of(i):
        return i // tiles_per_seq

    kern = functools.partial(_mixer_in_kernel, tiles_per_seq=tiles_per_seq,
                             col_chunk=col_chunk)
    return pl.pallas_call(
        kern,
        grid=(t // tm,),
        in_specs=[
            pl.BlockSpec((tm, d), lambda i: (i, 0)),
            _mod_spec(layer, SC_MIX, d, batch_of), _mod_spec(layer, SH_MIX, d, batch_of),
            _layer_spec(layer, (1, d)),
            _layer_spec(layer, (d, d_in), pipeline_mode=pl.Buffered(1)),
            _layer_spec(layer, (3, d_conv)),
        ],
        out_specs=[
            pl.BlockSpec((tm, d_conv), lambda i: (i, 0)),
            pl.BlockSpec((tm, d_ssm), lambda i: (i, 0)),
        ],
        out_shape=[jax.ShapeDtypeStruct((t, d_conv), BF16),
                   jax.ShapeDtypeStruct((t, d_ssm), F32)],
        scratch_shapes=[pltpu.VMEM((SUBLANES, d_conv), F32)],
        compiler_params=_params(("arbitrary",), 48),
        name="mixer_in",
    )(x, mod5, mod5, g_pre, w_in16, conv_wt)


def _ssm_prep_kernel(are_ref, aim_ref, ldt_ref, bre_ref, bim_ref,
                     bbre_ref, bbim_ref, abre_ref, abim_ref):
    lam_re = are_ref[...]
    lam_im = aim_ref[...]
    dt = jnp.exp(ldt_ref[...])
    mag = jnp.exp(lam_re * dt)
    ang = lam_im * dt
    ab_re = mag * jnp.cos(ang)
    ab_im = mag * jnp.sin(ang)
    den = lam_re * lam_re + lam_im * lam_im
    nr = ab_re - 1.0
    q_re = (nr * lam_re + ab_im * lam_im) / den
    q_im = (ab_im * lam_re - nr * lam_im) / den
    b_re = bre_ref[...]
    b_im = bim_ref[...]
    bbre_ref[...] = q_re * b_re - q_im * b_im
    bbim_ref[...] = q_re * b_im + q_im * b_re
    abre_ref[...] = ab_re
    abim_ref[...] = ab_im


def ssm_prep(a_re, a_im, log_dt, b_re, b_im):
    depth, g, p, h = b_re.shape
    ldt = jnp.broadcast_to(log_dt.reshape(depth, g, 1, 1), (depth, g, p, 1))

    def spec(last):
        return pl.BlockSpec((None, g, p, last), lambda l: (l, 0, 0, 0))

    return pl.pallas_call(
        _ssm_prep_kernel,
        grid=(depth,),
        in_specs=[spec(1), spec(1), spec(1), spec(h), spec(h)],
        out_specs=[spec(h), spec(h), spec(1), spec(1)],
        out_shape=[jax.ShapeDtypeStruct((depth, g, p, h), F32),
                   jax.ShapeDtypeStruct((depth, g, p, h), F32),
                   jax.ShapeDtypeStruct((depth, g, p, 1), F32),
                   jax.ShapeDtypeStruct((depth, g, p, 1), F32)],
        compiler_params=_params(("arbitrary",), 56),
        name="ssm_prep",
    )(a_re.reshape(depth, g, p, 1), a_im.reshape(depth, g, p, 1), ldt, b_re, b_im)


def _block_diag_slabs(w, groups_per_slab):
    depth, g, r, c = w.shape
    ns = g // groups_per_slab
    w = w.reshape(depth, ns, groups_per_slab, r, c)
    eye = jnp.eye(groups_per_slab, dtype=w.dtype)
    out = w[:, :, :, :, None, :] * eye[None, None, :, None, :, None]
    return out.reshape(depth, ns, groups_per_slab * r, groups_per_slab * c)


def _gelu_tanh(x):
    return 0.5 * x * (1.0 + jnp.tanh(0.7978845608028654 * (x + 0.044715 * x * x * x)))


def _ssm_kernel(u_ref, bbd_ref, cre_ref, cim_ref, are_ref, aim_ref, d_ref, wglu_ref,
                bglu_ref, o_ref, sre_scr, sim_scr, stre_scr, stim_scr, y_scr, *,
                pitch, groups_per_step):
    nb, tc, d_ssm = u_ref.shape
    n_slabs, k_slab, two_sw = bbd_ref.shape
    sw = two_sw // 2
    tiles_per_slab = sw // LANES
    pack = SUBLANES // nb
    n_groups = sre_scr.shape[0]

    def slab_rows(lane_tile, b):
        q, h = divmod(lane_tile, pack)
        r0 = (h * nb + b) * pitch
        return q, slice(r0, r0 + tc)

    @pl.when(pl.program_id(0) == 0)
    def _():
        stre_scr[...] = jnp.zeros(stre_scr.shape, F32)
        stim_scr[...] = jnp.zeros(stim_scr.shape, F32)

    for b in range(nb):
        for s in range(n_slabs):
            ub = u_ref[b, :, s * k_slab:(s + 1) * k_slab].astype(BF16)
            r = _dot(ub, bbd_ref[s])
            for j in range(tiles_per_slab):
                q, rows = slab_rows(s * tiles_per_slab + j, b)
                sre_scr[q, rows, :] = r[:, j * LANES:(j + 1) * LANES]
                sim_scr[q, rows, :] = r[:, sw + j * LANES:sw + (j + 1) * LANES]

    for q0 in range(0, n_groups, groups_per_step):
        qs = list(range(q0, min(q0 + groups_per_step, n_groups)))
        a_re = [are_ref[q] for q in qs]
        a_im = [aim_ref[q] for q in qs]

        def step(t, carry, qs=qs, a_re=a_re, a_im=a_im):
            rows = pl.ds(t, SUBLANES, stride=pitch)
            out = []
            for n, q in enumerate(qs):
                s_re, s_im = carry[2 * n], carry[2 * n + 1]
                n_re = a_re[n] * s_re - a_im[n] * s_im + sre_scr[q, rows, :]
                n_im = a_re[n] * s_im + a_im[n] * s_re + sim_scr[q, rows, :]
                sre_scr[q, rows, :] = n_re
                sim_scr[q, rows, :] = n_im
                out += [n_re, n_im]
            return tuple(out)

        init = []
        for q in qs:
            init += [stre_scr[q], stim_scr[q]]
        fin = lax.fori_loop(0, tc, step, tuple(init), unroll=2)
        for n, q in enumerate(qs):
            stre_scr[q] = fin[2 * n]
            stim_scr[q] = fin[2 * n + 1]

    for b in range(nb):
        for s in range(n_slabs):
            cols = slice(s * k_slab, (s + 1) * k_slab)
            pieces_re, pieces_im = [], []
            for j in range(tiles_per_slab):
                q, rows = slab_rows(s * tiles_per_slab + j, b)
                pieces_re.append(sre_scr[q, rows, :].astype(BF16))
                pieces_im.append(sim_scr[q, rows, :].astype(BF16))
            y = (_dot(jnp.concatenate(pieces_re, axis=-1), cre_ref[s])
                 - _dot(jnp.concatenate(pieces_im, axis=-1), cim_ref[s]))
            y_scr[:, cols] = y + d_ref[:, cols] * u_ref[b, :, cols]
        y = _gelu_tanh(y_scr[...])
        gate = _dot(y.astype(BF16), wglu_ref[...]) + bglu_ref[...]
        o_ref[b] = (y * jax.nn.sigmoid(gate)).astype(o_ref.dtype)


def _pack_state_rows(a, nb):
    depth = a.shape[0]
    pack = SUBLANES // nb
    n_groups = a.shape[1] // (LANES * pack)
    a = a.reshape(depth, n_groups, pack, 1, LANES)
    return jnp.broadcast_to(a, (depth, n_groups, pack, nb, LANES)).reshape(
        depth, n_groups, SUBLANES, LANES)


def ssm(u3, bbd, cre, cim, a_re, a_im, d_skip, w_glu16, b_glu, layer):
    nb, seq_len, d_ssm = u3.shape
    n_groups = a_re.shape[1]
    assert SUBLANES % nb == 0
    tc = _tile(seq_len, 128)
    pitch = tc + SUBLANES
    kern = functools.partial(_ssm_kernel, pitch=pitch, groups_per_step=min(4, n_groups))

    def full(a):
        return _layer_spec(layer, a.shape[1:])

    return pl.pallas_call(
        kern,
        grid=(seq_len // tc,),
        in_specs=[pl.BlockSpec((nb, tc, d_ssm), lambda c: (0, c, 0)),
                  full(bbd), full(cre), full(cim), full(a_re), full(a_im),
                  full(d_skip), full(w_glu16), full(b_glu)],
        out_specs=pl.BlockSpec((nb, tc, d_ssm), lambda c: (0, c, 0)),
        out_shape=jax.ShapeDtypeStruct((nb, seq_len, d_ssm), BF16),
        scratch_shapes=[pltpu.VMEM((n_groups, SUBLANES * pitch, LANES), F32),
                        pltpu.VMEM((n_groups, SUBLANES * pitch, LANES), F32),
                        pltpu.VMEM((n_groups, SUBLANES, LANES), F32),
                        pltpu.VMEM((n_groups, SUBLANES, LANES), F32),
                        pltpu.VMEM((tc, d_ssm), F32)],
        compiler_params=_params(("arbitrary",), 56),
        name="ssm_scan",
    )(u3, bbd, cre, cim, a_re, a_im, d_skip, w_glu16, b_glu)


def _mixer_out_kernel(*refs, routed):
    if routed:
        (yc_ref, ys_ref, wo_ref, x_ref, gt_ref, gpost_ref, gpre_ref, sc_ref,
         sh_ref, rw_ref, rb_ref, xo_ref, h_ref, lg_ref) = refs
    else:
        (yc_ref, ys_ref, wo_ref, x_ref, gt_ref, gpost_ref, gpre_ref, sc_ref,
         sh_ref, xo_ref, h_ref) = refs
    d_conv = yc_ref.shape[1]
    y = _dot(yc_ref[...], wo_ref[:d_conv, :]) + _dot(ys_ref[...], wo_ref[d_conv:, :])
    x_new = x_ref[...] + gt_ref[...] * (_rms(y) * gpost_ref[...])
    xo_ref[...] = x_new
    h = _rms(x_new) * gpre_ref[...] * (1.0 + sc_ref[...]) + sh_ref[...]
    h_ref[...] = h.astype(h_ref.dtype)
    if routed:
        w = rw_ref[...]
        h_hi = h.astype(BF16)
        w_hi = w.astype(BF16)
        h_lo = (h - h_hi.astype(F32)).astype(BF16)
        w_lo = (w - w_hi.astype(F32)).astype(BF16)
        lg_ref[...] = (_dot(h_hi, w_hi) + (_dot(h_hi, w_lo) + _dot(h_lo, w_hi))
                       + rb_ref[...])


def mixer_out(y_conv, y_ssm, w_out16, x, mod5, g_post, g_pre, layer, router=None, *,
              seq_len):
    t, d = x.shape
    d_conv = y_conv.shape[1]
    d_ssm = y_ssm.shape[1]
    tm = _tile(seq_len, 512)
    tiles_per_seq = seq_len // tm
    routed = router is not None

    def batch_of(i):
        return i // tiles_per_seq

    def row_spec(n):
        return pl.BlockSpec((tm, n), lambda i: (i, 0))

    def const_spec(shape):
        return pl.BlockSpec(shape, lambda i: (0, 0))

    in_specs = [row_spec(d_conv), row_spec(d_ssm),
                _layer_spec(layer, (d_conv + d_ssm, d), pipeline_mode=pl.Buffered(1)),
                row_spec(d), _mod_spec(layer, GT_MIX, d, batch_of),
                _layer_spec(layer, (1, d)), _layer_spec(layer, (1, d)),
                _mod_spec(layer, SC_FFN, d, batch_of), _mod_spec(layer, SH_FFN, d, batch_of)]
    args = [y_conv, y_ssm, w_out16, x, mod5, g_post, g_pre, mod5, mod5]
    out_specs = [row_spec(d), row_spec(d)]
    out_shape = [jax.ShapeDtypeStruct((t, d), F32),
                 jax.ShapeDtypeStruct((t, d), F32 if routed else BF16)]
    if routed:
        rw_pad, rb_pad = router
        in_specs += [const_spec(rw_pad.shape), const_spec(rb_pad.shape)]
        args += [rw_pad, rb_pad]
        out_specs.append(row_spec(LANES))
        out_shape.append(jax.ShapeDtypeStruct((t, LANES), F32))
    return pl.pallas_call(
        functools.partial(_mixer_out_kernel, routed=routed),
        grid=(t // tm,),
        in_specs=in_specs, out_specs=out_specs, out_shape=out_shape,
        compiler_params=_params(("arbitrary",), 56),
        name="mixer_out_routed" if routed else "mixer_out",
    )(*args)


def _ffn_kernel(h_ref, wg_ref, wu_ref, wd_ref, x_hbm, gt_ref, gpost_ref, o_ref,
                wg16, wu16, wd16, x_buf, sem, *, sub_rows):
    i = pl.program_id(0)
    k = pl.program_id(1)
    tm = o_ref.shape[0]
    x_copy = pltpu.make_async_copy(x_hbm.at[pl.ds(i * tm, tm)], x_buf, sem)

    @pl.when(k == 0)
    def _():
        o_ref[...] = jnp.zeros(o_ref.shape, F32)
        x_copy.start()

    wg16[...] = wg_ref[...].astype(BF16)
    wu16[...] = wu_ref[...].astype(BF16)
    wd16[...] = wd_ref[...].astype(BF16)
    for r0 in range(0, tm, sub_rows):
        h = h_ref[r0:r0 + sub_rows, :]
        gate = _dot(h, wg16[...])
        act = (gate * jax.nn.sigmoid(gate)) * _dot(h, wu16[...])
        o_ref[r0:r0 + sub_rows, :] += _dot(act.astype(BF16), wd16[...])

    @pl.when(k == pl.num_programs(1) - 1)
    def _():
        x_copy.wait()
        scale = gt_ref[...] * gpost_ref[...]
        for r0 in range(0, tm, EPILOGUE_ROWS):
            rows = slice(r0, r0 + EPILOGUE_ROWS)
            o_ref[rows, :] = x_buf[rows, :] + _rms(o_ref[rows, :]) * scale


def ffn_dense(h16, w_gate, w_up, w_down, x, mod5, g_post, layer, ffn_index, *, seq_len):
    t, d = x.shape
    d_ff = w_gate.shape[2]
    tm = _tile(seq_len, 1024)
    sub_rows = _tile(tm, 512)
    tk = _tile(d_ff, 256)
    tiles_per_seq = seq_len // tm
    return pl.pallas_call(
        functools.partial(_ffn_kernel, sub_rows=sub_rows),
        grid=(t // tm, d_ff // tk),
        in_specs=[
            pl.BlockSpec((tm, d), lambda i, k: (i, 0)),
            pl.BlockSpec((None, d, tk), lambda i, k: (ffn_index, 0, k)),
            pl.BlockSpec((None, d, tk), lambda i, k: (ffn_index, 0, k)),
            pl.BlockSpec((None, tk, d), lambda i, k: (ffn_index, k, 0)),
            pl.BlockSpec(memory_space=pl.ANY),
            _mod_spec(layer, GT_FFN, d, lambda i: i // tiles_per_seq),
            _layer_spec(layer, (1, d)),
        ],
        out_specs=pl.BlockSpec((tm, d), lambda i, k: (i, 0)),
        out_shape=jax.ShapeDtypeStruct((t, d), F32),
        scratch_shapes=[pltpu.VMEM((d, tk), BF16), pltpu.VMEM((d, tk), BF16),
                        pltpu.VMEM((tk, d), BF16), pltpu.VMEM((tm, d), F32),
                        pltpu.SemaphoreType.DMA(())],
        compiler_params=_params(("arbitrary", "arbitrary"), 56),
        name="ffn_dense",
    )(h16, w_gate, w_up, w_down, x, mod5, g_post)


def _router_kernel(lg_ref, ri_ref, rw_ref, cnt_ref, carry_scr, *, n_experts):
    @pl.when(pl.program_id(0) == 0)
    def _():
        carry_scr[...] = jnp.zeros(carry_scr.shape, F32)

    tm = lg_ref.shape[0]
    lane = lax.broadcasted_iota(jnp.int32, (tm, LANES), 1)
    neg = jnp.float32(-jnp.inf)
    logit = jnp.where(lane < n_experts, lg_ref[...], neg)
    m1 = jnp.max(logit, axis=-1, keepdims=True)
    i1 = jnp.min(jnp.where(logit == m1, lane, LANES), axis=-1, keepdims=True)
    rest = jnp.where(lane == i1, neg, logit)
    m2 = jnp.max(rest, axis=-1, keepdims=True)
    i2 = jnp.min(jnp.where(rest == m2, lane, LANES), axis=-1, keepdims=True)
    e2 = jnp.exp(m2 - m1)
    w1 = 1.0 / (1.0 + e2)
    w2 = e2 / (1.0 + e2)
    hit1 = lane == i1
    hit2 = lane == i2
    onehot = jnp.where(hit1 | hit2, 1.0, 0.0)
    r = lax.broadcasted_iota(jnp.int32, (tm, tm), 0)
    c = lax.broadcasted_iota(jnp.int32, (tm, tm), 1)
    earlier = jnp.where(c < r, 1.0, 0.0).astype(BF16)
    carry = carry_scr[0:1, :]
    before = _dot(earlier, onehot.astype(BF16)) + carry
    pos1 = jnp.sum(jnp.where(hit1, before, 0.0), axis=-1, keepdims=True).astype(jnp.int32)
    pos2 = jnp.sum(jnp.where(hit2, before, 0.0), axis=-1, keepdims=True).astype(jnp.int32)
    total = carry + jnp.sum(onehot, axis=0, keepdims=True)
    carry_scr[...] = jnp.broadcast_to(total, carry_scr.shape)
    cnt_ref[...] = jnp.broadcast_to(total, cnt_ref.shape).astype(jnp.int32)
    zero_i = jnp.zeros((tm, LANES), jnp.int32)
    ri_ref[...] = jnp.where(lane == 0, i1, jnp.where(lane == 1, i2,
                            jnp.where(lane == 2, pos1, jnp.where(lane == 3, pos2, zero_i))))
    rw_ref[...] = jnp.where(lane == 0, w1, jnp.where(lane == 1, w2, 0.0))


def router(logits, n_experts):
    t = logits.shape[0]
    tm = _tile(t, 256)
    return pl.pallas_call(
        functools.partial(_router_kernel, n_experts=n_experts),
        grid=(t // tm,),
        in_specs=[pl.BlockSpec((tm, LANES), lambda i: (i, 0))],
        out_specs=[pl.BlockSpec((tm, LANES), lambda i: (i, 0)),
                   pl.BlockSpec((tm, LANES), lambda i: (i, 0)),
                   pl.BlockSpec((SUBLANES, LANES), lambda i: (0, 0))],
        out_shape=[jax.ShapeDtypeStruct((t, LANES), jnp.int32),
                   jax.ShapeDtypeStruct((t, LANES), F32),
                   jax.ShapeDtypeStruct((SUBLANES, LANES), jnp.int32)],
        scratch_shapes=[pltpu.VMEM((SUBLANES, LANES), F32)],
        compiler_params=_params(("arbitrary",), 32),
        name="router_top2",
    )(logits)


def _moe_ffn_kernel(ge_ref, ns_ref, nu_ref, tok_ref, h_hbm, wg_ref, wu_ref, wd_ref, o_ref,
                    h_scr, gbuf, wg16, wu16, wd16, sem, *, rows_per_step):
    g = pl.program_id(0)
    k = pl.program_id(1)
    n_used = nu_ref[0]
    n_sub_max, ts, _ = h_scr.shape
    group_rows = n_sub_max * ts
    nk = pl.num_programs(1)
    n_gather = gbuf.shape[0]

    def row_copy(src_row, dst_row):
        return pltpu.make_async_copy(h_hbm.at[pl.ds(src_row, 1)],
                                     gbuf.at[pl.ds(dst_row, 1)], sem)

    def for_each_sub_tile(group, fn):
        for j in range(n_sub_max):
            @pl.when(j < ns_ref[group])
            def _(j=j):
                fn(j)

    def wait_rows():
        def body(i, carry):
            row_copy(0, i).wait()
            return carry
        lax.fori_loop(0, n_gather, body, 0, unroll=8)

    @pl.when(k == 0)
    def _():
        o_ref[...] = jnp.zeros(o_ref.shape, o_ref.dtype)

    @pl.when(g < n_used)
    def _():
        @pl.when(k == 0)
        def _():
            @pl.when(g == 0)
            def _():
                def body(i, carry):
                    row_copy(tok_ref[i], i).start()
                    return carry
                lax.fori_loop(0, n_gather, body, 0, unroll=8)

            wait_rows()

            def cast(j):
                h_scr[j] = gbuf[j * ts:(j + 1) * ts, :].astype(BF16)
            for_each_sub_tile(g, cast)

        next_base = jnp.minimum(g + 1, n_used - 1) * group_rows + k * rows_per_step

        def sub_tile(j):
            if j == 0:
                wg16[...] = wg_ref[...].astype(BF16)
                wu16[...] = wu_ref[...].astype(BF16)
                wd16[...] = wd_ref[...].astype(BF16)
                for i in range(rows_per_step):
                    row_copy(tok_ref[next_base + i], k * rows_per_step + i).start()
            h = h_scr[j]
            gate = _dot(h, wg16[...])
            act = (gate * jax.nn.sigmoid(gate)) * _dot(h, wu16[...])
            o_ref[j * ts:(j + 1) * ts, :] += _dot(act.astype(BF16), wd16[...])
        for_each_sub_tile(g, sub_tile)

        @pl.when((g == n_used - 1) & (k == nk - 1))
        def _():
            wait_rows()


def moe_ffn(group_expert, group_subs, n_used, row_tok, h, w_gate, w_up, w_down, layer, *,
            sub_rows, n_sub_max):
    d = h.shape[1]
    d_ff = w_gate.shape[3]
    tk = _tile(d_ff, 256)
    nk = d_ff // tk
    group_rows = sub_rows * n_sub_max
    rows_per_step = -(-group_rows // nk)
    n_gather = rows_per_step * nk
    n_rows = group_expert.shape[0] * group_rows
    assert row_tok.shape[0] - n_rows >= n_gather - group_rows

    def ff_idx(g, k, nu):
        return jnp.where(g < nu[0], k, nk - 1)

    return pl.pallas_call(
        functools.partial(_moe_ffn_kernel, rows_per_step=rows_per_step),
        grid_spec=pltpu.PrefetchScalarGridSpec(
            num_scalar_prefetch=4,
            grid=(n_rows // group_rows, nk),
            in_specs=[
                pl.BlockSpec(memory_space=pl.ANY),
                pl.BlockSpec((None, None, d, tk),
                             lambda g, k, ge, ns, nu, tok: (layer, ge[g], 0, ff_idx(g, k, nu))),
                pl.BlockSpec((None, None, d, tk),
                             lambda g, k, ge, ns, nu, tok: (layer, ge[g], 0, ff_idx(g, k, nu))),
                pl.BlockSpec((None, None, tk, d),
                             lambda g, k, ge, ns, nu, tok: (layer, ge[g], ff_idx(g, k, nu), 0)),
            ],
            out_specs=pl.BlockSpec((group_rows, d), lambda g, k, ge, ns, nu, tok: (g, 0)),
            scratch_shapes=[pltpu.VMEM((n_sub_max, sub_rows, d), BF16),
                            pltpu.VMEM((n_gather, d), F32),
                            pltpu.VMEM((d, tk), BF16), pltpu.VMEM((d, tk), BF16),
                            pltpu.VMEM((tk, d), BF16),
                            pltpu.SemaphoreType.DMA(())],
        ),
        out_shape=jax.ShapeDtypeStruct((n_rows, d), F32),
        compiler_params=_params(("arbitrary", "arbitrary"), 56),
        name="moe_ffn",
    )(group_expert, group_subs, n_used, row_tok, h, w_gate, w_up, w_down)


def _moe_combine_kernel(d0_ref, d1_ref, ys_ref, rw_ref, x_ref, gt_ref, gpost_ref, o_ref,
                        buf, sem):
    tm = x_ref.shape[0]
    base = pl.program_id(0) * tm

    def row_copy(src_row, slot, t):
        return pltpu.make_async_copy(ys_ref.at[pl.ds(src_row, 1)],
                                     buf.at[slot, pl.ds(t, 1)], sem)

    def issue(t, carry):
        row_copy(d0_ref[base + t], 0, t).start()
        row_copy(d1_ref[base + t], 1, t).start()
        return carry

    lax.fori_loop(0, tm, issue, 0)

    def drain(t, carry):
        row_copy(0, 0, t).wait()
        row_copy(0, 1, t).wait()
        return carry

    lax.fori_loop(0, tm, drain, 0)
    y = buf[0] * rw_ref[:, 0:1] + buf[1] * rw_ref[:, 1:2]
    o_ref[...] = x_ref[...] + gt_ref[...] * (_rms(y) * gpost_ref[...])


def moe_combine(dest0, dest1, ys, route_w, x, mod5, g_post, layer, *, seq_len):
    t, d = x.shape
    tm = _tile(seq_len, 256)
    tiles_per_seq = seq_len // tm
    return pl.pallas_call(
        _moe_combine_kernel,
        grid_spec=pltpu.PrefetchScalarGridSpec(
            num_scalar_prefetch=2,
            grid=(t // tm,),
            in_specs=[
                pl.BlockSpec(memory_space=pl.ANY),
                pl.BlockSpec((tm, LANES), lambda i, d0, d1: (i, 0)),
                pl.BlockSpec((tm, d), lambda i, d0, d1: (i, 0)),
                _mod_spec(layer, GT_FFN, d, lambda i: i // tiles_per_seq),
                _layer_spec(layer, (1, d)),
            ],
            out_specs=pl.BlockSpec((tm, d), lambda i, d0, d1: (i, 0)),
            scratch_shapes=[pltpu.VMEM((2, tm, d), F32), pltpu.SemaphoreType.DMA(())],
        ),
        out_shape=jax.ShapeDtypeStruct((t, d), F32),
        compiler_params=_params(("arbitrary",), 32),
        name="moe_combine",
    )(dest0, dest1, ys, route_w, x, mod5, g_post)


MOE_SUB_ROWS = 512
MOE_GROUP_SUBS = 2


def moe_layer(h, logits, x, mod5, g_post, w_gate, w_up, w_down, layer, moe_index, *,
              seq_len):
    t, d = h.shape
    n_experts = w_gate.shape[1]
    ts = min(MOE_SUB_ROWS, t)
    group_rows = ts * MOE_GROUP_SUBS
    route_i, route_w, counts = router(logits, n_experts)
    e0, e1, pos0, pos1 = (route_i[:, k] for k in range(4))
    counts = counts[0, :n_experts]
    groups_e = (counts + group_rows - 1) // group_rows
    g_end = jnp.cumsum(groups_e)
    g_start = g_end - groups_e
    dest0 = g_start[e0] * group_rows + pos0
    dest1 = g_start[e1] * group_rows + pos1
    n_groups = -(-(2 * t) // group_rows) + n_experts
    n_rows = n_groups * group_rows
    tok = jnp.arange(t, dtype=jnp.int32)
    row_tok = jnp.zeros((n_rows + group_rows,), jnp.int32).at[
        jnp.concatenate([dest0, dest1])].set(jnp.concatenate([tok, tok]))
    n_used = g_end[-1].astype(jnp.int32)
    gid = jnp.arange(n_groups, dtype=jnp.int32)
    group_e = jnp.clip(jnp.searchsorted(g_end, gid, side='right'), 0, n_experts - 1)
    group_e = jnp.where(gid < n_used, group_e, group_e[n_used - 1]).astype(jnp.int32)
    rows_in_group = jnp.clip(counts[group_e] - (gid - g_start[group_e]) * group_rows,
                             0, group_rows)
    group_subs = jnp.where(gid < n_used, (rows_in_group + ts - 1) // ts, 0).astype(jnp.int32)
    ys = moe_ffn(group_e, group_subs, n_used.reshape(1), row_tok, h, w_gate, w_up, w_down,
                 moe_index, sub_rows=ts, n_sub_max=MOE_GROUP_SUBS)
    return moe_combine(dest0, dest1, ys, route_w, x, mod5, g_post, layer, seq_len=seq_len)


def kernel(x, c, w_ada, b_ada, g_mix_pre, g_mix_post, g_ffn_pre, g_ffn_post, w_in, conv_w,
           ssm_a_re, ssm_a_im, ssm_log_dt, ssm_b_re, ssm_b_im, ssm_c_re, ssm_c_im, ssm_d,
           w_glu, b_glu, w_out, ffn_w_gate, ffn_w_up, ffn_w_down, router_w, router_b,
           moe_w_gate, moe_w_up, moe_w_down):
    nb, seq_len, d = x.shape
    depth = w_ada.shape[0]
    d_conv = conv_w.shape[1]
    g, p, h_dim = ssm_b_re.shape[1:]
    d_ssm = g * h_dim
    n_experts = router_w.shape[2]
    t = nb * seq_len
    groups_per_slab = max(1, min(g, MXU_DIM // h_dim))

    mod_rows = 2 * SUBLANES
    c_pad = jnp.zeros((mod_rows, d), F32).at[:nb].set(c)
    mod = adaln_mod(c_pad, w_ada, b_ada)
    mod5 = mod.reshape(depth, mod_rows, 6, 1, d)
    xf = x.reshape(t, d)

    def rows(v):
        return v.reshape(v.shape[0], 1, v.shape[1])
    g_mix_pre, g_mix_post, g_ffn_pre, g_ffn_post = (
        rows(v) for v in (g_mix_pre, g_mix_post, g_ffn_pre, g_ffn_post))
    w_in16 = w_in.astype(BF16)
    w_out16 = w_out.astype(BF16)
    w_glu16 = w_glu.astype(BF16)
    conv_wt = jnp.swapaxes(conv_w, 1, 2)
    bb_re, bb_im, ab_re, ab_im = ssm_prep(ssm_a_re, ssm_a_im, ssm_log_dt, ssm_b_re, ssm_b_im)
    def slabs(w):
        return _block_diag_slabs(jnp.swapaxes(w, 2, 3).astype(BF16), groups_per_slab)
    bbd = jnp.concatenate([slabs(bb_re), slabs(bb_im)], axis=-1)
    cre = slabs(ssm_c_re)
    cim = slabs(ssm_c_im)
    a_re = _pack_state_rows(ab_re.reshape(depth, g * p), nb)
    a_im = _pack_state_rows(ab_im.reshape(depth, g * p), nb)
    d_skip = ssm_d.reshape(depth, 1, d_ssm)
    b_glu = rows(b_glu)
    n_moe = router_w.shape[0]
    rw_pad = jnp.zeros((n_moe, d, LANES), F32).at[:, :, :n_experts].set(router_w)
    rb_pad = jnp.zeros((n_moe, 1, LANES), F32).at[:, 0, :n_experts].set(router_b)

    for l in range(depth):
        y_conv, u = mixer_in(xf, mod5, g_mix_pre, w_in16, conv_wt, l, seq_len=seq_len)
        y_ssm = ssm(u.reshape(nb, seq_len, d_ssm), bbd, cre, cim, a_re, a_im, d_skip,
                    w_glu16, b_glu, l).reshape(t, d_ssm)
        routed = l % 2 == 1
        i = l // 2
        rt = (rw_pad[i], rb_pad[i]) if routed else None
        outs = mixer_out(y_conv, y_ssm, w_out16, xf, mod5, g_mix_post, g_ffn_pre, l, rt,
                         seq_len=seq_len)
        if routed:
            xf, h, logits = outs
            xf = moe_layer(h, logits, xf, mod5, g_ffn_post, moe_w_gate, moe_w_up, moe_w_down,
                           l, i, seq_len=seq_len)
        else:
            xf, h = outs
            xf = ffn_dense(h, ffn_w_gate, ffn_w_up, ffn_w_down, xf, mod5, g_ffn_post, l, i,
                           seq_len=seq_len)
    return xf.reshape(nb, seq_len, d)
```

```python
import functools

import jax
import jax.numpy as jnp
from jax import lax
from jax.experimental import pallas as pl
from jax.experimental.pallas import tpu as pltpu

F32 = jnp.float32
BF16 = jnp.bfloat16
NORM_EPS = 1e-6
LANES = 128
SUBLANES = 8
MXU_DIM = 256
EPILOGUE_ROWS = 64
MIB = 1024 * 1024


def _params(semantics, vmem_mib):
    return pltpu.CompilerParams(dimension_semantics=semantics,
                                vmem_limit_bytes=vmem_mib * MIB)


def _dot(a, b):
    return jnp.dot(a, b, preferred_element_type=F32)


def _rms(x):
    return x * lax.rsqrt(jnp.mean(x * x, axis=-1, keepdims=True) + NORM_EPS)


def _tile(n, want):
    t = min(n, want)
    while n % t:
        t -= 1
    return t


def _adaln_kernel(c_ref, w_ref, b_ref, o_ref):
    c = c_ref[...]
    c_act = c * jax.nn.sigmoid(c)
    c_hi = c_act.astype(BF16)
    c_lo = (c_act - c_hi.astype(F32)).astype(BF16)
    w16 = w_ref[...].astype(BF16)
    o_ref[...] = (_dot(c_hi, w16) + _dot(c_lo, w16)) + b_ref[...]


def adaln_mod(c_pad, w_ada, b_ada):
    depth, d, n = w_ada.shape
    rows = c_pad.shape[0]
    tn = _tile(n, 1024)
    return pl.pallas_call(
        _adaln_kernel,
        grid=(depth, n // tn),
        in_specs=[
            pl.BlockSpec((rows, d), lambda l, j: (0, 0)),
            pl.BlockSpec((None, d, tn), lambda l, j: (l, 0, j)),
            pl.BlockSpec((None, 1, tn), lambda l, j: (l, 0, j)),
        ],
        out_specs=pl.BlockSpec((None, rows, tn), lambda l, j: (l, 0, j)),
        out_shape=jax.ShapeDtypeStruct((depth, rows, n), F32),
        compiler_params=_params(("arbitrary", "arbitrary"), 32),
        name="adaln_mod",
    )(c_pad, w_ada, b_ada.reshape(depth, 1, n))


def _mixer_in_kernel(x_ref, sc_ref, sh_ref, g_ref, w_ref, cw_ref, yconv_ref, u_ref,
                     halo_scr, *, tiles_per_seq, col_chunk):
    tm = x_ref.shape[0]
    d_conv = yconv_ref.shape[1]
    d_ssm = u_ref.shape[1]
    h = (_rms(x_ref[...]) * g_ref[...] * (1.0 + sc_ref[...]) + sh_ref[...]).astype(BF16)

    @pl.when(pl.program_id(0) % tiles_per_seq == 0)
    def _():
        halo_scr[...] = jnp.zeros(halo_scr.shape, F32)

    for c0 in range(0, d_conv, col_chunk):
        cols = slice(c0, c0 + col_chunk)
        gate_b = _dot(h, w_ref[:, c0:c0 + col_chunk])
        z = (_dot(h, w_ref[:, d_conv + c0:d_conv + c0 + col_chunk])
             * _dot(h, w_ref[:, 2 * d_conv + c0:2 * d_conv + c0 + col_chunk]))
        prev = halo_scr[:, cols]
        halo_scr[:, cols] = z[tm - SUBLANES:, :]
        w0 = cw_ref[0:1, cols]
        w1 = cw_ref[1:2, cols]
        w2 = cw_ref[2:3, cols]
        z1 = pltpu.roll(z, 1, 0)
        z2 = pltpu.roll(z, 2, 0)
        yconv_ref[:, cols] = (gate_b * (z * w2 + z1 * w1 + z2 * w0)).astype(yconv_ref.dtype)
        row = lax.broadcasted_iota(jnp.int32, (SUBLANES, col_chunk), 0)
        z1t = jnp.where(row < 1, pltpu.roll(prev, 1, 0), z1[:SUBLANES, :])
        z2t = jnp.where(row < 2, pltpu.roll(prev, 2, 0), z2[:SUBLANES, :])
        top = gate_b[:SUBLANES, :] * (z[:SUBLANES, :] * w2 + z1t * w1 + z2t * w0)
        yconv_ref[0:SUBLANES, cols] = top.astype(yconv_ref.dtype)

    for c0 in range(0, d_ssm, col_chunk):
        u_ref[:, c0:c0 + col_chunk] = _dot(
            h, w_ref[:, 3 * d_conv + c0:3 * d_conv + c0 + col_chunk])


SH_MIX, SC_MIX, GT_MIX, SH_FFN, SC_FFN, GT_FFN = range(6)


def _mod_spec(layer, chunk, d, batch_of):
    return pl.BlockSpec((None, None, None, 1, d),
                        lambda i, *_: (layer, batch_of(i), chunk, 0, 0))


def _layer_spec(layer, shape, **kw):
    zeros = (0,) * len(shape)
    return pl.BlockSpec((None,) + tuple(shape), lambda *_: (layer,) + zeros, **kw)


def mixer_in(x, mod5, g_pre, w_in16, conv_wt, layer, *, seq_len):
    t, d = x.shape
    d_in = w_in16.shape[2]
    d_conv = conv_wt.shape[2]
    d_ssm = d_in - 3 * d_conv
    tm = _tile(seq_len, 512)
    col_chunk = _tile(min(d_conv, d_ssm), 512)
    assert d_conv % col_chunk == 0 and d_ssm % col_chunk == 0
    tiles_per_seq = seq_len // tm

    def batch_of(i):
        return i // tiles_per_seq

    kern = functools.partial(_mixer_in_kernel, tiles_per_seq=tiles_per_seq,
                             col_chunk=col_chunk)
    return pl.pallas_call(
        kern,
        grid=(t // tm,),
        in_specs=[
            pl.BlockSpec((tm, d), lambda i: (i, 0)),
            _mod_spec(layer, SC_MIX, d, batch_of), _mod_spec(layer, SH_MIX, d, batch_of),
            _layer_spec(layer, (1, d)),
            _layer_spec(layer, (d, d_in), pipeline_mode=pl.Buffered(1)),
            _layer_spec(layer, (3, d_conv)),
        ],
        out_specs=[
            pl.BlockSpec((tm, d_conv), lambda i: (i, 0)),
            pl.BlockSpec((tm, d_ssm), lambda i: (i, 0)),
        ],
        out_shape=[jax.ShapeDtypeStruct((t, d_conv), BF16),
                   jax.ShapeDtypeStruct((t, d_ssm), F32)],
        scratch_shapes=[pltpu.VMEM((SUBLANES, d_conv), F32)],
        compiler_params=_params(("arbitrary",), 48),
        name="mixer_in",
    )(x, mod5, mod5, g_pre, w_in16, conv_wt)


def _ssm_prep_kernel(are_ref, aim_ref, ldt_ref, bre_ref, bim_ref,
                     bbre_ref, bbim_ref, abre_ref, abim_ref):
    lam_re = are_ref[...]
    lam_im = aim_ref[...]
    dt = jnp.exp(ldt_ref[...])
    mag = jnp.exp(lam_re * dt)
    ang = lam_im * dt
    ab_re = mag * jnp.cos(ang)
    ab_im = mag * jnp.sin(ang)
    den = lam_re * lam_re + lam_im * lam_im
    nr = ab_re - 1.0
    q_re = (nr * lam_re + ab_im * lam_im) / den
    q_im = (ab_im * lam_re - nr * lam_im) / den
    b_re = bre_ref[...]
    b_im = bim_ref[...]
    bbre_ref[...] = q_re * b_re - q_im * b_im
    bbim_ref[...] = q_re * b_im + q_im * b_re
    abre_ref[...] = ab_re
    abim_ref[...] = ab_im


def ssm_prep(a_re, a_im, log_dt, b_re, b_im):
    depth, g, p, h = b_re.shape
    ldt = jnp.broadcast_to(log_dt.reshape(depth, g, 1, 1), (depth, g, p, 1))

    def spec(last):
        return pl.BlockSpec((None, g, p, last), lambda l: (l, 0, 0, 0))

    return pl.pallas_call(
        _ssm_prep_kernel,
        grid=(depth,),
        in_specs=[spec(1), spec(1), spec(1), spec(h), spec(h)],
        out_specs=[spec(h), spec(h), spec(1), spec(1)],
        out_shape=[jax.ShapeDtypeStruct((depth, g, p, h), F32),
                   jax.ShapeDtypeStruct((depth, g, p, h), F32),
                   jax.ShapeDtypeStruct((depth, g, p, 1), F32),
                   jax.ShapeDtypeStruct((depth, g, p, 1), F32)],
        compiler_params=_params(("arbitrary",), 56),
        name="ssm_prep",
    )(a_re.reshape(depth, g, p, 1), a_im.reshape(depth, g, p, 1), ldt, b_re, b_im)


def _block_diag_slabs(w, groups_per_slab):
    depth, g, r, c = w.shape
    ns = g // groups_per_slab
    w = w.reshape(depth, ns, groups_per_slab, r, c)
    eye = jnp.eye(groups_per_slab, dtype=w.dtype)
    out = w[:, :, :, :, None, :] * eye[None, None, :, None, :, None]
    return out.reshape(depth, ns, groups_per_slab * r, groups_per_slab * c)


def _gelu_tanh(x):
    return 0.5 * x * (1.0 + jnp.tanh(0.7978845608028654 * (x + 0.044715 * x * x * x)))


def _ssm_kernel(u_ref, bbd_ref, cre_ref, cim_ref, are_ref, aim_ref, d_ref, wglu_ref,
                bglu_ref, o_ref, sre_scr, sim_scr, stre_scr, stim_scr, y_scr, *,
                pitch, groups_per_step):
    nb, tc, d_ssm = u_ref.shape
    n_slabs, k_slab, two_sw = bbd_ref.shape
    sw = two_sw // 2
    tiles_per_slab = sw // LANES
    pack = SUBLANES // nb
    n_groups = sre_scr.shape[0]

    def slab_rows(lane_tile, b):
        q, h = divmod(lane_tile, pack)
        r0 = (h * nb + b) * pitch
        return q, slice(r0, r0 + tc)

    @pl.when(pl.program_id(0) == 0)
    def _():
        stre_scr[...] = jnp.zeros(stre_scr.shape, F32)
        stim_scr[...] = jnp.zeros(stim_scr.shape, F32)

    for b in range(nb):
        for s in range(n_slabs):
            ub = u_ref[b, :, s * k_slab:(s + 1) * k_slab].astype(BF16)
            r = _dot(ub, bbd_ref[s])
            for j in range(tiles_per_slab):
                q, rows = slab_rows(s * tiles_per_slab + j, b)
                sre_scr[q, rows, :] = r[:, j * LANES:(j + 1) * LANES]
                sim_scr[q, rows, :] = r[:, sw + j * LANES:sw + (j + 1) * LANES]

    for q0 in range(0, n_groups, groups_per_step):
        qs = list(range(q0, min(q0 + groups_per_step, n_groups)))
        a_re = [are_ref[q] for q in qs]
        a_im = [aim_ref[q] for q in qs]

        def step(t, carry, qs=qs, a_re=a_re, a_im=a_im):
            rows = pl.ds(t, SUBLANES, stride=pitch)
            out = []
            for n, q in enumerate(qs):
                s_re, s_im = carry[2 * n], carry[2 * n + 1]
                n_re = a_re[n] * s_re - a_im[n] * s_im + sre_scr[q, rows, :]
                n_im = a_re[n] * s_im + a_im[n] * s_re + sim_scr[q, rows, :]
                sre_scr[q, rows, :] = n_re
                sim_scr[q, rows, :] = n_im
                out += [n_re, n_im]
            return tuple(out)

        init = []
        for q in qs:
            init += [stre_scr[q], stim_scr[q]]
        fin = lax.fori_loop(0, tc, step, tuple(init), unroll=2)
        for n, q in enumerate(qs):
            stre_scr[q] = fin[2 * n]
            stim_scr[q] = fin[2 * n + 1]

    for b in range(nb):
        for s in range(n_slabs):
            cols = slice(s * k_slab, (s + 1) * k_slab)
            pieces_re, pieces_im = [], []
            for j in range(tiles_per_slab):
                q, rows = slab_rows(s * tiles_per_slab + j, b)
                pieces_re.append(sre_scr[q, rows, :].astype(BF16))
                pieces_im.append(sim_scr[q, rows, :].astype(BF16))
            y = (_dot(jnp.concatenate(pieces_re, axis=-1), cre_ref[s])
                 - _dot(jnp.concatenate(pieces_im, axis=-1), cim_ref[s]))
            y_scr[:, cols] = y + d_ref[:, cols] * u_ref[b, :, cols]
        y = _gelu_tanh(y_scr[...])
        gate = _dot(y.astype(BF16), wglu_ref[...]) + bglu_ref[...]
        o_ref[b] = (y * jax.nn.sigmoid(gate)).astype(o_ref.dtype)


def _pack_state_rows(a, nb):
    depth = a.shape[0]
    pack = SUBLANES // nb
    n_groups = a.shape[1] // (LANES * pack)
    a = a.reshape(depth, n_groups, pack, 1, LANES)
    return jnp.broadcast_to(a, (depth, n_groups, pack, nb, LANES)).reshape(
        depth, n_groups, SUBLANES, LANES)


def ssm(u3, bbd, cre, cim, a_re, a_im, d_skip, w_glu16, b_glu, layer):
    nb, seq_len, d_ssm = u3.shape
    n_groups = a_re.shape[1]
    assert SUBLANES % nb == 0
    tc = _tile(seq_len, 128)
    pitch = tc + SUBLANES
    kern = functools.partial(_ssm_kernel, pitch=pitch, groups_per_step=min(4, n_groups))

    def full(a):
        return _layer_spec(layer, a.shape[1:])

    return pl.pallas_call(
        kern,
        grid=(seq_len // tc,),
        in_specs=[pl.BlockSpec((nb, tc, d_ssm), lambda c: (0, c, 0)),
                  full(bbd), full(cre), full(cim), full(a_re), full(a_im),
                  full(d_skip), full(w_glu16), full(b_glu)],
        out_specs=pl.BlockSpec((nb, tc, d_ssm), lambda c: (0, c, 0)),
        out_shape=jax.ShapeDtypeStruct((nb, seq_len, d_ssm), BF16),
        scratch_shapes=[pltpu.VMEM((n_groups, SUBLANES * pitch, LANES), F32),
                        pltpu.VMEM((n_groups, SUBLANES * pitch, LANES), F32),
                        pltpu.VMEM((n_groups, SUBLANES, LANES), F32),
                        pltpu.VMEM((n_groups, SUBLANES, LANES), F32),
                        pltpu.VMEM((tc, d_ssm), F32)],
        compiler_params=_params(("arbitrary",), 56),
        name="ssm_scan",
    )(u3, bbd, cre, cim, a_re, a_im, d_skip, w_glu16, b_glu)


def _mixer_out_kernel(*refs, routed):
    if routed:
        (yc_ref, ys_ref, wo_ref, x_ref, gt_ref, gpost_ref, gpre_ref, sc_ref,
         sh_ref, rw_ref, rb_ref, xo_ref, h_ref, lg_ref) = refs
    else:
        (yc_ref, ys_ref, wo_ref, x_ref, gt_ref, gpost_ref, gpre_ref, sc_ref,
         sh_ref, xo_ref, h_ref) = refs
    d_conv = yc_ref.shape[1]
    y = _dot(yc_ref[...], wo_ref[:d_conv, :]) + _dot(ys_ref[...], wo_ref[d_conv:, :])
    x_new = x_ref[...] + gt_ref[...] * (_rms(y) * gpost_ref[...])
    xo_ref[...] = x_new
    h = _rms(x_new) * gpre_ref[...] * (1.0 + sc_ref[...]) + sh_ref[...]
    h_ref[...] = h.astype(h_ref.dtype)
    if routed:
        w = rw_ref[...]
        h_hi = h.astype(BF16)
        w_hi = w.astype(BF16)
        h_lo = (h - h_hi.astype(F32)).astype(BF16)
        w_lo = (w - w_hi.astype(F32)).astype(BF16)
        lg_ref[...] = (_dot(h_hi, w_hi) + (_dot(h_hi, w_lo) + _dot(h_lo, w_hi))
                       + rb_ref[...])


def mixer_out(y_conv, y_ssm, w_out16, x, mod5, g_post, g_pre, layer, router=None, *,
              seq_len):
    t, d = x.shape
    d_conv = y_conv.shape[1]
    d_ssm = y_ssm.shape[1]
    tm = _tile(seq_len, 512)
    tiles_per_seq = seq_len // tm
    routed = router is not None

    def batch_of(i):
        return i // tiles_per_seq

    def row_spec(n):
        return pl.BlockSpec((tm, n), lambda i: (i, 0))

    def const_spec(shape):
        return pl.BlockSpec(shape, lambda i: (0, 0))

    in_specs = [row_spec(d_conv), row_spec(d_ssm),
                _layer_spec(layer, (d_conv + d_ssm, d), pipeline_mode=pl.Buffered(1)),
                row_spec(d), _mod_spec(layer, GT_MIX, d, batch_of),
                _layer_spec(layer, (1, d)), _layer_spec(layer, (1, d)),
                _mod_spec(layer, SC_FFN, d, batch_of), _mod_spec(layer, SH_FFN, d, batch_of)]
    args = [y_conv, y_ssm, w_out16, x, mod5, g_post, g_pre, mod5, mod5]
    out_specs = [row_spec(d), row_spec(d)]
    out_shape = [jax.ShapeDtypeStruct((t, d), F32),
                 jax.ShapeDtypeStruct((t, d), F32 if routed else BF16)]
    if routed:
        rw_pad, rb_pad = router
        in_specs += [const_spec(rw_pad.shape), const_spec(rb_pad.shape)]
        args += [rw_pad, rb_pad]
        out_specs.append(row_spec(LANES))
        out_shape.append(jax.ShapeDtypeStruct((t, LANES), F32))
    return pl.pallas_call(
        functools.partial(_mixer_out_kernel, routed=routed),
        grid=(t // tm,),
        in_specs=in_specs, out_specs=out_specs, out_shape=out_shape,
        compiler_params=_params(("arbitrary",), 56),
        name="mixer_out_routed" if routed else "mixer_out",
    )(*args)


def _ffn_kernel(h_ref, wg_ref, wu_ref, wd_ref, x_hbm, gt_ref, gpost_ref, o_ref,
                wg16, wu16, wd16, x_buf, sem, *, sub_rows):
    i = pl.program_id(0)
    k = pl.program_id(1)
    tm = o_ref.shape[0]
    x_copy = pltpu.make_async_copy(x_hbm.at[pl.ds(i * tm, tm)], x_buf, sem)

    @pl.when(k == 0)
    def _():
        o_ref[...] = jnp.zeros(o_ref.shape, F32)
        x_copy.start()

    wg16[...] = wg_ref[...].astype(BF16)
    wu16[...] = wu_ref[...].astype(BF16)
    wd16[...] = wd_ref[...].astype(BF16)
    for r0 in range(0, tm, sub_rows):
        h = h_ref[r0:r0 + sub_rows, :]
        gate = _dot(h, wg16[...])
        act = (gate * jax.nn.sigmoid(gate)) * _dot(h, wu16[...])
        o_ref[r0:r0 + sub_rows, :] += _dot(act.astype(BF16), wd16[...])

    @pl.when(k == pl.num_programs(1) - 1)
    def _():
        x_copy.wait()
        scale = gt_ref[...] * gpost_ref[...]
        for r0 in range(0, tm, EPILOGUE_ROWS):
            rows = slice(r0, r0 + EPILOGUE_ROWS)
            o_ref[rows, :] = x_buf[rows, :] + _rms(o_ref[rows, :]) * scale


def ffn_dense(h16, w_gate, w_up, w_down, x, mod5, g_post, layer, ffn_index, *, seq_len):
    t, d = x.shape
    d_ff = w_gate.shape[2]
    tm = _tile(seq_len, 1024)
    sub_rows = _tile(tm, 512)
    tk = _tile(d_ff, 256)
    tiles_per_seq = seq_len // tm
    return pl.pallas_call(
        functools.partial(_ffn_kernel, sub_rows=sub_rows),
        grid=(t // tm, d_ff // tk),
        in_specs=[
            pl.BlockSpec((tm, d), lambda i, k: (i, 0)),
            pl.BlockSpec((None, d, tk), lambda i, k: (ffn_index, 0, k)),
            pl.BlockSpec((None, d, tk), lambda i, k: (ffn_index, 0, k)),
            pl.BlockSpec((None, tk, d), lambda i, k: (ffn_index, k, 0)),
            pl.BlockSpec(memory_space=pl.ANY),
            _mod_spec(layer, GT_FFN, d, lambda i: i // tiles_per_seq),
            _layer_spec(layer, (1, d)),
        ],
        out_specs=pl.BlockSpec((tm, d), lambda i, k: (i, 0)),
        out_shape=jax.ShapeDtypeStruct((t, d), F32),
        scratch_shapes=[pltpu.VMEM((d, tk), BF16), pltpu.VMEM((d, tk), BF16),
                        pltpu.VMEM((tk, d), BF16), pltpu.VMEM((tm, d), F32),
                        pltpu.SemaphoreType.DMA(())],
        compiler_params=_params(("arbitrary", "arbitrary"), 56),
        name="ffn_dense",
    )(h16, w_gate, w_up, w_down, x, mod5, g_post)


def _router_kernel(lg_ref, ri_ref, rw_ref, cnt_ref, carry_scr, *, n_experts):
    @pl.when(pl.program_id(0) == 0)
    def _():
        carry_scr[...] = jnp.zeros(carry_scr.shape, F32)

    tm = lg_ref.shape[0]
    lane = lax.broadcasted_iota(jnp.int32, (tm, LANES), 1)
    neg = jnp.float32(-jnp.inf)
    logit = jnp.where(lane < n_experts, lg_ref[...], neg)
    m1 = jnp.max(logit, axis=-1, keepdims=True)
    i1 = jnp.min(jnp.where(logit == m1, lane, LANES), axis=-1, keepdims=True)
    rest = jnp.where(lane == i1, neg, logit)
    m2 = jnp.max(rest, axis=-1, keepdims=True)
    i2 = jnp.min(jnp.where(rest == m2, lane, LANES), axis=-1, keepdims=True)
    e2 = jnp.exp(m2 - m1)
    w1 = 1.0 / (1.0 + e2)
    w2 = e2 / (1.0 + e2)
    hit1 = lane == i1
    hit2 = lane == i2
    onehot = jnp.where(hit1 | hit2, 1.0, 0.0)
    r = lax.broadcasted_iota(jnp.int32, (tm, tm), 0)
    c = lax.broadcasted_iota(jnp.int32, (tm, tm), 1)
    earlier = jnp.where(c < r, 1.0, 0.0).astype(BF16)
    carry = carry_scr[0:1, :]
    before = _dot(earlier, onehot.astype(BF16)) + carry
    pos1 = jnp.sum(jnp.where(hit1, before, 0.0), axis=-1, keepdims=True).astype(jnp.int32)
    pos2 = jnp.sum(jnp.where(hit2, before, 0.0), axis=-1, keepdims=True).astype(jnp.int32)
    total = carry + jnp.sum(onehot, axis=0, keepdims=True)
    carry_scr[...] = jnp.broadcast_to(total, carry_scr.shape)
    cnt_ref[...] = jnp.broadcast_to(total, cnt_ref.shape).astype(jnp.int32)
    zero_i = jnp.zeros((tm, LANES), jnp.int32)
    ri_ref[...] = jnp.where(lane == 0, i1, jnp.where(lane == 1, i2,
                            jnp.where(lane == 2, pos1, jnp.where(lane == 3, pos2, zero_i))))
    rw_ref[...] = jnp.where(lane == 0, w1, jnp.where(lane == 1, w2, 0.0))


def router(logits, n_experts):
    t = logits.shape[0]
    tm = _tile(t, 256)
    return pl.pallas_call(
        functools.partial(_router_kernel, n_experts=n_experts),
        grid=(t // tm,),
        in_specs=[pl.BlockSpec((tm, LANES), lambda i: (i, 0))],
        out_specs=[pl.BlockSpec((tm, LANES), lambda i: (i, 0)),
                   pl.BlockSpec((tm, LANES), lambda i: (i, 0)),
                   pl.BlockSpec((SUBLANES, LANES), lambda i: (0, 0))],
        out_shape=[jax.ShapeDtypeStruct((t, LANES), jnp.int32),
                   jax.ShapeDtypeStruct((t, LANES), F32),
                   jax.ShapeDtypeStruct((SUBLANES, LANES), jnp.int32)],
        scratch_shapes=[pltpu.VMEM((SUBLANES, LANES), F32)],
        compiler_params=_params(("arbitrary",), 32),
        name="router_top2",
    )(logits)


def _moe_ffn_kernel(ge_ref, nr_ref, nu_ref, tok_ref, h_hbm, wg_ref, wu_ref, wd_ref, o_ref,
                    h_scr, gbuf, wg16, wu16, wd16, sem, *, rows_per_step):
    g = pl.program_id(0)
    k = pl.program_id(1)
    n_used = nu_ref[0]
    n_sub_max, ts, _ = h_scr.shape
    group_rows = n_sub_max * ts
    nk = pl.num_programs(1)
    n_gather = gbuf.shape[0]

    def row_copy(src_row, dst_row):
        return pltpu.make_async_copy(h_hbm.at[pl.ds(src_row, 1)],
                                     gbuf.at[pl.ds(dst_row, 1)], sem)

    def for_each_sub_tile(group, fn):
        for j in range(n_sub_max):
            @pl.when(j * ts < nr_ref[group])
            def _(j=j):
                fn(j)

    def wait_rows():
        pltpu.make_async_copy(h_hbm.at[pl.ds(0, n_gather)], gbuf, sem).wait()

    @pl.when(k == 0)
    def _():
        o_ref[...] = jnp.zeros(o_ref.shape, o_ref.dtype)

    @pl.when(g < n_used)
    def _():
        @pl.when(k == 0)
        def _():
            @pl.when(g == 0)
            def _():
                def body(i, carry):
                    row_copy(tok_ref[i], i).start()
                    return carry
                lax.fori_loop(0, n_gather, body, 0, unroll=8)

            wait_rows()

            def cast(j):
                h_scr[j] = gbuf[j * ts:(j + 1) * ts, :].astype(BF16)
            for_each_sub_tile(g, cast)

        next_base = jnp.minimum(g + 1, n_used - 1) * group_rows + k * rows_per_step

        def sub_tile(j, m):
            if j == 0:
                wg16[...] = wg_ref[...].astype(BF16)
                wu16[...] = wu_ref[...].astype(BF16)
                wd16[...] = wd_ref[...].astype(BF16)
                for i in range(rows_per_step):
                    row_copy(tok_ref[next_base + i], k * rows_per_step + i).start()
            h = h_scr[j, 0:m, :]
            gate = _dot(h, wg16[...])
            act = (gate * jax.nn.sigmoid(gate)) * _dot(h, wu16[...])
            o_ref[j * ts:j * ts + m, :] += _dot(act.astype(BF16), wd16[...])

        half = ts // 2
        for j in range(n_sub_max):
            rows_j = nr_ref[g] - j * ts

            @pl.when(rows_j > half)
            def _(j=j):
                sub_tile(j, ts)

            @pl.when((rows_j > 0) & (rows_j <= half))
            def _(j=j):
                sub_tile(j, half)

        @pl.when((g == n_used - 1) & (k == nk - 1))
        def _():
            wait_rows()


def moe_ffn(group_expert, group_rows_used, n_used, row_tok, h, w_gate, w_up, w_down, layer, *,
            sub_rows, n_sub_max):
    d = h.shape[1]
    d_ff = w_gate.shape[3]
    tk = _tile(d_ff, 256)
    nk = d_ff // tk
    group_rows = sub_rows * n_sub_max
    rows_per_step = -(-group_rows // (nk * SUBLANES)) * SUBLANES
    n_gather = rows_per_step * nk
    n_rows = group_expert.shape[0] * group_rows
    assert row_tok.shape[0] - n_rows >= n_gather - group_rows

    def ff_idx(g, k, nu):
        return jnp.where(g < nu[0], k, nk - 1)

    return pl.pallas_call(
        functools.partial(_moe_ffn_kernel, rows_per_step=rows_per_step),
        grid_spec=pltpu.PrefetchScalarGridSpec(
            num_scalar_prefetch=4,
            grid=(n_rows // group_rows, nk),
            in_specs=[
                pl.BlockSpec(memory_space=pl.ANY),
                pl.BlockSpec((None, None, d, tk),
                             lambda g, k, ge, ns, nu, tok: (layer, ge[g], 0, ff_idx(g, k, nu))),
                pl.BlockSpec((None, None, d, tk),
                             lambda g, k, ge, ns, nu, tok: (layer, ge[g], 0, ff_idx(g, k, nu))),
                pl.BlockSpec((None, None, tk, d),
                             lambda g, k, ge, ns, nu, tok: (layer, ge[g], ff_idx(g, k, nu), 0)),
            ],
            out_specs=pl.BlockSpec((group_rows, d), lambda g, k, ge, ns, nu, tok: (g, 0)),
            scratch_shapes=[pltpu.VMEM((n_sub_max, sub_rows, d), BF16),
                            pltpu.VMEM((n_gather, d), F32),
                            pltpu.VMEM((d, tk), BF16), pltpu.VMEM((d, tk), BF16),
                            pltpu.VMEM((tk, d), BF16),
                            pltpu.SemaphoreType.DMA(())],
        ),
        out_shape=jax.ShapeDtypeStruct((n_rows, d), F32),
        compiler_params=_params(("arbitrary", "arbitrary"), 56),
        name="moe_ffn",
    )(group_expert, group_rows_used, n_used, row_tok, h, w_gate, w_up, w_down)


def _moe_combine_kernel(d0_ref, d1_ref, ys_ref, rw_ref, x_ref, gt_ref, gpost_ref, o_ref,
                        buf, sems):
    i = pl.program_id(0)
    tm = x_ref.shape[0]
    slot = i % 2

    def start_tile(tile, s):
        def body(t, carry):
            for which, dest in enumerate((d0_ref, d1_ref)):
                pltpu.make_async_copy(ys_ref.at[pl.ds(dest[tile * tm + t], 1)],
                                      buf.at[s, pl.ds(which * tm + t, 1)],
                                      sems.at[s]).start()
            return carry
        lax.fori_loop(0, tm, body, 0, unroll=8)

    @pl.when(i == 0)
    def _():
        start_tile(0, 0)

    @pl.when(i + 1 < pl.num_programs(0))
    def _():
        start_tile(i + 1, 1 - slot)

    pltpu.make_async_copy(ys_ref.at[pl.ds(0, 2 * tm)], buf.at[slot], sems.at[slot]).wait()
    scale = gt_ref[...] * gpost_ref[...]
    for r0 in range(0, tm, EPILOGUE_ROWS):
        rows = slice(r0, r0 + EPILOGUE_ROWS)
        y = (buf[slot, r0:r0 + EPILOGUE_ROWS, :] * rw_ref[rows, 0:1]
             + buf[slot, tm + r0:tm + r0 + EPILOGUE_ROWS, :] * rw_ref[rows, 1:2])
        o_ref[rows, :] = x_ref[rows, :] + _rms(y) * scale


def moe_combine(dest0, dest1, ys, route_w, x, mod5, g_post, layer, *, seq_len):
    t, d = x.shape
    tm = _tile(seq_len, 256)
    tiles_per_seq = seq_len // tm
    return pl.pallas_call(
        _moe_combine_kernel,
        grid_spec=pltpu.PrefetchScalarGridSpec(
            num_scalar_prefetch=2,
            grid=(t // tm,),
            in_specs=[
                pl.BlockSpec(memory_space=pl.ANY),
                pl.BlockSpec((tm, LANES), lambda i, d0, d1: (i, 0)),
                pl.BlockSpec((tm, d), lambda i, d0, d1: (i, 0)),
                _mod_spec(layer, GT_FFN, d, lambda i: i // tiles_per_seq),
                _layer_spec(layer, (1, d)),
            ],
            out_specs=pl.BlockSpec((tm, d), lambda i, d0, d1: (i, 0)),
            scratch_shapes=[pltpu.VMEM((2, 2 * tm, d), F32), pltpu.SemaphoreType.DMA((2,))],
        ),
        out_shape=jax.ShapeDtypeStruct((t, d), F32),
        compiler_params=_params(("arbitrary",), 32),
        name="moe_combine",
    )(dest0, dest1, ys, route_w, x, mod5, g_post)


MOE_SUB_ROWS = 512
MOE_GROUP_SUBS = 2


def moe_layer(h, logits, x, mod5, g_post, w_gate, w_up, w_down, layer, moe_index, *,
              seq_len):
    t, d = h.shape
    n_experts = w_gate.shape[1]
    ts = min(MOE_SUB_ROWS, t)
    group_rows = ts * MOE_GROUP_SUBS
    route_i, route_w, counts = router(logits, n_experts)
    e0, e1, pos0, pos1 = (route_i[:, k] for k in range(4))
    counts = counts[0, :n_experts]
    groups_e = (counts + group_rows - 1) // group_rows
    g_end = jnp.cumsum(groups_e)
    g_start = g_end - groups_e
    dest0 = g_start[e0] * group_rows + pos0
    dest1 = g_start[e1] * group_rows + pos1
    n_groups = -(-(2 * t) // group_rows) + n_experts
    n_rows = n_groups * group_rows
    tok = jnp.arange(t, dtype=jnp.int32)
    row_tok = jnp.zeros((n_rows + group_rows,), jnp.int32).at[
        jnp.concatenate([dest0, dest1])].set(jnp.concatenate([tok, tok]))
    n_used = g_end[-1].astype(jnp.int32)
    gid = jnp.arange(n_groups, dtype=jnp.int32)
    group_e = jnp.clip(jnp.searchsorted(g_end, gid, side='right'), 0, n_experts - 1)
    group_e = jnp.where(gid < n_used, group_e, group_e[n_used - 1]).astype(jnp.int32)
    group_rows_used = jnp.where(
        gid < n_used,
        jnp.clip(counts[group_e] - (gid - g_start[group_e]) * group_rows, 0, group_rows),
        0).astype(jnp.int32)
    ys = moe_ffn(group_e, group_rows_used, n_used.reshape(1), row_tok, h, w_gate, w_up, w_down,
                 moe_index, sub_rows=ts, n_sub_max=MOE_GROUP_SUBS)
    return moe_combine(dest0, dest1, ys, route_w, x, mod5, g_post, layer, seq_len=seq_len)


def kernel(x, c, w_ada, b_ada, g_mix_pre, g_mix_post, g_ffn_pre, g_ffn_post, w_in, conv_w,
           ssm_a_re, ssm_a_im, ssm_log_dt, ssm_b_re, ssm_b_im, ssm_c_re, ssm_c_im, ssm_d,
           w_glu, b_glu, w_out, ffn_w_gate, ffn_w_up, ffn_w_down, router_w, router_b,
           moe_w_gate, moe_w_up, moe_w_down):
    nb, seq_len, d = x.shape
    depth = w_ada.shape[0]
    d_conv = conv_w.shape[1]
    g, p, h_dim = ssm_b_re.shape[1:]
    d_ssm = g * h_dim
    n_experts = router_w.shape[2]
    t = nb * seq_len
    groups_per_slab = max(1, min(g, MXU_DIM // h_dim))

    mod_rows = 2 * SUBLANES
    c_pad = jnp.zeros((mod_rows, d), F32).at[:nb].set(c)
    mod = adaln_mod(c_pad, w_ada, b_ada)
    mod5 = mod.reshape(depth, mod_rows, 6, 1, d)
    xf = x.reshape(t, d)

    def rows(v):
        return v.reshape(v.shape[0], 1, v.shape[1])
    g_mix_pre, g_mix_post, g_ffn_pre, g_ffn_post = (
        rows(v) for v in (g_mix_pre, g_mix_post, g_ffn_pre, g_ffn_post))
    w_in16 = w_in.astype(BF16)
    w_out16 = w_out.astype(BF16)
    w_glu16 = w_glu.astype(BF16)
    conv_wt = jnp.swapaxes(conv_w, 1, 2)
    bb_re, bb_im, ab_re, ab_im = ssm_prep(ssm_a_re, ssm_a_im, ssm_log_dt, ssm_b_re, ssm_b_im)
    def slabs(w):
        return _block_diag_slabs(jnp.swapaxes(w, 2, 3).astype(BF16), groups_per_slab)
    bbd = jnp.concatenate([slabs(bb_re), slabs(bb_im)], axis=-1)
    cre = slabs(ssm_c_re)
    cim = slabs(ssm_c_im)
    a_re = _pack_state_rows(ab_re.reshape(depth, g * p), nb)
    a_im = _pack_state_rows(ab_im.reshape(depth, g * p), nb)
    d_skip = ssm_d.reshape(depth, 1, d_ssm)
    b_glu = rows(b_glu)
    n_moe = router_w.shape[0]
    rw_pad = jnp.zeros((n_moe, d, LANES), F32).at[:, :, :n_experts].set(router_w)
    rb_pad = jnp.zeros((n_moe, 1, LANES), F32).at[:, 0, :n_experts].set(router_b)

    for l in range(depth):
        y_conv, u = mixer_in(xf, mod5, g_mix_pre, w_in16, conv_wt, l, seq_len=seq_len)
        y_ssm = ssm(u.reshape(nb, seq_len, d_ssm), bbd, cre, cim, a_re, a_im, d_skip,
                    w_glu16, b_glu, l).reshape(t, d_ssm)
        routed = l % 2 == 1
        i = l // 2
        rt = (rw_pad[i], rb_pad[i]) if routed else None
        outs = mixer_out(y_conv, y_ssm, w_out16, xf, mod5, g_mix_post, g_ffn_pre, l, rt,
                         seq_len=seq_len)
        if routed:
            xf, h, logits = outs
            xf = moe_layer(h, logits, xf, mod5, g_ffn_post, moe_w_gate, moe_w_up, moe_w_down,
                           l, i, seq_len=seq_len)
        else:
            xf, h = outs
            xf = ffn_dense(h, ffn_w_gate, ffn_w_up, ffn_w_down, xf, mod5, g_ffn_post, l, i,
                           seq_len=seq_len)
    return xf.reshape(nb, seq_len, d)
```

```python
import functools

import jax
import jax.numpy as jnp
from jax import lax
from jax.experimental import pallas as pl
from jax.experimental.pallas import tpu as pltpu

F32 = jnp.float32
BF16 = jnp.bfloat16
NORM_EPS = 1e-6
LANES = 128
SUBLANES = 8
MXU_DIM = 256
EPILOGUE_ROWS = 64
MIB = 1024 * 1024


def _params(semantics, vmem_mib):
    return pltpu.CompilerParams(dimension_semantics=semantics,
                                vmem_limit_bytes=vmem_mib * MIB)


def _dot(a, b):
    return jnp.dot(a, b, preferred_element_type=F32)


def _rms(x):
    return x * lax.rsqrt(jnp.mean(x * x, axis=-1, keepdims=True) + NORM_EPS)


def _tile(n, want):
    t = min(n, want)
    while n % t:
        t -= 1
    return t


def _adaln_kernel(c_ref, w_ref, b_ref, o_ref):
    c = c_ref[...]
    c_act = c * jax.nn.sigmoid(c)
    c_hi = c_act.astype(BF16)
    c_lo = (c_act - c_hi.astype(F32)).astype(BF16)
    w16 = w_ref[...].astype(BF16)
    o_ref[...] = (_dot(c_hi, w16) + _dot(c_lo, w16)) + b_ref[...]


def adaln_mod(c_pad, w_ada, b_ada):
    depth, d, n = w_ada.shape
    rows = c_pad.shape[0]
    tn = _tile(n, 1024)
    return pl.pallas_call(
        _adaln_kernel,
        grid=(depth, n // tn),
        in_specs=[
            pl.BlockSpec((rows, d), lambda l, j: (0, 0)),
            pl.BlockSpec((None, d, tn), lambda l, j: (l, 0, j)),
            pl.BlockSpec((None, 1, tn), lambda l, j: (l, 0, j)),
        ],
        out_specs=pl.BlockSpec((None, rows, tn), lambda l, j: (l, 0, j)),
        out_shape=jax.ShapeDtypeStruct((depth, rows, n), F32),
        compiler_params=_params(("arbitrary", "arbitrary"), 32),
        name="adaln_mod",
    )(c_pad, w_ada, b_ada.reshape(depth, 1, n))


def _mixer_in_kernel(x_ref, sc_ref, sh_ref, g_ref, w_ref, cw_ref, yconv_ref, u_ref,
                     halo_scr, *, tiles_per_seq, col_chunk):
    tm = x_ref.shape[0]
    d_conv = yconv_ref.shape[1]
    d_ssm = u_ref.shape[1]
    h = (_rms(x_ref[...]) * g_ref[...] * (1.0 + sc_ref[...]) + sh_ref[...]).astype(BF16)

    @pl.when(pl.program_id(0) % tiles_per_seq == 0)
    def _():
        halo_scr[...] = jnp.zeros(halo_scr.shape, F32)

    for c0 in range(0, d_conv, col_chunk):
        cols = slice(c0, c0 + col_chunk)
        gate_b = _dot(h, w_ref[:, c0:c0 + col_chunk])
        z = (_dot(h, w_ref[:, d_conv + c0:d_conv + c0 + col_chunk])
             * _dot(h, w_ref[:, 2 * d_conv + c0:2 * d_conv + c0 + col_chunk]))
        prev = halo_scr[:, cols]
        halo_scr[:, cols] = z[tm - SUBLANES:, :]
        w0 = cw_ref[0:1, cols]
        w1 = cw_ref[1:2, cols]
        w2 = cw_ref[2:3, cols]
        z1 = pltpu.roll(z, 1, 0)
        z2 = pltpu.roll(z, 2, 0)
        yconv_ref[:, cols] = (gate_b * (z * w2 + z1 * w1 + z2 * w0)).astype(yconv_ref.dtype)
        row = lax.broadcasted_iota(jnp.int32, (SUBLANES, col_chunk), 0)
        z1t = jnp.where(row < 1, pltpu.roll(prev, 1, 0), z1[:SUBLANES, :])
        z2t = jnp.where(row < 2, pltpu.roll(prev, 2, 0), z2[:SUBLANES, :])
        top = gate_b[:SUBLANES, :] * (z[:SUBLANES, :] * w2 + z1t * w1 + z2t * w0)
        yconv_ref[0:SUBLANES, cols] = top.astype(yconv_ref.dtype)

    for c0 in range(0, d_ssm, col_chunk):
        u_ref[:, c0:c0 + col_chunk] = _dot(
            h, w_ref[:, 3 * d_conv + c0:3 * d_conv + c0 + col_chunk])


SH_MIX, SC_MIX, GT_MIX, SH_FFN, SC_FFN, GT_FFN = range(6)


def _mod_spec(layer, chunk, d, batch_of):
    return pl.BlockSpec((None, None, None, 1, d),
                        lambda i, *_: (layer, batch_of(i), chunk, 0, 0))


def _layer_spec(layer, shape, **kw):
    zeros = (0,) * len(shape)
    return pl.BlockSpec((None,) + tuple(shape), lambda *_: (layer,) + zeros, **kw)


def mixer_in(x, mod5, g_pre, w_in16, conv_wt, layer, *, seq_len):
    t, d = x.shape
    d_in = w_in16.shape[2]
    d_conv = conv_wt.shape[2]
    d_ssm = d_in - 3 * d_conv
    tm = _tile(seq_len, 512)
    col_chunk = _tile(min(d_conv, d_ssm), 512)
    assert d_conv % col_chunk == 0 and d_ssm % col_chunk == 0
    tiles_per_seq = seq_len // tm

    def batch_of(i):
        return i // tiles_per_seq

    kern = functools.partial(_mixer_in_kernel, tiles_per_seq=tiles_per_seq,
                             col_chunk=col_chunk)
    return pl.pallas_call(
        kern,
        grid=(t // tm,),
        in_specs=[
            pl.BlockSpec((tm, d), lambda i: (i, 0)),
            _mod_spec(layer, SC_MIX, d, batch_of), _mod_spec(layer, SH_MIX, d, batch_of),
            _layer_spec(layer, (1, d)),
            _layer_spec(layer, (d, d_in), pipeline_mode=pl.Buffered(1)),
            _layer_spec(layer, (3, d_conv)),
        ],
        out_specs=[
            pl.BlockSpec((tm, d_conv), lambda i: (i, 0)),
            pl.BlockSpec((tm, d_ssm), lambda i: (i, 0)),
        ],
        out_shape=[jax.ShapeDtypeStruct((t, d_conv), BF16),
                   jax.ShapeDtypeStruct((t, d_ssm), F32)],
        scratch_shapes=[pltpu.VMEM((SUBLANES, d_conv), F32)],
        compiler_params=_params(("arbitrary",), 48),
        name="mixer_in",
    )(x, mod5, mod5, g_pre, w_in16, conv_wt)


def _ssm_prep_kernel(are_ref, aim_ref, ldt_ref, bre_ref, bim_ref,
                     bbre_ref, bbim_ref, abre_ref, abim_ref):
    lam_re = are_ref[...]
    lam_im = aim_ref[...]
    dt = jnp.exp(ldt_ref[...])
    mag = jnp.exp(lam_re * dt)
    ang = lam_im * dt
    ab_re = mag * jnp.cos(ang)
    ab_im = mag * jnp.sin(ang)
    den = lam_re * lam_re + lam_im * lam_im
    nr = ab_re - 1.0
    q_re = (nr * lam_re + ab_im * lam_im) / den
    q_im = (ab_im * lam_re - nr * lam_im) / den
    b_re = bre_ref[...]
    b_im = bim_ref[...]
    bbre_ref[...] = q_re * b_re - q_im * b_im
    bbim_ref[...] = q_re * b_im + q_im * b_re
    abre_ref[...] = ab_re
    abim_ref[...] = ab_im


def ssm_prep(a_re, a_im, log_dt, b_re, b_im):
    depth, g, p, h = b_re.shape
    ldt = jnp.broadcast_to(log_dt.reshape(depth, g, 1, 1), (depth, g, p, 1))

    def spec(last):
        return pl.BlockSpec((None, g, p, last), lambda l: (l, 0, 0, 0))

    return pl.pallas_call(
        _ssm_prep_kernel,
        grid=(depth,),
        in_specs=[spec(1), spec(1), spec(1), spec(h), spec(h)],
        out_specs=[spec(h), spec(h), spec(1), spec(1)],
        out_shape=[jax.ShapeDtypeStruct((depth, g, p, h), F32),
                   jax.ShapeDtypeStruct((depth, g, p, h), F32),
                   jax.ShapeDtypeStruct((depth, g, p, 1), F32),
                   jax.ShapeDtypeStruct((depth, g, p, 1), F32)],
        compiler_params=_params(("arbitrary",), 56),
        name="ssm_prep",
    )(a_re.reshape(depth, g, p, 1), a_im.reshape(depth, g, p, 1), ldt, b_re, b_im)


def _block_diag_slabs(w, groups_per_slab):
    depth, g, r, c = w.shape
    ns = g // groups_per_slab
    w = w.reshape(depth, ns, groups_per_slab, r, c)
    eye = jnp.eye(groups_per_slab, dtype=w.dtype)
    out = w[:, :, :, :, None, :] * eye[None, None, :, None, :, None]
    return out.reshape(depth, ns, groups_per_slab * r, groups_per_slab * c)


def _gelu_tanh(x):
    return 0.5 * x * (1.0 + jnp.tanh(0.7978845608028654 * (x + 0.044715 * x * x * x)))


def _ssm_kernel(u_ref, bbd_ref, c_ref, are_ref, aim_ref, d_ref, wglu_ref,
                bglu_ref, o_ref, sre_scr, sim_scr, stre_scr, stim_scr, u_scr, y_scr, *,
                pitch, groups_per_step):
    nb, tc, d_ssm = u_ref.shape
    n_slabs, k_slab, two_sw = bbd_ref.shape
    sw = two_sw // 2
    tiles_per_slab = sw // LANES
    pack = SUBLANES // nb
    n_groups = sre_scr.shape[0]
    rows_all = nb * pitch

    def tile_rows(lane_tile):
        q, h = divmod(lane_tile, pack)
        return q, slice(h * rows_all, (h + 1) * rows_all)

    @pl.when(pl.program_id(0) == 0)
    def _():
        stre_scr[...] = jnp.zeros(stre_scr.shape, F32)
        stim_scr[...] = jnp.zeros(stim_scr.shape, F32)
        u_scr[...] = jnp.zeros(u_scr.shape, F32)

    for b in range(nb):
        u_scr[b * pitch:b * pitch + tc, :] = u_ref[b]

    for s in range(n_slabs):
        r = _dot(u_scr[:, s * k_slab:(s + 1) * k_slab].astype(BF16), bbd_ref[s])
        for j in range(tiles_per_slab):
            qr = tile_rows(s * tiles_per_slab + j)
            sre_scr[qr] = r[:, j * LANES:(j + 1) * LANES]
            sim_scr[qr] = r[:, sw + j * LANES:sw + (j + 1) * LANES]

    for q0 in range(0, n_groups, groups_per_step):
        qs = list(range(q0, min(q0 + groups_per_step, n_groups)))
        a_re = [are_ref[q] for q in qs]
        a_im = [aim_ref[q] for q in qs]

        def step(t, carry, qs=qs, a_re=a_re, a_im=a_im):
            rows = pl.ds(t, SUBLANES, stride=pitch)
            out = []
            for n, q in enumerate(qs):
                s_re, s_im = carry[2 * n], carry[2 * n + 1]
                n_re = a_re[n] * s_re - a_im[n] * s_im + sre_scr[q, rows, :]
                n_im = a_re[n] * s_im + a_im[n] * s_re + sim_scr[q, rows, :]
                sre_scr[q, rows, :] = n_re
                sim_scr[q, rows, :] = n_im
                out += [n_re, n_im]
            return tuple(out)

        init = []
        for q in qs:
            init += [stre_scr[q], stim_scr[q]]
        fin = lax.fori_loop(0, tc, step, tuple(init), unroll=2)
        for n, q in enumerate(qs):
            stre_scr[q] = fin[2 * n]
            stim_scr[q] = fin[2 * n + 1]

    def state_rows(s):
        tiles = [tile_rows(s * tiles_per_slab + j) for j in range(tiles_per_slab)]
        return jnp.concatenate([sre_scr[qr].astype(BF16) for qr in tiles]
                               + [(-sim_scr[qr]).astype(BF16) for qr in tiles], axis=-1)

    for s in range(n_slabs):
        cols = slice(s * k_slab, (s + 1) * k_slab)
        y = _dot(state_rows(s), c_ref[s])
        y_scr[:, cols] = y + d_ref[:, cols] * u_scr[:, cols]
    y = _gelu_tanh(y_scr[...])
    gate = _dot(y.astype(BF16), wglu_ref[...]) + bglu_ref[...]
    y_scr[...] = y * jax.nn.sigmoid(gate)
    for b in range(nb):
        o_ref[b] = y_scr[b * pitch:b * pitch + tc, :].astype(o_ref.dtype)


def _pack_state_rows(a, nb):
    depth = a.shape[0]
    pack = SUBLANES // nb
    n_groups = a.shape[1] // (LANES * pack)
    a = a.reshape(depth, n_groups, pack, 1, LANES)
    return jnp.broadcast_to(a, (depth, n_groups, pack, nb, LANES)).reshape(
        depth, n_groups, SUBLANES, LANES)


def ssm(u3, bbd, c_cat, a_re, a_im, d_skip, w_glu16, b_glu, layer):
    nb, seq_len, d_ssm = u3.shape
    n_groups = a_re.shape[1]
    assert SUBLANES % nb == 0
    tc = _tile(seq_len, 128)
    pitch = tc + SUBLANES // 2
    assert (nb * pitch) % SUBLANES == 0
    kern = functools.partial(_ssm_kernel, pitch=pitch, groups_per_step=min(4, n_groups))

    def full(a):
        return _layer_spec(layer, a.shape[1:])

    return pl.pallas_call(
        kern,
        grid=(seq_len // tc,),
        in_specs=[pl.BlockSpec((nb, tc, d_ssm), lambda c: (0, c, 0)),
                  full(bbd), full(c_cat), full(a_re), full(a_im),
                  full(d_skip), full(w_glu16), full(b_glu)],
        out_specs=pl.BlockSpec((nb, tc, d_ssm), lambda c: (0, c, 0)),
        out_shape=jax.ShapeDtypeStruct((nb, seq_len, d_ssm), BF16),
        scratch_shapes=[pltpu.VMEM((n_groups, SUBLANES * pitch, LANES), F32),
                        pltpu.VMEM((n_groups, SUBLANES * pitch, LANES), F32),
                        pltpu.VMEM((n_groups, SUBLANES, LANES), F32),
                        pltpu.VMEM((n_groups, SUBLANES, LANES), F32),
                        pltpu.VMEM((nb * pitch, d_ssm), F32),
                        pltpu.VMEM((nb * pitch, d_ssm), F32)],
        compiler_params=_params(("arbitrary",), 56),
        name="ssm_scan",
    )(u3, bbd, c_cat, a_re, a_im, d_skip, w_glu16, b_glu)


def _mixer_out_kernel(*refs, routed):
    if routed:
        (yc_ref, ys_ref, wo_ref, x_ref, gt_ref, gpost_ref, gpre_ref, sc_ref,
         sh_ref, rw_ref, rb_ref, xo_ref, h_ref, lg_ref, y_scr) = refs
    else:
        (yc_ref, ys_ref, wo_ref, x_ref, gt_ref, gpost_ref, gpre_ref, sc_ref,
         sh_ref, xo_ref, h_ref, y_scr) = refs
    tm = yc_ref.shape[0]
    y_scr[...] = _dot(jnp.concatenate([yc_ref[...], ys_ref[...]], axis=-1), wo_ref[...])
    post_scale = gt_ref[...] * gpost_ref[...]
    pre_scale = gpre_ref[...] * (1.0 + sc_ref[...])
    for r0 in range(0, tm, EPILOGUE_ROWS):
        rows = slice(r0, r0 + EPILOGUE_ROWS)
        x_new = x_ref[rows, :] + _rms(y_scr[rows, :]) * post_scale
        xo_ref[rows, :] = x_new
        h_ref[rows, :] = (_rms(x_new) * pre_scale + sh_ref[...]).astype(h_ref.dtype)
    if routed:
        h = h_ref[...]
        w = rw_ref[...]
        h_hi = h.astype(BF16)
        w_hi = w.astype(BF16)
        h_lo = (h - h_hi.astype(F32)).astype(BF16)
        w_lo = (w - w_hi.astype(F32)).astype(BF16)
        lg_ref[...] = (_dot(h_hi, w_hi) + (_dot(h_hi, w_lo) + _dot(h_lo, w_hi))
                       + rb_ref[...])


def mixer_out(y_conv, y_ssm, w_out16, x, mod5, g_post, g_pre, layer, router=None, *,
              seq_len):
    t, d = x.shape
    d_conv = y_conv.shape[1]
    d_ssm = y_ssm.shape[1]
    tm = _tile(seq_len, 512)
    tiles_per_seq = seq_len // tm
    routed = router is not None

    def batch_of(i):
        return i // tiles_per_seq

    def row_spec(n):
        return pl.BlockSpec((tm, n), lambda i: (i, 0))

    def const_spec(shape):
        return pl.BlockSpec(shape, lambda i: (0, 0))

    in_specs = [row_spec(d_conv), row_spec(d_ssm),
                _layer_spec(layer, (d_conv + d_ssm, d), pipeline_mode=pl.Buffered(1)),
                row_spec(d), _mod_spec(layer, GT_MIX, d, batch_of),
                _layer_spec(layer, (1, d)), _layer_spec(layer, (1, d)),
                _mod_spec(layer, SC_FFN, d, batch_of), _mod_spec(layer, SH_FFN, d, batch_of)]
    args = [y_conv, y_ssm, w_out16, x, mod5, g_post, g_pre, mod5, mod5]
    out_specs = [row_spec(d), row_spec(d)]
    out_shape = [jax.ShapeDtypeStruct((t, d), F32),
                 jax.ShapeDtypeStruct((t, d), F32 if routed else BF16)]
    if routed:
        rw_pad, rb_pad = router
        in_specs += [const_spec(rw_pad.shape), const_spec(rb_pad.shape)]
        args += [rw_pad, rb_pad]
        out_specs.append(row_spec(LANES))
        out_shape.append(jax.ShapeDtypeStruct((t, LANES), F32))
    return pl.pallas_call(
        functools.partial(_mixer_out_kernel, routed=routed),
        grid=(t // tm,),
        in_specs=in_specs, out_specs=out_specs, out_shape=out_shape,
        scratch_shapes=[pltpu.VMEM((tm, d), F32)],
        compiler_params=_params(("arbitrary",), 56),
        name="mixer_out_routed" if routed else "mixer_out",
    )(*args)


def _ffn_kernel(h_ref, wg_ref, wu_ref, wd_ref, x_hbm, gt_ref, gpost_ref, o_ref,
                wg16, wu16, wd16, x_buf, sem, *, sub_rows):
    i = pl.program_id(0)
    k = pl.program_id(1)
    tm = o_ref.shape[0]
    x_copy = pltpu.make_async_copy(x_hbm.at[pl.ds(i * tm, tm)], x_buf, sem)

    @pl.when(k == 0)
    def _():
        o_ref[...] = jnp.zeros(o_ref.shape, F32)
        x_copy.start()

    wg16[...] = wg_ref[...].astype(BF16)
    wu16[...] = wu_ref[...].astype(BF16)
    wd16[...] = wd_ref[...].astype(BF16)
    for r0 in range(0, tm, sub_rows):
        h = h_ref[r0:r0 + sub_rows, :]
        gate = _dot(h, wg16[...])
        act = (gate * jax.nn.sigmoid(gate)) * _dot(h, wu16[...])
        o_ref[r0:r0 + sub_rows, :] += _dot(act.astype(BF16), wd16[...])

    @pl.when(k == pl.num_programs(1) - 1)
    def _():
        x_copy.wait()
        scale = gt_ref[...] * gpost_ref[...]
        for r0 in range(0, tm, EPILOGUE_ROWS):
            rows = slice(r0, r0 + EPILOGUE_ROWS)
            o_ref[rows, :] = x_buf[rows, :] + _rms(o_ref[rows, :]) * scale


def ffn_dense(h16, w_gate, w_up, w_down, x, mod5, g_post, layer, ffn_index, *, seq_len):
    t, d = x.shape
    d_ff = w_gate.shape[2]
    tm = _tile(seq_len, 1024)
    sub_rows = _tile(tm, 512)
    tk = _tile(d_ff, 256)
    tiles_per_seq = seq_len // tm
    return pl.pallas_call(
        functools.partial(_ffn_kernel, sub_rows=sub_rows),
        grid=(t // tm, d_ff // tk),
        in_specs=[
            pl.BlockSpec((tm, d), lambda i, k: (i, 0)),
            pl.BlockSpec((None, d, tk), lambda i, k: (ffn_index, 0, k)),
            pl.BlockSpec((None, d, tk), lambda i, k: (ffn_index, 0, k)),
            pl.BlockSpec((None, tk, d), lambda i, k: (ffn_index, k, 0)),
            pl.BlockSpec(memory_space=pl.ANY),
            _mod_spec(layer, GT_FFN, d, lambda i: i // tiles_per_seq),
            _layer_spec(layer, (1, d)),
        ],
        out_specs=pl.BlockSpec((tm, d), lambda i, k: (i, 0)),
        out_shape=jax.ShapeDtypeStruct((t, d), F32),
        scratch_shapes=[pltpu.VMEM((d, tk), BF16), pltpu.VMEM((d, tk), BF16),
                        pltpu.VMEM((tk, d), BF16), pltpu.VMEM((tm, d), F32),
                        pltpu.SemaphoreType.DMA(())],
        compiler_params=_params(("arbitrary", "arbitrary"), 56),
        name="ffn_dense",
    )(h16, w_gate, w_up, w_down, x, mod5, g_post)


def _router_kernel(lg_ref, ri_ref, rw_ref, cnt_ref, carry_scr, *, n_experts):
    @pl.when(pl.program_id(0) == 0)
    def _():
        carry_scr[...] = jnp.zeros(carry_scr.shape, F32)

    tm = lg_ref.shape[0]
    lane = lax.broadcasted_iota(jnp.int32, (tm, LANES), 1)
    neg = jnp.float32(-jnp.inf)
    logit = jnp.where(lane < n_experts, lg_ref[...], neg)
    m1 = jnp.max(logit, axis=-1, keepdims=True)
    i1 = jnp.min(jnp.where(logit == m1, lane, LANES), axis=-1, keepdims=True)
    rest = jnp.where(lane == i1, neg, logit)
    m2 = jnp.max(rest, axis=-1, keepdims=True)
    i2 = jnp.min(jnp.where(rest == m2, lane, LANES), axis=-1, keepdims=True)
    e2 = jnp.exp(m2 - m1)
    w1 = 1.0 / (1.0 + e2)
    w2 = e2 / (1.0 + e2)
    hit1 = lane == i1
    hit2 = lane == i2
    onehot = jnp.where(hit1 | hit2, 1.0, 0.0)
    r = lax.broadcasted_iota(jnp.int32, (tm, tm), 0)
    c = lax.broadcasted_iota(jnp.int32, (tm, tm), 1)
    earlier = jnp.where(c < r, 1.0, 0.0).astype(BF16)
    carry = carry_scr[0:1, :]
    before = _dot(earlier, onehot.astype(BF16)) + carry
    pos1 = jnp.sum(jnp.where(hit1, before, 0.0), axis=-1, keepdims=True).astype(jnp.int32)
    pos2 = jnp.sum(jnp.where(hit2, before, 0.0), axis=-1, keepdims=True).astype(jnp.int32)
    total = carry + jnp.sum(onehot, axis=0, keepdims=True)
    carry_scr[...] = jnp.broadcast_to(total, carry_scr.shape)
    cnt_ref[...] = jnp.broadcast_to(total, cnt_ref.shape).astype(jnp.int32)
    zero_i = jnp.zeros((tm, LANES), jnp.int32)
    ri_ref[...] = jnp.where(lane == 0, i1, jnp.where(lane == 1, i2,
                            jnp.where(lane == 2, pos1, jnp.where(lane == 3, pos2, zero_i))))
    rw_ref[...] = jnp.where(lane == 0, w1, jnp.where(lane == 1, w2, 0.0))


def router(logits, n_experts):
    t = logits.shape[0]
    tm = _tile(t, 256)
    return pl.pallas_call(
        functools.partial(_router_kernel, n_experts=n_experts),
        grid=(t // tm,),
        in_specs=[pl.BlockSpec((tm, LANES), lambda i: (i, 0))],
        out_specs=[pl.BlockSpec((tm, LANES), lambda i: (i, 0)),
                   pl.BlockSpec((tm, LANES), lambda i: (i, 0)),
                   pl.BlockSpec((SUBLANES, LANES), lambda i: (0, 0))],
        out_shape=[jax.ShapeDtypeStruct((t, LANES), jnp.int32),
                   jax.ShapeDtypeStruct((t, LANES), F32),
                   jax.ShapeDtypeStruct((SUBLANES, LANES), jnp.int32)],
        scratch_shapes=[pltpu.VMEM((SUBLANES, LANES), F32)],
        compiler_params=_params(("arbitrary",), 32),
        name="router_top2",
    )(logits)


def _moe_ffn_kernel(ge_ref, nr_ref, nu_ref, tok_ref, h_hbm, wg_ref, wu_ref, wd_ref, o_ref,
                    h_scr, gbuf, wg16, wu16, wd16, sem, *, rows_per_step):
    g = pl.program_id(0)
    k = pl.program_id(1)
    n_used = nu_ref[0]
    n_sub_max, ts, _ = h_scr.shape
    group_rows = n_sub_max * ts
    nk = pl.num_programs(1)
    n_gather = gbuf.shape[0]

    def row_copy(src_row, dst_row):
        return pltpu.make_async_copy(h_hbm.at[pl.ds(src_row, 1)],
                                     gbuf.at[pl.ds(dst_row, 1)], sem)

    def for_each_sub_tile(group, fn):
        for j in range(n_sub_max):
            @pl.when(j * ts < nr_ref[group])
            def _(j=j):
                fn(j)

    def wait_rows():
        pltpu.make_async_copy(h_hbm.at[pl.ds(0, n_gather)], gbuf, sem).wait()

    @pl.when(k == 0)
    def _():
        o_ref[...] = jnp.zeros(o_ref.shape, o_ref.dtype)

    @pl.when(g < n_used)
    def _():
        @pl.when(k == 0)
        def _():
            @pl.when(g == 0)
            def _():
                def body(i, carry):
                    row_copy(tok_ref[i], i).start()
                    return carry
                lax.fori_loop(0, n_gather, body, 0, unroll=8)

            wait_rows()

            def cast(j):
                h_scr[j] = gbuf[j * ts:(j + 1) * ts, :].astype(BF16)
            for_each_sub_tile(g, cast)

        next_base = jnp.minimum(g + 1, n_used - 1) * group_rows + k * rows_per_step

        def sub_tile(j, m):
            if j == 0:
                wg16[...] = wg_ref[...].astype(BF16)
                wu16[...] = wu_ref[...].astype(BF16)
                wd16[...] = wd_ref[...].astype(BF16)
                for i in range(rows_per_step):
                    row_copy(tok_ref[next_base + i], k * rows_per_step + i).start()
            h = h_scr[j, 0:m, :]
            gate = _dot(h, wg16[...])
            act = (gate * jax.nn.sigmoid(gate)) * _dot(h, wu16[...])
            o_ref[j * ts:j * ts + m, :] += _dot(act.astype(BF16), wd16[...])

        half = ts // 2
        for j in range(n_sub_max):
            rows_j = nr_ref[g] - j * ts

            @pl.when(rows_j > half)
            def _(j=j):
                sub_tile(j, ts)

            @pl.when((rows_j > 0) & (rows_j <= half))
            def _(j=j):
                sub_tile(j, half)

        @pl.when((g == n_used - 1) & (k == nk - 1))
        def _():
            wait_rows()


def moe_ffn(group_expert, group_rows_used, n_used, row_tok, h, w_gate, w_up, w_down, layer, *,
            sub_rows, n_sub_max):
    d = h.shape[1]
    d_ff = w_gate.shape[3]
    tk = _tile(d_ff, 256)
    nk = d_ff // tk
    group_rows = sub_rows * n_sub_max
    rows_per_step = -(-group_rows // (nk * SUBLANES)) * SUBLANES
    n_gather = rows_per_step * nk
    n_rows = group_expert.shape[0] * group_rows
    assert row_tok.shape[0] - n_rows >= n_gather - group_rows

    def ff_idx(g, k, nu):
        return jnp.where(g < nu[0], k, nk - 1)

    return pl.pallas_call(
        functools.partial(_moe_ffn_kernel, rows_per_step=rows_per_step),
        grid_spec=pltpu.PrefetchScalarGridSpec(
            num_scalar_prefetch=4,
            grid=(n_rows // group_rows, nk),
            in_specs=[
                pl.BlockSpec(memory_space=pl.ANY),
                pl.BlockSpec((None, None, d, tk),
                             lambda g, k, ge, ns, nu, tok: (layer, ge[g], 0, ff_idx(g, k, nu))),
                pl.BlockSpec((None, None, d, tk),
                             lambda g, k, ge, ns, nu, tok: (layer, ge[g], 0, ff_idx(g, k, nu))),
                pl.BlockSpec((None, None, tk, d),
                             lambda g, k, ge, ns, nu, tok: (layer, ge[g], ff_idx(g, k, nu), 0)),
            ],
            out_specs=pl.BlockSpec((group_rows, d), lambda g, k, ge, ns, nu, tok: (g, 0)),
            scratch_shapes=[pltpu.VMEM((n_sub_max, sub_rows, d), BF16),
                            pltpu.VMEM((n_gather, d), F32),
                            pltpu.VMEM((d, tk), BF16), pltpu.VMEM((d, tk), BF16),
                            pltpu.VMEM((tk, d), BF16),
                            pltpu.SemaphoreType.DMA(())],
        ),
        out_shape=jax.ShapeDtypeStruct((n_rows, d), F32),
        compiler_params=_params(("arbitrary", "arbitrary"), 56),
        name="moe_ffn",
    )(group_expert, group_rows_used, n_used, row_tok, h, w_gate, w_up, w_down)


def _moe_combine_kernel(d0_ref, d1_ref, ys_ref, rw_ref, x_ref, gt_ref, gpost_ref, o_ref,
                        buf, sems):
    i = pl.program_id(0)
    tm = x_ref.shape[0]
    slot = i % 2

    def start_tile(tile, s):
        def body(t, carry):
            for which, dest in enumerate((d0_ref, d1_ref)):
                pltpu.make_async_copy(ys_ref.at[pl.ds(dest[tile * tm + t], 1)],
                                      buf.at[s, pl.ds(which * tm + t, 1)],
                                      sems.at[s]).start()
            return carry
        lax.fori_loop(0, tm, body, 0, unroll=8)

    @pl.when(i == 0)
    def _():
        start_tile(0, 0)

    @pl.when(i + 1 < pl.num_programs(0))
    def _():
        start_tile(i + 1, 1 - slot)

    pltpu.make_async_copy(ys_ref.at[pl.ds(0, 2 * tm)], buf.at[slot], sems.at[slot]).wait()
    scale = gt_ref[...] * gpost_ref[...]
    for r0 in range(0, tm, EPILOGUE_ROWS):
        rows = slice(r0, r0 + EPILOGUE_ROWS)
        y = (buf[slot, r0:r0 + EPILOGUE_ROWS, :] * rw_ref[rows, 0:1]
             + buf[slot, tm + r0:tm + r0 + EPILOGUE_ROWS, :] * rw_ref[rows, 1:2])
        o_ref[rows, :] = x_ref[rows, :] + _rms(y) * scale


def moe_combine(dest0, dest1, ys, route_w, x, mod5, g_post, layer, *, seq_len):
    t, d = x.shape
    tm = _tile(seq_len, 256)
    tiles_per_seq = seq_len // tm
    return pl.pallas_call(
        _moe_combine_kernel,
        grid_spec=pltpu.PrefetchScalarGridSpec(
            num_scalar_prefetch=2,
            grid=(t // tm,),
            in_specs=[
                pl.BlockSpec(memory_space=pl.ANY),
                pl.BlockSpec((tm, LANES), lambda i, d0, d1: (i, 0)),
                pl.BlockSpec((tm, d), lambda i, d0, d1: (i, 0)),
                _mod_spec(layer, GT_FFN, d, lambda i: i // tiles_per_seq),
                _layer_spec(layer, (1, d)),
            ],
            out_specs=pl.BlockSpec((tm, d), lambda i, d0, d1: (i, 0)),
            scratch_shapes=[pltpu.VMEM((2, 2 * tm, d), F32), pltpu.SemaphoreType.DMA((2,))],
        ),
        out_shape=jax.ShapeDtypeStruct((t, d), F32),
        compiler_params=_params(("arbitrary",), 32),
        name="moe_combine",
    )(dest0, dest1, ys, route_w, x, mod5, g_post)


MOE_SUB_ROWS = 512
MOE_GROUP_SUBS = 2


def moe_layer(h, logits, x, mod5, g_post, w_gate, w_up, w_down, layer, moe_index, *,
              seq_len):
    t, d = h.shape
    n_experts = w_gate.shape[1]
    ts = min(MOE_SUB_ROWS, t)
    group_rows = ts * MOE_GROUP_SUBS
    route_i, route_w, counts = router(logits, n_experts)
    e0, e1, pos0, pos1 = (route_i[:, k] for k in range(4))
    counts = counts[0, :n_experts]
    groups_e = (counts + group_rows - 1) // group_rows
    g_end = jnp.cumsum(groups_e)
    g_start = g_end - groups_e
    dest0 = g_start[e0] * group_rows + pos0
    dest1 = g_start[e1] * group_rows + pos1
    n_groups = -(-(2 * t) // group_rows) + n_experts
    n_rows = n_groups * group_rows
    tok = jnp.arange(t, dtype=jnp.int32)
    row_tok = jnp.zeros((n_rows + group_rows,), jnp.int32).at[
        jnp.concatenate([dest0, dest1])].set(jnp.concatenate([tok, tok]))
    n_used = g_end[-1].astype(jnp.int32)
    gid = jnp.arange(n_groups, dtype=jnp.int32)
    group_e = jnp.clip(jnp.searchsorted(g_end, gid, side='right'), 0, n_experts - 1)
    group_e = jnp.where(gid < n_used, group_e, group_e[n_used - 1]).astype(jnp.int32)
    group_rows_used = jnp.where(
        gid < n_used,
        jnp.clip(counts[group_e] - (gid - g_start[group_e]) * group_rows, 0, group_rows),
        0).astype(jnp.int32)
    ys = moe_ffn(group_e, group_rows_used, n_used.reshape(1), row_tok, h, w_gate, w_up, w_down,
                 moe_index, sub_rows=ts, n_sub_max=MOE_GROUP_SUBS)
    return moe_combine(dest0, dest1, ys, route_w, x, mod5, g_post, layer, seq_len=seq_len)


def kernel(x, c, w_ada, b_ada, g_mix_pre, g_mix_post, g_ffn_pre, g_ffn_post, w_in, conv_w,
           ssm_a_re, ssm_a_im, ssm_log_dt, ssm_b_re, ssm_b_im, ssm_c_re, ssm_c_im, ssm_d,
           w_glu, b_glu, w_out, ffn_w_gate, ffn_w_up, ffn_w_down, router_w, router_b,
           moe_w_gate, moe_w_up, moe_w_down):
    nb, seq_len, d = x.shape
    depth = w_ada.shape[0]
    d_conv = conv_w.shape[1]
    g, p, h_dim = ssm_b_re.shape[1:]
    d_ssm = g * h_dim
    n_experts = router_w.shape[2]
    t = nb * seq_len
    groups_per_slab = max(1, min(g, MXU_DIM // h_dim))

    mod_rows = 2 * SUBLANES
    c_pad = jnp.zeros((mod_rows, d), F32).at[:nb].set(c)
    mod = adaln_mod(c_pad, w_ada, b_ada)
    mod5 = mod.reshape(depth, mod_rows, 6, 1, d)
    xf = x.reshape(t, d)

    def rows(v):
        return v.reshape(v.shape[0], 1, v.shape[1])
    g_mix_pre, g_mix_post, g_ffn_pre, g_ffn_post = (
        rows(v) for v in (g_mix_pre, g_mix_post, g_ffn_pre, g_ffn_post))
    w_in16 = w_in.astype(BF16)
    w_out16 = w_out.astype(BF16)
    w_glu16 = w_glu.astype(BF16)
    conv_wt = jnp.swapaxes(conv_w, 1, 2)
    bb_re, bb_im, ab_re, ab_im = ssm_prep(ssm_a_re, ssm_a_im, ssm_log_dt, ssm_b_re, ssm_b_im)
    def slabs(w):
        return _block_diag_slabs(jnp.swapaxes(w, 2, 3).astype(BF16), groups_per_slab)
    bbd = jnp.concatenate([slabs(bb_re), slabs(bb_im)], axis=-1)
    c_cat = jnp.concatenate([slabs(ssm_c_re), slabs(ssm_c_im)], axis=-2)
    a_re = _pack_state_rows(ab_re.reshape(depth, g * p), nb)
    a_im = _pack_state_rows(ab_im.reshape(depth, g * p), nb)
    d_skip = ssm_d.reshape(depth, 1, d_ssm)
    b_glu = rows(b_glu)
    n_moe = router_w.shape[0]
    rw_pad = jnp.zeros((n_moe, d, LANES), F32).at[:, :, :n_experts].set(router_w)
    rb_pad = jnp.zeros((n_moe, 1, LANES), F32).at[:, 0, :n_experts].set(router_b)

    for l in range(depth):
        y_conv, u = mixer_in(xf, mod5, g_mix_pre, w_in16, conv_wt, l, seq_len=seq_len)
        y_ssm = ssm(u.reshape(nb, seq_len, d_ssm), bbd, c_cat, a_re, a_im, d_skip,
                    w_glu16, b_glu, l).reshape(t, d_ssm)
        routed = l % 2 == 1
        i = l // 2
        rt = (rw_pad[i], rb_pad[i]) if routed else None
        outs = mixer_out(y_conv, y_ssm, w_out16, xf, mod5, g_mix_post, g_ffn_pre, l, rt,
                         seq_len=seq_len)
        if routed:
            xf, h, logits = outs
            xf = moe_layer(h, logits, xf, mod5, g_ffn_post, moe_w_gate, moe_w_up, moe_w_down,
                           l, i, seq_len=seq_len)
        else:
            xf, h = outs
            xf = ffn_dense(h, ffn_w_gate, ffn_w_up, ffn_w_down, xf, mod5, g_ffn_post, l, i,
                           seq_len=seq_len)
    return xf.reshape(nb, seq_len, d)
```

```python
import functools

import jax
import jax.numpy as jnp
from jax import lax
from jax.experimental import pallas as pl
from jax.experimental.pallas import tpu as pltpu

F32 = jnp.float32
BF16 = jnp.bfloat16
NORM_EPS = 1e-6
LANES = 128
SUBLANES = 8
MXU_DIM = 256
EPILOGUE_ROWS = 64
FFN_EPILOGUE_ROWS = 16
MIB = 1024 * 1024


def _params(semantics, vmem_mib):
    return pltpu.CompilerParams(dimension_semantics=semantics,
                                vmem_limit_bytes=vmem_mib * MIB)


def _dot(a, b):
    return jnp.dot(a, b, preferred_element_type=F32)


def _rms(x):
    return x * lax.rsqrt(jnp.mean(x * x, axis=-1, keepdims=True) + NORM_EPS)


def _tile(n, want):
    t = min(n, want)
    while n % t:
        t -= 1
    return t


def _adaln_kernel(c_ref, w_ref, b_ref, o_ref):
    c = c_ref[...]
    c_act = c * jax.nn.sigmoid(c)
    c_hi = c_act.astype(BF16)
    c_lo = (c_act - c_hi.astype(F32)).astype(BF16)
    w16 = w_ref[...].astype(BF16)
    o_ref[...] = (_dot(c_hi, w16) + _dot(c_lo, w16)) + b_ref[...]


def adaln_mod(c_pad, w_ada, b_ada):
    depth, d, n = w_ada.shape
    rows = c_pad.shape[0]
    tn = _tile(n, 2048)
    return pl.pallas_call(
        _adaln_kernel,
        grid=(depth, n // tn),
        in_specs=[
            pl.BlockSpec((rows, d), lambda l, j: (0, 0)),
            pl.BlockSpec((None, d, tn), lambda l, j: (l, 0, j)),
            pl.BlockSpec((None, 1, tn), lambda l, j: (l, 0, j)),
        ],
        out_specs=pl.BlockSpec((None, rows, tn), lambda l, j: (l, 0, j)),
        out_shape=jax.ShapeDtypeStruct((depth, rows, n), F32),
        compiler_params=_params(("arbitrary", "arbitrary"), 56),
        name="adaln_mod",
    )(c_pad, w_ada, b_ada.reshape(depth, 1, n))


def _mixer_in_kernel(x_ref, sc_ref, sh_ref, g_ref, w_ref, cw_ref, yconv_ref, u_ref,
                     halo_scr, h_scr, *, tiles_per_seq, col_chunk):
    tm = x_ref.shape[0]
    d_conv = yconv_ref.shape[1]
    d_ssm = u_ref.shape[1]
    pre_scale = g_ref[...] * (1.0 + sc_ref[...])
    for r0 in range(0, tm, EPILOGUE_ROWS):
        rows = slice(r0, r0 + EPILOGUE_ROWS)
        h_scr[rows, :] = (_rms(x_ref[rows, :]) * pre_scale + sh_ref[...]).astype(BF16)
    h = h_scr[...]

    @pl.when(pl.program_id(0) % tiles_per_seq == 0)
    def _():
        halo_scr[...] = jnp.zeros(halo_scr.shape, F32)

    for c0 in range(0, d_conv, col_chunk):
        cols = slice(c0, c0 + col_chunk)
        gate_b = _dot(h, w_ref[:, c0:c0 + col_chunk])
        z = (_dot(h, w_ref[:, d_conv + c0:d_conv + c0 + col_chunk])
             * _dot(h, w_ref[:, 2 * d_conv + c0:2 * d_conv + c0 + col_chunk]))
        prev = halo_scr[:, cols]
        halo_scr[:, cols] = z[tm - SUBLANES:, :]
        w0 = cw_ref[0:1, cols]
        w1 = cw_ref[1:2, cols]
        w2 = cw_ref[2:3, cols]
        z1 = pltpu.roll(z, 1, 0)
        z2 = pltpu.roll(z, 2, 0)
        yconv_ref[:, cols] = (gate_b * (z * w2 + z1 * w1 + z2 * w0)).astype(yconv_ref.dtype)
        row = lax.broadcasted_iota(jnp.int32, (SUBLANES, col_chunk), 0)
        z1t = jnp.where(row < 1, pltpu.roll(prev, 1, 0), z1[:SUBLANES, :])
        z2t = jnp.where(row < 2, pltpu.roll(prev, 2, 0), z2[:SUBLANES, :])
        top = gate_b[:SUBLANES, :] * (z[:SUBLANES, :] * w2 + z1t * w1 + z2t * w0)
        yconv_ref[0:SUBLANES, cols] = top.astype(yconv_ref.dtype)

    for c0 in range(0, d_ssm, col_chunk):
        u_ref[:, c0:c0 + col_chunk] = _dot(
            h, w_ref[:, 3 * d_conv + c0:3 * d_conv + c0 + col_chunk])


SH_MIX, SC_MIX, GT_MIX, SH_FFN, SC_FFN, GT_FFN = range(6)


def _mod_spec(layer, chunk, d, batch_of):
    return pl.BlockSpec((None, None, None, 1, d),
                        lambda i, *_: (layer, batch_of(i), chunk, 0, 0))


def _layer_spec(layer, shape, **kw):
    zeros = (0,) * len(shape)
    return pl.BlockSpec((None,) + tuple(shape), lambda *_: (layer,) + zeros, **kw)


def mixer_in(x, mod5, g_pre, w_in16, conv_wt, layer, *, seq_len):
    t, d = x.shape
    d_in = w_in16.shape[2]
    d_conv = conv_wt.shape[2]
    d_ssm = d_in - 3 * d_conv
    tm = _tile(seq_len, 512)
    col_chunk = _tile(min(d_conv, d_ssm), 512)
    assert d_conv % col_chunk == 0 and d_ssm % col_chunk == 0
    tiles_per_seq = seq_len // tm

    def batch_of(i):
        return i // tiles_per_seq

    kern = functools.partial(_mixer_in_kernel, tiles_per_seq=tiles_per_seq,
                             col_chunk=col_chunk)
    return pl.pallas_call(
        kern,
        grid=(t // tm,),
        in_specs=[
            pl.BlockSpec((tm, d), lambda i: (i, 0)),
            _mod_spec(layer, SC_MIX, d, batch_of), _mod_spec(layer, SH_MIX, d, batch_of),
            _layer_spec(layer, (1, d)),
            _layer_spec(layer, (d, d_in), pipeline_mode=pl.Buffered(1)),
            _layer_spec(layer, (3, d_conv)),
        ],
        out_specs=[
            pl.BlockSpec((tm, d_conv), lambda i: (i, 0)),
            pl.BlockSpec((tm, d_ssm), lambda i: (i, 0)),
        ],
        out_shape=[jax.ShapeDtypeStruct((t, d_conv), BF16),
                   jax.ShapeDtypeStruct((t, d_ssm), F32)],
        scratch_shapes=[pltpu.VMEM((SUBLANES, d_conv), F32), pltpu.VMEM((tm, d), BF16)],
        compiler_params=_params(("arbitrary",), 48),
        name="mixer_in",
    )(x, mod5, mod5, g_pre, w_in16, conv_wt)


def _ssm_prep_kernel(are_ref, aim_ref, ldt_ref, bre_ref, bim_ref,
                     bbre_ref, bbim_ref, abre_ref, abim_ref):
    lam_re = are_ref[...]
    lam_im = aim_ref[...]
    dt = jnp.exp(ldt_ref[...])
    mag = jnp.exp(lam_re * dt)
    ang = lam_im * dt
    ab_re = mag * jnp.cos(ang)
    ab_im = mag * jnp.sin(ang)
    den = lam_re * lam_re + lam_im * lam_im
    nr = ab_re - 1.0
    q_re = (nr * lam_re + ab_im * lam_im) / den
    q_im = (ab_im * lam_re - nr * lam_im) / den
    b_re = bre_ref[...]
    b_im = bim_ref[...]
    bbre_ref[...] = q_re * b_re - q_im * b_im
    bbim_ref[...] = q_re * b_im + q_im * b_re
    abre_ref[...] = ab_re
    abim_ref[...] = ab_im


def ssm_prep(a_re, a_im, log_dt, b_re, b_im):
    depth, g, p, h = b_re.shape
    ldt = jnp.broadcast_to(log_dt.reshape(depth, g, 1, 1), (depth, g, p, 1))

    def spec(last):
        return pl.BlockSpec((None, g, p, last), lambda l: (l, 0, 0, 0))

    return pl.pallas_call(
        _ssm_prep_kernel,
        grid=(depth,),
        in_specs=[spec(1), spec(1), spec(1), spec(h), spec(h)],
        out_specs=[spec(h), spec(h), spec(1), spec(1)],
        out_shape=[jax.ShapeDtypeStruct((depth, g, p, h), F32),
                   jax.ShapeDtypeStruct((depth, g, p, h), F32),
                   jax.ShapeDtypeStruct((depth, g, p, 1), F32),
                   jax.ShapeDtypeStruct((depth, g, p, 1), F32)],
        compiler_params=_params(("arbitrary",), 56),
        name="ssm_prep",
    )(a_re.reshape(depth, g, p, 1), a_im.reshape(depth, g, p, 1), ldt, b_re, b_im)


def _block_diag_slabs(w, groups_per_slab):
    depth, g, r, c = w.shape
    ns = g // groups_per_slab
    w = w.reshape(depth, ns, groups_per_slab, r, c)
    eye = jnp.eye(groups_per_slab, dtype=w.dtype)
    out = w[:, :, :, :, None, :] * eye[None, None, :, None, :, None]
    return out.reshape(depth, ns, groups_per_slab * r, groups_per_slab * c)


def _gelu_tanh(x):
    return 0.5 * x * (1.0 + jnp.tanh(0.7978845608028654 * (x + 0.044715 * x * x * x)))


def _ssm_kernel(u_ref, bbd_ref, c_ref, are_ref, aim_ref, d_ref, wglu_ref,
                bglu_ref, o_ref, sre_scr, sim_scr, stre_scr, stim_scr, u_scr, y_scr, *,
                pitch, groups_per_step):
    nb, tc, d_ssm = u_ref.shape
    n_slabs, k_slab, two_sw = bbd_ref.shape
    sw = two_sw // 2
    tiles_per_slab = sw // LANES
    pack = SUBLANES // nb
    n_groups = sre_scr.shape[0]
    rows_all = nb * pitch

    def tile_rows(lane_tile):
        q, h = divmod(lane_tile, pack)
        return q, slice(h * rows_all, (h + 1) * rows_all)

    @pl.when(pl.program_id(0) == 0)
    def _():
        stre_scr[...] = jnp.zeros(stre_scr.shape, F32)
        stim_scr[...] = jnp.zeros(stim_scr.shape, F32)
        u_scr[...] = jnp.zeros(u_scr.shape, F32)

    for b in range(nb):
        u_scr[b * pitch:b * pitch + tc, :] = u_ref[b]

    for s in range(n_slabs):
        r = _dot(u_scr[:, s * k_slab:(s + 1) * k_slab].astype(BF16), bbd_ref[s])
        for j in range(tiles_per_slab):
            qr = tile_rows(s * tiles_per_slab + j)
            sre_scr[qr] = r[:, j * LANES:(j + 1) * LANES]
            sim_scr[qr] = r[:, sw + j * LANES:sw + (j + 1) * LANES]

    for q0 in range(0, n_groups, groups_per_step):
        qs = list(range(q0, min(q0 + groups_per_step, n_groups)))
        a_re = [are_ref[q] for q in qs]
        a_im = [aim_ref[q] for q in qs]

        def step(t, carry, qs=qs, a_re=a_re, a_im=a_im):
            rows = pl.ds(t, SUBLANES, stride=pitch)
            out = []
            for n, q in enumerate(qs):
                s_re, s_im = carry[2 * n], carry[2 * n + 1]
                n_re = a_re[n] * s_re - a_im[n] * s_im + sre_scr[q, rows, :]
                n_im = a_re[n] * s_im + a_im[n] * s_re + sim_scr[q, rows, :]
                sre_scr[q, rows, :] = n_re
                sim_scr[q, rows, :] = n_im
                out += [n_re, n_im]
            return tuple(out)

        init = []
        for q in qs:
            init += [stre_scr[q], stim_scr[q]]
        fin = lax.fori_loop(0, tc, step, tuple(init), unroll=2)
        for n, q in enumerate(qs):
            stre_scr[q] = fin[2 * n]
            stim_scr[q] = fin[2 * n + 1]

    def state_rows(s):
        tiles = [tile_rows(s * tiles_per_slab + j) for j in range(tiles_per_slab)]
        return jnp.concatenate([sre_scr[qr].astype(BF16) for qr in tiles]
                               + [(-sim_scr[qr]).astype(BF16) for qr in tiles], axis=-1)

    for s in range(n_slabs):
        cols = slice(s * k_slab, (s + 1) * k_slab)
        y = _dot(state_rows(s), c_ref[s])
        y_scr[:, cols] = y + d_ref[:, cols] * u_scr[:, cols]
    y = _gelu_tanh(y_scr[...])
    gate = _dot(y.astype(BF16), wglu_ref[...]) + bglu_ref[...]
    y_scr[...] = y * jax.nn.sigmoid(gate)
    for b in range(nb):
        o_ref[b] = y_scr[b * pitch:b * pitch + tc, :].astype(o_ref.dtype)


def _pack_state_rows(a, nb):
    depth = a.shape[0]
    pack = SUBLANES // nb
    n_groups = a.shape[1] // (LANES * pack)
    a = a.reshape(depth, n_groups, pack, 1, LANES)
    return jnp.broadcast_to(a, (depth, n_groups, pack, nb, LANES)).reshape(
        depth, n_groups, SUBLANES, LANES)


def ssm(u3, bbd, c_cat, a_re, a_im, d_skip, w_glu16, b_glu, layer):
    nb, seq_len, d_ssm = u3.shape
    n_groups = a_re.shape[1]
    assert SUBLANES % nb == 0
    tc = _tile(seq_len, 128)
    pitch = tc + SUBLANES // 2
    assert (nb * pitch) % SUBLANES == 0
    kern = functools.partial(_ssm_kernel, pitch=pitch, groups_per_step=min(4, n_groups))

    def full(a):
        return _layer_spec(layer, a.shape[1:])

    return pl.pallas_call(
        kern,
        grid=(seq_len // tc,),
        in_specs=[pl.BlockSpec((nb, tc, d_ssm), lambda c: (0, c, 0)),
                  full(bbd), full(c_cat), full(a_re), full(a_im),
                  full(d_skip), full(w_glu16), full(b_glu)],
        out_specs=pl.BlockSpec((nb, tc, d_ssm), lambda c: (0, c, 0)),
        out_shape=jax.ShapeDtypeStruct((nb, seq_len, d_ssm), BF16),
        scratch_shapes=[pltpu.VMEM((n_groups, SUBLANES * pitch, LANES), F32),
                        pltpu.VMEM((n_groups, SUBLANES * pitch, LANES), F32),
                        pltpu.VMEM((n_groups, SUBLANES, LANES), F32),
                        pltpu.VMEM((n_groups, SUBLANES, LANES), F32),
                        pltpu.VMEM((nb * pitch, d_ssm), F32),
                        pltpu.VMEM((nb * pitch, d_ssm), F32)],
        compiler_params=_params(("arbitrary",), 56),
        name="ssm_scan",
    )(u3, bbd, c_cat, a_re, a_im, d_skip, w_glu16, b_glu)


def _mixer_out_kernel(*refs, routed):
    if routed:
        (yc_ref, ys_ref, wo_ref, x_ref, gt_ref, gpost_ref, gpre_ref, sc_ref,
         sh_ref, rw_ref, rb_ref, xo_ref, h_ref, lg_ref, y_scr) = refs
    else:
        (yc_ref, ys_ref, wo_ref, x_ref, gt_ref, gpost_ref, gpre_ref, sc_ref,
         sh_ref, xo_ref, h_ref, y_scr) = refs
    tm = yc_ref.shape[0]
    y_scr[...] = _dot(jnp.concatenate([yc_ref[...], ys_ref[...]], axis=-1), wo_ref[...])
    post_scale = gt_ref[...] * gpost_ref[...]
    pre_scale = gpre_ref[...] * (1.0 + sc_ref[...])
    for r0 in range(0, tm, EPILOGUE_ROWS):
        rows = slice(r0, r0 + EPILOGUE_ROWS)
        x_new = x_ref[rows, :] + _rms(y_scr[rows, :]) * post_scale
        xo_ref[rows, :] = x_new
        h_ref[rows, :] = (_rms(x_new) * pre_scale + sh_ref[...]).astype(h_ref.dtype)
    if routed:
        h = h_ref[...]
        w = rw_ref[...]
        h_hi = h.astype(BF16)
        w_hi = w.astype(BF16)
        h_lo = (h - h_hi.astype(F32)).astype(BF16)
        w_lo = (w - w_hi.astype(F32)).astype(BF16)
        lg_ref[...] = (_dot(h_hi, w_hi) + (_dot(h_hi, w_lo) + _dot(h_lo, w_hi))
                       + rb_ref[...])


def mixer_out(y_conv, y_ssm, w_out16, x, mod5, g_post, g_pre, layer, router=None, *,
              seq_len):
    t, d = x.shape
    d_conv = y_conv.shape[1]
    d_ssm = y_ssm.shape[1]
    tm = _tile(seq_len, 512)
    tiles_per_seq = seq_len // tm
    routed = router is not None

    def batch_of(i):
        return i // tiles_per_seq

    def row_spec(n):
        return pl.BlockSpec((tm, n), lambda i: (i, 0))

    def const_spec(shape):
        return pl.BlockSpec(shape, lambda i: (0, 0))

    in_specs = [row_spec(d_conv), row_spec(d_ssm),
                _layer_spec(layer, (d_conv + d_ssm, d), pipeline_mode=pl.Buffered(1)),
                row_spec(d), _mod_spec(layer, GT_MIX, d, batch_of),
                _layer_spec(layer, (1, d)), _layer_spec(layer, (1, d)),
                _mod_spec(layer, SC_FFN, d, batch_of), _mod_spec(layer, SH_FFN, d, batch_of)]
    args = [y_conv, y_ssm, w_out16, x, mod5, g_post, g_pre, mod5, mod5]
    out_specs = [row_spec(d), row_spec(d)]
    out_shape = [jax.ShapeDtypeStruct((t, d), F32),
                 jax.ShapeDtypeStruct((t, d), F32 if routed else BF16)]
    if routed:
        rw_pad, rb_pad = router
        in_specs += [const_spec(rw_pad.shape), const_spec(rb_pad.shape)]
        args += [rw_pad, rb_pad]
        out_specs.append(row_spec(LANES))
        out_shape.append(jax.ShapeDtypeStruct((t, LANES), F32))
    return pl.pallas_call(
        functools.partial(_mixer_out_kernel, routed=routed),
        grid=(t // tm,),
        in_specs=in_specs, out_specs=out_specs, out_shape=out_shape,
        scratch_shapes=[pltpu.VMEM((tm, d), F32)],
        compiler_params=_params(("arbitrary",), 56),
        name="mixer_out_routed" if routed else "mixer_out",
    )(*args)


def _ffn_kernel(h_ref, wg_ref, wu_ref, wd_ref, x_hbm, gt_ref, gpost_ref, o_ref,
                wg16, wu16, wd16, x_buf, sem, *, sub_rows):
    i = pl.program_id(0)
    k = pl.program_id(1)
    tm = o_ref.shape[0]
    x_copy = pltpu.make_async_copy(x_hbm.at[pl.ds(i * tm, tm)], x_buf, sem)

    @pl.when(k == 0)
    def _():
        o_ref[...] = jnp.zeros(o_ref.shape, F32)
        x_copy.start()

    wg16[...] = wg_ref[...].astype(BF16)
    wu16[...] = wu_ref[...].astype(BF16)
    wd16[...] = wd_ref[...].astype(BF16)
    for r0 in range(0, tm, sub_rows):
        h = h_ref[r0:r0 + sub_rows, :]
        gate = _dot(h, wg16[...])
        act = (gate * jax.nn.sigmoid(gate)) * _dot(h, wu16[...])
        o_ref[r0:r0 + sub_rows, :] += _dot(act.astype(BF16), wd16[...])

    @pl.when(k == pl.num_programs(1) - 1)
    def _():
        x_copy.wait()
        scale = gt_ref[...] * gpost_ref[...]
        for r0 in range(0, tm, FFN_EPILOGUE_ROWS):
            rows = slice(r0, r0 + FFN_EPILOGUE_ROWS)
            o_ref[rows, :] = x_buf[rows, :] + _rms(o_ref[rows, :]) * scale


def ffn_dense(h16, w_gate, w_up, w_down, x, mod5, g_post, layer, ffn_index, *, seq_len):
    t, d = x.shape
    d_ff = w_gate.shape[2]
    tm = _tile(seq_len, 1024)
    sub_rows = _tile(tm, 512)
    tk = _tile(d_ff, 256)
    tiles_per_seq = seq_len // tm
    return pl.pallas_call(
        functools.partial(_ffn_kernel, sub_rows=sub_rows),
        grid=(t // tm, d_ff // tk),
        in_specs=[
            pl.BlockSpec((tm, d), lambda i, k: (i, 0)),
            pl.BlockSpec((None, d, tk), lambda i, k: (ffn_index, 0, k)),
            pl.BlockSpec((None, d, tk), lambda i, k: (ffn_index, 0, k)),
            pl.BlockSpec((None, tk, d), lambda i, k: (ffn_index, k, 0)),
            pl.BlockSpec(memory_space=pl.ANY),
            _mod_spec(layer, GT_FFN, d, lambda i: i // tiles_per_seq),
            _layer_spec(layer, (1, d)),
        ],
        out_specs=pl.BlockSpec((tm, d), lambda i, k: (i, 0)),
        out_shape=jax.ShapeDtypeStruct((t, d), F32),
        scratch_shapes=[pltpu.VMEM((d, tk), BF16), pltpu.VMEM((d, tk), BF16),
                        pltpu.VMEM((tk, d), BF16), pltpu.VMEM((tm, d), F32),
                        pltpu.SemaphoreType.DMA(())],
        compiler_params=_params(("arbitrary", "arbitrary"), 56),
        name="ffn_dense",
    )(h16, w_gate, w_up, w_down, x, mod5, g_post)


def _router_kernel(lg_ref, ri_ref, rw_ref, cnt_ref, carry_scr, *, n_experts):
    @pl.when(pl.program_id(0) == 0)
    def _():
        carry_scr[...] = jnp.zeros(carry_scr.shape, F32)

    tm = lg_ref.shape[0]
    lane = lax.broadcasted_iota(jnp.int32, (tm, LANES), 1)
    neg = jnp.float32(-jnp.inf)
    logit = jnp.where(lane < n_experts, lg_ref[...], neg)
    m1 = jnp.max(logit, axis=-1, keepdims=True)
    i1 = jnp.min(jnp.where(logit == m1, lane, LANES), axis=-1, keepdims=True)
    rest = jnp.where(lane == i1, neg, logit)
    m2 = jnp.max(rest, axis=-1, keepdims=True)
    i2 = jnp.min(jnp.where(rest == m2, lane, LANES), axis=-1, keepdims=True)
    e2 = jnp.exp(m2 - m1)
    w1 = 1.0 / (1.0 + e2)
    w2 = e2 / (1.0 + e2)
    hit1 = lane == i1
    hit2 = lane == i2
    onehot = jnp.where(hit1 | hit2, 1.0, 0.0)
    r = lax.broadcasted_iota(jnp.int32, (tm, tm), 0)
    c = lax.broadcasted_iota(jnp.int32, (tm, tm), 1)
    earlier = jnp.where(c < r, 1.0, 0.0).astype(BF16)
    carry = carry_scr[0:1, :]
    before = _dot(earlier, onehot.astype(BF16)) + carry
    pos1 = jnp.sum(jnp.where(hit1, before, 0.0), axis=-1, keepdims=True).astype(jnp.int32)
    pos2 = jnp.sum(jnp.where(hit2, before, 0.0), axis=-1, keepdims=True).astype(jnp.int32)
    total = carry + jnp.sum(onehot, axis=0, keepdims=True)
    carry_scr[...] = jnp.broadcast_to(total, carry_scr.shape)
    cnt_ref[...] = jnp.broadcast_to(total, cnt_ref.shape).astype(jnp.int32)
    zero_i = jnp.zeros((tm, LANES), jnp.int32)
    ri_ref[...] = jnp.where(lane == 0, i1, jnp.where(lane == 1, i2,
                            jnp.where(lane == 2, pos1, jnp.where(lane == 3, pos2, zero_i))))
    rw_ref[...] = jnp.where(lane == 0, w1, jnp.where(lane == 1, w2, 0.0))


def router(logits, n_experts):
    t = logits.shape[0]
    tm = _tile(t, 512)
    return pl.pallas_call(
        functools.partial(_router_kernel, n_experts=n_experts),
        grid=(t // tm,),
        in_specs=[pl.BlockSpec((tm, LANES), lambda i: (i, 0))],
        out_specs=[pl.BlockSpec((tm, LANES), lambda i: (i, 0)),
                   pl.BlockSpec((tm, LANES), lambda i: (i, 0)),
                   pl.BlockSpec((SUBLANES, LANES), lambda i: (0, 0))],
        out_shape=[jax.ShapeDtypeStruct((t, LANES), jnp.int32),
                   jax.ShapeDtypeStruct((t, LANES), F32),
                   jax.ShapeDtypeStruct((SUBLANES, LANES), jnp.int32)],
        scratch_shapes=[pltpu.VMEM((SUBLANES, LANES), F32)],
        compiler_params=_params(("arbitrary",), 32),
        name="router_top2",
    )(logits)


def _moe_ffn_kernel(ge_ref, nr_ref, nu_ref, tok_ref, h_hbm, wg_ref, wu_ref, wd_ref, o_ref,
                    h_scr, gbuf, wg16, wu16, wd16, sem, *, rows_per_step):
    g = pl.program_id(0)
    k = pl.program_id(1)
    n_used = nu_ref[0]
    n_sub_max, ts, _ = h_scr.shape
    group_rows = n_sub_max * ts
    nk = pl.num_programs(1)
    n_gather = gbuf.shape[0]

    def row_copy(src_row, dst_row):
        return pltpu.make_async_copy(h_hbm.at[pl.ds(src_row, 1)],
                                     gbuf.at[pl.ds(dst_row, 1)], sem)

    def for_each_sub_tile(group, fn):
        for j in range(n_sub_max):
            @pl.when(j * ts < nr_ref[group])
            def _(j=j):
                fn(j)

    def wait_rows():
        pltpu.make_async_copy(h_hbm.at[pl.ds(0, n_gather)], gbuf, sem).wait()

    @pl.when(k == 0)
    def _():
        o_ref[...] = jnp.zeros(o_ref.shape, o_ref.dtype)

    @pl.when(g < n_used)
    def _():
        @pl.when(k == 0)
        def _():
            @pl.when(g == 0)
            def _():
                def body(i, carry):
                    row_copy(tok_ref[i], i).start()
                    return carry
                lax.fori_loop(0, n_gather, body, 0, unroll=8)

            wait_rows()

            def cast(j):
                h_scr[j] = gbuf[j * ts:(j + 1) * ts, :].astype(BF16)
            for_each_sub_tile(g, cast)

        next_base = jnp.minimum(g + 1, n_used - 1) * group_rows + k * rows_per_step

        def sub_tile(j, m):
            if j == 0:
                wg16[...] = wg_ref[...].astype(BF16)
                wu16[...] = wu_ref[...].astype(BF16)
                wd16[...] = wd_ref[...].astype(BF16)
                for i in range(rows_per_step):
                    row_copy(tok_ref[next_base + i], k * rows_per_step + i).start()
            h = h_scr[j, 0:m, :]
            gate = _dot(h, wg16[...])
            act = (gate * jax.nn.sigmoid(gate)) * _dot(h, wu16[...])
            o_ref[j * ts:j * ts + m, :] += _dot(act.astype(BF16), wd16[...])

        half = ts // 2
        all_full = nr_ref[g] > group_rows - half

        @pl.when(all_full)
        def _():
            for j in range(n_sub_max):
                sub_tile(j, ts)

        @pl.when(jnp.logical_not(all_full))
        def _():
            for j in range(n_sub_max):
                rows_j = nr_ref[g] - j * ts

                @pl.when(rows_j > half)
                def _(j=j):
                    sub_tile(j, ts)

                @pl.when((rows_j > 0) & (rows_j <= half))
                def _(j=j):
                    sub_tile(j, half)

        @pl.when((g == n_used - 1) & (k == nk - 1))
        def _():
            wait_rows()


def moe_ffn(group_expert, group_rows_used, n_used, row_tok, h, w_gate, w_up, w_down, layer, *,
            sub_rows, n_sub_max):
    d = h.shape[1]
    d_ff = w_gate.shape[3]
    tk = _tile(d_ff, 256)
    nk = d_ff // tk
    group_rows = sub_rows * n_sub_max
    rows_per_step = -(-group_rows // (nk * SUBLANES)) * SUBLANES
    n_gather = rows_per_step * nk
    n_rows = group_expert.shape[0] * group_rows
    assert row_tok.shape[0] - n_rows >= n_gather - group_rows

    def ff_idx(g, k, nu):
        return jnp.where(g < nu[0], k, nk - 1)

    return pl.pallas_call(
        functools.partial(_moe_ffn_kernel, rows_per_step=rows_per_step),
        grid_spec=pltpu.PrefetchScalarGridSpec(
            num_scalar_prefetch=4,
            grid=(n_rows // group_rows, nk),
            in_specs=[
                pl.BlockSpec(memory_space=pl.ANY),
                pl.BlockSpec((None, None, d, tk),
                             lambda g, k, ge, ns, nu, tok: (layer, ge[g], 0, ff_idx(g, k, nu))),
                pl.BlockSpec((None, None, d, tk),
                             lambda g, k, ge, ns, nu, tok: (layer, ge[g], 0, ff_idx(g, k, nu))),
                pl.BlockSpec((None, None, tk, d),
                             lambda g, k, ge, ns, nu, tok: (layer, ge[g], ff_idx(g, k, nu), 0)),
            ],
            out_specs=pl.BlockSpec((group_rows, d), lambda g, k, ge, ns, nu, tok: (g, 0)),
            scratch_shapes=[pltpu.VMEM((n_sub_max, sub_rows, d), BF16),
                            pltpu.VMEM((n_gather, d), F32),
                            pltpu.VMEM((d, tk), BF16), pltpu.VMEM((d, tk), BF16),
                            pltpu.VMEM((tk, d), BF16),
                            pltpu.SemaphoreType.DMA(())],
        ),
        out_shape=jax.ShapeDtypeStruct((n_rows, d), F32),
        compiler_params=_params(("arbitrary", "arbitrary"), 56),
        name="moe_ffn",
    )(group_expert, group_rows_used, n_used, row_tok, h, w_gate, w_up, w_down)


def _moe_combine_kernel(d0_ref, d1_ref, ys_ref, rw_ref, x_ref, gt_ref, gpost_ref, o_ref,
                        buf, sems):
    i = pl.program_id(0)
    tm = x_ref.shape[0]
    slot = i % 2

    def start_tile(tile, s):
        def body(t, carry):
            for which, dest in enumerate((d0_ref, d1_ref)):
                pltpu.make_async_copy(ys_ref.at[pl.ds(dest[tile * tm + t], 1)],
                                      buf.at[s, pl.ds(which * tm + t, 1)],
                                      sems.at[s]).start()
            return carry
        lax.fori_loop(0, tm, body, 0, unroll=8)

    @pl.when(i == 0)
    def _():
        start_tile(0, 0)

    @pl.when(i + 1 < pl.num_programs(0))
    def _():
        start_tile(i + 1, 1 - slot)

    pltpu.make_async_copy(ys_ref.at[pl.ds(0, 2 * tm)], buf.at[slot], sems.at[slot]).wait()
    scale = gt_ref[...] * gpost_ref[...]
    for r0 in range(0, tm, EPILOGUE_ROWS):
        rows = slice(r0, r0 + EPILOGUE_ROWS)
        y = (buf[slot, r0:r0 + EPILOGUE_ROWS, :] * rw_ref[rows, 0:1]
             + buf[slot, tm + r0:tm + r0 + EPILOGUE_ROWS, :] * rw_ref[rows, 1:2])
        o_ref[rows, :] = x_ref[rows, :] + _rms(y) * scale


def moe_combine(dest0, dest1, ys, route_w, x, mod5, g_post, layer, *, seq_len):
    t, d = x.shape
    tm = _tile(seq_len, 512)
    tiles_per_seq = seq_len // tm
    return pl.pallas_call(
        _moe_combine_kernel,
        grid_spec=pltpu.PrefetchScalarGridSpec(
            num_scalar_prefetch=2,
            grid=(t // tm,),
            in_specs=[
                pl.BlockSpec(memory_space=pl.ANY),
                pl.BlockSpec((tm, LANES), lambda i, d0, d1: (i, 0)),
                pl.BlockSpec((tm, d), lambda i, d0, d1: (i, 0)),
                _mod_spec(layer, GT_FFN, d, lambda i: i // tiles_per_seq),
                _layer_spec(layer, (1, d)),
            ],
            out_specs=pl.BlockSpec((tm, d), lambda i, d0, d1: (i, 0)),
            scratch_shapes=[pltpu.VMEM((2, 2 * tm, d), F32), pltpu.SemaphoreType.DMA((2,))],
        ),
        out_shape=jax.ShapeDtypeStruct((t, d), F32),
        compiler_params=_params(("arbitrary",), 48),
        name="moe_combine",
    )(dest0, dest1, ys, route_w, x, mod5, g_post)


MOE_SUB_ROWS = 512
MOE_GROUP_SUBS = 2


def moe_layer(h, logits, x, mod5, g_post, w_gate, w_up, w_down, layer, moe_index, *,
              seq_len):
    t, d = h.shape
    n_experts = w_gate.shape[1]
    ts = min(MOE_SUB_ROWS, t)
    group_rows = ts * MOE_GROUP_SUBS
    route_i, route_w, counts = router(logits, n_experts)
    e0, e1, pos0, pos1 = (route_i[:, k] for k in range(4))
    counts = counts[0, :n_experts]
    groups_e = (counts + group_rows - 1) // group_rows
    g_end = jnp.cumsum(groups_e)
    g_start = g_end - groups_e
    dest0 = g_start[e0] * group_rows + pos0
    dest1 = g_start[e1] * group_rows + pos1
    n_groups = -(-(2 * t) // group_rows) + n_experts
    n_rows = n_groups * group_rows
    tok = jnp.arange(t, dtype=jnp.int32)
    row_tok = jnp.zeros((n_rows + group_rows,), jnp.int32).at[
        jnp.concatenate([dest0, dest1])].set(jnp.concatenate([tok, tok]))
    n_used = g_end[-1].astype(jnp.int32)
    gid = jnp.arange(n_groups, dtype=jnp.int32)
    group_e = jnp.clip(jnp.searchsorted(g_end, gid, side='right'), 0, n_experts - 1)
    group_e = jnp.where(gid < n_used, group_e, group_e[n_used - 1]).astype(jnp.int32)
    group_rows_used = jnp.where(
        gid < n_used,
        jnp.clip(counts[group_e] - (gid - g_start[group_e]) * group_rows, 0, group_rows),
        0).astype(jnp.int32)
    ys = moe_ffn(group_e, group_rows_used, n_used.reshape(1), row_tok, h, w_gate, w_up, w_down,
                 moe_index, sub_rows=ts, n_sub_max=MOE_GROUP_SUBS)
    return moe_combine(dest0, dest1, ys, route_w, x, mod5, g_post, layer, seq_len=seq_len)


def kernel(x, c, w_ada, b_ada, g_mix_pre, g_mix_post, g_ffn_pre, g_ffn_post, w_in, conv_w,
           ssm_a_re, ssm_a_im, ssm_log_dt, ssm_b_re, ssm_b_im, ssm_c_re, ssm_c_im, ssm_d,
           w_glu, b_glu, w_out, ffn_w_gate, ffn_w_up, ffn_w_down, router_w, router_b,
           moe_w_gate, moe_w_up, moe_w_down):
    nb, seq_len, d = x.shape
    depth = w_ada.shape[0]
    d_conv = conv_w.shape[1]
    g, p, h_dim = ssm_b_re.shape[1:]
    d_ssm = g * h_dim
    n_experts = router_w.shape[2]
    t = nb * seq_len
    groups_per_slab = max(1, min(g, MXU_DIM // h_dim))

    mod_rows = 2 * SUBLANES
    c_pad = jnp.zeros((mod_rows, d), F32).at[:nb].set(c)
    mod = adaln_mod(c_pad, w_ada, b_ada)
    mod5 = mod.reshape(depth, mod_rows, 6, 1, d)
    xf = x.reshape(t, d)

    def rows(v):
        return v.reshape(v.shape[0], 1, v.shape[1])
    g_mix_pre, g_mix_post, g_ffn_pre, g_ffn_post = (
        rows(v) for v in (g_mix_pre, g_mix_post, g_ffn_pre, g_ffn_post))
    w_in16 = w_in.astype(BF16)
    w_out16 = w_out.astype(BF16)
    w_glu16 = w_glu.astype(BF16)
    conv_wt = jnp.swapaxes(conv_w, 1, 2)
    bb_re, bb_im, ab_re, ab_im = ssm_prep(ssm_a_re, ssm_a_im, ssm_log_dt, ssm_b_re, ssm_b_im)
    def slabs(w):
        return _block_diag_slabs(jnp.swapaxes(w, 2, 3).astype(BF16), groups_per_slab)
    bbd = jnp.concatenate([slabs(bb_re), slabs(bb_im)], axis=-1)
    c_cat = jnp.concatenate([slabs(ssm_c_re), slabs(ssm_c_im)], axis=-2)
    a_re = _pack_state_rows(ab_re.reshape(depth, g * p), nb)
    a_im = _pack_state_rows(ab_im.reshape(depth, g * p), nb)
    d_skip = ssm_d.reshape(depth, 1, d_ssm)
    b_glu = rows(b_glu)
    n_moe = router_w.shape[0]
    rw_pad = jnp.zeros((n_moe, d, LANES), F32).at[:, :, :n_experts].set(router_w)
    rb_pad = jnp.zeros((n_moe, 1, LANES), F32).at[:, 0, :n_experts].set(router_b)

    for l in range(depth):
        y_conv, u = mixer_in(xf, mod5, g_mix_pre, w_in16, conv_wt, l, seq_len=seq_len)
        y_ssm = ssm(u.reshape(nb, seq_len, d_ssm), bbd, c_cat, a_re, a_im, d_skip,
                    w_glu16, b_glu, l).reshape(t, d_ssm)
        routed = l % 2 == 1
        i = l // 2
        rt = (rw_pad[i], rb_pad[i]) if routed else None
        outs = mixer_out(y_conv, y_ssm, w_out16, xf, mod5, g_mix_post, g_ffn_pre, l, rt,
                         seq_len=seq_len)
        if routed:
            xf, h, logits = outs
            xf = moe_layer(h, logits, xf, mod5, g_ffn_post, moe_w_gate, moe_w_up, moe_w_down,
                           l, i, seq_len=seq_len)
        else:
            xf, h = outs
            xf = ffn_dense(h, ffn_w_gate, ffn_w_up, ffn_w_down, xf, mod5, g_ffn_post, l, i,
                           seq_len=seq_len)
    return xf.reshape(nb, seq_len, d)
```

```python
import functools

import jax
import jax.numpy as jnp
from jax import lax
from jax.experimental import pallas as pl
from jax.experimental.pallas import tpu as pltpu

F32 = jnp.float32
BF16 = jnp.bfloat16
NORM_EPS = 1e-6
LANES = 128
SUBLANES = 8
MXU_DIM = 256
EPILOGUE_ROWS = 64
FFN_EPILOGUE_ROWS = 16
MIB = 1024 * 1024


def _params(semantics, vmem_mib):
    return pltpu.CompilerParams(dimension_semantics=semantics,
                                vmem_limit_bytes=vmem_mib * MIB)


def _dot(a, b):
    return jnp.dot(a, b, preferred_element_type=F32)


def _rms(x):
    return x * lax.rsqrt(jnp.mean(x * x, axis=-1, keepdims=True) + NORM_EPS)


def _tile(n, want):
    t = min(n, want)
    while n % t:
        t -= 1
    return t


def _adaln_kernel(c_ref, w_ref, b_ref, o_ref):
    c = c_ref[...]
    c_act = c * jax.nn.sigmoid(c)
    c_hi = c_act.astype(BF16)
    c_lo = (c_act - c_hi.astype(F32)).astype(BF16)
    w16 = w_ref[...].astype(BF16)
    o_ref[...] = (_dot(c_hi, w16) + _dot(c_lo, w16)) + b_ref[...]


def adaln_mod(c_pad, w_ada, b_ada):
    depth, d, n = w_ada.shape
    rows = c_pad.shape[0]
    tn = _tile(n, 1024)
    return pl.pallas_call(
        _adaln_kernel,
        grid=(depth, n // tn),
        in_specs=[
            pl.BlockSpec((rows, d), lambda l, j: (0, 0)),
            pl.BlockSpec((None, d, tn), lambda l, j: (l, 0, j)),
            pl.BlockSpec((None, 1, tn), lambda l, j: (l, 0, j)),
        ],
        out_specs=pl.BlockSpec((None, rows, tn), lambda l, j: (l, 0, j)),
        out_shape=jax.ShapeDtypeStruct((depth, rows, n), F32),
        compiler_params=_params(("arbitrary", "arbitrary"), 56),
        name="adaln_mod",
    )(c_pad, w_ada, b_ada.reshape(depth, 1, n))


def _mixer_in_kernel(x_ref, sc_ref, sh_ref, g_ref, w_ref, cw_ref, yconv_ref, u_ref,
                     halo_scr, h_scr, *, tiles_per_seq, col_chunk):
    tm = x_ref.shape[0]
    d_conv = yconv_ref.shape[1]
    d_ssm = u_ref.shape[1]
    pre_scale = g_ref[...] * (1.0 + sc_ref[...])
    for r0 in range(0, tm, EPILOGUE_ROWS):
        rows = slice(r0, r0 + EPILOGUE_ROWS)
        h_scr[rows, :] = (_rms(x_ref[rows, :]) * pre_scale + sh_ref[...]).astype(BF16)
    h = h_scr[...]

    @pl.when(pl.program_id(0) % tiles_per_seq == 0)
    def _():
        halo_scr[...] = jnp.zeros(halo_scr.shape, F32)

    for c0 in range(0, d_conv, col_chunk):
        cols = slice(c0, c0 + col_chunk)
        gate_b = _dot(h, w_ref[:, c0:c0 + col_chunk])
        z = (_dot(h, w_ref[:, d_conv + c0:d_conv + c0 + col_chunk])
             * _dot(h, w_ref[:, 2 * d_conv + c0:2 * d_conv + c0 + col_chunk]))
        prev = halo_scr[:, cols]
        halo_scr[:, cols] = z[tm - SUBLANES:, :]
        w0 = cw_ref[0:1, cols]
        w1 = cw_ref[1:2, cols]
        w2 = cw_ref[2:3, cols]
        z1 = pltpu.roll(z, 1, 0)
        z2 = pltpu.roll(z, 2, 0)
        yconv_ref[:, cols] = (gate_b * (z * w2 + z1 * w1 + z2 * w0)).astype(yconv_ref.dtype)
        row = lax.broadcasted_iota(jnp.int32, (SUBLANES, col_chunk), 0)
        z1t = jnp.where(row < 1, pltpu.roll(prev, 1, 0), z1[:SUBLANES, :])
        z2t = jnp.where(row < 2, pltpu.roll(prev, 2, 0), z2[:SUBLANES, :])
        top = gate_b[:SUBLANES, :] * (z[:SUBLANES, :] * w2 + z1t * w1 + z2t * w0)
        yconv_ref[0:SUBLANES, cols] = top.astype(yconv_ref.dtype)

    for c0 in range(0, d_ssm, col_chunk):
        u_ref[:, c0:c0 + col_chunk] = _dot(
            h, w_ref[:, 3 * d_conv + c0:3 * d_conv + c0 + col_chunk])


SH_MIX, SC_MIX, GT_MIX, SH_FFN, SC_FFN, GT_FFN = range(6)


def _mod_spec(layer, chunk, d, batch_of):
    return pl.BlockSpec((None, None, None, 1, d),
                        lambda i, *_: (layer, batch_of(i), chunk, 0, 0))


def _layer_spec(layer, shape, **kw):
    zeros = (0,) * len(shape)
    return pl.BlockSpec((None,) + tuple(shape), lambda *_: (layer,) + zeros, **kw)


def mixer_in(x, mod5, g_pre, w_in16, conv_wt, layer, *, seq_len):
    t, d = x.shape
    d_in = w_in16.shape[2]
    d_conv = conv_wt.shape[2]
    d_ssm = d_in - 3 * d_conv
    tm = _tile(seq_len, 512)
    col_chunk = _tile(min(d_conv, d_ssm), 512)
    assert d_conv % col_chunk == 0 and d_ssm % col_chunk == 0
    tiles_per_seq = seq_len // tm

    def batch_of(i):
        return i // tiles_per_seq

    kern = functools.partial(_mixer_in_kernel, tiles_per_seq=tiles_per_seq,
                             col_chunk=col_chunk)
    return pl.pallas_call(
        kern,
        grid=(t // tm,),
        in_specs=[
            pl.BlockSpec((tm, d), lambda i: (i, 0)),
            _mod_spec(layer, SC_MIX, d, batch_of), _mod_spec(layer, SH_MIX, d, batch_of),
            _layer_spec(layer, (1, d)),
            _layer_spec(layer, (d, d_in), pipeline_mode=pl.Buffered(1)),
            _layer_spec(layer, (3, d_conv)),
        ],
        out_specs=[
            pl.BlockSpec((tm, d_conv), lambda i: (i, 0)),
            pl.BlockSpec((tm, d_ssm), lambda i: (i, 0)),
        ],
        out_shape=[jax.ShapeDtypeStruct((t, d_conv), BF16),
                   jax.ShapeDtypeStruct((t, d_ssm), F32)],
        scratch_shapes=[pltpu.VMEM((SUBLANES, d_conv), F32), pltpu.VMEM((tm, d), BF16)],
        compiler_params=_params(("arbitrary",), 48),
        name="mixer_in",
    )(x, mod5, mod5, g_pre, w_in16, conv_wt)


def _ssm_prep_kernel(are_ref, aim_ref, ldt_ref, bre_ref, bim_ref,
                     bbre_ref, bbim_ref, abre_ref, abim_ref):
    lam_re = are_ref[...]
    lam_im = aim_ref[...]
    dt = jnp.exp(ldt_ref[...])
    mag = jnp.exp(lam_re * dt)
    ang = lam_im * dt
    ab_re = mag * jnp.cos(ang)
    ab_im = mag * jnp.sin(ang)
    den = lam_re * lam_re + lam_im * lam_im
    nr = ab_re - 1.0
    q_re = (nr * lam_re + ab_im * lam_im) / den
    q_im = (ab_im * lam_re - nr * lam_im) / den
    b_re = bre_ref[...]
    b_im = bim_ref[...]
    bbre_ref[...] = q_re * b_re - q_im * b_im
    bbim_ref[...] = q_re * b_im + q_im * b_re
    abre_ref[...] = ab_re
    abim_ref[...] = ab_im


def ssm_prep(a_re, a_im, log_dt, b_re, b_im):
    depth, g, p, h = b_re.shape
    ldt = jnp.broadcast_to(log_dt.reshape(depth, g, 1, 1), (depth, g, p, 1))

    def spec(last):
        return pl.BlockSpec((None, g, p, last), lambda l: (l, 0, 0, 0))

    return pl.pallas_call(
        _ssm_prep_kernel,
        grid=(depth,),
        in_specs=[spec(1), spec(1), spec(1), spec(h), spec(h)],
        out_specs=[spec(h), spec(h), spec(1), spec(1)],
        out_shape=[jax.ShapeDtypeStruct((depth, g, p, h), F32),
                   jax.ShapeDtypeStruct((depth, g, p, h), F32),
                   jax.ShapeDtypeStruct((depth, g, p, 1), F32),
                   jax.ShapeDtypeStruct((depth, g, p, 1), F32)],
        compiler_params=_params(("arbitrary",), 56),
        name="ssm_prep",
    )(a_re.reshape(depth, g, p, 1), a_im.reshape(depth, g, p, 1), ldt, b_re, b_im)


def _block_diag_slabs(w, groups_per_slab):
    depth, g, r, c = w.shape
    ns = g // groups_per_slab
    w = w.reshape(depth, ns, groups_per_slab, r, c)
    eye = jnp.eye(groups_per_slab, dtype=w.dtype)
    out = w[:, :, :, :, None, :] * eye[None, None, :, None, :, None]
    return out.reshape(depth, ns, groups_per_slab * r, groups_per_slab * c)


def _gelu_tanh(x):
    return 0.5 * x * (1.0 + jnp.tanh(0.7978845608028654 * (x + 0.044715 * x * x * x)))


def _ssm_kernel(u_ref, bbd_ref, c_ref, are_ref, aim_ref, d_ref, wglu_ref,
                bglu_ref, o_ref, sre_scr, sim_scr, stre_scr, stim_scr, u_scr, y_scr, *,
                pitch, groups_per_step):
    nb, tc, d_ssm = u_ref.shape
    n_slabs, k_slab, two_sw = bbd_ref.shape
    sw = two_sw // 2
    tiles_per_slab = sw // LANES
    pack = SUBLANES // nb
    n_groups = sre_scr.shape[0]
    rows_all = nb * pitch

    def tile_rows(lane_tile):
        q, h = divmod(lane_tile, pack)
        return q, slice(h * rows_all, (h + 1) * rows_all)

    @pl.when(pl.program_id(0) == 0)
    def _():
        stre_scr[...] = jnp.zeros(stre_scr.shape, F32)
        stim_scr[...] = jnp.zeros(stim_scr.shape, F32)
        u_scr[...] = jnp.zeros(u_scr.shape, F32)

    for b in range(nb):
        u_scr[b * pitch:b * pitch + tc, :] = u_ref[b]

    for s in range(n_slabs):
        r = _dot(u_scr[:, s * k_slab:(s + 1) * k_slab].astype(BF16), bbd_ref[s])
        for j in range(tiles_per_slab):
            qr = tile_rows(s * tiles_per_slab + j)
            sre_scr[qr] = r[:, j * LANES:(j + 1) * LANES]
            sim_scr[qr] = r[:, sw + j * LANES:sw + (j + 1) * LANES]

    for q0 in range(0, n_groups, groups_per_step):
        qs = list(range(q0, min(q0 + groups_per_step, n_groups)))
        a_re = [are_ref[q] for q in qs]
        a_im = [aim_ref[q] for q in qs]

        def step(t, carry, qs=qs, a_re=a_re, a_im=a_im):
            rows = pl.ds(t, SUBLANES, stride=pitch)
            out = []
            for n, q in enumerate(qs):
                s_re, s_im = carry[2 * n], carry[2 * n + 1]
                n_re = a_re[n] * s_re - a_im[n] * s_im + sre_scr[q, rows, :]
                n_im = a_re[n] * s_im + a_im[n] * s_re + sim_scr[q, rows, :]
                sre_scr[q, rows, :] = n_re
                sim_scr[q, rows, :] = n_im
                out += [n_re, n_im]
            return tuple(out)

        init = []
        for q in qs:
            init += [stre_scr[q], stim_scr[q]]
        fin = lax.fori_loop(0, tc, step, tuple(init), unroll=2)
        for n, q in enumerate(qs):
            stre_scr[q] = fin[2 * n]
            stim_scr[q] = fin[2 * n + 1]

    def state_rows(s):
        tiles = [tile_rows(s * tiles_per_slab + j) for j in range(tiles_per_slab)]
        return jnp.concatenate([sre_scr[qr].astype(BF16) for qr in tiles]
                               + [(-sim_scr[qr]).astype(BF16) for qr in tiles], axis=-1)

    for s in range(n_slabs):
        cols = slice(s * k_slab, (s + 1) * k_slab)
        y = _dot(state_rows(s), c_ref[s])
        y_scr[:, cols] = y + d_ref[:, cols] * u_scr[:, cols]
    y = _gelu_tanh(y_scr[...])
    gate = _dot(y.astype(BF16), wglu_ref[...]) + bglu_ref[...]
    y_scr[...] = y * jax.nn.sigmoid(gate)
    for b in range(nb):
        o_ref[b] = y_scr[b * pitch:b * pitch + tc, :].astype(o_ref.dtype)


def _pack_state_rows(a, nb):
    depth = a.shape[0]
    pack = SUBLANES // nb
    n_groups = a.shape[1] // (LANES * pack)
    a = a.reshape(depth, n_groups, pack, 1, LANES)
    return jnp.broadcast_to(a, (depth, n_groups, pack, nb, LANES)).reshape(
        depth, n_groups, SUBLANES, LANES)


def ssm(u3, bbd, c_cat, a_re, a_im, d_skip, w_glu16, b_glu, layer):
    nb, seq_len, d_ssm = u3.shape
    n_groups = a_re.shape[1]
    assert SUBLANES % nb == 0
    tc = _tile(seq_len, 128)
    pitch = tc + SUBLANES // 2
    assert (nb * pitch) % SUBLANES == 0
    kern = functools.partial(_ssm_kernel, pitch=pitch, groups_per_step=min(4, n_groups))

    def full(a):
        return _layer_spec(layer, a.shape[1:])

    return pl.pallas_call(
        kern,
        grid=(seq_len // tc,),
        in_specs=[pl.BlockSpec((nb, tc, d_ssm), lambda c: (0, c, 0)),
                  full(bbd), full(c_cat), full(a_re), full(a_im),
                  full(d_skip), full(w_glu16), full(b_glu)],
        out_specs=pl.BlockSpec((nb, tc, d_ssm), lambda c: (0, c, 0)),
        out_shape=jax.ShapeDtypeStruct((nb, seq_len, d_ssm), BF16),
        scratch_shapes=[pltpu.VMEM((n_groups, SUBLANES * pitch, LANES), F32),
                        pltpu.VMEM((n_groups, SUBLANES * pitch, LANES), F32),
                        pltpu.VMEM((n_groups, SUBLANES, LANES), F32),
                        pltpu.VMEM((n_groups, SUBLANES, LANES), F32),
                        pltpu.VMEM((nb * pitch, d_ssm), F32),
                        pltpu.VMEM((nb * pitch, d_ssm), F32)],
        compiler_params=_params(("arbitrary",), 56),
        name="ssm_scan",
    )(u3, bbd, c_cat, a_re, a_im, d_skip, w_glu16, b_glu)


def _mixer_out_kernel(*refs, routed):
    if routed:
        (yc_ref, ys_ref, wo_ref, x_ref, gt_ref, gpost_ref, gpre_ref, sc_ref,
         sh_ref, rw_ref, rb_ref, xo_ref, h_ref, lg_ref, y_scr) = refs
    else:
        (yc_ref, ys_ref, wo_ref, x_ref, gt_ref, gpost_ref, gpre_ref, sc_ref,
         sh_ref, xo_ref, h_ref, y_scr) = refs
    tm = yc_ref.shape[0]
    y_scr[...] = _dot(jnp.concatenate([yc_ref[...], ys_ref[...]], axis=-1), wo_ref[...])
    post_scale = gt_ref[...] * gpost_ref[...]
    pre_scale = gpre_ref[...] * (1.0 + sc_ref[...])
    for r0 in range(0, tm, EPILOGUE_ROWS):
        rows = slice(r0, r0 + EPILOGUE_ROWS)
        x_new = x_ref[rows, :] + _rms(y_scr[rows, :]) * post_scale
        xo_ref[rows, :] = x_new
        h_ref[rows, :] = (_rms(x_new) * pre_scale + sh_ref[...]).astype(h_ref.dtype)
    if routed:
        h = h_ref[...]
        w = rw_ref[...]
        h_hi = h.astype(BF16)
        w_hi = w.astype(BF16)
        h_lo = (h - h_hi.astype(F32)).astype(BF16)
        w_lo = (w - w_hi.astype(F32)).astype(BF16)
        lg_ref[...] = (_dot(h_hi, w_hi) + (_dot(h_hi, w_lo) + _dot(h_lo, w_hi))
                       + rb_ref[...])


def mixer_out(y_conv, y_ssm, w_out16, x, mod5, g_post, g_pre, layer, router=None, *,
              seq_len):
    t, d = x.shape
    d_conv = y_conv.shape[1]
    d_ssm = y_ssm.shape[1]
    tm = _tile(seq_len, 512)
    tiles_per_seq = seq_len // tm
    routed = router is not None

    def batch_of(i):
        return i // tiles_per_seq

    def row_spec(n):
        return pl.BlockSpec((tm, n), lambda i: (i, 0))

    def const_spec(shape):
        return pl.BlockSpec(shape, lambda i: (0, 0))

    in_specs = [row_spec(d_conv), row_spec(d_ssm),
                _layer_spec(layer, (d_conv + d_ssm, d), pipeline_mode=pl.Buffered(1)),
                row_spec(d), _mod_spec(layer, GT_MIX, d, batch_of),
                _layer_spec(layer, (1, d)), _layer_spec(layer, (1, d)),
                _mod_spec(layer, SC_FFN, d, batch_of), _mod_spec(layer, SH_FFN, d, batch_of)]
    args = [y_conv, y_ssm, w_out16, x, mod5, g_post, g_pre, mod5, mod5]
    out_specs = [row_spec(d), row_spec(d)]
    out_shape = [jax.ShapeDtypeStruct((t, d), F32),
                 jax.ShapeDtypeStruct((t, d), F32 if routed else BF16)]
    if routed:
        rw_pad, rb_pad = router
        in_specs += [const_spec(rw_pad.shape), const_spec(rb_pad.shape)]
        args += [rw_pad, rb_pad]
        out_specs.append(row_spec(LANES))
        out_shape.append(jax.ShapeDtypeStruct((t, LANES), F32))
    return pl.pallas_call(
        functools.partial(_mixer_out_kernel, routed=routed),
        grid=(t // tm,),
        in_specs=in_specs, out_specs=out_specs, out_shape=out_shape,
        scratch_shapes=[pltpu.VMEM((tm, d), F32)],
        compiler_params=_params(("arbitrary",), 56),
        name="mixer_out_routed" if routed else "mixer_out",
    )(*args)


def _ffn_kernel(h_ref, wg_ref, wu_ref, wd_ref, x_hbm, gt_ref, gpost_ref, o_ref,
                wg16, wu16, wd16, x_buf, sem, *, sub_rows):
    i = pl.program_id(0)
    k = pl.program_id(1)
    tm = o_ref.shape[0]
    x_copy = pltpu.make_async_copy(x_hbm.at[pl.ds(i * tm, tm)], x_buf, sem)

    @pl.when(k == 0)
    def _():
        o_ref[...] = jnp.zeros(o_ref.shape, F32)
        x_copy.start()

    wg16[...] = wg_ref[...].astype(BF16)
    wu16[...] = wu_ref[...].astype(BF16)
    wd16[...] = wd_ref[...].astype(BF16)
    for r0 in range(0, tm, sub_rows):
        h = h_ref[r0:r0 + sub_rows, :]
        gate = _dot(h, wg16[...])
        act = (gate * jax.nn.sigmoid(gate)) * _dot(h, wu16[...])
        o_ref[r0:r0 + sub_rows, :] += _dot(act.astype(BF16), wd16[...])

    @pl.when(k == pl.num_programs(1) - 1)
    def _():
        x_copy.wait()
        scale = gt_ref[...] * gpost_ref[...]
        for r0 in range(0, tm, FFN_EPILOGUE_ROWS):
            rows = slice(r0, r0 + FFN_EPILOGUE_ROWS)
            o_ref[rows, :] = x_buf[rows, :] + _rms(o_ref[rows, :]) * scale


def ffn_dense(h16, w_gate, w_up, w_down, x, mod5, g_post, layer, ffn_index, *, seq_len):
    t, d = x.shape
    d_ff = w_gate.shape[2]
    tm = _tile(seq_len, 1024)
    sub_rows = _tile(tm, 512)
    tk = _tile(d_ff, 256)
    tiles_per_seq = seq_len // tm
    return pl.pallas_call(
        functools.partial(_ffn_kernel, sub_rows=sub_rows),
        grid=(t // tm, d_ff // tk),
        in_specs=[
            pl.BlockSpec((tm, d), lambda i, k: (i, 0)),
            pl.BlockSpec((None, d, tk), lambda i, k: (ffn_index, 0, k)),
            pl.BlockSpec((None, d, tk), lambda i, k: (ffn_index, 0, k)),
            pl.BlockSpec((None, tk, d), lambda i, k: (ffn_index, k, 0)),
            pl.BlockSpec(memory_space=pl.ANY),
            _mod_spec(layer, GT_FFN, d, lambda i: i // tiles_per_seq),
            _layer_spec(layer, (1, d)),
        ],
        out_specs=pl.BlockSpec((tm, d), lambda i, k: (i, 0)),
        out_shape=jax.ShapeDtypeStruct((t, d), F32),
        scratch_shapes=[pltpu.VMEM((d, tk), BF16), pltpu.VMEM((d, tk), BF16),
                        pltpu.VMEM((tk, d), BF16), pltpu.VMEM((tm, d), F32),
                        pltpu.SemaphoreType.DMA(())],
        compiler_params=_params(("arbitrary", "arbitrary"), 56),
        name="ffn_dense",
    )(h16, w_gate, w_up, w_down, x, mod5, g_post)


def _router_kernel(lg_ref, ri_ref, rw_ref, cnt_ref, carry_scr, *, n_experts):
    @pl.when(pl.program_id(0) == 0)
    def _():
        carry_scr[...] = jnp.zeros(carry_scr.shape, F32)

    tm = lg_ref.shape[0]
    lane = lax.broadcasted_iota(jnp.int32, (tm, LANES), 1)
    neg = jnp.float32(-jnp.inf)
    logit = jnp.where(lane < n_experts, lg_ref[...], neg)
    m1 = jnp.max(logit, axis=-1, keepdims=True)
    i1 = jnp.min(jnp.where(logit == m1, lane, LANES), axis=-1, keepdims=True)
    rest = jnp.where(lane == i1, neg, logit)
    m2 = jnp.max(rest, axis=-1, keepdims=True)
    i2 = jnp.min(jnp.where(rest == m2, lane, LANES), axis=-1, keepdims=True)
    e2 = jnp.exp(m2 - m1)
    w1 = 1.0 / (1.0 + e2)
    w2 = e2 / (1.0 + e2)
    hit1 = lane == i1
    hit2 = lane == i2
    onehot = jnp.where(hit1 | hit2, 1.0, 0.0)
    r = lax.broadcasted_iota(jnp.int32, (tm, tm), 0)
    c = lax.broadcasted_iota(jnp.int32, (tm, tm), 1)
    earlier = jnp.where(c < r, 1.0, 0.0).astype(BF16)
    carry = carry_scr[0:1, :]
    before = _dot(earlier, onehot.astype(BF16)) + carry
    pos1 = jnp.sum(jnp.where(hit1, before, 0.0), axis=-1, keepdims=True).astype(jnp.int32)
    pos2 = jnp.sum(jnp.where(hit2, before, 0.0), axis=-1, keepdims=True).astype(jnp.int32)
    total = carry + jnp.sum(onehot, axis=0, keepdims=True)
    carry_scr[...] = jnp.broadcast_to(total, carry_scr.shape)
    cnt_ref[...] = jnp.broadcast_to(total, cnt_ref.shape).astype(jnp.int32)
    zero_i = jnp.zeros((tm, LANES), jnp.int32)
    ri_ref[...] = jnp.where(lane == 0, i1, jnp.where(lane == 1, i2,
                            jnp.where(lane == 2, pos1, jnp.where(lane == 3, pos2, zero_i))))
    rw_ref[...] = jnp.where(lane == 0, w1, jnp.where(lane == 1, w2, 0.0))


def router(logits, n_experts):
    t = logits.shape[0]
    tm = _tile(t, 512)
    return pl.pallas_call(
        functools.partial(_router_kernel, n_experts=n_experts),
        grid=(t // tm,),
        in_specs=[pl.BlockSpec((tm, LANES), lambda i: (i, 0))],
        out_specs=[pl.BlockSpec((tm, LANES), lambda i: (i, 0)),
                   pl.BlockSpec((tm, LANES), lambda i: (i, 0)),
                   pl.BlockSpec((SUBLANES, LANES), lambda i: (0, 0))],
        out_shape=[jax.ShapeDtypeStruct((t, LANES), jnp.int32),
                   jax.ShapeDtypeStruct((t, LANES), F32),
                   jax.ShapeDtypeStruct((SUBLANES, LANES), jnp.int32)],
        scratch_shapes=[pltpu.VMEM((SUBLANES, LANES), F32)],
        compiler_params=_params(("arbitrary",), 32),
        name="router_top2",
    )(logits)


def _moe_ffn_kernel(ge_ref, nr_ref, nu_ref, tok_ref, h_hbm, wg_ref, wu_ref, wd_ref, o_hbm,
                    acc, h_scr, gbuf, wg16, wu16, wd16, sem, out_sem, *, rows_per_step):
    g = pl.program_id(0)
    k = pl.program_id(1)
    n_used = nu_ref[0]
    n_sub_max, ts, _ = h_scr.shape
    group_rows = n_sub_max * ts
    nk = pl.num_programs(1)
    n_gather = gbuf.shape[0]

    def row_copy(src_row, dst_row):
        return pltpu.make_async_copy(h_hbm.at[pl.ds(src_row, 1)],
                                     gbuf.at[pl.ds(dst_row, 1)], sem)

    def write_back(group):
        return pltpu.make_async_copy(acc, o_hbm.at[pl.ds(group * group_rows, group_rows)],
                                     out_sem)

    def for_each_sub_tile(group, fn):
        for j in range(n_sub_max):
            @pl.when(j * ts < nr_ref[group])
            def _(j=j):
                fn(j)

    def wait_rows():
        pltpu.make_async_copy(h_hbm.at[pl.ds(0, n_gather)], gbuf, sem).wait()

    @pl.when(k == 0)
    def _():
        @pl.when(g < n_used)
        def _():
            @pl.when(g == 0)
            def _():
                def body(i, carry):
                    row_copy(tok_ref[i], i).start()
                    return carry
                lax.fori_loop(0, n_gather, body, 0, unroll=8)

            wait_rows()

            def cast(j):
                h_scr[j] = gbuf[j * ts:(j + 1) * ts, :].astype(BF16)
            for_each_sub_tile(g, cast)

        @pl.when(g > 0)
        def _():
            write_back(g - 1).wait()
        acc[...] = jnp.zeros(acc.shape, F32)

    @pl.when(g < n_used)
    def _():
        next_base = jnp.minimum(g + 1, n_used - 1) * group_rows + k * rows_per_step

        def sub_tile(j, m):
            if j == 0:
                wg16[...] = wg_ref[...].astype(BF16)
                wu16[...] = wu_ref[...].astype(BF16)
                wd16[...] = wd_ref[...].astype(BF16)
                for i in range(rows_per_step):
                    row_copy(tok_ref[next_base + i], k * rows_per_step + i).start()
            h = h_scr[j, 0:m, :]
            gate = _dot(h, wg16[...])
            act = (gate * jax.nn.sigmoid(gate)) * _dot(h, wu16[...])
            acc[j * ts:j * ts + m, :] += _dot(act.astype(BF16), wd16[...])

        half = ts // 2
        all_full = nr_ref[g] > group_rows - half

        @pl.when(all_full)
        def _():
            for j in range(n_sub_max):
                sub_tile(j, ts)

        @pl.when(jnp.logical_not(all_full))
        def _():
            for j in range(n_sub_max):
                rows_j = nr_ref[g] - j * ts

                @pl.when(rows_j > half)
                def _(j=j):
                    sub_tile(j, ts)

                @pl.when((rows_j > 0) & (rows_j <= half))
                def _(j=j):
                    sub_tile(j, half)

        @pl.when((g == n_used - 1) & (k == nk - 1))
        def _():
            wait_rows()

    @pl.when(k == nk - 1)
    def _():
        write_back(g).start()

        @pl.when(g == pl.num_programs(0) - 1)
        def _():
            write_back(g).wait()


def moe_ffn(group_expert, group_rows_used, n_used, row_tok, h, w_gate, w_up, w_down, layer, *,
            sub_rows, n_sub_max):
    d = h.shape[1]
    d_ff = w_gate.shape[3]
    tk = _tile(d_ff, 512)
    nk = d_ff // tk
    group_rows = sub_rows * n_sub_max
    rows_per_step = -(-group_rows // (nk * SUBLANES)) * SUBLANES
    n_gather = rows_per_step * nk
    n_rows = group_expert.shape[0] * group_rows
    assert row_tok.shape[0] - n_rows >= n_gather - group_rows

    def ff_idx(g, k, nu):
        return jnp.where(g < nu[0], k, nk - 1)

    return pl.pallas_call(
        functools.partial(_moe_ffn_kernel, rows_per_step=rows_per_step),
        grid_spec=pltpu.PrefetchScalarGridSpec(
            num_scalar_prefetch=4,
            grid=(n_rows // group_rows, nk),
            in_specs=[
                pl.BlockSpec(memory_space=pl.ANY),
                pl.BlockSpec((None, None, d, tk),
                             lambda g, k, ge, ns, nu, tok: (layer, ge[g], 0, ff_idx(g, k, nu))),
                pl.BlockSpec((None, None, d, tk),
                             lambda g, k, ge, ns, nu, tok: (layer, ge[g], 0, ff_idx(g, k, nu))),
                pl.BlockSpec((None, None, tk, d),
                             lambda g, k, ge, ns, nu, tok: (layer, ge[g], ff_idx(g, k, nu), 0)),
            ],
            out_specs=pl.BlockSpec(memory_space=pl.ANY),
            scratch_shapes=[pltpu.VMEM((group_rows, d), F32),
                            pltpu.VMEM((n_sub_max, sub_rows, d), BF16),
                            pltpu.VMEM((n_gather, d), F32),
                            pltpu.VMEM((d, tk), BF16), pltpu.VMEM((d, tk), BF16),
                            pltpu.VMEM((tk, d), BF16),
                            pltpu.SemaphoreType.DMA(()), pltpu.SemaphoreType.DMA(())],
        ),
        out_shape=jax.ShapeDtypeStruct((n_rows, d), F32),
        compiler_params=_params(("arbitrary", "arbitrary"), 58),
        name="moe_ffn",
    )(group_expert, group_rows_used, n_used, row_tok, h, w_gate, w_up, w_down)


def _moe_combine_kernel(d0_ref, d1_ref, ys_ref, rw_ref, x_ref, gt_ref, gpost_ref, o_ref,
                        buf, sems):
    i = pl.program_id(0)
    tm = x_ref.shape[0]
    slot = i % 2

    def start_tile(tile, s):
        def body(t, carry):
            for which, dest in enumerate((d0_ref, d1_ref)):
                pltpu.make_async_copy(ys_ref.at[pl.ds(dest[tile * tm + t], 1)],
                                      buf.at[s, pl.ds(which * tm + t, 1)],
                                      sems.at[s]).start()
            return carry
        lax.fori_loop(0, tm, body, 0, unroll=8)

    @pl.when(i == 0)
    def _():
        start_tile(0, 0)

    @pl.when(i + 1 < pl.num_programs(0))
    def _():
        start_tile(i + 1, 1 - slot)

    pltpu.make_async_copy(ys_ref.at[pl.ds(0, 2 * tm)], buf.at[slot], sems.at[slot]).wait()
    scale = gt_ref[...] * gpost_ref[...]
    for r0 in range(0, tm, EPILOGUE_ROWS):
        rows = slice(r0, r0 + EPILOGUE_ROWS)
        y = (buf[slot, r0:r0 + EPILOGUE_ROWS, :] * rw_ref[rows, 0:1]
             + buf[slot, tm + r0:tm + r0 + EPILOGUE_ROWS, :] * rw_ref[rows, 1:2])
        o_ref[rows, :] = x_ref[rows, :] + _rms(y) * scale


def moe_combine(dest0, dest1, ys, route_w, x, mod5, g_post, layer, *, seq_len):
    t, d = x.shape
    tm = _tile(seq_len, 256)
    tiles_per_seq = seq_len // tm
    return pl.pallas_call(
        _moe_combine_kernel,
        grid_spec=pltpu.PrefetchScalarGridSpec(
            num_scalar_prefetch=2,
            grid=(t // tm,),
            in_specs=[
                pl.BlockSpec(memory_space=pl.ANY),
                pl.BlockSpec((tm, LANES), lambda i, d0, d1: (i, 0)),
                pl.BlockSpec((tm, d), lambda i, d0, d1: (i, 0)),
                _mod_spec(layer, GT_FFN, d, lambda i: i // tiles_per_seq),
                _layer_spec(layer, (1, d)),
            ],
            out_specs=pl.BlockSpec((tm, d), lambda i, d0, d1: (i, 0)),
            scratch_shapes=[pltpu.VMEM((2, 2 * tm, d), F32), pltpu.SemaphoreType.DMA((2,))],
        ),
        out_shape=jax.ShapeDtypeStruct((t, d), F32),
        compiler_params=_params(("arbitrary",), 48),
        name="moe_combine",
    )(dest0, dest1, ys, route_w, x, mod5, g_post)


MOE_SUB_ROWS = 512
MOE_GROUP_SUBS = 2


def moe_layer(h, logits, x, mod5, g_post, w_gate, w_up, w_down, layer, moe_index, *,
              seq_len):
    t, d = h.shape
    n_experts = w_gate.shape[1]
    ts = min(MOE_SUB_ROWS, t)
    group_rows = ts * MOE_GROUP_SUBS
    route_i, route_w, counts = router(logits, n_experts)
    e0, e1, pos0, pos1 = (route_i[:, k] for k in range(4))
    counts = counts[0, :n_experts]
    groups_e = (counts + group_rows - 1) // group_rows
    g_end = jnp.cumsum(groups_e)
    g_start = g_end - groups_e
    dest0 = g_start[e0] * group_rows + pos0
    dest1 = g_start[e1] * group_rows + pos1
    n_groups = -(-(2 * t) // group_rows) + n_experts
    n_rows = n_groups * group_rows
    tok = jnp.arange(t, dtype=jnp.int32)
    row_tok = jnp.zeros((n_rows + group_rows,), jnp.int32).at[
        jnp.concatenate([dest0, dest1])].set(jnp.concatenate([tok, tok]))
    n_used = g_end[-1].astype(jnp.int32)
    gid = jnp.arange(n_groups, dtype=jnp.int32)
    group_e = jnp.clip(jnp.searchsorted(g_end, gid, side='right'), 0, n_experts - 1)
    group_e = jnp.where(gid < n_used, group_e, group_e[n_used - 1]).astype(jnp.int32)
    group_rows_used = jnp.where(
        gid < n_used,
        jnp.clip(counts[group_e] - (gid - g_start[group_e]) * group_rows, 0, group_rows),
        0).astype(jnp.int32)
    ys = moe_ffn(group_e, group_rows_used, n_used.reshape(1), row_tok, h, w_gate, w_up, w_down,
                 moe_index, sub_rows=ts, n_sub_max=MOE_GROUP_SUBS)
    return moe_combine(dest0, dest1, ys, route_w, x, mod5, g_post, layer, seq_len=seq_len)


def kernel(x, c, w_ada, b_ada, g_mix_pre, g_mix_post, g_ffn_pre, g_ffn_post, w_in, conv_w,
           ssm_a_re, ssm_a_im, ssm_log_dt, ssm_b_re, ssm_b_im, ssm_c_re, ssm_c_im, ssm_d,
           w_glu, b_glu, w_out, ffn_w_gate, ffn_w_up, ffn_w_down, router_w, router_b,
           moe_w_gate, moe_w_up, moe_w_down):
    nb, seq_len, d = x.shape
    depth = w_ada.shape[0]
    d_conv = conv_w.shape[1]
    g, p, h_dim = ssm_b_re.shape[1:]
    d_ssm = g * h_dim
    n_experts = router_w.shape[2]
    t = nb * seq_len
    groups_per_slab = max(1, min(g, MXU_DIM // h_dim))

    mod_rows = 2 * SUBLANES
    c_pad = jnp.zeros((mod_rows, d), F32).at[:nb].set(c)
    mod = adaln_mod(c_pad, w_ada, b_ada)
    mod5 = mod.reshape(depth, mod_rows, 6, 1, d)
    xf = x.reshape(t, d)

    def rows(v):
        return v.reshape(v.shape[0], 1, v.shape[1])
    g_mix_pre, g_mix_post, g_ffn_pre, g_ffn_post = (
        rows(v) for v in (g_mix_pre, g_mix_post, g_ffn_pre, g_ffn_post))
    w_in16 = w_in.astype(BF16)
    w_out16 = w_out.astype(BF16)
    w_glu16 = w_glu.astype(BF16)
    conv_wt = jnp.swapaxes(conv_w, 1, 2)
    bb_re, bb_im, ab_re, ab_im = ssm_prep(ssm_a_re, ssm_a_im, ssm_log_dt, ssm_b_re, ssm_b_im)
    def slabs(w):
        return _block_diag_slabs(jnp.swapaxes(w, 2, 3).astype(BF16), groups_per_slab)
    bbd = jnp.concatenate([slabs(bb_re), slabs(bb_im)], axis=-1)
    c_cat = jnp.concatenate([slabs(ssm_c_re), slabs(ssm_c_im)], axis=-2)
    a_re = _pack_state_rows(ab_re.reshape(depth, g * p), nb)
    a_im = _pack_state_rows(ab_im.reshape(depth, g * p), nb)
    d_skip = ssm_d.reshape(depth, 1, d_ssm)
    b_glu = rows(b_glu)
    n_moe = router_w.shape[0]
    rw_pad = jnp.zeros((n_moe, d, LANES), F32).at[:, :, :n_experts].set(router_w)
    rb_pad = jnp.zeros((n_moe, 1, LANES), F32).at[:, 0, :n_experts].set(router_b)

    for l in range(depth):
        y_conv, u = mixer_in(xf, mod5, g_mix_pre, w_in16, conv_wt, l, seq_len=seq_len)
        y_ssm = ssm(u.reshape(nb, seq_len, d_ssm), bbd, c_cat, a_re, a_im, d_skip,
                    w_glu16, b_glu, l).reshape(t, d_ssm)
        routed = l % 2 == 1
        i = l // 2
        rt = (rw_pad[i], rb_pad[i]) if routed else None
        outs = mixer_out(y_conv, y_ssm, w_out16, xf, mod5, g_mix_post, g_ffn_pre, l, rt,
                         seq_len=seq_len)
        if routed:
            xf, h, logits = outs
            xf = moe_layer(h, logits, xf, mod5, g_ffn_post, moe_w_gate, moe_w_up, moe_w_down,
                           l, i, seq_len=seq_len)
        else:
            xf, h = outs
            xf = ffn_dense(h, ffn_w_gate, ffn_w_up, ffn_w_down, xf, mod5, g_ffn_post, l, i,
                           seq_len=seq_len)
    return xf.reshape(nb, seq_len, d)
```

```python
import functools

import jax
import jax.numpy as jnp
from jax import lax
from jax.experimental import pallas as pl
from jax.experimental.pallas import tpu as pltpu

F32 = jnp.float32
BF16 = jnp.bfloat16
NORM_EPS = 1e-6
LANES = 128
SUBLANES = 8
MXU_DIM = 256
EPILOGUE_ROWS = 64
FFN_EPILOGUE_ROWS = 16
MIB = 1024 * 1024


def _params(semantics, vmem_mib):
    return pltpu.CompilerParams(dimension_semantics=semantics,
                                vmem_limit_bytes=vmem_mib * MIB)


def _dot(a, b):
    return jnp.dot(a, b, preferred_element_type=F32)


def _rms(x):
    return x * lax.rsqrt(jnp.mean(x * x, axis=-1, keepdims=True) + NORM_EPS)


def _tile(n, want):
    t = min(n, want)
    while n % t:
        t -= 1
    return t


def _adaln_kernel(c_ref, w_ref, b_ref, o_ref):
    c = c_ref[...]
    c_act = c * jax.nn.sigmoid(c)
    c_hi = c_act.astype(BF16)
    c_lo = (c_act - c_hi.astype(F32)).astype(BF16)
    w16 = w_ref[...].astype(BF16)
    o_ref[...] = (_dot(c_hi, w16) + _dot(c_lo, w16)) + b_ref[...]


def adaln_mod(c_pad, w_ada, b_ada):
    depth, d, n = w_ada.shape
    rows = c_pad.shape[0]
    tn = _tile(n, 1024)
    return pl.pallas_call(
        _adaln_kernel,
        grid=(depth, n // tn),
        in_specs=[
            pl.BlockSpec((rows, d), lambda l, j: (0, 0)),
            pl.BlockSpec((None, d, tn), lambda l, j: (l, 0, j)),
            pl.BlockSpec((None, 1, tn), lambda l, j: (l, 0, j)),
        ],
        out_specs=pl.BlockSpec((None, rows, tn), lambda l, j: (l, 0, j)),
        out_shape=jax.ShapeDtypeStruct((depth, rows, n), F32),
        compiler_params=_params(("arbitrary", "arbitrary"), 56),
        name="adaln_mod",
    )(c_pad, w_ada, b_ada.reshape(depth, 1, n))


def _mixer_in_kernel(x_ref, sc_ref, sh_ref, g_ref, w_ref, cw_ref, yconv_ref, u_ref,
                     halo_scr, h_scr, *, tiles_per_seq, col_chunk):
    tm = x_ref.shape[0]
    d_conv = yconv_ref.shape[1]
    d_ssm = u_ref.shape[1]
    pre_scale = g_ref[...] * (1.0 + sc_ref[...])
    for r0 in range(0, tm, EPILOGUE_ROWS):
        rows = slice(r0, r0 + EPILOGUE_ROWS)
        h_scr[rows, :] = (_rms(x_ref[rows, :]) * pre_scale + sh_ref[...]).astype(BF16)
    h = h_scr[...]

    @pl.when(pl.program_id(0) % tiles_per_seq == 0)
    def _():
        halo_scr[...] = jnp.zeros(halo_scr.shape, F32)

    for c0 in range(0, d_conv, col_chunk):
        cols = slice(c0, c0 + col_chunk)
        gate_b = _dot(h, w_ref[:, c0:c0 + col_chunk])
        z = (_dot(h, w_ref[:, d_conv + c0:d_conv + c0 + col_chunk])
             * _dot(h, w_ref[:, 2 * d_conv + c0:2 * d_conv + c0 + col_chunk]))
        prev = halo_scr[:, cols]
        halo_scr[:, cols] = z[tm - SUBLANES:, :]
        w0 = cw_ref[0:1, cols]
        w1 = cw_ref[1:2, cols]
        w2 = cw_ref[2:3, cols]
        z1 = pltpu.roll(z, 1, 0)
        z2 = pltpu.roll(z, 2, 0)
        yconv_ref[:, cols] = (gate_b * (z * w2 + z1 * w1 + z2 * w0)).astype(yconv_ref.dtype)
        row = lax.broadcasted_iota(jnp.int32, (SUBLANES, col_chunk), 0)
        z1t = jnp.where(row < 1, pltpu.roll(prev, 1, 0), z1[:SUBLANES, :])
        z2t = jnp.where(row < 2, pltpu.roll(prev, 2, 0), z2[:SUBLANES, :])
        top = gate_b[:SUBLANES, :] * (z[:SUBLANES, :] * w2 + z1t * w1 + z2t * w0)
        yconv_ref[0:SUBLANES, cols] = top.astype(yconv_ref.dtype)

    for c0 in range(0, d_ssm, col_chunk):
        u_ref[:, c0:c0 + col_chunk] = _dot(
            h, w_ref[:, 3 * d_conv + c0:3 * d_conv + c0 + col_chunk])


SH_MIX, SC_MIX, GT_MIX, SH_FFN, SC_FFN, GT_FFN = range(6)


def _mod_spec(layer, chunk, d, batch_of):
    return pl.BlockSpec((None, None, None, 1, d),
                        lambda i, *_: (layer, batch_of(i), chunk, 0, 0))


def _layer_spec(layer, shape, **kw):
    zeros = (0,) * len(shape)
    return pl.BlockSpec((None,) + tuple(shape), lambda *_: (layer,) + zeros, **kw)


def mixer_in(x, mod5, g_pre, w_in16, conv_wt, layer, *, seq_len):
    t, d = x.shape
    d_in = w_in16.shape[2]
    d_conv = conv_wt.shape[2]
    d_ssm = d_in - 3 * d_conv
    tm = _tile(seq_len, 512)
    col_chunk = _tile(min(d_conv, d_ssm), 512)
    assert d_conv % col_chunk == 0 and d_ssm % col_chunk == 0
    tiles_per_seq = seq_len // tm

    def batch_of(i):
        return i // tiles_per_seq

    kern = functools.partial(_mixer_in_kernel, tiles_per_seq=tiles_per_seq,
                             col_chunk=col_chunk)
    return pl.pallas_call(
        kern,
        grid=(t // tm,),
        in_specs=[
            pl.BlockSpec((tm, d), lambda i: (i, 0)),
            _mod_spec(layer, SC_MIX, d, batch_of), _mod_spec(layer, SH_MIX, d, batch_of),
            _layer_spec(layer, (1, d)),
            _layer_spec(layer, (d, d_in), pipeline_mode=pl.Buffered(1)),
            _layer_spec(layer, (3, d_conv)),
        ],
        out_specs=[
            pl.BlockSpec((tm, d_conv), lambda i: (i, 0)),
            pl.BlockSpec((tm, d_ssm), lambda i: (i, 0)),
        ],
        out_shape=[jax.ShapeDtypeStruct((t, d_conv), BF16),
                   jax.ShapeDtypeStruct((t, d_ssm), F32)],
        scratch_shapes=[pltpu.VMEM((SUBLANES, d_conv), F32), pltpu.VMEM((tm, d), BF16)],
        compiler_params=_params(("arbitrary",), 48),
        name="mixer_in",
    )(x, mod5, mod5, g_pre, w_in16, conv_wt)


def _ssm_prep_kernel(are_ref, aim_ref, ldt_ref, bre_ref, bim_ref,
                     bbre_ref, bbim_ref, abre_ref, abim_ref):
    lam_re = are_ref[...]
    lam_im = aim_ref[...]
    dt = jnp.exp(ldt_ref[...])
    mag = jnp.exp(lam_re * dt)
    ang = lam_im * dt
    ab_re = mag * jnp.cos(ang)
    ab_im = mag * jnp.sin(ang)
    den = lam_re * lam_re + lam_im * lam_im
    nr = ab_re - 1.0
    q_re = (nr * lam_re + ab_im * lam_im) / den
    q_im = (ab_im * lam_re - nr * lam_im) / den
    b_re = bre_ref[...]
    b_im = bim_ref[...]
    bbre_ref[...] = q_re * b_re - q_im * b_im
    bbim_ref[...] = q_re * b_im + q_im * b_re
    abre_ref[...] = ab_re
    abim_ref[...] = ab_im


def ssm_prep(a_re, a_im, log_dt, b_re, b_im):
    depth, g, p, h = b_re.shape
    ldt = jnp.broadcast_to(log_dt.reshape(depth, g, 1, 1), (depth, g, p, 1))

    def spec(last):
        return pl.BlockSpec((None, g, p, last), lambda l: (l, 0, 0, 0))

    return pl.pallas_call(
        _ssm_prep_kernel,
        grid=(depth,),
        in_specs=[spec(1), spec(1), spec(1), spec(h), spec(h)],
        out_specs=[spec(h), spec(h), spec(1), spec(1)],
        out_shape=[jax.ShapeDtypeStruct((depth, g, p, h), F32),
                   jax.ShapeDtypeStruct((depth, g, p, h), F32),
                   jax.ShapeDtypeStruct((depth, g, p, 1), F32),
                   jax.ShapeDtypeStruct((depth, g, p, 1), F32)],
        compiler_params=_params(("arbitrary",), 56),
        name="ssm_prep",
    )(a_re.reshape(depth, g, p, 1), a_im.reshape(depth, g, p, 1), ldt, b_re, b_im)


def _block_diag_slabs(w, groups_per_slab):
    depth, g, r, c = w.shape
    ns = g // groups_per_slab
    w = w.reshape(depth, ns, groups_per_slab, r, c)
    eye = jnp.eye(groups_per_slab, dtype=w.dtype)
    out = w[:, :, :, :, None, :] * eye[None, None, :, None, :, None]
    return out.reshape(depth, ns, groups_per_slab * r, groups_per_slab * c)


def _gelu_tanh(x):
    return 0.5 * x * (1.0 + jnp.tanh(0.7978845608028654 * (x + 0.044715 * x * x * x)))


def _ssm_kernel(u_ref, bbd_ref, c_ref, are_ref, aim_ref, d_ref, wglu_ref,
                bglu_ref, o_ref, sre_scr, sim_scr, stre_scr, stim_scr, u_scr, y_scr, *,
                pitch, groups_per_step):
    nb, tc, d_ssm = u_ref.shape
    n_slabs, k_slab, two_sw = bbd_ref.shape
    sw = two_sw // 2
    tiles_per_slab = sw // LANES
    pack = SUBLANES // nb
    n_groups = sre_scr.shape[0]
    rows_all = nb * pitch

    def tile_rows(lane_tile):
        q, h = divmod(lane_tile, pack)
        return q, slice(h * rows_all, (h + 1) * rows_all)

    @pl.when(pl.program_id(0) == 0)
    def _():
        stre_scr[...] = jnp.zeros(stre_scr.shape, F32)
        stim_scr[...] = jnp.zeros(stim_scr.shape, F32)
        u_scr[...] = jnp.zeros(u_scr.shape, F32)

    for b in range(nb):
        u_scr[b * pitch:b * pitch + tc, :] = u_ref[b]

    for s in range(n_slabs):
        r = _dot(u_scr[:, s * k_slab:(s + 1) * k_slab].astype(BF16), bbd_ref[s])
        for j in range(tiles_per_slab):
            qr = tile_rows(s * tiles_per_slab + j)
            sre_scr[qr] = r[:, j * LANES:(j + 1) * LANES]
            sim_scr[qr] = r[:, sw + j * LANES:sw + (j + 1) * LANES]

    for q0 in range(0, n_groups, groups_per_step):
        qs = list(range(q0, min(q0 + groups_per_step, n_groups)))
        a_re = [are_ref[q] for q in qs]
        a_im = [aim_ref[q] for q in qs]

        def step(t, carry, qs=qs, a_re=a_re, a_im=a_im):
            rows = pl.ds(t, SUBLANES, stride=pitch)
            out = []
            for n, q in enumerate(qs):
                s_re, s_im = carry[2 * n], carry[2 * n + 1]
                n_re = a_re[n] * s_re - a_im[n] * s_im + sre_scr[q, rows, :]
                n_im = a_re[n] * s_im + a_im[n] * s_re + sim_scr[q, rows, :]
                sre_scr[q, rows, :] = n_re
                sim_scr[q, rows, :] = n_im
                out += [n_re, n_im]
            return tuple(out)

        init = []
        for q in qs:
            init += [stre_scr[q], stim_scr[q]]
        fin = lax.fori_loop(0, tc, step, tuple(init), unroll=2)
        for n, q in enumerate(qs):
            stre_scr[q] = fin[2 * n]
            stim_scr[q] = fin[2 * n + 1]

    def state_rows(s):
        tiles = [tile_rows(s * tiles_per_slab + j) for j in range(tiles_per_slab)]
        return jnp.concatenate([sre_scr[qr].astype(BF16) for qr in tiles]
                               + [(-sim_scr[qr]).astype(BF16) for qr in tiles], axis=-1)

    for s in range(n_slabs):
        cols = slice(s * k_slab, (s + 1) * k_slab)
        y = _dot(state_rows(s), c_ref[s])
        y_scr[:, cols] = y + d_ref[:, cols] * u_scr[:, cols]
    y = _gelu_tanh(y_scr[...])
    gate = _dot(y.astype(BF16), wglu_ref[...]) + bglu_ref[...]
    y_scr[...] = y * jax.nn.sigmoid(gate)
    for b in range(nb):
        o_ref[b] = y_scr[b * pitch:b * pitch + tc, :].astype(o_ref.dtype)


def _pack_state_rows(a, nb):
    depth = a.shape[0]
    pack = SUBLANES // nb
    n_groups = a.shape[1] // (LANES * pack)
    a = a.reshape(depth, n_groups, pack, 1, LANES)
    return jnp.broadcast_to(a, (depth, n_groups, pack, nb, LANES)).reshape(
        depth, n_groups, SUBLANES, LANES)


def ssm(u3, bbd, c_cat, a_re, a_im, d_skip, w_glu16, b_glu, layer):
    nb, seq_len, d_ssm = u3.shape
    n_groups = a_re.shape[1]
    assert SUBLANES % nb == 0
    tc = _tile(seq_len, 128)
    pitch = tc + SUBLANES // 2
    assert (nb * pitch) % SUBLANES == 0
    kern = functools.partial(_ssm_kernel, pitch=pitch, groups_per_step=min(8, n_groups))

    def full(a):
        return _layer_spec(layer, a.shape[1:])

    return pl.pallas_call(
        kern,
        grid=(seq_len // tc,),
        in_specs=[pl.BlockSpec((nb, tc, d_ssm), lambda c: (0, c, 0)),
                  full(bbd), full(c_cat), full(a_re), full(a_im),
                  full(d_skip), full(w_glu16), full(b_glu)],
        out_specs=pl.BlockSpec((nb, tc, d_ssm), lambda c: (0, c, 0)),
        out_shape=jax.ShapeDtypeStruct((nb, seq_len, d_ssm), BF16),
        scratch_shapes=[pltpu.VMEM((n_groups, SUBLANES * pitch, LANES), F32),
                        pltpu.VMEM((n_groups, SUBLANES * pitch, LANES), F32),
                        pltpu.VMEM((n_groups, SUBLANES, LANES), F32),
                        pltpu.VMEM((n_groups, SUBLANES, LANES), F32),
                        pltpu.VMEM((nb * pitch, d_ssm), F32),
                        pltpu.VMEM((nb * pitch, d_ssm), F32)],
        compiler_params=_params(("arbitrary",), 56),
        name="ssm_scan",
    )(u3, bbd, c_cat, a_re, a_im, d_skip, w_glu16, b_glu)


def _mixer_out_kernel(*refs, routed):
    if routed:
        (yc_ref, ys_ref, wo_ref, x_ref, gt_ref, gpost_ref, gpre_ref, sc_ref,
         sh_ref, rw_ref, rb_ref, xo_ref, h_ref, lg_ref, y_scr) = refs
    else:
        (yc_ref, ys_ref, wo_ref, x_ref, gt_ref, gpost_ref, gpre_ref, sc_ref,
         sh_ref, xo_ref, h_ref, y_scr) = refs
    tm = yc_ref.shape[0]
    y_scr[...] = _dot(jnp.concatenate([yc_ref[...], ys_ref[...]], axis=-1), wo_ref[...])
    post_scale = gt_ref[...] * gpost_ref[...]
    pre_scale = gpre_ref[...] * (1.0 + sc_ref[...])
    for r0 in range(0, tm, EPILOGUE_ROWS):
        rows = slice(r0, r0 + EPILOGUE_ROWS)
        x_new = x_ref[rows, :] + _rms(y_scr[rows, :]) * post_scale
        xo_ref[rows, :] = x_new
        h_ref[rows, :] = (_rms(x_new) * pre_scale + sh_ref[...]).astype(h_ref.dtype)
    if routed:
        h = h_ref[...]
        w = rw_ref[...]
        h_hi = h.astype(BF16)
        w_hi = w.astype(BF16)
        h_lo = (h - h_hi.astype(F32)).astype(BF16)
        w_lo = (w - w_hi.astype(F32)).astype(BF16)
        lg_ref[...] = (_dot(h_hi, w_hi) + (_dot(h_hi, w_lo) + _dot(h_lo, w_hi))
                       + rb_ref[...])


def mixer_out(y_conv, y_ssm, w_out16, x, mod5, g_post, g_pre, layer, router=None, *,
              seq_len):
    t, d = x.shape
    d_conv = y_conv.shape[1]
    d_ssm = y_ssm.shape[1]
    tm = _tile(seq_len, 512)
    tiles_per_seq = seq_len // tm
    routed = router is not None

    def batch_of(i):
        return i // tiles_per_seq

    def row_spec(n):
        return pl.BlockSpec((tm, n), lambda i: (i, 0))

    def const_spec(shape):
        return pl.BlockSpec(shape, lambda i: (0, 0))

    in_specs = [row_spec(d_conv), row_spec(d_ssm),
                _layer_spec(layer, (d_conv + d_ssm, d), pipeline_mode=pl.Buffered(1)),
                row_spec(d), _mod_spec(layer, GT_MIX, d, batch_of),
                _layer_spec(layer, (1, d)), _layer_spec(layer, (1, d)),
                _mod_spec(layer, SC_FFN, d, batch_of), _mod_spec(layer, SH_FFN, d, batch_of)]
    args = [y_conv, y_ssm, w_out16, x, mod5, g_post, g_pre, mod5, mod5]
    out_specs = [row_spec(d), row_spec(d)]
    out_shape = [jax.ShapeDtypeStruct((t, d), F32),
                 jax.ShapeDtypeStruct((t, d), F32 if routed else BF16)]
    if routed:
        rw_pad, rb_pad = router
        in_specs += [const_spec(rw_pad.shape), const_spec(rb_pad.shape)]
        args += [rw_pad, rb_pad]
        out_specs.append(row_spec(LANES))
        out_shape.append(jax.ShapeDtypeStruct((t, LANES), F32))
    return pl.pallas_call(
        functools.partial(_mixer_out_kernel, routed=routed),
        grid=(t // tm,),
        in_specs=in_specs, out_specs=out_specs, out_shape=out_shape,
        scratch_shapes=[pltpu.VMEM((tm, d), F32)],
        compiler_params=_params(("arbitrary",), 56),
        name="mixer_out_routed" if routed else "mixer_out",
    )(*args)


def _ffn_kernel(h_ref, wg_ref, wu_ref, wd_ref, x_hbm, gt_ref, gpost_ref, o_ref,
                wg16, wu16, wd16, x_buf, sem, *, sub_rows):
    i = pl.program_id(0)
    k = pl.program_id(1)
    tm = o_ref.shape[0]
    x_copy = pltpu.make_async_copy(x_hbm.at[pl.ds(i * tm, tm)], x_buf, sem)

    @pl.when(k == 0)
    def _():
        o_ref[...] = jnp.zeros(o_ref.shape, F32)
        x_copy.start()

    wg16[...] = wg_ref[...].astype(BF16)
    wu16[...] = wu_ref[...].astype(BF16)
    wd16[...] = wd_ref[...].astype(BF16)
    for r0 in range(0, tm, sub_rows):
        h = h_ref[r0:r0 + sub_rows, :]
        gate = _dot(h, wg16[...])
        act = (gate * jax.nn.sigmoid(gate)) * _dot(h, wu16[...])
        o_ref[r0:r0 + sub_rows, :] += _dot(act.astype(BF16), wd16[...])

    @pl.when(k == pl.num_programs(1) - 1)
    def _():
        x_copy.wait()
        scale = gt_ref[...] * gpost_ref[...]
        for r0 in range(0, tm, FFN_EPILOGUE_ROWS):
            rows = slice(r0, r0 + FFN_EPILOGUE_ROWS)
            o_ref[rows, :] = x_buf[rows, :] + _rms(o_ref[rows, :]) * scale


def ffn_dense(h16, w_gate, w_up, w_down, x, mod5, g_post, layer, ffn_index, *, seq_len):
    t, d = x.shape
    d_ff = w_gate.shape[2]
    tm = _tile(seq_len, 1024)
    sub_rows = _tile(tm, 512)
    tk = _tile(d_ff, 256)
    tiles_per_seq = seq_len // tm
    return pl.pallas_call(
        functools.partial(_ffn_kernel, sub_rows=sub_rows),
        grid=(t // tm, d_ff // tk),
        in_specs=[
            pl.BlockSpec((tm, d), lambda i, k: (i, 0)),
            pl.BlockSpec((None, d, tk), lambda i, k: (ffn_index, 0, k)),
            pl.BlockSpec((None, d, tk), lambda i, k: (ffn_index, 0, k)),
            pl.BlockSpec((None, tk, d), lambda i, k: (ffn_index, k, 0)),
            pl.BlockSpec(memory_space=pl.ANY),
            _mod_spec(layer, GT_FFN, d, lambda i: i // tiles_per_seq),
            _layer_spec(layer, (1, d)),
        ],
        out_specs=pl.BlockSpec((tm, d), lambda i, k: (i, 0)),
        out_shape=jax.ShapeDtypeStruct((t, d), F32),
        scratch_shapes=[pltpu.VMEM((d, tk), BF16), pltpu.VMEM((d, tk), BF16),
                        pltpu.VMEM((tk, d), BF16), pltpu.VMEM((tm, d), F32),
                        pltpu.SemaphoreType.DMA(())],
        compiler_params=_params(("arbitrary", "arbitrary"), 56),
        name="ffn_dense",
    )(h16, w_gate, w_up, w_down, x, mod5, g_post)


def _router_kernel(lg_ref, ri_ref, rw_ref, cnt_ref, carry_scr, *, n_experts):
    @pl.when(pl.program_id(0) == 0)
    def _():
        carry_scr[...] = jnp.zeros(carry_scr.shape, F32)

    tm = lg_ref.shape[0]
    lane = lax.broadcasted_iota(jnp.int32, (tm, LANES), 1)
    neg = jnp.float32(-jnp.inf)
    logit = jnp.where(lane < n_experts, lg_ref[...], neg)
    m1 = jnp.max(logit, axis=-1, keepdims=True)
    i1 = jnp.min(jnp.where(logit == m1, lane, LANES), axis=-1, keepdims=True)
    rest = jnp.where(lane == i1, neg, logit)
    m2 = jnp.max(rest, axis=-1, keepdims=True)
    i2 = jnp.min(jnp.where(rest == m2, lane, LANES), axis=-1, keepdims=True)
    e2 = jnp.exp(m2 - m1)
    w1 = 1.0 / (1.0 + e2)
    w2 = e2 / (1.0 + e2)
    hit1 = lane == i1
    hit2 = lane == i2
    onehot = jnp.where(hit1 | hit2, 1.0, 0.0)
    r = lax.broadcasted_iota(jnp.int32, (tm, tm), 0)
    c = lax.broadcasted_iota(jnp.int32, (tm, tm), 1)
    earlier = jnp.where(c < r, 1.0, 0.0).astype(BF16)
    carry = carry_scr[0:1, :]
    before = _dot(earlier, onehot.astype(BF16)) + carry
    pos1 = jnp.sum(jnp.where(hit1, before, 0.0), axis=-1, keepdims=True).astype(jnp.int32)
    pos2 = jnp.sum(jnp.where(hit2, before, 0.0), axis=-1, keepdims=True).astype(jnp.int32)
    total = carry + jnp.sum(onehot, axis=0, keepdims=True)
    carry_scr[...] = jnp.broadcast_to(total, carry_scr.shape)
    cnt_ref[...] = jnp.broadcast_to(total, cnt_ref.shape).astype(jnp.int32)
    zero_i = jnp.zeros((tm, LANES), jnp.int32)
    ri_ref[...] = jnp.where(lane == 0, i1, jnp.where(lane == 1, i2,
                            jnp.where(lane == 2, pos1, jnp.where(lane == 3, pos2, zero_i))))
    rw_ref[...] = jnp.where(lane == 0, w1, jnp.where(lane == 1, w2, 0.0))


def router(logits, n_experts):
    t = logits.shape[0]
    tm = _tile(t, 512)
    return pl.pallas_call(
        functools.partial(_router_kernel, n_experts=n_experts),
        grid=(t // tm,),
        in_specs=[pl.BlockSpec((tm, LANES), lambda i: (i, 0))],
        out_specs=[pl.BlockSpec((tm, LANES), lambda i: (i, 0)),
                   pl.BlockSpec((tm, LANES), lambda i: (i, 0)),
                   pl.BlockSpec((SUBLANES, LANES), lambda i: (0, 0))],
        out_shape=[jax.ShapeDtypeStruct((t, LANES), jnp.int32),
                   jax.ShapeDtypeStruct((t, LANES), F32),
                   jax.ShapeDtypeStruct((SUBLANES, LANES), jnp.int32)],
        scratch_shapes=[pltpu.VMEM((SUBLANES, LANES), F32)],
        compiler_params=_params(("arbitrary",), 32),
        name="router_top2",
    )(logits)


def _moe_ffn_kernel(ge_ref, nr_ref, nu_ref, tok_ref, h_hbm, wg_ref, wu_ref, wd_ref, o_ref,
                    h_scr, gbuf, wg16, wu16, wd16, sem, *, rows_per_step):
    g = pl.program_id(0)
    k = pl.program_id(1)
    n_used = nu_ref[0]
    n_sub_max, ts, _ = h_scr.shape
    group_rows = n_sub_max * ts
    nk = pl.num_programs(1)
    n_gather = gbuf.shape[0]

    def row_copy(src_row, dst_row):
        return pltpu.make_async_copy(h_hbm.at[pl.ds(src_row, 1)],
                                     gbuf.at[pl.ds(dst_row, 1)], sem)

    def for_each_sub_tile(group, fn):
        for j in range(n_sub_max):
            @pl.when(j * ts < nr_ref[group])
            def _(j=j):
                fn(j)

    def wait_rows():
        pltpu.make_async_copy(h_hbm.at[pl.ds(0, n_gather)], gbuf, sem).wait()

    @pl.when(k == 0)
    def _():
        o_ref[...] = jnp.zeros(o_ref.shape, o_ref.dtype)

    @pl.when(g < n_used)
    def _():
        @pl.when(k == 0)
        def _():
            @pl.when(g == 0)
            def _():
                def body(i, carry):
                    row_copy(tok_ref[i], i).start()
                    return carry
                lax.fori_loop(0, n_gather, body, 0, unroll=8)

            wait_rows()

            def cast(j):
                h_scr[j] = gbuf[j * ts:(j + 1) * ts, :].astype(BF16)
            for_each_sub_tile(g, cast)

        next_base = jnp.minimum(g + 1, n_used - 1) * group_rows + k * rows_per_step

        def sub_tile(j, m):
            if j == 0:
                wg16[...] = wg_ref[...].astype(BF16)
                wu16[...] = wu_ref[...].astype(BF16)
                wd16[...] = wd_ref[...].astype(BF16)
                for i in range(rows_per_step):
                    row_copy(tok_ref[next_base + i], k * rows_per_step + i).start()
            h = h_scr[j, 0:m, :]
            gate = _dot(h, wg16[...])
            act = (gate * jax.nn.sigmoid(gate)) * _dot(h, wu16[...])
            o_ref[j * ts:j * ts + m, :] += _dot(act.astype(BF16), wd16[...])

        half = ts // 2
        all_full = nr_ref[g] > group_rows - half

        @pl.when(all_full)
        def _():
            for j in range(n_sub_max):
                sub_tile(j, ts)

        @pl.when(jnp.logical_not(all_full))
        def _():
            for j in range(n_sub_max):
                rows_j = nr_ref[g] - j * ts

                @pl.when(rows_j > half)
                def _(j=j):
                    sub_tile(j, ts)

                @pl.when((rows_j > 0) & (rows_j <= half))
                def _(j=j):
                    sub_tile(j, half)

        @pl.when((g == n_used - 1) & (k == nk - 1))
        def _():
            wait_rows()


def moe_ffn(group_expert, group_rows_used, n_used, row_tok, h, w_gate, w_up, w_down, layer, *,
            sub_rows, n_sub_max):
    d = h.shape[1]
    d_ff = w_gate.shape[3]
    tk = _tile(d_ff, 256)
    nk = d_ff // tk
    group_rows = sub_rows * n_sub_max
    rows_per_step = -(-group_rows // (nk * SUBLANES)) * SUBLANES
    n_gather = rows_per_step * nk
    n_rows = group_expert.shape[0] * group_rows
    assert row_tok.shape[0] - n_rows >= n_gather - group_rows

    def ff_idx(g, k, nu):
        return jnp.where(g < nu[0], k, nk - 1)

    return pl.pallas_call(
        functools.partial(_moe_ffn_kernel, rows_per_step=rows_per_step),
        grid_spec=pltpu.PrefetchScalarGridSpec(
            num_scalar_prefetch=4,
            grid=(n_rows // group_rows, nk),
            in_specs=[
                pl.BlockSpec(memory_space=pl.ANY),
                pl.BlockSpec((None, None, d, tk),
                             lambda g, k, ge, ns, nu, tok: (layer, ge[g], 0, ff_idx(g, k, nu))),
                pl.BlockSpec((None, None, d, tk),
                             lambda g, k, ge, ns, nu, tok: (layer, ge[g], 0, ff_idx(g, k, nu))),
                pl.BlockSpec((None, None, tk, d),
                             lambda g, k, ge, ns, nu, tok: (layer, ge[g], ff_idx(g, k, nu), 0)),
            ],
            out_specs=pl.BlockSpec((group_rows, d), lambda g, k, ge, ns, nu, tok: (g, 0)),
            scratch_shapes=[pltpu.VMEM((n_sub_max, sub_rows, d), BF16),
                            pltpu.VMEM((n_gather, d), F32),
                            pltpu.VMEM((d, tk), BF16), pltpu.VMEM((d, tk), BF16),
                            pltpu.VMEM((tk, d), BF16),
                            pltpu.SemaphoreType.DMA(())],
        ),
        out_shape=jax.ShapeDtypeStruct((n_rows, d), F32),
        compiler_params=_params(("arbitrary", "arbitrary"), 56),
        name="moe_ffn",
    )(group_expert, group_rows_used, n_used, row_tok, h, w_gate, w_up, w_down)


def _moe_combine_kernel(d0_ref, d1_ref, ys_ref, rw_ref, x_ref, gt_ref, gpost_ref, o_ref,
                        buf, sems):
    i = pl.program_id(0)
    tm = x_ref.shape[0]
    slot = i % 2

    def start_tile(tile, s):
        def body(t, carry):
            for which, dest in enumerate((d0_ref, d1_ref)):
                pltpu.make_async_copy(ys_ref.at[pl.ds(dest[tile * tm + t], 1)],
                                      buf.at[s, pl.ds(which * tm + t, 1)],
                                      sems.at[s]).start()
            return carry
        lax.fori_loop(0, tm, body, 0, unroll=8)

    @pl.when(i == 0)
    def _():
        start_tile(0, 0)

    @pl.when(i + 1 < pl.num_programs(0))
    def _():
        start_tile(i + 1, 1 - slot)

    pltpu.make_async_copy(ys_ref.at[pl.ds(0, 2 * tm)], buf.at[slot], sems.at[slot]).wait()
    scale = gt_ref[...] * gpost_ref[...]
    for r0 in range(0, tm, EPILOGUE_ROWS):
        rows = slice(r0, r0 + EPILOGUE_ROWS)
        y = (buf[slot, r0:r0 + EPILOGUE_ROWS, :] * rw_ref[rows, 0:1]
             + buf[slot, tm + r0:tm + r0 + EPILOGUE_ROWS, :] * rw_ref[rows, 1:2])
        o_ref[rows, :] = x_ref[rows, :] + _rms(y) * scale


def moe_combine(dest0, dest1, ys, route_w, x, mod5, g_post, layer, *, seq_len):
    t, d = x.shape
    tm = _tile(seq_len, 256)
    tiles_per_seq = seq_len // tm
    return pl.pallas_call(
        _moe_combine_kernel,
        grid_spec=pltpu.PrefetchScalarGridSpec(
            num_scalar_prefetch=2,
            grid=(t // tm,),
            in_specs=[
                pl.BlockSpec(memory_space=pl.ANY),
                pl.BlockSpec((tm, LANES), lambda i, d0, d1: (i, 0)),
                pl.BlockSpec((tm, d), lambda i, d0, d1: (i, 0)),
                _mod_spec(layer, GT_FFN, d, lambda i: i // tiles_per_seq),
                _layer_spec(layer, (1, d)),
            ],
            out_specs=pl.BlockSpec((tm, d), lambda i, d0, d1: (i, 0)),
            scratch_shapes=[pltpu.VMEM((2, 2 * tm, d), F32), pltpu.SemaphoreType.DMA((2,))],
        ),
        out_shape=jax.ShapeDtypeStruct((t, d), F32),
        compiler_params=_params(("arbitrary",), 48),
        name="moe_combine",
    )(dest0, dest1, ys, route_w, x, mod5, g_post)


MOE_SUB_ROWS = 512
MOE_GROUP_SUBS = 2


def moe_layer(h, logits, x, mod5, g_post, w_gate, w_up, w_down, layer, moe_index, *,
              seq_len):
    t, d = h.shape
    n_experts = w_gate.shape[1]
    ts = min(MOE_SUB_ROWS, t)
    group_rows = ts * MOE_GROUP_SUBS
    route_i, route_w, counts = router(logits, n_experts)
    e0, e1, pos0, pos1 = (route_i[:, k] for k in range(4))
    counts = counts[0, :n_experts]
    groups_e = (counts + group_rows - 1) // group_rows
    g_end = jnp.cumsum(groups_e)
    g_start = g_end - groups_e
    dest0 = g_start[e0] * group_rows + pos0
    dest1 = g_start[e1] * group_rows + pos1
    n_groups = -(-(2 * t) // group_rows) + n_experts
    n_rows = n_groups * group_rows
    tok = jnp.arange(t, dtype=jnp.int32)
    row_tok = jnp.zeros((n_rows + group_rows,), jnp.int32).at[
        jnp.concatenate([dest0, dest1])].set(jnp.concatenate([tok, tok]))
    n_used = g_end[-1].astype(jnp.int32)
    gid = jnp.arange(n_groups, dtype=jnp.int32)
    group_e = jnp.clip(jnp.searchsorted(g_end, gid, side='right'), 0, n_experts - 1)
    group_e = jnp.where(gid < n_used, group_e, group_e[n_used - 1]).astype(jnp.int32)
    group_rows_used = jnp.where(
        gid < n_used,
        jnp.clip(counts[group_e] - (gid - g_start[group_e]) * group_rows, 0, group_rows),
        0).astype(jnp.int32)
    ys = moe_ffn(group_e, group_rows_used, n_used.reshape(1), row_tok, h, w_gate, w_up, w_down,
                 moe_index, sub_rows=ts, n_sub_max=MOE_GROUP_SUBS)
    return moe_combine(dest0, dest1, ys, route_w, x, mod5, g_post, layer, seq_len=seq_len)


def kernel(x, c, w_ada, b_ada, g_mix_pre, g_mix_post, g_ffn_pre, g_ffn_post, w_in, conv_w,
           ssm_a_re, ssm_a_im, ssm_log_dt, ssm_b_re, ssm_b_im, ssm_c_re, ssm_c_im, ssm_d,
           w_glu, b_glu, w_out, ffn_w_gate, ffn_w_up, ffn_w_down, router_w, router_b,
           moe_w_gate, moe_w_up, moe_w_down):
    nb, seq_len, d = x.shape
    depth = w_ada.shape[0]
    d_conv = conv_w.shape[1]
    g, p, h_dim = ssm_b_re.shape[1:]
    d_ssm = g * h_dim
    n_experts = router_w.shape[2]
    t = nb * seq_len
    groups_per_slab = max(1, min(g, MXU_DIM // h_dim))

    mod_rows = 2 * SUBLANES
    c_pad = jnp.zeros((mod_rows, d), F32).at[:nb].set(c)
    mod = adaln_mod(c_pad, w_ada, b_ada)
    mod5 = mod.reshape(depth, mod_rows, 6, 1, d)
    xf = x.reshape(t, d)

    def rows(v):
        return v.reshape(v.shape[0], 1, v.shape[1])
    g_mix_pre, g_mix_post, g_ffn_pre, g_ffn_post = (
        rows(v) for v in (g_mix_pre, g_mix_post, g_ffn_pre, g_ffn_post))
    w_in16 = w_in.astype(BF16)
    w_out16 = w_out.astype(BF16)
    w_glu16 = w_glu.astype(BF16)
    conv_wt = jnp.swapaxes(conv_w, 1, 2)
    bb_re, bb_im, ab_re, ab_im = ssm_prep(ssm_a_re, ssm_a_im, ssm_log_dt, ssm_b_re, ssm_b_im)
    def slabs(w):
        return _block_diag_slabs(jnp.swapaxes(w, 2, 3).astype(BF16), groups_per_slab)
    bbd = jnp.concatenate([slabs(bb_re), slabs(bb_im)], axis=-1)
    c_cat = jnp.concatenate([slabs(ssm_c_re), slabs(ssm_c_im)], axis=-2)
    a_re = _pack_state_rows(ab_re.reshape(depth, g * p), nb)
    a_im = _pack_state_rows(ab_im.reshape(depth, g * p), nb)
    d_skip = ssm_d.reshape(depth, 1, d_ssm)
    b_glu = rows(b_glu)
    n_moe = router_w.shape[0]
    rw_pad = jnp.zeros((n_moe, d, LANES), F32).at[:, :, :n_experts].set(router_w)
    rb_pad = jnp.zeros((n_moe, 1, LANES), F32).at[:, 0, :n_experts].set(router_b)

    for l in range(depth):
        y_conv, u = mixer_in(xf, mod5, g_mix_pre, w_in16, conv_wt, l, seq_len=seq_len)
        y_ssm = ssm(u.reshape(nb, seq_len, d_ssm), bbd, c_cat, a_re, a_im, d_skip,
                    w_glu16, b_glu, l).reshape(t, d_ssm)
        routed = l % 2 == 1
        i = l // 2
        rt = (rw_pad[i], rb_pad[i]) if routed else None
        outs = mixer_out(y_conv, y_ssm, w_out16, xf, mod5, g_mix_post, g_ffn_pre, l, rt,
                         seq_len=seq_len)
        if routed:
            xf, h, logits = outs
            xf = moe_layer(h, logits, xf, mod5, g_ffn_post, moe_w_gate, moe_w_up, moe_w_down,
                           l, i, seq_len=seq_len)
        else:
            xf, h = outs
            xf = ffn_dense(h, ffn_w_gate, ffn_w_up, ffn_w_down, xf, mod5, g_ffn_post, l, i,
                           seq_len=seq_len)
    return xf.reshape(nb, seq_len, d)
```

```python
import functools

import jax
import jax.numpy as jnp
from jax import lax
from jax.experimental import pallas as pl
from jax.experimental.pallas import tpu as pltpu

F32 = jnp.float32
BF16 = jnp.bfloat16
NORM_EPS = 1e-6
LANES = 128
SUBLANES = 8
MXU_DIM = 256
EPILOGUE_ROWS = 64
FFN_EPILOGUE_ROWS = 16
MIB = 1024 * 1024


def _params(semantics, vmem_mib):
    return pltpu.CompilerParams(dimension_semantics=semantics,
                                vmem_limit_bytes=vmem_mib * MIB)


def _dot(a, b):
    return jnp.dot(a, b, preferred_element_type=F32)


def _rms(x):
    return x * lax.rsqrt(jnp.mean(x * x, axis=-1, keepdims=True) + NORM_EPS)


def _tile(n, want):
    t = min(n, want)
    while n % t:
        t -= 1
    return t


def _adaln_kernel(c_ref, w_ref, b_ref, o_ref):
    c = c_ref[...]
    c_act = c * jax.nn.sigmoid(c)
    c_hi = c_act.astype(BF16)
    c_lo = (c_act - c_hi.astype(F32)).astype(BF16)
    w16 = w_ref[...].astype(BF16)
    o_ref[...] = (_dot(c_hi, w16) + _dot(c_lo, w16)) + b_ref[...]


def adaln_mod(c_pad, w_ada, b_ada):
    depth, d, n = w_ada.shape
    rows = c_pad.shape[0]
    tn = _tile(n, 1024)
    return pl.pallas_call(
        _adaln_kernel,
        grid=(depth, n // tn),
        in_specs=[
            pl.BlockSpec((rows, d), lambda l, j: (0, 0)),
            pl.BlockSpec((None, d, tn), lambda l, j: (l, 0, j)),
            pl.BlockSpec((None, 1, tn), lambda l, j: (l, 0, j)),
        ],
        out_specs=pl.BlockSpec((None, rows, tn), lambda l, j: (l, 0, j)),
        out_shape=jax.ShapeDtypeStruct((depth, rows, n), F32),
        compiler_params=_params(("arbitrary", "arbitrary"), 56),
        name="adaln_mod",
    )(c_pad, w_ada, b_ada.reshape(depth, 1, n))


def _mixer_in_kernel(x_ref, sc_ref, sh_ref, g_ref, w_ref, cw_ref, yconv_ref, u_ref,
                     halo_scr, h_scr, *, tiles_per_seq, col_chunk):
    tm = x_ref.shape[0]
    d_conv = yconv_ref.shape[1]
    d_ssm = u_ref.shape[1]
    pre_scale = g_ref[...] * (1.0 + sc_ref[...])
    for r0 in range(0, tm, EPILOGUE_ROWS):
        rows = slice(r0, r0 + EPILOGUE_ROWS)
        h_scr[rows, :] = (_rms(x_ref[rows, :]) * pre_scale + sh_ref[...]).astype(BF16)
    h = h_scr[...]

    @pl.when(pl.program_id(0) % tiles_per_seq == 0)
    def _():
        halo_scr[...] = jnp.zeros(halo_scr.shape, F32)

    for c0 in range(0, d_conv, col_chunk):
        cols = slice(c0, c0 + col_chunk)
        gate_b = _dot(h, w_ref[:, c0:c0 + col_chunk])
        z = (_dot(h, w_ref[:, d_conv + c0:d_conv + c0 + col_chunk])
             * _dot(h, w_ref[:, 2 * d_conv + c0:2 * d_conv + c0 + col_chunk]))
        prev = halo_scr[:, cols]
        halo_scr[:, cols] = z[tm - SUBLANES:, :]
        w0 = cw_ref[0:1, cols]
        w1 = cw_ref[1:2, cols]
        w2 = cw_ref[2:3, cols]
        z1 = pltpu.roll(z, 1, 0)
        z2 = pltpu.roll(z, 2, 0)
        yconv_ref[:, cols] = (gate_b * (z * w2 + z1 * w1 + z2 * w0)).astype(yconv_ref.dtype)
        row = lax.broadcasted_iota(jnp.int32, (SUBLANES, col_chunk), 0)
        z1t = jnp.where(row < 1, pltpu.roll(prev, 1, 0), z1[:SUBLANES, :])
        z2t = jnp.where(row < 2, pltpu.roll(prev, 2, 0), z2[:SUBLANES, :])
        top = gate_b[:SUBLANES, :] * (z[:SUBLANES, :] * w2 + z1t * w1 + z2t * w0)
        yconv_ref[0:SUBLANES, cols] = top.astype(yconv_ref.dtype)

    for c0 in range(0, d_ssm, col_chunk):
        u_ref[:, c0:c0 + col_chunk] = _dot(
            h, w_ref[:, 3 * d_conv + c0:3 * d_conv + c0 + col_chunk])


SH_MIX, SC_MIX, GT_MIX, SH_FFN, SC_FFN, GT_FFN = range(6)


def _mod_spec(layer, chunk, d, batch_of):
    return pl.BlockSpec((None, None, None, 1, d),
                        lambda i, *_: (layer, batch_of(i), chunk, 0, 0))


def _layer_spec(layer, shape, **kw):
    zeros = (0,) * len(shape)
    return pl.BlockSpec((None,) + tuple(shape), lambda *_: (layer,) + zeros, **kw)


def mixer_in(x, mod5, g_pre, w_in16, conv_wt, layer, *, seq_len):
    t, d = x.shape
    d_in = w_in16.shape[2]
    d_conv = conv_wt.shape[2]
    d_ssm = d_in - 3 * d_conv
    tm = _tile(seq_len, 512)
    col_chunk = _tile(min(d_conv, d_ssm), 512)
    assert d_conv % col_chunk == 0 and d_ssm % col_chunk == 0
    tiles_per_seq = seq_len // tm

    def batch_of(i):
        return i // tiles_per_seq

    kern = functools.partial(_mixer_in_kernel, tiles_per_seq=tiles_per_seq,
                             col_chunk=col_chunk)
    return pl.pallas_call(
        kern,
        grid=(t // tm,),
        in_specs=[
            pl.BlockSpec((tm, d), lambda i: (i, 0)),
            _mod_spec(layer, SC_MIX, d, batch_of), _mod_spec(layer, SH_MIX, d, batch_of),
            _layer_spec(layer, (1, d)),
            _layer_spec(layer, (d, d_in), pipeline_mode=pl.Buffered(1)),
            _layer_spec(layer, (3, d_conv)),
        ],
        out_specs=[
            pl.BlockSpec((tm, d_conv), lambda i: (i, 0)),
            pl.BlockSpec((tm, d_ssm), lambda i: (i, 0)),
        ],
        out_shape=[jax.ShapeDtypeStruct((t, d_conv), BF16),
                   jax.ShapeDtypeStruct((t, d_ssm), F32)],
        scratch_shapes=[pltpu.VMEM((SUBLANES, d_conv), F32), pltpu.VMEM((tm, d), BF16)],
        compiler_params=_params(("arbitrary",), 48),
        name="mixer_in",
    )(x, mod5, mod5, g_pre, w_in16, conv_wt)


def _ssm_prep_kernel(are_ref, aim_ref, ldt_ref, bre_ref, bim_ref,
                     bbre_ref, bbim_ref, abre_ref, abim_ref):
    lam_re = are_ref[...]
    lam_im = aim_ref[...]
    dt = jnp.exp(ldt_ref[...])
    mag = jnp.exp(lam_re * dt)
    ang = lam_im * dt
    ab_re = mag * jnp.cos(ang)
    ab_im = mag * jnp.sin(ang)
    den = lam_re * lam_re + lam_im * lam_im
    nr = ab_re - 1.0
    q_re = (nr * lam_re + ab_im * lam_im) / den
    q_im = (ab_im * lam_re - nr * lam_im) / den
    b_re = bre_ref[...]
    b_im = bim_ref[...]
    bbre_ref[...] = q_re * b_re - q_im * b_im
    bbim_ref[...] = q_re * b_im + q_im * b_re
    abre_ref[...] = ab_re
    abim_ref[...] = ab_im


def ssm_prep(a_re, a_im, log_dt, b_re, b_im):
    depth, g, p, h = b_re.shape
    ldt = jnp.broadcast_to(log_dt.reshape(depth, g, 1, 1), (depth, g, p, 1))

    def spec(last):
        return pl.BlockSpec((None, g, p, last), lambda l: (l, 0, 0, 0))

    return pl.pallas_call(
        _ssm_prep_kernel,
        grid=(depth,),
        in_specs=[spec(1), spec(1), spec(1), spec(h), spec(h)],
        out_specs=[spec(h), spec(h), spec(1), spec(1)],
        out_shape=[jax.ShapeDtypeStruct((depth, g, p, h), F32),
                   jax.ShapeDtypeStruct((depth, g, p, h), F32),
                   jax.ShapeDtypeStruct((depth, g, p, 1), F32),
                   jax.ShapeDtypeStruct((depth, g, p, 1), F32)],
        compiler_params=_params(("arbitrary",), 56),
        name="ssm_prep",
    )(a_re.reshape(depth, g, p, 1), a_im.reshape(depth, g, p, 1), ldt, b_re, b_im)


def _block_diag_slabs(w, groups_per_slab):
    depth, g, r, c = w.shape
    ns = g // groups_per_slab
    w = w.reshape(depth, ns, groups_per_slab, r, c)
    eye = jnp.eye(groups_per_slab, dtype=w.dtype)
    out = w[:, :, :, :, None, :] * eye[None, None, :, None, :, None]
    return out.reshape(depth, ns, groups_per_slab * r, groups_per_slab * c)


def _gelu_tanh(x):
    return 0.5 * x * (1.0 + jnp.tanh(0.7978845608028654 * (x + 0.044715 * x * x * x)))


def _ssm_kernel(u_ref, bbd_ref, c_ref, are_ref, aim_ref, d_ref, wglu_ref,
                bglu_ref, o_ref, sre_scr, sim_scr, stre_scr, stim_scr, u_scr, y_scr, *,
                pitch, groups_per_step):
    nb, tc, d_ssm = u_ref.shape
    n_slabs, k_slab, two_sw = bbd_ref.shape
    sw = two_sw // 2
    tiles_per_slab = sw // LANES
    pack = SUBLANES // nb
    n_groups = sre_scr.shape[0]
    rows_all = nb * pitch

    def tile_rows(lane_tile):
        q, h = divmod(lane_tile, pack)
        return q, slice(h * rows_all, (h + 1) * rows_all)

    @pl.when(pl.program_id(0) == 0)
    def _():
        stre_scr[...] = jnp.zeros(stre_scr.shape, F32)
        stim_scr[...] = jnp.zeros(stim_scr.shape, F32)
        u_scr[...] = jnp.zeros(u_scr.shape, F32)

    for b in range(nb):
        u_scr[b * pitch:b * pitch + tc, :] = u_ref[b]

    for s in range(n_slabs):
        r = _dot(u_scr[:, s * k_slab:(s + 1) * k_slab].astype(BF16), bbd_ref[s])
        for j in range(tiles_per_slab):
            qr = tile_rows(s * tiles_per_slab + j)
            sre_scr[qr] = r[:, j * LANES:(j + 1) * LANES]
            sim_scr[qr] = r[:, sw + j * LANES:sw + (j + 1) * LANES]

    for q0 in range(0, n_groups, groups_per_step):
        qs = list(range(q0, min(q0 + groups_per_step, n_groups)))
        a_re = [are_ref[q] for q in qs]
        a_im = [aim_ref[q] for q in qs]

        def step(t, carry, qs=qs, a_re=a_re, a_im=a_im):
            rows = pl.ds(t, SUBLANES, stride=pitch)
            out = []
            for n, q in enumerate(qs):
                s_re, s_im = carry[2 * n], carry[2 * n + 1]
                n_re = a_re[n] * s_re - a_im[n] * s_im + sre_scr[q, rows, :]
                n_im = a_re[n] * s_im + a_im[n] * s_re + sim_scr[q, rows, :]
                sre_scr[q, rows, :] = n_re
                sim_scr[q, rows, :] = n_im
                out += [n_re, n_im]
            return tuple(out)

        init = []
        for q in qs:
            init += [stre_scr[q], stim_scr[q]]
        fin = lax.fori_loop(0, tc, step, tuple(init), unroll=2)
        for n, q in enumerate(qs):
            stre_scr[q] = fin[2 * n]
            stim_scr[q] = fin[2 * n + 1]

    def state_rows(s):
        tiles = [tile_rows(s * tiles_per_slab + j) for j in range(tiles_per_slab)]
        return jnp.concatenate([sre_scr[qr].astype(BF16) for qr in tiles]
                               + [(-sim_scr[qr]).astype(BF16) for qr in tiles], axis=-1)

    for s in range(n_slabs):
        cols = slice(s * k_slab, (s + 1) * k_slab)
        y = _dot(state_rows(s), c_ref[s])
        y_scr[:, cols] = y + d_ref[:, cols] * u_scr[:, cols]
    y = _gelu_tanh(y_scr[...])
    gate = _dot(y.astype(BF16), wglu_ref[...]) + bglu_ref[...]
    y_scr[...] = y * jax.nn.sigmoid(gate)
    for b in range(nb):
        o_ref[b] = y_scr[b * pitch:b * pitch + tc, :].astype(o_ref.dtype)


def _pack_state_rows(a, nb):
    depth = a.shape[0]
    pack = SUBLANES // nb
    n_groups = a.shape[1] // (LANES * pack)
    a = a.reshape(depth, n_groups, pack, 1, LANES)
    return jnp.broadcast_to(a, (depth, n_groups, pack, nb, LANES)).reshape(
        depth, n_groups, SUBLANES, LANES)


def ssm(u3, bbd, c_cat, a_re, a_im, d_skip, w_glu16, b_glu, layer):
    nb, seq_len, d_ssm = u3.shape
    n_groups = a_re.shape[1]
    assert SUBLANES % nb == 0
    tc = _tile(seq_len, 128)
    pitch = tc + SUBLANES // 2
    assert (nb * pitch) % SUBLANES == 0
    kern = functools.partial(_ssm_kernel, pitch=pitch, groups_per_step=min(8, n_groups))

    def full(a):
        return _layer_spec(layer, a.shape[1:])

    return pl.pallas_call(
        kern,
        grid=(seq_len // tc,),
        in_specs=[pl.BlockSpec((nb, tc, d_ssm), lambda c: (0, c, 0)),
                  full(bbd), full(c_cat), full(a_re), full(a_im),
                  full(d_skip), full(w_glu16), full(b_glu)],
        out_specs=pl.BlockSpec((nb, tc, d_ssm), lambda c: (0, c, 0)),
        out_shape=jax.ShapeDtypeStruct((nb, seq_len, d_ssm), BF16),
        scratch_shapes=[pltpu.VMEM((n_groups, SUBLANES * pitch, LANES), F32),
                        pltpu.VMEM((n_groups, SUBLANES * pitch, LANES), F32),
                        pltpu.VMEM((n_groups, SUBLANES, LANES), F32),
                        pltpu.VMEM((n_groups, SUBLANES, LANES), F32),
                        pltpu.VMEM((nb * pitch, d_ssm), F32),
                        pltpu.VMEM((nb * pitch, d_ssm), F32)],
        compiler_params=_params(("arbitrary",), 56),
        name="ssm_scan",
    )(u3, bbd, c_cat, a_re, a_im, d_skip, w_glu16, b_glu)


def _mixer_out_kernel(*refs, n_experts):
    routed = n_experts is not None
    if routed:
        (yc_ref, ys_ref, wo_ref, x_ref, gt_ref, gpost_ref, gpre_ref, sc_ref,
         sh_ref, rw_ref, rb_ref, xo_ref, h_ref, route_i_ref, route_w_ref, cnt_ref,
         y_scr, carry_scr) = refs
    else:
        (yc_ref, ys_ref, wo_ref, x_ref, gt_ref, gpost_ref, gpre_ref, sc_ref,
         sh_ref, xo_ref, h_ref, y_scr) = refs
    tm = yc_ref.shape[0]
    y_scr[...] = _dot(jnp.concatenate([yc_ref[...], ys_ref[...]], axis=-1), wo_ref[...])
    post_scale = gt_ref[...] * gpost_ref[...]
    pre_scale = gpre_ref[...] * (1.0 + sc_ref[...])
    for r0 in range(0, tm, EPILOGUE_ROWS):
        rows = slice(r0, r0 + EPILOGUE_ROWS)
        x_new = x_ref[rows, :] + _rms(y_scr[rows, :]) * post_scale
        xo_ref[rows, :] = x_new
        h_ref[rows, :] = (_rms(x_new) * pre_scale + sh_ref[...]).astype(h_ref.dtype)
    if routed:
        h = h_ref[...]
        w = rw_ref[...]
        h_hi = h.astype(BF16)
        w_hi = w.astype(BF16)
        h_lo = (h - h_hi.astype(F32)).astype(BF16)
        w_lo = (w - w_hi.astype(F32)).astype(BF16)
        logits = _dot(h_hi, w_hi) + (_dot(h_hi, w_lo) + _dot(h_lo, w_hi)) + rb_ref[...]
        _route(logits, route_i_ref, route_w_ref, cnt_ref, carry_scr, n_experts)


def mixer_out(y_conv, y_ssm, w_out16, x, mod5, g_post, g_pre, layer, router=None, *,
              seq_len):
    t, d = x.shape
    d_conv = y_conv.shape[1]
    d_ssm = y_ssm.shape[1]
    tm = _tile(seq_len, 512)
    tiles_per_seq = seq_len // tm
    routed = router is not None

    def batch_of(i):
        return i // tiles_per_seq

    def row_spec(n):
        return pl.BlockSpec((tm, n), lambda i: (i, 0))

    def const_spec(shape):
        return pl.BlockSpec(shape, lambda i: (0, 0))

    in_specs = [row_spec(d_conv), row_spec(d_ssm),
                _layer_spec(layer, (d_conv + d_ssm, d), pipeline_mode=pl.Buffered(1)),
                row_spec(d), _mod_spec(layer, GT_MIX, d, batch_of),
                _layer_spec(layer, (1, d)), _layer_spec(layer, (1, d)),
                _mod_spec(layer, SC_FFN, d, batch_of), _mod_spec(layer, SH_FFN, d, batch_of)]
    args = [y_conv, y_ssm, w_out16, x, mod5, g_post, g_pre, mod5, mod5]
    out_specs = [row_spec(d), row_spec(d)]
    out_shape = [jax.ShapeDtypeStruct((t, d), F32),
                 jax.ShapeDtypeStruct((t, d), F32 if routed else BF16)]
    scratch_shapes = [pltpu.VMEM((tm, d), F32)]
    n_experts = None
    if routed:
        rw_pad, rb_pad, n_experts = router
        in_specs += [const_spec(rw_pad.shape), const_spec(rb_pad.shape)]
        args += [rw_pad, rb_pad]
        out_specs += [row_spec(LANES), row_spec(LANES), const_spec((SUBLANES, LANES))]
        out_shape += [jax.ShapeDtypeStruct((t, LANES), jnp.int32),
                      jax.ShapeDtypeStruct((t, LANES), F32),
                      jax.ShapeDtypeStruct((SUBLANES, LANES), jnp.int32)]
        scratch_shapes.append(pltpu.VMEM((SUBLANES, LANES), F32))
    return pl.pallas_call(
        functools.partial(_mixer_out_kernel, n_experts=n_experts),
        grid=(t // tm,),
        in_specs=in_specs, out_specs=out_specs, out_shape=out_shape,
        scratch_shapes=scratch_shapes,
        compiler_params=_params(("arbitrary",), 56),
        name="mixer_out_routed" if routed else "mixer_out",
    )(*args)


def _ffn_kernel(h_ref, wg_ref, wu_ref, wd_ref, x_hbm, gt_ref, gpost_ref, o_ref,
                wg16, wu16, wd16, x_buf, sem, *, sub_rows):
    i = pl.program_id(0)
    k = pl.program_id(1)
    tm = o_ref.shape[0]
    x_copy = pltpu.make_async_copy(x_hbm.at[pl.ds(i * tm, tm)], x_buf, sem)

    @pl.when(k == 0)
    def _():
        o_ref[...] = jnp.zeros(o_ref.shape, F32)
        x_copy.start()

    wg16[...] = wg_ref[...].astype(BF16)
    wu16[...] = wu_ref[...].astype(BF16)
    wd16[...] = wd_ref[...].astype(BF16)
    for r0 in range(0, tm, sub_rows):
        h = h_ref[r0:r0 + sub_rows, :]
        gate = _dot(h, wg16[...])
        act = (gate * jax.nn.sigmoid(gate)) * _dot(h, wu16[...])
        o_ref[r0:r0 + sub_rows, :] += _dot(act.astype(BF16), wd16[...])

    @pl.when(k == pl.num_programs(1) - 1)
    def _():
        x_copy.wait()
        scale = gt_ref[...] * gpost_ref[...]
        for r0 in range(0, tm, FFN_EPILOGUE_ROWS):
            rows = slice(r0, r0 + FFN_EPILOGUE_ROWS)
            o_ref[rows, :] = x_buf[rows, :] + _rms(o_ref[rows, :]) * scale


def ffn_dense(h16, w_gate, w_up, w_down, x, mod5, g_post, layer, ffn_index, *, seq_len):
    t, d = x.shape
    d_ff = w_gate.shape[2]
    tm = _tile(seq_len, 1024)
    sub_rows = _tile(tm, 512)
    tk = _tile(d_ff, 256)
    tiles_per_seq = seq_len // tm
    return pl.pallas_call(
        functools.partial(_ffn_kernel, sub_rows=sub_rows),
        grid=(t // tm, d_ff // tk),
        in_specs=[
            pl.BlockSpec((tm, d), lambda i, k: (i, 0)),
            pl.BlockSpec((None, d, tk), lambda i, k: (ffn_index, 0, k)),
            pl.BlockSpec((None, d, tk), lambda i, k: (ffn_index, 0, k)),
            pl.BlockSpec((None, tk, d), lambda i, k: (ffn_index, k, 0)),
            pl.BlockSpec(memory_space=pl.ANY),
            _mod_spec(layer, GT_FFN, d, lambda i: i // tiles_per_seq),
            _layer_spec(layer, (1, d)),
        ],
        out_specs=pl.BlockSpec((tm, d), lambda i, k: (i, 0)),
        out_shape=jax.ShapeDtypeStruct((t, d), F32),
        scratch_shapes=[pltpu.VMEM((d, tk), BF16), pltpu.VMEM((d, tk), BF16),
                        pltpu.VMEM((tk, d), BF16), pltpu.VMEM((tm, d), F32),
                        pltpu.SemaphoreType.DMA(())],
        compiler_params=_params(("arbitrary", "arbitrary"), 56),
        name="ffn_dense",
    )(h16, w_gate, w_up, w_down, x, mod5, g_post)


def _route(logits, ri_ref, rw_ref, cnt_ref, carry_scr, n_experts):
    @pl.when(pl.program_id(0) == 0)
    def _():
        carry_scr[...] = jnp.zeros(carry_scr.shape, F32)

    tm = logits.shape[0]
    lane = lax.broadcasted_iota(jnp.int32, (tm, LANES), 1)
    neg = jnp.float32(-jnp.inf)
    logit = jnp.where(lane < n_experts, logits, neg)
    m1 = jnp.max(logit, axis=-1, keepdims=True)
    i1 = jnp.min(jnp.where(logit == m1, lane, LANES), axis=-1, keepdims=True)
    rest = jnp.where(lane == i1, neg, logit)
    m2 = jnp.max(rest, axis=-1, keepdims=True)
    i2 = jnp.min(jnp.where(rest == m2, lane, LANES), axis=-1, keepdims=True)
    e2 = jnp.exp(m2 - m1)
    w1 = 1.0 / (1.0 + e2)
    w2 = e2 / (1.0 + e2)
    hit1 = lane == i1
    hit2 = lane == i2
    onehot = jnp.where(hit1 | hit2, 1.0, 0.0)
    r = lax.broadcasted_iota(jnp.int32, (tm, tm), 0)
    c = lax.broadcasted_iota(jnp.int32, (tm, tm), 1)
    earlier = jnp.where(c < r, 1.0, 0.0).astype(BF16)
    carry = carry_scr[0:1, :]
    before = _dot(earlier, onehot.astype(BF16)) + carry
    pos1 = jnp.sum(jnp.where(hit1, before, 0.0), axis=-1, keepdims=True).astype(jnp.int32)
    pos2 = jnp.sum(jnp.where(hit2, before, 0.0), axis=-1, keepdims=True).astype(jnp.int32)
    total = carry + jnp.sum(onehot, axis=0, keepdims=True)
    carry_scr[...] = jnp.broadcast_to(total, carry_scr.shape)
    cnt_ref[...] = jnp.broadcast_to(total, cnt_ref.shape).astype(jnp.int32)
    zero_i = jnp.zeros((tm, LANES), jnp.int32)
    ri_ref[...] = jnp.where(lane == 0, i1, jnp.where(lane == 1, i2,
                            jnp.where(lane == 2, pos1, jnp.where(lane == 3, pos2, zero_i))))
    rw_ref[...] = jnp.where(lane == 0, w1, jnp.where(lane == 1, w2, 0.0))


def _moe_ffn_kernel(ge_ref, nr_ref, nu_ref, tok_ref, h_hbm, wg_ref, wu_ref, wd_ref, o_ref,
                    h_scr, gbuf, wg16, wu16, wd16, sem, *, rows_per_step):
    g = pl.program_id(0)
    k = pl.program_id(1)
    n_used = nu_ref[0]
    n_sub_max, ts, _ = h_scr.shape
    group_rows = n_sub_max * ts
    nk = pl.num_programs(1)
    n_gather = gbuf.shape[0]

    def row_copy(src_row, dst_row):
        return pltpu.make_async_copy(h_hbm.at[pl.ds(src_row, 1)],
                                     gbuf.at[pl.ds(dst_row, 1)], sem)

    def for_each_sub_tile(group, fn):
        for j in range(n_sub_max):
            @pl.when(j * ts < nr_ref[group])
            def _(j=j):
                fn(j)

    def wait_rows():
        pltpu.make_async_copy(h_hbm.at[pl.ds(0, n_gather)], gbuf, sem).wait()

    @pl.when(k == 0)
    def _():
        o_ref[...] = jnp.zeros(o_ref.shape, o_ref.dtype)

    @pl.when(g < n_used)
    def _():
        @pl.when(k == 0)
        def _():
            @pl.when(g == 0)
            def _():
                def body(i, carry):
                    row_copy(tok_ref[i], i).start()
                    return carry
                lax.fori_loop(0, n_gather, body, 0, unroll=8)

            wait_rows()

            def cast(j):
                h_scr[j] = gbuf[j * ts:(j + 1) * ts, :].astype(BF16)
            for_each_sub_tile(g, cast)

        next_base = jnp.minimum(g + 1, n_used - 1) * group_rows + k * rows_per_step

        def sub_tile(j, m):
            if j == 0:
                wg16[...] = wg_ref[...].astype(BF16)
                wu16[...] = wu_ref[...].astype(BF16)
                wd16[...] = wd_ref[...].astype(BF16)
                for i in range(rows_per_step):
                    row_copy(tok_ref[next_base + i], k * rows_per_step + i).start()
            h = h_scr[j, 0:m, :]
            gate = _dot(h, wg16[...])
            act = (gate * jax.nn.sigmoid(gate)) * _dot(h, wu16[...])
            o_ref[j * ts:j * ts + m, :] += _dot(act.astype(BF16), wd16[...])

        half = ts // 2
        all_full = nr_ref[g] > group_rows - half

        @pl.when(all_full)
        def _():
            for j in range(n_sub_max):
                sub_tile(j, ts)

        @pl.when(jnp.logical_not(all_full))
        def _():
            for j in range(n_sub_max):
                rows_j = nr_ref[g] - j * ts

                @pl.when(rows_j > half)
                def _(j=j):
                    sub_tile(j, ts)

                @pl.when((rows_j > 0) & (rows_j <= half))
                def _(j=j):
                    sub_tile(j, half)

        @pl.when((g == n_used - 1) & (k == nk - 1))
        def _():
            wait_rows()


def moe_ffn(group_expert, group_rows_used, n_used, row_tok, h, w_gate, w_up, w_down, layer, *,
            sub_rows, n_sub_max):
    d = h.shape[1]
    d_ff = w_gate.shape[3]
    tk = _tile(d_ff, 256)
    nk = d_ff // tk
    group_rows = sub_rows * n_sub_max
    rows_per_step = -(-group_rows // (nk * SUBLANES)) * SUBLANES
    n_gather = rows_per_step * nk
    n_rows = group_expert.shape[0] * group_rows
    assert row_tok.shape[0] - n_rows >= n_gather - group_rows

    def ff_idx(g, k, nu):
        return jnp.where(g < nu[0], k, nk - 1)

    return pl.pallas_call(
        functools.partial(_moe_ffn_kernel, rows_per_step=rows_per_step),
        grid_spec=pltpu.PrefetchScalarGridSpec(
            num_scalar_prefetch=4,
            grid=(n_rows // group_rows, nk),
            in_specs=[
                pl.BlockSpec(memory_space=pl.ANY),
                pl.BlockSpec((None, None, d, tk),
                             lambda g, k, ge, ns, nu, tok: (layer, ge[g], 0, ff_idx(g, k, nu))),
                pl.BlockSpec((None, None, d, tk),
                             lambda g, k, ge, ns, nu, tok: (layer, ge[g], 0, ff_idx(g, k, nu))),
                pl.BlockSpec((None, None, tk, d),
                             lambda g, k, ge, ns, nu, tok: (layer, ge[g], ff_idx(g, k, nu), 0)),
            ],
            out_specs=pl.BlockSpec((group_rows, d), lambda g, k, ge, ns, nu, tok: (g, 0)),
            scratch_shapes=[pltpu.VMEM((n_sub_max, sub_rows, d), BF16),
                            pltpu.VMEM((n_gather, d), F32),
                            pltpu.VMEM((d, tk), BF16), pltpu.VMEM((d, tk), BF16),
                            pltpu.VMEM((tk, d), BF16),
                            pltpu.SemaphoreType.DMA(())],
        ),
        out_shape=jax.ShapeDtypeStruct((n_rows, d), F32),
        compiler_params=_params(("arbitrary", "arbitrary"), 56),
        name="moe_ffn",
    )(group_expert, group_rows_used, n_used, row_tok, h, w_gate, w_up, w_down)


def _moe_combine_kernel(d0_ref, d1_ref, ys_ref, rw_ref, x_ref, gt_ref, gpost_ref, o_ref,
                        buf, sems):
    i = pl.program_id(0)
    tm = x_ref.shape[0]
    slot = i % 2

    def start_tile(tile, s):
        def body(t, carry):
            for which, dest in enumerate((d0_ref, d1_ref)):
                pltpu.make_async_copy(ys_ref.at[pl.ds(dest[tile * tm + t], 1)],
                                      buf.at[s, pl.ds(which * tm + t, 1)],
                                      sems.at[s]).start()
            return carry
        lax.fori_loop(0, tm, body, 0, unroll=8)

    @pl.when(i == 0)
    def _():
        start_tile(0, 0)

    @pl.when(i + 1 < pl.num_programs(0))
    def _():
        start_tile(i + 1, 1 - slot)

    pltpu.make_async_copy(ys_ref.at[pl.ds(0, 2 * tm)], buf.at[slot], sems.at[slot]).wait()
    scale = gt_ref[...] * gpost_ref[...]
    for r0 in range(0, tm, EPILOGUE_ROWS):
        rows = slice(r0, r0 + EPILOGUE_ROWS)
        y = (buf[slot, r0:r0 + EPILOGUE_ROWS, :] * rw_ref[rows, 0:1]
             + buf[slot, tm + r0:tm + r0 + EPILOGUE_ROWS, :] * rw_ref[rows, 1:2])
        o_ref[rows, :] = x_ref[rows, :] + _rms(y) * scale


def moe_combine(dest0, dest1, ys, route_w, x, mod5, g_post, layer, *, seq_len):
    t, d = x.shape
    tm = _tile(seq_len, 256)
    tiles_per_seq = seq_len // tm
    return pl.pallas_call(
        _moe_combine_kernel,
        grid_spec=pltpu.PrefetchScalarGridSpec(
            num_scalar_prefetch=2,
            grid=(t // tm,),
            in_specs=[
                pl.BlockSpec(memory_space=pl.ANY),
                pl.BlockSpec((tm, LANES), lambda i, d0, d1: (i, 0)),
                pl.BlockSpec((tm, d), lambda i, d0, d1: (i, 0)),
                _mod_spec(layer, GT_FFN, d, lambda i: i // tiles_per_seq),
                _layer_spec(layer, (1, d)),
            ],
            out_specs=pl.BlockSpec((tm, d), lambda i, d0, d1: (i, 0)),
            scratch_shapes=[pltpu.VMEM((2, 2 * tm, d), F32), pltpu.SemaphoreType.DMA((2,))],
        ),
        out_shape=jax.ShapeDtypeStruct((t, d), F32),
        compiler_params=_params(("arbitrary",), 48),
        name="moe_combine",
    )(dest0, dest1, ys, route_w, x, mod5, g_post)


MOE_SUB_ROWS = 512
MOE_GROUP_SUBS = 2


def _row_tok_kernel(d0_ref, d1_ref, o_ref):
    def clear(r, carry):
        o_ref[r] = 0
        return carry
    lax.fori_loop(0, o_ref.shape[0], clear, 0, unroll=8)

    def place(t, carry):
        o_ref[d0_ref[t]] = t
        o_ref[d1_ref[t]] = t
        return carry
    lax.fori_loop(0, d0_ref.shape[0], place, 0, unroll=8)


def row_tokens(dest0, dest1, n_rows):
    smem = pl.BlockSpec(memory_space=pltpu.SMEM)
    return pl.pallas_call(
        _row_tok_kernel,
        in_specs=[smem, smem],
        out_specs=smem,
        out_shape=jax.ShapeDtypeStruct((n_rows,), jnp.int32),
        name="row_tokens",
    )(dest0, dest1)


def moe_layer(h, route_i, route_w, counts, x, mod5, g_post, w_gate, w_up, w_down, layer,
              moe_index, *, seq_len):
    t, d = h.shape
    n_experts = w_gate.shape[1]
    ts = min(MOE_SUB_ROWS, t)
    group_rows = ts * MOE_GROUP_SUBS
    e0, e1, pos0, pos1 = (route_i[:, k] for k in range(4))
    counts = counts[0, :n_experts]
    groups_e = (counts + group_rows - 1) // group_rows
    g_end = jnp.cumsum(groups_e)
    g_start = g_end - groups_e
    dest0 = g_start[e0] * group_rows + pos0
    dest1 = g_start[e1] * group_rows + pos1
    n_groups = -(-(2 * t) // group_rows) + n_experts
    n_rows = n_groups * group_rows
    row_tok = row_tokens(dest0, dest1, n_rows + group_rows)
    n_used = g_end[-1].astype(jnp.int32)
    gid = jnp.arange(n_groups, dtype=jnp.int32)
    group_e = jnp.clip(jnp.searchsorted(g_end, gid, side='right'), 0, n_experts - 1)
    group_e = jnp.where(gid < n_used, group_e, group_e[n_used - 1]).astype(jnp.int32)
    group_rows_used = jnp.where(
        gid < n_used,
        jnp.clip(counts[group_e] - (gid - g_start[group_e]) * group_rows, 0, group_rows),
        0).astype(jnp.int32)
    ys = moe_ffn(group_e, group_rows_used, n_used.reshape(1), row_tok, h, w_gate, w_up, w_down,
                 moe_index, sub_rows=ts, n_sub_max=MOE_GROUP_SUBS)
    return moe_combine(dest0, dest1, ys, route_w, x, mod5, g_post, layer, seq_len=seq_len)


def kernel(x, c, w_ada, b_ada, g_mix_pre, g_mix_post, g_ffn_pre, g_ffn_post, w_in, conv_w,
           ssm_a_re, ssm_a_im, ssm_log_dt, ssm_b_re, ssm_b_im, ssm_c_re, ssm_c_im, ssm_d,
           w_glu, b_glu, w_out, ffn_w_gate, ffn_w_up, ffn_w_down, router_w, router_b,
           moe_w_gate, moe_w_up, moe_w_down):
    nb, seq_len, d = x.shape
    depth = w_ada.shape[0]
    d_conv = conv_w.shape[1]
    g, p, h_dim = ssm_b_re.shape[1:]
    d_ssm = g * h_dim
    n_experts = router_w.shape[2]
    t = nb * seq_len
    groups_per_slab = max(1, min(g, MXU_DIM // h_dim))

    mod_rows = 2 * SUBLANES
    c_pad = jnp.zeros((mod_rows, d), F32).at[:nb].set(c)
    mod = adaln_mod(c_pad, w_ada, b_ada)
    mod5 = mod.reshape(depth, mod_rows, 6, 1, d)
    xf = x.reshape(t, d)

    def rows(v):
        return v.reshape(v.shape[0], 1, v.shape[1])
    g_mix_pre, g_mix_post, g_ffn_pre, g_ffn_post = (
        rows(v) for v in (g_mix_pre, g_mix_post, g_ffn_pre, g_ffn_post))
    w_in16 = w_in.astype(BF16)
    w_out16 = w_out.astype(BF16)
    w_glu16 = w_glu.astype(BF16)
    conv_wt = jnp.swapaxes(conv_w, 1, 2)
    bb_re, bb_im, ab_re, ab_im = ssm_prep(ssm_a_re, ssm_a_im, ssm_log_dt, ssm_b_re, ssm_b_im)
    def slabs(w):
        return _block_diag_slabs(jnp.swapaxes(w, 2, 3).astype(BF16), groups_per_slab)
    bbd = jnp.concatenate([slabs(bb_re), slabs(bb_im)], axis=-1)
    c_cat = jnp.concatenate([slabs(ssm_c_re), slabs(ssm_c_im)], axis=-2)
    a_re = _pack_state_rows(ab_re.reshape(depth, g * p), nb)
    a_im = _pack_state_rows(ab_im.reshape(depth, g * p), nb)
    d_skip = ssm_d.reshape(depth, 1, d_ssm)
    b_glu = rows(b_glu)
    n_moe = router_w.shape[0]
    rw_pad = jnp.zeros((n_moe, d, LANES), F32).at[:, :, :n_experts].set(router_w)
    rb_pad = jnp.zeros((n_moe, 1, LANES), F32).at[:, 0, :n_experts].set(router_b)

    for l in range(depth):
        y_conv, u = mixer_in(xf, mod5, g_mix_pre, w_in16, conv_wt, l, seq_len=seq_len)
        y_ssm = ssm(u.reshape(nb, seq_len, d_ssm), bbd, c_cat, a_re, a_im, d_skip,
                    w_glu16, b_glu, l).reshape(t, d_ssm)
        routed = l % 2 == 1
        i = l // 2
        rt = (rw_pad[i], rb_pad[i], n_experts) if routed else None
        outs = mixer_out(y_conv, y_ssm, w_out16, xf, mod5, g_mix_post, g_ffn_pre, l, rt,
                         seq_len=seq_len)
        if routed:
            xf, h, route_i, route_w, counts = outs
            xf = moe_layer(h, route_i, route_w, counts, xf, mod5, g_ffn_post,
                           moe_w_gate, moe_w_up, moe_w_down, l, i, seq_len=seq_len)
        else:
            xf, h = outs
            xf = ffn_dense(h, ffn_w_gate, ffn_w_up, ffn_w_down, xf, mod5, g_ffn_post, l, i,
                           seq_len=seq_len)
    return xf.reshape(nb, seq_len, d)
```

```python
import functools

import jax
import jax.numpy as jnp
from jax import lax
from jax.experimental import pallas as pl
from jax.experimental.pallas import tpu as pltpu

F32 = jnp.float32
BF16 = jnp.bfloat16
NORM_EPS = 1e-6
LANES = 128
SUBLANES = 8
MXU_DIM = 256
EPILOGUE_ROWS = 64
FFN_EPILOGUE_ROWS = 16
MIB = 1024 * 1024


def _params(semantics, vmem_mib):
    return pltpu.CompilerParams(dimension_semantics=semantics,
                                vmem_limit_bytes=vmem_mib * MIB)


def _dot(a, b):
    return jnp.dot(a, b, preferred_element_type=F32)


def _rms(x):
    return x * lax.rsqrt(jnp.mean(x * x, axis=-1, keepdims=True) + NORM_EPS)


def _tile(n, want):
    t = min(n, want)
    while n % t:
        t -= 1
    return t


def _adaln_kernel(c_ref, w_ref, b_ref, o_ref):
    c = c_ref[...]
    c_act = c * jax.nn.sigmoid(c)
    c_hi = c_act.astype(BF16)
    c_lo = (c_act - c_hi.astype(F32)).astype(BF16)
    w16 = w_ref[...].astype(BF16)
    o_ref[...] = (_dot(c_hi, w16) + _dot(c_lo, w16)) + b_ref[...]


def adaln_mod(c_pad, w_ada, b_ada):
    depth, d, n = w_ada.shape
    rows = c_pad.shape[0]
    tn = _tile(n, 1024)
    return pl.pallas_call(
        _adaln_kernel,
        grid=(depth, n // tn),
        in_specs=[
            pl.BlockSpec((rows, d), lambda l, j: (0, 0)),
            pl.BlockSpec((None, d, tn), lambda l, j: (l, 0, j)),
            pl.BlockSpec((None, 1, tn), lambda l, j: (l, 0, j)),
        ],
        out_specs=pl.BlockSpec((None, rows, tn), lambda l, j: (l, 0, j)),
        out_shape=jax.ShapeDtypeStruct((depth, rows, n), F32),
        compiler_params=_params(("arbitrary", "arbitrary"), 56),
        name="adaln_mod",
    )(c_pad, w_ada, b_ada.reshape(depth, 1, n))


def _mixer_in_kernel(x_ref, sc_ref, sh_ref, g_ref, w_ref, cw_ref, yconv_ref, u_ref,
                     halo_scr, h_scr, *, tiles_per_seq, col_chunk):
    tm = x_ref.shape[0]
    d_conv = yconv_ref.shape[1]
    d_ssm = u_ref.shape[1]
    pre_scale = g_ref[...] * (1.0 + sc_ref[...])
    for r0 in range(0, tm, EPILOGUE_ROWS):
        rows = slice(r0, r0 + EPILOGUE_ROWS)
        h_scr[rows, :] = (_rms(x_ref[rows, :]) * pre_scale + sh_ref[...]).astype(BF16)
    h = h_scr[...]

    @pl.when(pl.program_id(0) % tiles_per_seq == 0)
    def _():
        halo_scr[...] = jnp.zeros(halo_scr.shape, F32)

    for c0 in range(0, d_conv, col_chunk):
        cols = slice(c0, c0 + col_chunk)
        gate_b = _dot(h, w_ref[:, c0:c0 + col_chunk])
        z = (_dot(h, w_ref[:, d_conv + c0:d_conv + c0 + col_chunk])
             * _dot(h, w_ref[:, 2 * d_conv + c0:2 * d_conv + c0 + col_chunk]))
        prev = halo_scr[:, cols]
        halo_scr[:, cols] = z[tm - SUBLANES:, :]
        w0 = cw_ref[0:1, cols]
        w1 = cw_ref[1:2, cols]
        w2 = cw_ref[2:3, cols]
        z1 = pltpu.roll(z, 1, 0)
        z2 = pltpu.roll(z, 2, 0)
        yconv_ref[:, cols] = (gate_b * (z * w2 + z1 * w1 + z2 * w0)).astype(yconv_ref.dtype)
        row = lax.broadcasted_iota(jnp.int32, (SUBLANES, col_chunk), 0)
        z1t = jnp.where(row < 1, pltpu.roll(prev, 1, 0), z1[:SUBLANES, :])
        z2t = jnp.where(row < 2, pltpu.roll(prev, 2, 0), z2[:SUBLANES, :])
        top = gate_b[:SUBLANES, :] * (z[:SUBLANES, :] * w2 + z1t * w1 + z2t * w0)
        yconv_ref[0:SUBLANES, cols] = top.astype(yconv_ref.dtype)

    for c0 in range(0, d_ssm, col_chunk):
        u_ref[:, c0:c0 + col_chunk] = _dot(
            h, w_ref[:, 3 * d_conv + c0:3 * d_conv + c0 + col_chunk])


SH_MIX, SC_MIX, GT_MIX, SH_FFN, SC_FFN, GT_FFN = range(6)


def _mod_spec(layer, chunk, d, batch_of):
    return pl.BlockSpec((None, None, None, 1, d),
                        lambda i, *_: (layer, batch_of(i), chunk, 0, 0))


def _layer_spec(layer, shape, **kw):
    zeros = (0,) * len(shape)
    return pl.BlockSpec((None,) + tuple(shape), lambda *_: (layer,) + zeros, **kw)


def mixer_in(x, mod5, g_pre, w_in16, conv_wt, layer, *, seq_len):
    t, d = x.shape
    d_in = w_in16.shape[2]
    d_conv = conv_wt.shape[2]
    d_ssm = d_in - 3 * d_conv
    tm = _tile(seq_len, 512)
    col_chunk = _tile(min(d_conv, d_ssm), 512)
    assert d_conv % col_chunk == 0 and d_ssm % col_chunk == 0
    tiles_per_seq = seq_len // tm

    def batch_of(i):
        return i // tiles_per_seq

    kern = functools.partial(_mixer_in_kernel, tiles_per_seq=tiles_per_seq,
                             col_chunk=col_chunk)
    return pl.pallas_call(
        kern,
        grid=(t // tm,),
        in_specs=[
            pl.BlockSpec((tm, d), lambda i: (i, 0)),
            _mod_spec(layer, SC_MIX, d, batch_of), _mod_spec(layer, SH_MIX, d, batch_of),
            _layer_spec(layer, (1, d)),
            _layer_spec(layer, (d, d_in), pipeline_mode=pl.Buffered(1)),
            _layer_spec(layer, (3, d_conv)),
        ],
        out_specs=[
            pl.BlockSpec((tm, d_conv), lambda i: (i, 0)),
            pl.BlockSpec((tm, d_ssm), lambda i: (i, 0)),
        ],
        out_shape=[jax.ShapeDtypeStruct((t, d_conv), BF16),
                   jax.ShapeDtypeStruct((t, d_ssm), F32)],
        scratch_shapes=[pltpu.VMEM((SUBLANES, d_conv), F32), pltpu.VMEM((tm, d), BF16)],
        compiler_params=_params(("arbitrary",), 48),
        name="mixer_in",
    )(x, mod5, mod5, g_pre, w_in16, conv_wt)


def _ssm_prep_kernel(are_ref, aim_ref, ldt_ref, bre_ref, bim_ref,
                     bbre_ref, bbim_ref, abre_ref, abim_ref):
    lam_re = are_ref[...]
    lam_im = aim_ref[...]
    dt = jnp.exp(ldt_ref[...])
    mag = jnp.exp(lam_re * dt)
    ang = lam_im * dt
    ab_re = mag * jnp.cos(ang)
    ab_im = mag * jnp.sin(ang)
    den = lam_re * lam_re + lam_im * lam_im
    nr = ab_re - 1.0
    q_re = (nr * lam_re + ab_im * lam_im) / den
    q_im = (ab_im * lam_re - nr * lam_im) / den
    b_re = bre_ref[...]
    b_im = bim_ref[...]
    bbre_ref[...] = q_re * b_re - q_im * b_im
    bbim_ref[...] = q_re * b_im + q_im * b_re
    abre_ref[...] = ab_re
    abim_ref[...] = ab_im


def ssm_prep(a_re, a_im, log_dt, b_re, b_im):
    depth, g, p, h = b_re.shape
    ldt = jnp.broadcast_to(log_dt.reshape(depth, g, 1, 1), (depth, g, p, 1))

    def spec(last):
        return pl.BlockSpec((None, g, p, last), lambda l: (l, 0, 0, 0))

    return pl.pallas_call(
        _ssm_prep_kernel,
        grid=(depth,),
        in_specs=[spec(1), spec(1), spec(1), spec(h), spec(h)],
        out_specs=[spec(h), spec(h), spec(1), spec(1)],
        out_shape=[jax.ShapeDtypeStruct((depth, g, p, h), F32),
                   jax.ShapeDtypeStruct((depth, g, p, h), F32),
                   jax.ShapeDtypeStruct((depth, g, p, 1), F32),
                   jax.ShapeDtypeStruct((depth, g, p, 1), F32)],
        compiler_params=_params(("arbitrary",), 56),
        name="ssm_prep",
    )(a_re.reshape(depth, g, p, 1), a_im.reshape(depth, g, p, 1), ldt, b_re, b_im)


def _group_rows(w, groups_per_slab):
    depth, g, r, c = w.shape
    return w.reshape(depth, g // groups_per_slab, groups_per_slab * r, c)


def _block_diag(stacked, block_rows):
    n_rows, c = stacked.shape
    n_cols = (n_rows // block_rows) * c
    src = lax.broadcasted_iota(jnp.int32, (c, n_cols), 0)
    dst = lax.broadcasted_iota(jnp.int32, (c, n_cols), 1)
    repeat = jnp.where(dst % c == src, 1.0, 0.0).astype(BF16)
    tiled = _dot(stacked.astype(BF16), repeat)
    row = lax.broadcasted_iota(jnp.int32, (n_rows, n_cols), 0)
    col = lax.broadcasted_iota(jnp.int32, (n_rows, n_cols), 1)
    return jnp.where(row // block_rows == col // c, tiled, 0.0).astype(BF16)


def _gelu_tanh(x):
    return 0.5 * x * (1.0 + jnp.tanh(0.7978845608028654 * (x + 0.044715 * x * x * x)))


def _ssm_kernel(u_ref, bre_ref, bim_ref, cre_ref, cim_ref, are_ref, aim_ref, d_ref, wglu_ref,
                bglu_ref, o_ref, sre_scr, sim_scr, stre_scr, stim_scr, u_scr, y_scr,
                bbd_ref, c_ref, *, pitch, groups_per_step):
    nb, tc, d_ssm = u_ref.shape
    n_slabs, k_slab, two_sw = bbd_ref.shape
    sw = two_sw // 2
    tiles_per_slab = sw // LANES
    pack = SUBLANES // nb
    n_groups = sre_scr.shape[0]
    rows_all = nb * pitch

    def tile_rows(lane_tile):
        q, h = divmod(lane_tile, pack)
        return q, slice(h * rows_all, (h + 1) * rows_all)

    @pl.when(pl.program_id(0) == 0)
    def _():
        stre_scr[...] = jnp.zeros(stre_scr.shape, F32)
        stim_scr[...] = jnp.zeros(stim_scr.shape, F32)
        u_scr[...] = jnp.zeros(u_scr.shape, F32)
        p_dim = bre_ref.shape[2]
        h_dim = k_slab // (sw // p_dim)
        for s in range(n_slabs):
            bbd_ref[s, :, 0:sw] = _block_diag(bre_ref[s], h_dim)
            bbd_ref[s, :, sw:2 * sw] = _block_diag(bim_ref[s], h_dim)
            c_ref[s, 0:sw, :] = _block_diag(cre_ref[s], p_dim)
            c_ref[s, sw:2 * sw, :] = _block_diag(cim_ref[s], p_dim)

    for b in range(nb):
        u_scr[b * pitch:b * pitch + tc, :] = u_ref[b]

    for s in range(n_slabs):
        r = _dot(u_scr[:, s * k_slab:(s + 1) * k_slab].astype(BF16), bbd_ref[s])
        for j in range(tiles_per_slab):
            qr = tile_rows(s * tiles_per_slab + j)
            sre_scr[qr] = r[:, j * LANES:(j + 1) * LANES]
            sim_scr[qr] = r[:, sw + j * LANES:sw + (j + 1) * LANES]

    for q0 in range(0, n_groups, groups_per_step):
        qs = list(range(q0, min(q0 + groups_per_step, n_groups)))
        a_re = [are_ref[q] for q in qs]
        a_im = [aim_ref[q] for q in qs]

        def step(t, carry, qs=qs, a_re=a_re, a_im=a_im):
            rows = pl.ds(t, SUBLANES, stride=pitch)
            out = []
            for n, q in enumerate(qs):
                s_re, s_im = carry[2 * n], carry[2 * n + 1]
                n_re = a_re[n] * s_re - a_im[n] * s_im + sre_scr[q, rows, :]
                n_im = a_re[n] * s_im + a_im[n] * s_re + sim_scr[q, rows, :]
                sre_scr[q, rows, :] = n_re
                sim_scr[q, rows, :] = n_im
                out += [n_re, n_im]
            return tuple(out)

        init = []
        for q in qs:
            init += [stre_scr[q], stim_scr[q]]
        fin = lax.fori_loop(0, tc, step, tuple(init), unroll=2)
        for n, q in enumerate(qs):
            stre_scr[q] = fin[2 * n]
            stim_scr[q] = fin[2 * n + 1]

    def state_rows(s):
        tiles = [tile_rows(s * tiles_per_slab + j) for j in range(tiles_per_slab)]
        return jnp.concatenate([sre_scr[qr].astype(BF16) for qr in tiles]
                               + [(-sim_scr[qr]).astype(BF16) for qr in tiles], axis=-1)

    for s in range(n_slabs):
        cols = slice(s * k_slab, (s + 1) * k_slab)
        y = _dot(state_rows(s), c_ref[s])
        y_scr[:, cols] = y + d_ref[:, cols] * u_scr[:, cols]
    y = _gelu_tanh(y_scr[...])
    gate = _dot(y.astype(BF16), wglu_ref[...]) + bglu_ref[...]
    y_scr[...] = y * jax.nn.sigmoid(gate)
    for b in range(nb):
        o_ref[b] = y_scr[b * pitch:b * pitch + tc, :].astype(o_ref.dtype)


def _pack_state_rows(a, nb):
    depth = a.shape[0]
    pack = SUBLANES // nb
    n_groups = a.shape[1] // (LANES * pack)
    a = a.reshape(depth, n_groups, pack, 1, LANES)
    return jnp.broadcast_to(a, (depth, n_groups, pack, nb, LANES)).reshape(
        depth, n_groups, SUBLANES, LANES)


def ssm(u3, bb_re, bb_im, c_re, c_im, a_re, a_im, d_skip, w_glu16, b_glu, layer):
    nb, seq_len, d_ssm = u3.shape
    n_groups = a_re.shape[1]
    n_slabs, k_slab, p_dim = bb_re.shape[1:]
    sw = c_re.shape[2]
    assert SUBLANES % nb == 0
    tc = _tile(seq_len, 128)
    pitch = tc + SUBLANES // 2
    assert (nb * pitch) % SUBLANES == 0
    kern = functools.partial(_ssm_kernel, pitch=pitch, groups_per_step=min(8, n_groups))

    def full(a):
        return _layer_spec(layer, a.shape[1:])

    return pl.pallas_call(
        kern,
        grid=(seq_len // tc,),
        in_specs=[pl.BlockSpec((nb, tc, d_ssm), lambda c: (0, c, 0)),
                  full(bb_re), full(bb_im), full(c_re), full(c_im), full(a_re), full(a_im),
                  full(d_skip), full(w_glu16), full(b_glu)],
        out_specs=pl.BlockSpec((nb, tc, d_ssm), lambda c: (0, c, 0)),
        out_shape=jax.ShapeDtypeStruct((nb, seq_len, d_ssm), BF16),
        scratch_shapes=[pltpu.VMEM((n_groups, SUBLANES * pitch, LANES), F32),
                        pltpu.VMEM((n_groups, SUBLANES * pitch, LANES), F32),
                        pltpu.VMEM((n_groups, SUBLANES, LANES), F32),
                        pltpu.VMEM((n_groups, SUBLANES, LANES), F32),
                        pltpu.VMEM((nb * pitch, d_ssm), F32),
                        pltpu.VMEM((nb * pitch, d_ssm), F32),
                        pltpu.VMEM((n_slabs, k_slab, 2 * sw), BF16),
                        pltpu.VMEM((n_slabs, 2 * sw, k_slab), BF16)],
        compiler_params=_params(("arbitrary",), 56),
        name="ssm_scan",
    )(u3, bb_re, bb_im, c_re, c_im, a_re, a_im, d_skip, w_glu16, b_glu)


def _mixer_out_kernel(*refs, n_experts):
    routed = n_experts is not None
    if routed:
        (yc_ref, ys_ref, wo_ref, x_ref, gt_ref, gpost_ref, gpre_ref, sc_ref,
         sh_ref, rw_ref, rb_ref, xo_ref, h_ref, route_i_ref, route_w_ref, cnt_ref,
         y_scr, carry_scr) = refs
    else:
        (yc_ref, ys_ref, wo_ref, x_ref, gt_ref, gpost_ref, gpre_ref, sc_ref,
         sh_ref, xo_ref, h_ref, y_scr) = refs
    tm = yc_ref.shape[0]
    y_scr[...] = _dot(jnp.concatenate([yc_ref[...], ys_ref[...]], axis=-1), wo_ref[...])
    post_scale = gt_ref[...] * gpost_ref[...]
    pre_scale = gpre_ref[...] * (1.0 + sc_ref[...])
    for r0 in range(0, tm, EPILOGUE_ROWS):
        rows = slice(r0, r0 + EPILOGUE_ROWS)
        x_new = x_ref[rows, :] + _rms(y_scr[rows, :]) * post_scale
        xo_ref[rows, :] = x_new
        h_ref[rows, :] = (_rms(x_new) * pre_scale + sh_ref[...]).astype(h_ref.dtype)
    if routed:
        h = h_ref[...]
        w = rw_ref[...]
        h_hi = h.astype(BF16)
        w_hi = w.astype(BF16)
        h_lo = (h - h_hi.astype(F32)).astype(BF16)
        w_lo = (w - w_hi.astype(F32)).astype(BF16)
        logits = _dot(h_hi, w_hi) + (_dot(h_hi, w_lo) + _dot(h_lo, w_hi)) + rb_ref[...]
        _route(logits, route_i_ref, route_w_ref, cnt_ref, carry_scr, n_experts)


def mixer_out(y_conv, y_ssm, w_out16, x, mod5, g_post, g_pre, layer, router=None, *,
              seq_len):
    t, d = x.shape
    d_conv = y_conv.shape[1]
    d_ssm = y_ssm.shape[1]
    tm = _tile(seq_len, 512)
    tiles_per_seq = seq_len // tm
    routed = router is not None

    def batch_of(i):
        return i // tiles_per_seq

    def row_spec(n):
        return pl.BlockSpec((tm, n), lambda i: (i, 0))

    def const_spec(shape):
        return pl.BlockSpec(shape, lambda i: (0, 0))

    in_specs = [row_spec(d_conv), row_spec(d_ssm),
                _layer_spec(layer, (d_conv + d_ssm, d), pipeline_mode=pl.Buffered(1)),
                row_spec(d), _mod_spec(layer, GT_MIX, d, batch_of),
                _layer_spec(layer, (1, d)), _layer_spec(layer, (1, d)),
                _mod_spec(layer, SC_FFN, d, batch_of), _mod_spec(layer, SH_FFN, d, batch_of)]
    args = [y_conv, y_ssm, w_out16, x, mod5, g_post, g_pre, mod5, mod5]
    out_specs = [row_spec(d), row_spec(d)]
    out_shape = [jax.ShapeDtypeStruct((t, d), F32),
                 jax.ShapeDtypeStruct((t, d), F32 if routed else BF16)]
    scratch_shapes = [pltpu.VMEM((tm, d), F32)]
    n_experts = None
    if routed:
        rw_pad, rb_pad, n_experts = router
        in_specs += [const_spec(rw_pad.shape), const_spec(rb_pad.shape)]
        args += [rw_pad, rb_pad]
        out_specs += [row_spec(LANES), row_spec(LANES), const_spec((SUBLANES, LANES))]
        out_shape += [jax.ShapeDtypeStruct((t, LANES), jnp.int32),
                      jax.ShapeDtypeStruct((t, LANES), F32),
                      jax.ShapeDtypeStruct((SUBLANES, LANES), jnp.int32)]
        scratch_shapes.append(pltpu.VMEM((SUBLANES, LANES), F32))
    return pl.pallas_call(
        functools.partial(_mixer_out_kernel, n_experts=n_experts),
        grid=(t // tm,),
        in_specs=in_specs, out_specs=out_specs, out_shape=out_shape,
        scratch_shapes=scratch_shapes,
        compiler_params=_params(("arbitrary",), 56),
        name="mixer_out_routed" if routed else "mixer_out",
    )(*args)


def _ffn_kernel(h_ref, wg_ref, wu_ref, wd_ref, x_hbm, gt_ref, gpost_ref, o_ref,
                wg16, wu16, wd16, x_buf, sem, *, sub_rows):
    i = pl.program_id(0)
    k = pl.program_id(1)
    tm = o_ref.shape[0]
    x_copy = pltpu.make_async_copy(x_hbm.at[pl.ds(i * tm, tm)], x_buf, sem)

    @pl.when(k == 0)
    def _():
        o_ref[...] = jnp.zeros(o_ref.shape, F32)
        x_copy.start()

    wg16[...] = wg_ref[...].astype(BF16)
    wu16[...] = wu_ref[...].astype(BF16)
    wd16[...] = wd_ref[...].astype(BF16)
    for r0 in range(0, tm, sub_rows):
        h = h_ref[r0:r0 + sub_rows, :]
        gate = _dot(h, wg16[...])
        act = (gate * jax.nn.sigmoid(gate)) * _dot(h, wu16[...])
        o_ref[r0:r0 + sub_rows, :] += _dot(act.astype(BF16), wd16[...])

    @pl.when(k == pl.num_programs(1) - 1)
    def _():
        x_copy.wait()
        scale = gt_ref[...] * gpost_ref[...]
        for r0 in range(0, tm, FFN_EPILOGUE_ROWS):
            rows = slice(r0, r0 + FFN_EPILOGUE_ROWS)
            o_ref[rows, :] = x_buf[rows, :] + _rms(o_ref[rows, :]) * scale


def ffn_dense(h16, w_gate, w_up, w_down, x, mod5, g_post, layer, ffn_index, *, seq_len):
    t, d = x.shape
    d_ff = w_gate.shape[2]
    tm = _tile(seq_len, 1024)
    sub_rows = _tile(tm, 512)
    tk = _tile(d_ff, 256)
    tiles_per_seq = seq_len // tm
    return pl.pallas_call(
        functools.partial(_ffn_kernel, sub_rows=sub_rows),
        grid=(t // tm, d_ff // tk),
        in_specs=[
            pl.BlockSpec((tm, d), lambda i, k: (i, 0)),
            pl.BlockSpec((None, d, tk), lambda i, k: (ffn_index, 0, k)),
            pl.BlockSpec((None, d, tk), lambda i, k: (ffn_index, 0, k)),
            pl.BlockSpec((None, tk, d), lambda i, k: (ffn_index, k, 0)),
            pl.BlockSpec(memory_space=pl.ANY),
            _mod_spec(layer, GT_FFN, d, lambda i: i // tiles_per_seq),
            _layer_spec(layer, (1, d)),
        ],
        out_specs=pl.BlockSpec((tm, d), lambda i, k: (i, 0)),
        out_shape=jax.ShapeDtypeStruct((t, d), F32),
        scratch_shapes=[pltpu.VMEM((d, tk), BF16), pltpu.VMEM((d, tk), BF16),
                        pltpu.VMEM((tk, d), BF16), pltpu.VMEM((tm, d), F32),
                        pltpu.SemaphoreType.DMA(())],
        compiler_params=_params(("arbitrary", "arbitrary"), 56),
        name="ffn_dense",
    )(h16, w_gate, w_up, w_down, x, mod5, g_post)


def _route(logits, ri_ref, rw_ref, cnt_ref, carry_scr, n_experts):
    @pl.when(pl.program_id(0) == 0)
    def _():
        carry_scr[...] = jnp.zeros(carry_scr.shape, F32)

    tm = logits.shape[0]
    lane = lax.broadcasted_iota(jnp.int32, (tm, LANES), 1)
    neg = jnp.float32(-jnp.inf)
    logit = jnp.where(lane < n_experts, logits, neg)
    m1 = jnp.max(logit, axis=-1, keepdims=True)
    i1 = jnp.min(jnp.where(logit == m1, lane, LANES), axis=-1, keepdims=True)
    rest = jnp.where(lane == i1, neg, logit)
    m2 = jnp.max(rest, axis=-1, keepdims=True)
    i2 = jnp.min(jnp.where(rest == m2, lane, LANES), axis=-1, keepdims=True)
    e2 = jnp.exp(m2 - m1)
    w1 = 1.0 / (1.0 + e2)
    w2 = e2 / (1.0 + e2)
    hit1 = lane == i1
    hit2 = lane == i2
    onehot = jnp.where(hit1 | hit2, 1.0, 0.0)
    r = lax.broadcasted_iota(jnp.int32, (tm, tm), 0)
    c = lax.broadcasted_iota(jnp.int32, (tm, tm), 1)
    earlier = jnp.where(c < r, 1.0, 0.0).astype(BF16)
    carry = carry_scr[0:1, :]
    before = _dot(earlier, onehot.astype(BF16)) + carry
    pos1 = jnp.sum(jnp.where(hit1, before, 0.0), axis=-1, keepdims=True).astype(jnp.int32)
    pos2 = jnp.sum(jnp.where(hit2, before, 0.0), axis=-1, keepdims=True).astype(jnp.int32)
    total = carry + jnp.sum(onehot, axis=0, keepdims=True)
    carry_scr[...] = jnp.broadcast_to(total, carry_scr.shape)
    cnt_ref[...] = jnp.broadcast_to(total, cnt_ref.shape).astype(jnp.int32)
    zero_i = jnp.zeros((tm, LANES), jnp.int32)
    ri_ref[...] = jnp.where(lane == 0, i1, jnp.where(lane == 1, i2,
                            jnp.where(lane == 2, pos1, jnp.where(lane == 3, pos2, zero_i))))
    rw_ref[...] = jnp.where(lane == 0, w1, jnp.where(lane == 1, w2, 0.0))


def _moe_ffn_kernel(ge_ref, nr_ref, nu_ref, tok_ref, h_hbm, wg_ref, wu_ref, wd_ref, o_ref,
                    h_scr, gbuf, wg16, wu16, wd16, sem, *, rows_per_step):
    g = pl.program_id(0)
    k = pl.program_id(1)
    n_used = nu_ref[0]
    n_sub_max, ts, _ = h_scr.shape
    group_rows = n_sub_max * ts
    nk = pl.num_programs(1)
    n_gather = gbuf.shape[0]

    def row_copy(src_row, dst_row):
        return pltpu.make_async_copy(h_hbm.at[pl.ds(src_row, 1)],
                                     gbuf.at[pl.ds(dst_row, 1)], sem)

    def for_each_sub_tile(group, fn):
        for j in range(n_sub_max):
            @pl.when(j * ts < nr_ref[group])
            def _(j=j):
                fn(j)

    def wait_rows():
        pltpu.make_async_copy(h_hbm.at[pl.ds(0, n_gather)], gbuf, sem).wait()

    @pl.when(k == 0)
    def _():
        o_ref[...] = jnp.zeros(o_ref.shape, o_ref.dtype)

    @pl.when(g < n_used)
    def _():
        @pl.when(k == 0)
        def _():
            @pl.when(g == 0)
            def _():
                def body(i, carry):
                    row_copy(tok_ref[i], i).start()
                    return carry
                lax.fori_loop(0, n_gather, body, 0, unroll=8)

            wait_rows()

            def cast(j):
                h_scr[j] = gbuf[j * ts:(j + 1) * ts, :].astype(BF16)
            for_each_sub_tile(g, cast)

        next_base = jnp.minimum(g + 1, n_used - 1) * group_rows + k * rows_per_step

        def sub_tile(j, m):
            if j == 0:
                wg16[...] = wg_ref[...].astype(BF16)
                wu16[...] = wu_ref[...].astype(BF16)
                wd16[...] = wd_ref[...].astype(BF16)
                for i in range(rows_per_step):
                    row_copy(tok_ref[next_base + i], k * rows_per_step + i).start()
            h = h_scr[j, 0:m, :]
            gate = _dot(h, wg16[...])
            act = (gate * jax.nn.sigmoid(gate)) * _dot(h, wu16[...])
            o_ref[j * ts:j * ts + m, :] += _dot(act.astype(BF16), wd16[...])

        half = ts // 2
        all_full = nr_ref[g] > group_rows - half

        @pl.when(all_full)
        def _():
            for j in range(n_sub_max):
                sub_tile(j, ts)

        @pl.when(jnp.logical_not(all_full))
        def _():
            for j in range(n_sub_max):
                rows_j = nr_ref[g] - j * ts

                @pl.when(rows_j > half)
                def _(j=j):
                    sub_tile(j, ts)

                @pl.when((rows_j > 0) & (rows_j <= half))
                def _(j=j):
                    sub_tile(j, half)

        @pl.when((g == n_used - 1) & (k == nk - 1))
        def _():
            wait_rows()


def moe_ffn(group_expert, group_rows_used, n_used, row_tok, h, w_gate, w_up, w_down, layer, *,
            sub_rows, n_sub_max):
    d = h.shape[1]
    d_ff = w_gate.shape[3]
    tk = _tile(d_ff, 256)
    nk = d_ff // tk
    group_rows = sub_rows * n_sub_max
    rows_per_step = -(-group_rows // (nk * SUBLANES)) * SUBLANES
    n_gather = rows_per_step * nk
    n_rows = group_expert.shape[0] * group_rows
    assert row_tok.shape[0] - n_rows >= n_gather - group_rows

    def ff_idx(g, k, nu):
        return jnp.where(g < nu[0], k, nk - 1)

    return pl.pallas_call(
        functools.partial(_moe_ffn_kernel, rows_per_step=rows_per_step),
        grid_spec=pltpu.PrefetchScalarGridSpec(
            num_scalar_prefetch=4,
            grid=(n_rows // group_rows, nk),
            in_specs=[
                pl.BlockSpec(memory_space=pl.ANY),
                pl.BlockSpec((None, None, d, tk),
                             lambda g, k, ge, ns, nu, tok: (layer, ge[g], 0, ff_idx(g, k, nu))),
                pl.BlockSpec((None, None, d, tk),
                             lambda g, k, ge, ns, nu, tok: (layer, ge[g], 0, ff_idx(g, k, nu))),
                pl.BlockSpec((None, None, tk, d),
                             lambda g, k, ge, ns, nu, tok: (layer, ge[g], ff_idx(g, k, nu), 0)),
            ],
            out_specs=pl.BlockSpec((group_rows, d), lambda g, k, ge, ns, nu, tok: (g, 0)),
            scratch_shapes=[pltpu.VMEM((n_sub_max, sub_rows, d), BF16),
                            pltpu.VMEM((n_gather, d), F32),
                            pltpu.VMEM((d, tk), BF16), pltpu.VMEM((d, tk), BF16),
                            pltpu.VMEM((tk, d), BF16),
                            pltpu.SemaphoreType.DMA(())],
        ),
        out_shape=jax.ShapeDtypeStruct((n_rows, d), F32),
        compiler_params=_params(("arbitrary", "arbitrary"), 56),
        name="moe_ffn",
    )(group_expert, group_rows_used, n_used, row_tok, h, w_gate, w_up, w_down)


def _moe_combine_kernel(d0_ref, d1_ref, ys_ref, rw_ref, x_ref, gt_ref, gpost_ref, o_ref,
                        buf, sems):
    i = pl.program_id(0)
    tm = x_ref.shape[0]
    slot = i % 2

    def start_tile(tile, s):
        def body(t, carry):
            for which, dest in enumerate((d0_ref, d1_ref)):
                pltpu.make_async_copy(ys_ref.at[pl.ds(dest[tile * tm + t], 1)],
                                      buf.at[s, pl.ds(which * tm + t, 1)],
                                      sems.at[s]).start()
            return carry
        lax.fori_loop(0, tm, body, 0, unroll=8)

    @pl.when(i == 0)
    def _():
        start_tile(0, 0)

    @pl.when(i + 1 < pl.num_programs(0))
    def _():
        start_tile(i + 1, 1 - slot)

    pltpu.make_async_copy(ys_ref.at[pl.ds(0, 2 * tm)], buf.at[slot], sems.at[slot]).wait()
    scale = gt_ref[...] * gpost_ref[...]
    for r0 in range(0, tm, EPILOGUE_ROWS):
        rows = slice(r0, r0 + EPILOGUE_ROWS)
        y = (buf[slot, r0:r0 + EPILOGUE_ROWS, :] * rw_ref[rows, 0:1]
             + buf[slot, tm + r0:tm + r0 + EPILOGUE_ROWS, :] * rw_ref[rows, 1:2])
        o_ref[rows, :] = x_ref[rows, :] + _rms(y) * scale


def moe_combine(dest0, dest1, ys, route_w, x, mod5, g_post, layer, *, seq_len):
    t, d = x.shape
    tm = _tile(seq_len, 256)
    tiles_per_seq = seq_len // tm
    return pl.pallas_call(
        _moe_combine_kernel,
        grid_spec=pltpu.PrefetchScalarGridSpec(
            num_scalar_prefetch=2,
            grid=(t // tm,),
            in_specs=[
                pl.BlockSpec(memory_space=pl.ANY),
                pl.BlockSpec((tm, LANES), lambda i, d0, d1: (i, 0)),
                pl.BlockSpec((tm, d), lambda i, d0, d1: (i, 0)),
                _mod_spec(layer, GT_FFN, d, lambda i: i // tiles_per_seq),
                _layer_spec(layer, (1, d)),
            ],
            out_specs=pl.BlockSpec((tm, d), lambda i, d0, d1: (i, 0)),
            scratch_shapes=[pltpu.VMEM((2, 2 * tm, d), F32), pltpu.SemaphoreType.DMA((2,))],
        ),
        out_shape=jax.ShapeDtypeStruct((t, d), F32),
        compiler_params=_params(("arbitrary",), 48),
        name="moe_combine",
    )(dest0, dest1, ys, route_w, x, mod5, g_post)


MOE_SUB_ROWS = 512
MOE_GROUP_SUBS = 2


def _row_tok_kernel(d0_ref, d1_ref, o_ref):
    def clear(r, carry):
        o_ref[r] = 0
        return carry
    lax.fori_loop(0, o_ref.shape[0], clear, 0, unroll=8)

    def place(t, carry):
        o_ref[d0_ref[t]] = t
        o_ref[d1_ref[t]] = t
        return carry
    lax.fori_loop(0, d0_ref.shape[0], place, 0, unroll=8)


def row_tokens(dest0, dest1, n_rows):
    smem = pl.BlockSpec(memory_space=pltpu.SMEM)
    return pl.pallas_call(
        _row_tok_kernel,
        in_specs=[smem, smem],
        out_specs=smem,
        out_shape=jax.ShapeDtypeStruct((n_rows,), jnp.int32),
        name="row_tokens",
    )(dest0, dest1)


def moe_layer(h, route_i, route_w, counts, x, mod5, g_post, w_gate, w_up, w_down, layer,
              moe_index, *, seq_len):
    t, d = h.shape
    n_experts = w_gate.shape[1]
    ts = min(MOE_SUB_ROWS, t)
    group_rows = ts * MOE_GROUP_SUBS
    e0, e1, pos0, pos1 = (route_i[:, k] for k in range(4))
    counts = counts[0, :n_experts]
    groups_e = (counts + group_rows - 1) // group_rows
    g_end = jnp.cumsum(groups_e)
    g_start = g_end - groups_e
    dest0 = g_start[e0] * group_rows + pos0
    dest1 = g_start[e1] * group_rows + pos1
    n_groups = -(-(2 * t) // group_rows) + n_experts
    n_rows = n_groups * group_rows
    row_tok = row_tokens(dest0, dest1, n_rows + group_rows)
    n_used = g_end[-1].astype(jnp.int32)
    gid = jnp.arange(n_groups, dtype=jnp.int32)
    group_e = jnp.clip(jnp.searchsorted(g_end, gid, side='right'), 0, n_experts - 1)
    group_e = jnp.where(gid < n_used, group_e, group_e[n_used - 1]).astype(jnp.int32)
    group_rows_used = jnp.where(
        gid < n_used,
        jnp.clip(counts[group_e] - (gid - g_start[group_e]) * group_rows, 0, group_rows),
        0).astype(jnp.int32)
    ys = moe_ffn(group_e, group_rows_used, n_used.reshape(1), row_tok, h, w_gate, w_up, w_down,
                 moe_index, sub_rows=ts, n_sub_max=MOE_GROUP_SUBS)
    return moe_combine(dest0, dest1, ys, route_w, x, mod5, g_post, layer, seq_len=seq_len)


def kernel(x, c, w_ada, b_ada, g_mix_pre, g_mix_post, g_ffn_pre, g_ffn_post, w_in, conv_w,
           ssm_a_re, ssm_a_im, ssm_log_dt, ssm_b_re, ssm_b_im, ssm_c_re, ssm_c_im, ssm_d,
           w_glu, b_glu, w_out, ffn_w_gate, ffn_w_up, ffn_w_down, router_w, router_b,
           moe_w_gate, moe_w_up, moe_w_down):
    nb, seq_len, d = x.shape
    depth = w_ada.shape[0]
    d_conv = conv_w.shape[1]
    g, p, h_dim = ssm_b_re.shape[1:]
    d_ssm = g * h_dim
    n_experts = router_w.shape[2]
    t = nb * seq_len
    groups_per_slab = max(1, min(g, MXU_DIM // h_dim))

    mod_rows = 2 * SUBLANES
    c_pad = jnp.zeros((mod_rows, d), F32).at[:nb].set(c)
    mod = adaln_mod(c_pad, w_ada, b_ada)
    mod5 = mod.reshape(depth, mod_rows, 6, 1, d)
    xf = x.reshape(t, d)

    def rows(v):
        return v.reshape(v.shape[0], 1, v.shape[1])
    g_mix_pre, g_mix_post, g_ffn_pre, g_ffn_post = (
        rows(v) for v in (g_mix_pre, g_mix_post, g_ffn_pre, g_ffn_post))
    w_in16 = w_in.astype(BF16)
    w_out16 = w_out.astype(BF16)
    w_glu16 = w_glu.astype(BF16)
    conv_wt = jnp.swapaxes(conv_w, 1, 2)
    bb_re, bb_im, ab_re, ab_im = ssm_prep(ssm_a_re, ssm_a_im, ssm_log_dt, ssm_b_re, ssm_b_im)
    def stacked_t(w):
        return _group_rows(jnp.swapaxes(w, 2, 3), groups_per_slab)
    bbs_re, bbs_im = stacked_t(bb_re), stacked_t(bb_im)
    cs_re, cs_im = stacked_t(ssm_c_re), stacked_t(ssm_c_im)
    a_re = _pack_state_rows(ab_re.reshape(depth, g * p), nb)
    a_im = _pack_state_rows(ab_im.reshape(depth, g * p), nb)
    d_skip = ssm_d.reshape(depth, 1, d_ssm)
    b_glu = rows(b_glu)
    n_moe = router_w.shape[0]
    rw_pad = jnp.zeros((n_moe, d, LANES), F32).at[:, :, :n_experts].set(router_w)
    rb_pad = jnp.zeros((n_moe, 1, LANES), F32).at[:, 0, :n_experts].set(router_b)

    for l in range(depth):
        y_conv, u = mixer_in(xf, mod5, g_mix_pre, w_in16, conv_wt, l, seq_len=seq_len)
        y_ssm = ssm(u.reshape(nb, seq_len, d_ssm), bbs_re, bbs_im, cs_re, cs_im, a_re, a_im, d_skip,
                    w_glu16, b_glu, l).reshape(t, d_ssm)
        routed = l % 2 == 1
        i = l // 2
        rt = (rw_pad[i], rb_pad[i], n_experts) if routed else None
        outs = mixer_out(y_conv, y_ssm, w_out16, xf, mod5, g_mix_post, g_ffn_pre, l, rt,
                         seq_len=seq_len)
        if routed:
            xf, h, route_i, route_w, counts = outs
            xf = moe_layer(h, route_i, route_w, counts, xf, mod5, g_ffn_post,
                           moe_w_gate, moe_w_up, moe_w_down, l, i, seq_len=seq_len)
        else:
            xf, h = outs
            xf = ffn_dense(h, ffn_w_gate, ffn_w_up, ffn_w_down, xf, mod5, g_ffn_post, l, i,
                           seq_len=seq_len)
    return xf.reshape(nb, seq_len, d)
```

```python
import functools

import jax
import jax.numpy as jnp
from jax import lax
from jax.experimental import pallas as pl
from jax.experimental.pallas import tpu as pltpu

F32 = jnp.float32
BF16 = jnp.bfloat16
NORM_EPS = 1e-6
LANES = 128
SUBLANES = 8
MXU_DIM = 256
EPILOGUE_ROWS = 64
FFN_EPILOGUE_ROWS = 16
MIB = 1024 * 1024


def _params(semantics, vmem_mib):
    return pltpu.CompilerParams(dimension_semantics=semantics,
                                vmem_limit_bytes=vmem_mib * MIB)


def _dot(a, b):
    return jnp.dot(a, b, preferred_element_type=F32)


def _rms(x):
    return x * lax.rsqrt(jnp.mean(x * x, axis=-1, keepdims=True) + NORM_EPS)


def _tile(n, want):
    t = min(n, want)
    while n % t:
        t -= 1
    return t


def _adaln_kernel(c_ref, w_ref, b_ref, o_ref):
    c = c_ref[...]
    c_act = c * jax.nn.sigmoid(c)
    c_hi = c_act.astype(BF16)
    c_lo = (c_act - c_hi.astype(F32)).astype(BF16)
    w16 = w_ref[...].astype(BF16)
    o_ref[...] = (_dot(c_hi, w16) + _dot(c_lo, w16)) + b_ref[...]


def adaln_mod(c_pad, w_ada, b_ada):
    depth, d, n = w_ada.shape
    rows = c_pad.shape[0]
    tn = _tile(n, 1024)
    return pl.pallas_call(
        _adaln_kernel,
        grid=(depth, n // tn),
        in_specs=[
            pl.BlockSpec((rows, d), lambda l, j: (0, 0)),
            pl.BlockSpec((None, d, tn), lambda l, j: (l, 0, j)),
            pl.BlockSpec((None, 1, tn), lambda l, j: (l, 0, j)),
        ],
        out_specs=pl.BlockSpec((None, rows, tn), lambda l, j: (l, 0, j)),
        out_shape=jax.ShapeDtypeStruct((depth, rows, n), F32),
        compiler_params=_params(("arbitrary", "arbitrary"), 56),
        name="adaln_mod",
    )(c_pad, w_ada, b_ada.reshape(depth, 1, n))


def _mixer_in_kernel(x_ref, sc_ref, sh_ref, g_ref, w_ref, cw_ref, yconv_ref, u_ref,
                     halo_scr, h_scr, *, tiles_per_seq, col_chunk):
    tm = x_ref.shape[0]
    d_conv = yconv_ref.shape[1]
    d_ssm = u_ref.shape[1]
    pre_scale = g_ref[...] * (1.0 + sc_ref[...])
    for r0 in range(0, tm, EPILOGUE_ROWS):
        rows = slice(r0, r0 + EPILOGUE_ROWS)
        h_scr[rows, :] = (_rms(x_ref[rows, :]) * pre_scale + sh_ref[...]).astype(BF16)
    h = h_scr[...]

    @pl.when(pl.program_id(0) % tiles_per_seq == 0)
    def _():
        halo_scr[...] = jnp.zeros(halo_scr.shape, F32)

    for c0 in range(0, d_conv, col_chunk):
        cols = slice(c0, c0 + col_chunk)
        gate_b = _dot(h, w_ref[:, c0:c0 + col_chunk])
        z = (_dot(h, w_ref[:, d_conv + c0:d_conv + c0 + col_chunk])
             * _dot(h, w_ref[:, 2 * d_conv + c0:2 * d_conv + c0 + col_chunk]))
        prev = halo_scr[:, cols]
        halo_scr[:, cols] = z[tm - SUBLANES:, :]
        w0 = cw_ref[0:1, cols]
        w1 = cw_ref[1:2, cols]
        w2 = cw_ref[2:3, cols]
        z1 = pltpu.roll(z, 1, 0)
        z2 = pltpu.roll(z, 2, 0)
        yconv_ref[:, cols] = (gate_b * (z * w2 + z1 * w1 + z2 * w0)).astype(yconv_ref.dtype)
        row = lax.broadcasted_iota(jnp.int32, (SUBLANES, col_chunk), 0)
        z1t = jnp.where(row < 1, pltpu.roll(prev, 1, 0), z1[:SUBLANES, :])
        z2t = jnp.where(row < 2, pltpu.roll(prev, 2, 0), z2[:SUBLANES, :])
        top = gate_b[:SUBLANES, :] * (z[:SUBLANES, :] * w2 + z1t * w1 + z2t * w0)
        yconv_ref[0:SUBLANES, cols] = top.astype(yconv_ref.dtype)

    for c0 in range(0, d_ssm, col_chunk):
        u_ref[:, c0:c0 + col_chunk] = _dot(
            h, w_ref[:, 3 * d_conv + c0:3 * d_conv + c0 + col_chunk])


SH_MIX, SC_MIX, GT_MIX, SH_FFN, SC_FFN, GT_FFN = range(6)


def _mod_spec(layer, chunk, d, batch_of):
    return pl.BlockSpec((None, None, None, 1, d),
                        lambda i, *_: (layer, batch_of(i), chunk, 0, 0))


def _layer_spec(layer, shape, **kw):
    zeros = (0,) * len(shape)
    return pl.BlockSpec((None,) + tuple(shape), lambda *_: (layer,) + zeros, **kw)


def mixer_in(x, mod5, g_pre, w_in16, conv_wt, layer, *, seq_len):
    t, d = x.shape
    d_in = w_in16.shape[2]
    d_conv = conv_wt.shape[2]
    d_ssm = d_in - 3 * d_conv
    tm = _tile(seq_len, 512)
    col_chunk = _tile(min(d_conv, d_ssm), 512)
    assert d_conv % col_chunk == 0 and d_ssm % col_chunk == 0
    tiles_per_seq = seq_len // tm

    def batch_of(i):
        return i // tiles_per_seq

    kern = functools.partial(_mixer_in_kernel, tiles_per_seq=tiles_per_seq,
                             col_chunk=col_chunk)
    return pl.pallas_call(
        kern,
        grid=(t // tm,),
        in_specs=[
            pl.BlockSpec((tm, d), lambda i: (i, 0)),
            _mod_spec(layer, SC_MIX, d, batch_of), _mod_spec(layer, SH_MIX, d, batch_of),
            _layer_spec(layer, (1, d)),
            _layer_spec(layer, (d, d_in), pipeline_mode=pl.Buffered(1)),
            _layer_spec(layer, (3, d_conv)),
        ],
        out_specs=[
            pl.BlockSpec((tm, d_conv), lambda i: (i, 0)),
            pl.BlockSpec((tm, d_ssm), lambda i: (i, 0)),
        ],
        out_shape=[jax.ShapeDtypeStruct((t, d_conv), BF16),
                   jax.ShapeDtypeStruct((t, d_ssm), F32)],
        scratch_shapes=[pltpu.VMEM((SUBLANES, d_conv), F32), pltpu.VMEM((tm, d), BF16)],
        compiler_params=_params(("arbitrary",), 48),
        name="mixer_in",
    )(x, mod5, mod5, g_pre, w_in16, conv_wt)


def _ssm_prep_kernel(are_ref, aim_ref, ldt_ref, bre_ref, bim_ref,
                     bbre_ref, bbim_ref, abre_ref, abim_ref):
    lam_re = are_ref[...]
    lam_im = aim_ref[...]
    dt = jnp.exp(ldt_ref[...])
    mag = jnp.exp(lam_re * dt)
    ang = lam_im * dt
    ab_re = mag * jnp.cos(ang)
    ab_im = mag * jnp.sin(ang)
    den = lam_re * lam_re + lam_im * lam_im
    nr = ab_re - 1.0
    q_re = (nr * lam_re + ab_im * lam_im) / den
    q_im = (ab_im * lam_re - nr * lam_im) / den
    b_re = bre_ref[...]
    b_im = bim_ref[...]
    bbre_ref[...] = q_re * b_re - q_im * b_im
    bbim_ref[...] = q_re * b_im + q_im * b_re
    abre_ref[...] = ab_re
    abim_ref[...] = ab_im


def ssm_prep(a_re, a_im, log_dt, b_re, b_im):
    depth, g, p, h = b_re.shape
    ldt = jnp.broadcast_to(log_dt.reshape(depth, g, 1, 1), (depth, g, p, 1))

    def spec(last):
        return pl.BlockSpec((None, g, p, last), lambda l: (l, 0, 0, 0))

    return pl.pallas_call(
        _ssm_prep_kernel,
        grid=(depth,),
        in_specs=[spec(1), spec(1), spec(1), spec(h), spec(h)],
        out_specs=[spec(h), spec(h), spec(1), spec(1)],
        out_shape=[jax.ShapeDtypeStruct((depth, g, p, h), F32),
                   jax.ShapeDtypeStruct((depth, g, p, h), F32),
                   jax.ShapeDtypeStruct((depth, g, p, 1), F32),
                   jax.ShapeDtypeStruct((depth, g, p, 1), F32)],
        compiler_params=_params(("arbitrary",), 56),
        name="ssm_prep",
    )(a_re.reshape(depth, g, p, 1), a_im.reshape(depth, g, p, 1), ldt, b_re, b_im)


def _group_rows(w, groups_per_slab):
    depth, g, r, c = w.shape
    return w.reshape(depth, g // groups_per_slab, groups_per_slab * r, c)


def _block_diag(stacked, block_rows):
    n_rows, c = stacked.shape
    n_cols = (n_rows // block_rows) * c
    src = lax.broadcasted_iota(jnp.int32, (c, n_cols), 0)
    dst = lax.broadcasted_iota(jnp.int32, (c, n_cols), 1)
    repeat = jnp.where(dst % c == src, 1.0, 0.0).astype(BF16)
    tiled = _dot(stacked.astype(BF16), repeat)
    row = lax.broadcasted_iota(jnp.int32, (n_rows, n_cols), 0)
    col = lax.broadcasted_iota(jnp.int32, (n_rows, n_cols), 1)
    return jnp.where(row // block_rows == col // c, tiled, 0.0).astype(BF16)


def _gelu_tanh(x):
    return 0.5 * x * (1.0 + jnp.tanh(0.7978845608028654 * (x + 0.044715 * x * x * x)))


def _ssm_kernel(u_ref, bre_ref, bim_ref, cre_ref, cim_ref, are_ref, aim_ref, d_ref, wglu_ref,
                bglu_ref, o_ref, sre_scr, sim_scr, stre_scr, stim_scr, u_scr, y_scr,
                bbd_ref, c_ref, *, pitch, groups_per_step):
    nb, tc, d_ssm = u_ref.shape
    n_slabs, k_slab, two_sw = bbd_ref.shape
    sw = two_sw // 2
    tiles_per_slab = sw // LANES
    pack = SUBLANES // nb
    n_groups = sre_scr.shape[0]
    rows_all = nb * pitch

    def tile_rows(lane_tile):
        q, h = divmod(lane_tile, pack)
        return q, slice(h * rows_all, (h + 1) * rows_all)

    @pl.when(pl.program_id(0) == 0)
    def _():
        stre_scr[...] = jnp.zeros(stre_scr.shape, F32)
        stim_scr[...] = jnp.zeros(stim_scr.shape, F32)
        u_scr[...] = jnp.zeros(u_scr.shape, F32)
        p_dim = bre_ref.shape[2]
        h_dim = k_slab // (sw // p_dim)
        for s in range(n_slabs):
            bbd_ref[s, :, 0:sw] = _block_diag(bre_ref[s], h_dim)
            bbd_ref[s, :, sw:2 * sw] = _block_diag(bim_ref[s], h_dim)
            c_ref[s, 0:sw, :] = _block_diag(cre_ref[s], p_dim)
            c_ref[s, sw:2 * sw, :] = _block_diag(cim_ref[s], p_dim)

    for b in range(nb):
        u_scr[b * pitch:b * pitch + tc, :] = u_ref[b]

    for s in range(n_slabs):
        r = _dot(u_scr[:, s * k_slab:(s + 1) * k_slab].astype(BF16), bbd_ref[s])
        for j in range(tiles_per_slab):
            qr = tile_rows(s * tiles_per_slab + j)
            sre_scr[qr] = r[:, j * LANES:(j + 1) * LANES]
            sim_scr[qr] = r[:, sw + j * LANES:sw + (j + 1) * LANES]

    for q0 in range(0, n_groups, groups_per_step):
        qs = list(range(q0, min(q0 + groups_per_step, n_groups)))
        a_re = [are_ref[q] for q in qs]
        a_im = [aim_ref[q] for q in qs]

        def step(t, carry, qs=qs, a_re=a_re, a_im=a_im):
            rows = pl.ds(t, SUBLANES, stride=pitch)
            out = []
            for n, q in enumerate(qs):
                s_re, s_im = carry[2 * n], carry[2 * n + 1]
                n_re = a_re[n] * s_re - a_im[n] * s_im + sre_scr[q, rows, :]
                n_im = a_re[n] * s_im + a_im[n] * s_re + sim_scr[q, rows, :]
                sre_scr[q, rows, :] = n_re
                sim_scr[q, rows, :] = n_im
                out += [n_re, n_im]
            return tuple(out)

        init = []
        for q in qs:
            init += [stre_scr[q], stim_scr[q]]
        fin = lax.fori_loop(0, tc, step, tuple(init), unroll=2)
        for n, q in enumerate(qs):
            stre_scr[q] = fin[2 * n]
            stim_scr[q] = fin[2 * n + 1]

    def state_rows(s):
        tiles = [tile_rows(s * tiles_per_slab + j) for j in range(tiles_per_slab)]
        return jnp.concatenate([sre_scr[qr].astype(BF16) for qr in tiles]
                               + [(-sim_scr[qr]).astype(BF16) for qr in tiles], axis=-1)

    for s in range(n_slabs):
        cols = slice(s * k_slab, (s + 1) * k_slab)
        y = _dot(state_rows(s), c_ref[s])
        y_scr[:, cols] = y + d_ref[:, cols] * u_scr[:, cols]
    y = _gelu_tanh(y_scr[...])
    gate = _dot(y.astype(BF16), wglu_ref[...]) + bglu_ref[...]
    y_scr[...] = y * jax.nn.sigmoid(gate)
    for b in range(nb):
        o_ref[b] = y_scr[b * pitch:b * pitch + tc, :].astype(o_ref.dtype)


def _pack_state_rows(a, nb):
    depth = a.shape[0]
    pack = SUBLANES // nb
    n_groups = a.shape[1] // (LANES * pack)
    a = a.reshape(depth, n_groups, pack, 1, LANES)
    return jnp.broadcast_to(a, (depth, n_groups, pack, nb, LANES)).reshape(
        depth, n_groups, SUBLANES, LANES)


def ssm(u3, bb_re, bb_im, c_re, c_im, a_re, a_im, d_skip, w_glu16, b_glu, layer):
    nb, seq_len, d_ssm = u3.shape
    n_groups = a_re.shape[1]
    n_slabs, k_slab, p_dim = bb_re.shape[1:]
    sw = c_re.shape[2]
    assert SUBLANES % nb == 0
    tc = _tile(seq_len, 128)
    pitch = tc + SUBLANES // 2
    assert (nb * pitch) % SUBLANES == 0
    kern = functools.partial(_ssm_kernel, pitch=pitch, groups_per_step=min(8, n_groups))

    def full(a):
        return _layer_spec(layer, a.shape[1:])

    return pl.pallas_call(
        kern,
        grid=(seq_len // tc,),
        in_specs=[pl.BlockSpec((nb, tc, d_ssm), lambda c: (0, c, 0)),
                  full(bb_re), full(bb_im), full(c_re), full(c_im), full(a_re), full(a_im),
                  full(d_skip), full(w_glu16), full(b_glu)],
        out_specs=pl.BlockSpec((nb, tc, d_ssm), lambda c: (0, c, 0)),
        out_shape=jax.ShapeDtypeStruct((nb, seq_len, d_ssm), BF16),
        scratch_shapes=[pltpu.VMEM((n_groups, SUBLANES * pitch, LANES), F32),
                        pltpu.VMEM((n_groups, SUBLANES * pitch, LANES), F32),
                        pltpu.VMEM((n_groups, SUBLANES, LANES), F32),
                        pltpu.VMEM((n_groups, SUBLANES, LANES), F32),
                        pltpu.VMEM((nb * pitch, d_ssm), F32),
                        pltpu.VMEM((nb * pitch, d_ssm), F32),
                        pltpu.VMEM((n_slabs, k_slab, 2 * sw), BF16),
                        pltpu.VMEM((n_slabs, 2 * sw, k_slab), BF16)],
        compiler_params=_params(("arbitrary",), 56),
        name="ssm_scan",
    )(u3, bb_re, bb_im, c_re, c_im, a_re, a_im, d_skip, w_glu16, b_glu)


def _mixer_out_kernel(*refs, n_experts):
    routed = n_experts is not None
    if routed:
        (yc_ref, ys_ref, wo_ref, x_ref, gt_ref, gpost_ref, gpre_ref, sc_ref,
         sh_ref, rw_ref, rb_ref, xo_ref, h_ref, route_i_ref, route_w_ref, cnt_ref,
         y_scr, carry_scr) = refs
    else:
        (yc_ref, ys_ref, wo_ref, x_ref, gt_ref, gpost_ref, gpre_ref, sc_ref,
         sh_ref, xo_ref, h_ref, y_scr) = refs
    tm = yc_ref.shape[0]
    y_scr[...] = _dot(jnp.concatenate([yc_ref[...], ys_ref[...]], axis=-1), wo_ref[...])
    post_scale = gt_ref[...] * gpost_ref[...]
    pre_scale = gpre_ref[...] * (1.0 + sc_ref[...])
    for r0 in range(0, tm, EPILOGUE_ROWS):
        rows = slice(r0, r0 + EPILOGUE_ROWS)
        x_new = x_ref[rows, :] + _rms(y_scr[rows, :]) * post_scale
        xo_ref[rows, :] = x_new
        h_ref[rows, :] = (_rms(x_new) * pre_scale + sh_ref[...]).astype(h_ref.dtype)
    if routed:
        h = h_ref[...]
        w = rw_ref[...]
        h_hi = h.astype(BF16)
        w_hi = w.astype(BF16)
        h_lo = (h - h_hi.astype(F32)).astype(BF16)
        w_lo = (w - w_hi.astype(F32)).astype(BF16)
        logits = _dot(h_hi, w_hi) + (_dot(h_hi, w_lo) + _dot(h_lo, w_hi)) + rb_ref[...]
        _route(logits, route_i_ref, route_w_ref, cnt_ref, carry_scr, n_experts)


def mixer_out(y_conv, y_ssm, w_out16, x, mod5, g_post, g_pre, layer, router=None, *,
              seq_len):
    t, d = x.shape
    d_conv = y_conv.shape[1]
    d_ssm = y_ssm.shape[1]
    tm = _tile(seq_len, 512)
    tiles_per_seq = seq_len // tm
    routed = router is not None

    def batch_of(i):
        return i // tiles_per_seq

    def row_spec(n):
        return pl.BlockSpec((tm, n), lambda i: (i, 0))

    def const_spec(shape):
        return pl.BlockSpec(shape, lambda i: (0, 0))

    in_specs = [row_spec(d_conv), row_spec(d_ssm),
                _layer_spec(layer, (d_conv + d_ssm, d), pipeline_mode=pl.Buffered(1)),
                row_spec(d), _mod_spec(layer, GT_MIX, d, batch_of),
                _layer_spec(layer, (1, d)), _layer_spec(layer, (1, d)),
                _mod_spec(layer, SC_FFN, d, batch_of), _mod_spec(layer, SH_FFN, d, batch_of)]
    args = [y_conv, y_ssm, w_out16, x, mod5, g_post, g_pre, mod5, mod5]
    out_specs = [row_spec(d), row_spec(d)]
    out_shape = [jax.ShapeDtypeStruct((t, d), F32),
                 jax.ShapeDtypeStruct((t, d), F32 if routed else BF16)]
    scratch_shapes = [pltpu.VMEM((tm, d), F32)]
    n_experts = None
    if routed:
        rw_pad, rb_pad, n_experts = router
        in_specs += [const_spec(rw_pad.shape), const_spec(rb_pad.shape)]
        args += [rw_pad, rb_pad]
        out_specs += [row_spec(LANES), row_spec(LANES), const_spec((SUBLANES, LANES))]
        out_shape += [jax.ShapeDtypeStruct((t, LANES), jnp.int32),
                      jax.ShapeDtypeStruct((t, LANES), F32),
                      jax.ShapeDtypeStruct((SUBLANES, LANES), jnp.int32)]
        scratch_shapes.append(pltpu.VMEM((SUBLANES, LANES), F32))
    return pl.pallas_call(
        functools.partial(_mixer_out_kernel, n_experts=n_experts),
        grid=(t // tm,),
        in_specs=in_specs, out_specs=out_specs, out_shape=out_shape,
        scratch_shapes=scratch_shapes,
        compiler_params=_params(("arbitrary",), 56),
        name="mixer_out_routed" if routed else "mixer_out",
    )(*args)


def _ffn_kernel(h_ref, wg_ref, wu_ref, wd_ref, x_hbm, gt_ref, gpost_ref, o_ref,
                wg16, wu16, wd16, x_buf, sem, *, sub_rows):
    i = pl.program_id(0)
    k = pl.program_id(1)
    tm = o_ref.shape[0]
    x_copy = pltpu.make_async_copy(x_hbm.at[pl.ds(i * tm, tm)], x_buf, sem)

    @pl.when(k == 0)
    def _():
        o_ref[...] = jnp.zeros(o_ref.shape, F32)
        x_copy.start()

    wg16[...] = wg_ref[...].astype(BF16)
    wu16[...] = wu_ref[...].astype(BF16)
    wd16[...] = wd_ref[...].astype(BF16)
    for r0 in range(0, tm, sub_rows):
        h = h_ref[r0:r0 + sub_rows, :]
        gate = _dot(h, wg16[...])
        act = (gate * jax.nn.sigmoid(gate)) * _dot(h, wu16[...])
        o_ref[r0:r0 + sub_rows, :] += _dot(act.astype(BF16), wd16[...])

    @pl.when(k == pl.num_programs(1) - 1)
    def _():
        x_copy.wait()
        scale = gt_ref[...] * gpost_ref[...]
        for r0 in range(0, tm, FFN_EPILOGUE_ROWS):
            rows = slice(r0, r0 + FFN_EPILOGUE_ROWS)
            o_ref[rows, :] = x_buf[rows, :] + _rms(o_ref[rows, :]) * scale


def ffn_dense(h16, w_gate, w_up, w_down, x, mod5, g_post, layer, ffn_index, *, seq_len):
    t, d = x.shape
    d_ff = w_gate.shape[2]
    tm = _tile(seq_len, 1024)
    sub_rows = _tile(tm, 512)
    tk = _tile(d_ff, 256)
    tiles_per_seq = seq_len // tm
    return pl.pallas_call(
        functools.partial(_ffn_kernel, sub_rows=sub_rows),
        grid=(t // tm, d_ff // tk),
        in_specs=[
            pl.BlockSpec((tm, d), lambda i, k: (i, 0)),
            pl.BlockSpec((None, d, tk), lambda i, k: (ffn_index, 0, k)),
            pl.BlockSpec((None, d, tk), lambda i, k: (ffn_index, 0, k)),
            pl.BlockSpec((None, tk, d), lambda i, k: (ffn_index, k, 0)),
            pl.BlockSpec(memory_space=pl.ANY),
            _mod_spec(layer, GT_FFN, d, lambda i: i // tiles_per_seq),
            _layer_spec(layer, (1, d)),
        ],
        out_specs=pl.BlockSpec((tm, d), lambda i, k: (i, 0)),
        out_shape=jax.ShapeDtypeStruct((t, d), F32),
        scratch_shapes=[pltpu.VMEM((d, tk), BF16), pltpu.VMEM((d, tk), BF16),
                        pltpu.VMEM((tk, d), BF16), pltpu.VMEM((tm, d), F32),
                        pltpu.SemaphoreType.DMA(())],
        compiler_params=_params(("arbitrary", "arbitrary"), 56),
        name="ffn_dense",
    )(h16, w_gate, w_up, w_down, x, mod5, g_post)


def _route(logits, ri_ref, rw_ref, cnt_ref, carry_scr, n_experts):
    @pl.when(pl.program_id(0) == 0)
    def _():
        carry_scr[...] = jnp.zeros(carry_scr.shape, F32)

    tm = logits.shape[0]
    lane = lax.broadcasted_iota(jnp.int32, (tm, LANES), 1)
    neg = jnp.float32(-jnp.inf)
    logit = jnp.where(lane < n_experts, logits, neg)
    m1 = jnp.max(logit, axis=-1, keepdims=True)
    i1 = jnp.min(jnp.where(logit == m1, lane, LANES), axis=-1, keepdims=True)
    rest = jnp.where(lane == i1, neg, logit)
    m2 = jnp.max(rest, axis=-1, keepdims=True)
    i2 = jnp.min(jnp.where(rest == m2, lane, LANES), axis=-1, keepdims=True)
    e2 = jnp.exp(m2 - m1)
    w1 = 1.0 / (1.0 + e2)
    w2 = e2 / (1.0 + e2)
    hit1 = lane == i1
    hit2 = lane == i2
    onehot = jnp.where(hit1 | hit2, 1.0, 0.0)
    r = lax.broadcasted_iota(jnp.int32, (tm, tm), 0)
    c = lax.broadcasted_iota(jnp.int32, (tm, tm), 1)
    earlier = jnp.where(c < r, 1.0, 0.0).astype(BF16)
    carry = carry_scr[0:1, :]
    before = _dot(earlier, onehot.astype(BF16)) + carry
    pos1 = jnp.sum(jnp.where(hit1, before, 0.0), axis=-1, keepdims=True).astype(jnp.int32)
    pos2 = jnp.sum(jnp.where(hit2, before, 0.0), axis=-1, keepdims=True).astype(jnp.int32)
    total = carry + jnp.sum(onehot, axis=0, keepdims=True)
    carry_scr[...] = jnp.broadcast_to(total, carry_scr.shape)
    cnt_ref[...] = jnp.broadcast_to(total, cnt_ref.shape).astype(jnp.int32)
    zero_i = jnp.zeros((tm, LANES), jnp.int32)
    ri_ref[...] = jnp.where(lane == 0, i1, jnp.where(lane == 1, i2,
                            jnp.where(lane == 2, pos1, jnp.where(lane == 3, pos2, zero_i))))
    rw_ref[...] = jnp.where(lane == 0, w1, jnp.where(lane == 1, w2, 0.0))


def _moe_ffn_kernel(ge_ref, nr_ref, nu_ref, tok_ref, h_hbm, wg_ref, wu_ref, wd_ref, o_ref,
                    h_scr, gbuf, wg16, wu16, wd16, sem, *, rows_per_step):
    g = pl.program_id(0)
    k = pl.program_id(1)
    n_used = nu_ref[0]
    n_sub_max, ts, _ = h_scr.shape
    group_rows = n_sub_max * ts
    nk = pl.num_programs(1)
    n_gather = gbuf.shape[0]

    def row_copy(src_row, dst_row):
        return pltpu.make_async_copy(h_hbm.at[pl.ds(src_row, 1)],
                                     gbuf.at[pl.ds(dst_row, 1)], sem)

    def for_each_sub_tile(group, fn):
        for j in range(n_sub_max):
            @pl.when(j * ts < nr_ref[group])
            def _(j=j):
                fn(j)

    def wait_rows():
        pltpu.make_async_copy(h_hbm.at[pl.ds(0, n_gather)], gbuf, sem).wait()

    @pl.when(k == 0)
    def _():
        o_ref[...] = jnp.zeros(o_ref.shape, o_ref.dtype)

    @pl.when(g < n_used)
    def _():
        @pl.when(k == 0)
        def _():
            @pl.when(g == 0)
            def _():
                def body(i, carry):
                    row_copy(tok_ref[i], i).start()
                    return carry
                lax.fori_loop(0, n_gather, body, 0, unroll=8)

            wait_rows()

            def cast(j):
                h_scr[j] = gbuf[j * ts:(j + 1) * ts, :].astype(BF16)
            for_each_sub_tile(g, cast)

        next_base = jnp.minimum(g + 1, n_used - 1) * group_rows + k * rows_per_step

        def sub_tile(j, m):
            if j == 0:
                wg16[...] = wg_ref[...].astype(BF16)
                wu16[...] = wu_ref[...].astype(BF16)
                wd16[...] = wd_ref[...].astype(BF16)
                for i in range(rows_per_step):
                    row_copy(tok_ref[next_base + i], k * rows_per_step + i).start()
            h = h_scr[j, 0:m, :]
            gate = _dot(h, wg16[...])
            act = (gate * jax.nn.sigmoid(gate)) * _dot(h, wu16[...])
            o_ref[j * ts:j * ts + m, :] += _dot(act.astype(BF16), wd16[...])

        half = ts // 2
        all_full = nr_ref[g] > group_rows - half

        @pl.when(all_full)
        def _():
            for j in range(n_sub_max):
                sub_tile(j, ts)

        @pl.when(jnp.logical_not(all_full))
        def _():
            for j in range(n_sub_max):
                rows_j = nr_ref[g] - j * ts

                @pl.when(rows_j > half)
                def _(j=j):
                    sub_tile(j, ts)

                @pl.when((rows_j > 0) & (rows_j <= half))
                def _(j=j):
                    sub_tile(j, half)

        @pl.when((g == n_used - 1) & (k == nk - 1))
        def _():
            wait_rows()


def moe_ffn(group_expert, group_rows_used, n_used, row_tok, h, w_gate, w_up, w_down, layer, *,
            sub_rows, n_sub_max):
    d = h.shape[1]
    d_ff = w_gate.shape[3]
    tk = _tile(d_ff, 256)
    nk = d_ff // tk
    group_rows = sub_rows * n_sub_max
    rows_per_step = -(-group_rows // (nk * SUBLANES)) * SUBLANES
    n_gather = rows_per_step * nk
    n_rows = group_expert.shape[0] * group_rows
    assert row_tok.shape[0] - n_rows >= n_gather - group_rows

    def ff_idx(g, k, nu):
        return jnp.where(g < nu[0], k, nk - 1)

    return pl.pallas_call(
        functools.partial(_moe_ffn_kernel, rows_per_step=rows_per_step),
        grid_spec=pltpu.PrefetchScalarGridSpec(
            num_scalar_prefetch=4,
            grid=(n_rows // group_rows, nk),
            in_specs=[
                pl.BlockSpec(memory_space=pl.ANY),
                pl.BlockSpec((None, None, d, tk),
                             lambda g, k, ge, ns, nu, tok: (layer, ge[g], 0, ff_idx(g, k, nu))),
                pl.BlockSpec((None, None, d, tk),
                             lambda g, k, ge, ns, nu, tok: (layer, ge[g], 0, ff_idx(g, k, nu))),
                pl.BlockSpec((None, None, tk, d),
                             lambda g, k, ge, ns, nu, tok: (layer, ge[g], ff_idx(g, k, nu), 0)),
            ],
            out_specs=pl.BlockSpec((group_rows, d), lambda g, k, ge, ns, nu, tok: (g, 0)),
            scratch_shapes=[pltpu.VMEM((n_sub_max, sub_rows, d), BF16),
                            pltpu.VMEM((n_gather, d), F32),
                            pltpu.VMEM((d, tk), BF16), pltpu.VMEM((d, tk), BF16),
                            pltpu.VMEM((tk, d), BF16),
                            pltpu.SemaphoreType.DMA(())],
        ),
        out_shape=jax.ShapeDtypeStruct((n_rows, d), F32),
        compiler_params=_params(("arbitrary", "arbitrary"), 56),
        name="moe_ffn",
    )(group_expert, group_rows_used, n_used, row_tok, h, w_gate, w_up, w_down)


def _moe_combine_kernel(d0_ref, d1_ref, ys_ref, rw_ref, x_ref, gt_ref, gpost_ref, o_ref,
                        buf, sems):
    i = pl.program_id(0)
    tm = x_ref.shape[0]
    slot = i % 2

    def start_tile(tile, s):
        def body(t, carry):
            for which, dest in enumerate((d0_ref, d1_ref)):
                pltpu.make_async_copy(ys_ref.at[pl.ds(dest[tile * tm + t], 1)],
                                      buf.at[s, pl.ds(which * tm + t, 1)],
                                      sems.at[s]).start()
            return carry
        lax.fori_loop(0, tm, body, 0, unroll=8)

    @pl.when(i == 0)
    def _():
        start_tile(0, 0)

    @pl.when(i + 1 < pl.num_programs(0))
    def _():
        start_tile(i + 1, 1 - slot)

    pltpu.make_async_copy(ys_ref.at[pl.ds(0, 2 * tm)], buf.at[slot], sems.at[slot]).wait()
    scale = gt_ref[...] * gpost_ref[...]
    for r0 in range(0, tm, EPILOGUE_ROWS):
        rows = slice(r0, r0 + EPILOGUE_ROWS)
        y = (buf[slot, r0:r0 + EPILOGUE_ROWS, :] * rw_ref[rows, 0:1]
             + buf[slot, tm + r0:tm + r0 + EPILOGUE_ROWS, :] * rw_ref[rows, 1:2])
        o_ref[rows, :] = x_ref[rows, :] + _rms(y) * scale


def moe_combine(dest0, dest1, ys, route_w, x, mod5, g_post, layer, *, seq_len):
    t, d = x.shape
    tm = _tile(seq_len, 256)
    tiles_per_seq = seq_len // tm
    return pl.pallas_call(
        _moe_combine_kernel,
        grid_spec=pltpu.PrefetchScalarGridSpec(
            num_scalar_prefetch=2,
            grid=(t // tm,),
            in_specs=[
                pl.BlockSpec(memory_space=pl.ANY),
                pl.BlockSpec((tm, LANES), lambda i, d0, d1: (i, 0)),
                pl.BlockSpec((tm, d), lambda i, d0, d1: (i, 0)),
                _mod_spec(layer, GT_FFN, d, lambda i: i // tiles_per_seq),
                _layer_spec(layer, (1, d)),
            ],
            out_specs=pl.BlockSpec((tm, d), lambda i, d0, d1: (i, 0)),
            scratch_shapes=[pltpu.VMEM((2, 2 * tm, d), F32), pltpu.SemaphoreType.DMA((2,))],
        ),
        out_shape=jax.ShapeDtypeStruct((t, d), F32),
        compiler_params=_params(("arbitrary",), 48),
        name="moe_combine",
    )(dest0, dest1, ys, route_w, x, mod5, g_post)


MOE_SUB_ROWS = 512
MOE_GROUP_SUBS = 2


def _row_tok_kernel(d0_ref, d1_ref, o_ref):
    def clear(r, carry):
        o_ref[r] = 0
        return carry
    lax.fori_loop(0, o_ref.shape[0], clear, 0, unroll=8)

    def place(t, carry):
        o_ref[d0_ref[t]] = t
        o_ref[d1_ref[t]] = t
        return carry
    lax.fori_loop(0, d0_ref.shape[0], place, 0, unroll=8)


def row_tokens(dest0, dest1, n_rows):
    smem = pl.BlockSpec(memory_space=pltpu.SMEM)
    return pl.pallas_call(
        _row_tok_kernel,
        in_specs=[smem, smem],
        out_specs=smem,
        out_shape=jax.ShapeDtypeStruct((n_rows,), jnp.int32),
        name="row_tokens",
    )(dest0, dest1)


def moe_layer(h, route_i, route_w, counts, x, mod5, g_post, w_gate, w_up, w_down, layer,
              moe_index, *, seq_len):
    t, d = h.shape
    n_experts = w_gate.shape[1]
    ts = min(MOE_SUB_ROWS, t)
    group_rows = ts * MOE_GROUP_SUBS
    e0, e1, pos0, pos1 = (route_i[:, k] for k in range(4))
    counts = counts[0, :n_experts]
    groups_e = (counts + group_rows - 1) // group_rows
    g_end = jnp.cumsum(groups_e)
    g_start = g_end - groups_e
    dest0 = g_start[e0] * group_rows + pos0
    dest1 = g_start[e1] * group_rows + pos1
    n_groups = -(-(2 * t) // group_rows) + n_experts
    n_rows = n_groups * group_rows
    row_tok = row_tokens(dest0, dest1, n_rows + group_rows)
    n_used = g_end[-1].astype(jnp.int32)
    gid = jnp.arange(n_groups, dtype=jnp.int32)
    group_e = jnp.clip(jnp.searchsorted(g_end, gid, side='right'), 0, n_experts - 1)
    group_e = jnp.where(gid < n_used, group_e, group_e[n_used - 1]).astype(jnp.int32)
    group_rows_used = jnp.where(
        gid < n_used,
        jnp.clip(counts[group_e] - (gid - g_start[group_e]) * group_rows, 0, group_rows),
        0).astype(jnp.int32)
    ys = moe_ffn(group_e, group_rows_used, n_used.reshape(1), row_tok, h, w_gate, w_up, w_down,
                 moe_index, sub_rows=ts, n_sub_max=MOE_GROUP_SUBS)
    return moe_combine(dest0, dest1, ys, route_w, x, mod5, g_post, layer, seq_len=seq_len)


def kernel(x, c, w_ada, b_ada, g_mix_pre, g_mix_post, g_ffn_pre, g_ffn_post, w_in, conv_w,
           ssm_a_re, ssm_a_im, ssm_log_dt, ssm_b_re, ssm_b_im, ssm_c_re, ssm_c_im, ssm_d,
           w_glu, b_glu, w_out, ffn_w_gate, ffn_w_up, ffn_w_down, router_w, router_b,
           moe_w_gate, moe_w_up, moe_w_down):
    nb, seq_len, d = x.shape
    depth = w_ada.shape[0]
    g, p, h_dim = ssm_b_re.shape[1:]
    d_ssm = g * h_dim
    n_experts = router_w.shape[2]
    t = nb * seq_len
    groups_per_slab = max(1, min(g, MXU_DIM // h_dim))

    mod_rows = 2 * SUBLANES
    c_pad = jnp.zeros((mod_rows, d), F32).at[:nb].set(c)
    mod = adaln_mod(c_pad, w_ada, b_ada)
    mod5 = mod.reshape(depth, mod_rows, 6, 1, d)
    xf = x.reshape(t, d)

    def rows(v):
        return v.reshape(v.shape[0], 1, v.shape[1])
    g_mix_pre, g_mix_post, g_ffn_pre, g_ffn_post = (
        rows(v) for v in (g_mix_pre, g_mix_post, g_ffn_pre, g_ffn_post))
    w_in16 = w_in.astype(BF16)
    w_out16 = w_out.astype(BF16)
    w_glu16 = w_glu.astype(BF16)
    conv_wt = jnp.swapaxes(conv_w, 1, 2)
    bb_re, bb_im, ab_re, ab_im = ssm_prep(ssm_a_re, ssm_a_im, ssm_log_dt, ssm_b_re, ssm_b_im)
    def stacked_t(w):
        return _group_rows(jnp.swapaxes(w, 2, 3), groups_per_slab)
    bbs_re, bbs_im = stacked_t(bb_re), stacked_t(bb_im)
    cs_re, cs_im = stacked_t(ssm_c_re), stacked_t(ssm_c_im)
    a_re = _pack_state_rows(ab_re.reshape(depth, g * p), nb)
    a_im = _pack_state_rows(ab_im.reshape(depth, g * p), nb)
    d_skip = ssm_d.reshape(depth, 1, d_ssm)
    b_glu = rows(b_glu)
    n_moe = router_w.shape[0]
    rw_pad = jnp.zeros((n_moe, d, LANES), F32).at[:, :, :n_experts].set(router_w)
    rb_pad = jnp.zeros((n_moe, 1, LANES), F32).at[:, 0, :n_experts].set(router_b)

    for l in range(depth):
        y_conv, u = mixer_in(xf, mod5, g_mix_pre, w_in16, conv_wt, l, seq_len=seq_len)
        y_ssm = ssm(u.reshape(nb, seq_len, d_ssm), bbs_re, bbs_im, cs_re, cs_im, a_re, a_im, d_skip,
                    w_glu16, b_glu, l).reshape(t, d_ssm)
        routed = l % 2 == 1
        i = l // 2
        rt = (rw_pad[i], rb_pad[i], n_experts) if routed else None
        outs = mixer_out(y_conv, y_ssm, w_out16, xf, mod5, g_mix_post, g_ffn_pre, l, rt,
                         seq_len=seq_len)
        if routed:
            xf, h, route_i, route_w, counts = outs
            xf = moe_layer(h, route_i, route_w, counts, xf, mod5, g_ffn_post,
                           moe_w_gate, moe_w_up, moe_w_down, l, i, seq_len=seq_len)
        else:
            xf, h = outs
            xf = ffn_dense(h, ffn_w_gate, ffn_w_up, ffn_w_down, xf, mod5, g_ffn_post, l, i,
                           seq_len=seq_len)
    return xf.reshape(nb, seq_len, d)
```

```python
import functools

import jax
import jax.numpy as jnp
from jax import lax
from jax.experimental import pallas as pl
from jax.experimental.pallas import tpu as pltpu

F32 = jnp.float32
BF16 = jnp.bfloat16
NORM_EPS = 1e-6
LANES = 128
SUBLANES = 8
MXU_DIM = 256
EPILOGUE_ROWS = 64
FFN_EPILOGUE_ROWS = 16
MIB = 1024 * 1024


def _params(semantics, vmem_mib):
    return pltpu.CompilerParams(dimension_semantics=semantics,
                                vmem_limit_bytes=vmem_mib * MIB)


def _dot(a, b):
    return jnp.dot(a, b, preferred_element_type=F32)


def _rms(x):
    return x * lax.rsqrt(jnp.mean(x * x, axis=-1, keepdims=True) + NORM_EPS)


def _tile(n, want):
    t = min(n, want)
    while n % t:
        t -= 1
    return t


def _adaln_kernel(c_ref, w_ref, b_ref, o_ref):
    c = c_ref[...]
    c_act = c * jax.nn.sigmoid(c)
    c_hi = c_act.astype(BF16)
    c_lo = (c_act - c_hi.astype(F32)).astype(BF16)
    w16 = w_ref[...].astype(BF16)
    o_ref[...] = (_dot(c_hi, w16) + _dot(c_lo, w16)) + b_ref[...]


def adaln_mod(c_pad, w_ada, b_ada):
    depth, d, n = w_ada.shape
    rows = c_pad.shape[0]
    tn = _tile(n, 1024)
    return pl.pallas_call(
        _adaln_kernel,
        grid=(depth, n // tn),
        in_specs=[
            pl.BlockSpec((rows, d), lambda l, j: (0, 0)),
            pl.BlockSpec((None, d, tn), lambda l, j: (l, 0, j)),
            pl.BlockSpec((None, 1, tn), lambda l, j: (l, 0, j)),
        ],
        out_specs=pl.BlockSpec((None, rows, tn), lambda l, j: (l, 0, j)),
        out_shape=jax.ShapeDtypeStruct((depth, rows, n), F32),
        compiler_params=_params(("arbitrary", "arbitrary"), 56),
        name="adaln_mod",
    )(c_pad, w_ada, b_ada.reshape(depth, 1, n))


def _mixer_in_kernel(x_ref, sc_ref, sh_ref, g_ref, w_ref, cw_ref, yconv_ref, u_ref,
                     halo_scr, h_scr, *, tiles_per_seq, col_chunk):
    tm = x_ref.shape[0]
    d_conv = yconv_ref.shape[1]
    d_ssm = u_ref.shape[1]
    pre_scale = g_ref[...] * (1.0 + sc_ref[...])
    for r0 in range(0, tm, EPILOGUE_ROWS):
        rows = slice(r0, r0 + EPILOGUE_ROWS)
        h_scr[rows, :] = (_rms(x_ref[rows, :]) * pre_scale + sh_ref[...]).astype(BF16)
    h = h_scr[...]

    @pl.when(pl.program_id(0) % tiles_per_seq == 0)
    def _():
        halo_scr[...] = jnp.zeros(halo_scr.shape, F32)

    for c0 in range(0, d_conv, col_chunk):
        cols = slice(c0, c0 + col_chunk)
        gate_b = _dot(h, w_ref[:, c0:c0 + col_chunk])
        z = (_dot(h, w_ref[:, d_conv + c0:d_conv + c0 + col_chunk])
             * _dot(h, w_ref[:, 2 * d_conv + c0:2 * d_conv + c0 + col_chunk]))
        prev = halo_scr[:, cols]
        halo_scr[:, cols] = z[tm - SUBLANES:, :]
        w0 = cw_ref[0:1, cols]
        w1 = cw_ref[1:2, cols]
        w2 = cw_ref[2:3, cols]
        z1 = pltpu.roll(z, 1, 0)
        z2 = pltpu.roll(z, 2, 0)
        yconv_ref[:, cols] = (gate_b * (z * w2 + z1 * w1 + z2 * w0)).astype(yconv_ref.dtype)
        row = lax.broadcasted_iota(jnp.int32, (SUBLANES, col_chunk), 0)
        z1t = jnp.where(row < 1, pltpu.roll(prev, 1, 0), z1[:SUBLANES, :])
        z2t = jnp.where(row < 2, pltpu.roll(prev, 2, 0), z2[:SUBLANES, :])
        top = gate_b[:SUBLANES, :] * (z[:SUBLANES, :] * w2 + z1t * w1 + z2t * w0)
        yconv_ref[0:SUBLANES, cols] = top.astype(yconv_ref.dtype)

    for c0 in range(0, d_ssm, col_chunk):
        u_ref[:, c0:c0 + col_chunk] = _dot(
            h, w_ref[:, 3 * d_conv + c0:3 * d_conv + c0 + col_chunk])


SH_MIX, SC_MIX, GT_MIX, SH_FFN, SC_FFN, GT_FFN = range(6)


def _mod_spec(layer, chunk, d, batch_of):
    return pl.BlockSpec((None, None, None, 1, d),
                        lambda i, *_: (layer, batch_of(i), chunk, 0, 0))


def _layer_spec(layer, shape, **kw):
    zeros = (0,) * len(shape)
    return pl.BlockSpec((None,) + tuple(shape), lambda *_: (layer,) + zeros, **kw)


def mixer_in(x, mod5, g_pre, w_in16, conv_wt, layer, *, seq_len):
    t, d = x.shape
    d_in = w_in16.shape[2]
    d_conv = conv_wt.shape[2]
    d_ssm = d_in - 3 * d_conv
    tm = _tile(seq_len, 512)
    col_chunk = _tile(min(d_conv, d_ssm), 512)
    assert d_conv % col_chunk == 0 and d_ssm % col_chunk == 0
    tiles_per_seq = seq_len // tm

    def batch_of(i):
        return i // tiles_per_seq

    kern = functools.partial(_mixer_in_kernel, tiles_per_seq=tiles_per_seq,
                             col_chunk=col_chunk)
    return pl.pallas_call(
        kern,
        grid=(t // tm,),
        in_specs=[
            pl.BlockSpec((tm, d), lambda i: (i, 0)),
            _mod_spec(layer, SC_MIX, d, batch_of), _mod_spec(layer, SH_MIX, d, batch_of),
            _layer_spec(layer, (1, d)),
            _layer_spec(layer, (d, d_in), pipeline_mode=pl.Buffered(1)),
            _layer_spec(layer, (3, d_conv)),
        ],
        out_specs=[
            pl.BlockSpec((tm, d_conv), lambda i: (i, 0)),
            pl.BlockSpec((tm, d_ssm), lambda i: (i, 0)),
        ],
        out_shape=[jax.ShapeDtypeStruct((t, d_conv), BF16),
                   jax.ShapeDtypeStruct((t, d_ssm), F32)],
        scratch_shapes=[pltpu.VMEM((SUBLANES, d_conv), F32), pltpu.VMEM((tm, d), BF16)],
        compiler_params=_params(("arbitrary",), 48),
        name="mixer_in",
    )(x, mod5, mod5, g_pre, w_in16, conv_wt)


def _ssm_prep_kernel(are_ref, aim_ref, ldt_ref, bre_ref, bim_ref,
                     bbre_ref, bbim_ref, abre_ref, abim_ref):
    lam_re = are_ref[...]
    lam_im = aim_ref[...]
    dt = jnp.exp(ldt_ref[...])
    mag = jnp.exp(lam_re * dt)
    ang = lam_im * dt
    ab_re = mag * jnp.cos(ang)
    ab_im = mag * jnp.sin(ang)
    den = lam_re * lam_re + lam_im * lam_im
    nr = ab_re - 1.0
    q_re = (nr * lam_re + ab_im * lam_im) / den
    q_im = (ab_im * lam_re - nr * lam_im) / den
    b_re = bre_ref[...]
    b_im = bim_ref[...]
    bbre_ref[...] = q_re * b_re - q_im * b_im
    bbim_ref[...] = q_re * b_im + q_im * b_re
    abre_ref[...] = ab_re
    abim_ref[...] = ab_im


def ssm_prep(a_re, a_im, log_dt, b_re, b_im):
    depth, g, p, h = b_re.shape
    ldt = jnp.broadcast_to(log_dt.reshape(depth, g, 1, 1), (depth, g, p, 1))

    def spec(last):
        return pl.BlockSpec((None, g, p, last), lambda l: (l, 0, 0, 0))

    return pl.pallas_call(
        _ssm_prep_kernel,
        grid=(depth,),
        in_specs=[spec(1), spec(1), spec(1), spec(h), spec(h)],
        out_specs=[spec(h), spec(h), spec(1), spec(1)],
        out_shape=[jax.ShapeDtypeStruct((depth, g, p, h), F32),
                   jax.ShapeDtypeStruct((depth, g, p, h), F32),
                   jax.ShapeDtypeStruct((depth, g, p, 1), F32),
                   jax.ShapeDtypeStruct((depth, g, p, 1), F32)],
        compiler_params=_params(("arbitrary",), 56),
        name="ssm_prep",
    )(a_re.reshape(depth, g, p, 1), a_im.reshape(depth, g, p, 1), ldt, b_re, b_im)


def _group_rows(w, groups_per_slab):
    depth, g, r, c = w.shape
    return w.reshape(depth, g // groups_per_slab, groups_per_slab * r, c)


def _block_diag(stacked, block_rows):
    n_rows, c = stacked.shape
    n_cols = (n_rows // block_rows) * c
    src = lax.broadcasted_iota(jnp.int32, (c, n_cols), 0)
    dst = lax.broadcasted_iota(jnp.int32, (c, n_cols), 1)
    repeat = jnp.where(dst % c == src, 1.0, 0.0).astype(BF16)
    tiled = _dot(stacked.astype(BF16), repeat)
    row = lax.broadcasted_iota(jnp.int32, (n_rows, n_cols), 0)
    col = lax.broadcasted_iota(jnp.int32, (n_rows, n_cols), 1)
    return jnp.where(row // block_rows == col // c, tiled, 0.0).astype(BF16)


def _gelu_tanh(x):
    return 0.5 * x * (1.0 + jnp.tanh(0.7978845608028654 * (x + 0.044715 * x * x * x)))


def _ssm_kernel(u_ref, bre_ref, bim_ref, cre_ref, cim_ref, are_ref, aim_ref, d_ref, wglu_ref,
                bglu_ref, o_ref, sre_scr, sim_scr, stre_scr, stim_scr, u_scr, y_scr,
                bbd_ref, c_ref, *, pitch, groups_per_step):
    nb, tc, d_ssm = u_ref.shape
    n_slabs, k_slab, two_sw = bbd_ref.shape
    sw = two_sw // 2
    tiles_per_slab = sw // LANES
    pack = SUBLANES // nb
    n_groups = sre_scr.shape[0]
    rows_all = nb * pitch

    def tile_rows(lane_tile):
        q, h = divmod(lane_tile, pack)
        return q, slice(h * rows_all, (h + 1) * rows_all)

    @pl.when(pl.program_id(0) == 0)
    def _():
        stre_scr[...] = jnp.zeros(stre_scr.shape, F32)
        stim_scr[...] = jnp.zeros(stim_scr.shape, F32)
        u_scr[...] = jnp.zeros(u_scr.shape, F32)
        p_dim = bre_ref.shape[2]
        h_dim = k_slab // (sw // p_dim)
        for s in range(n_slabs):
            bbd_ref[s, :, 0:sw] = _block_diag(bre_ref[s], h_dim)
            bbd_ref[s, :, sw:2 * sw] = _block_diag(bim_ref[s], h_dim)
            c_ref[s, 0:sw, :] = _block_diag(cre_ref[s], p_dim)
            c_ref[s, sw:2 * sw, :] = _block_diag(cim_ref[s], p_dim)

    for b in range(nb):
        u_scr[b * pitch:b * pitch + tc, :] = u_ref[b]

    for s in range(n_slabs):
        r = _dot(u_scr[:, s * k_slab:(s + 1) * k_slab].astype(BF16), bbd_ref[s])
        for j in range(tiles_per_slab):
            qr = tile_rows(s * tiles_per_slab + j)
            sre_scr[qr] = r[:, j * LANES:(j + 1) * LANES]
            sim_scr[qr] = r[:, sw + j * LANES:sw + (j + 1) * LANES]

    for q0 in range(0, n_groups, groups_per_step):
        qs = list(range(q0, min(q0 + groups_per_step, n_groups)))
        a_re = [are_ref[q] for q in qs]
        a_im = [aim_ref[q] for q in qs]

        def step(t, carry, qs=qs, a_re=a_re, a_im=a_im):
            rows = pl.ds(t, SUBLANES, stride=pitch)
            out = []
            for n, q in enumerate(qs):
                s_re, s_im = carry[2 * n], carry[2 * n + 1]
                n_re = a_re[n] * s_re - a_im[n] * s_im + sre_scr[q, rows, :]
                n_im = a_re[n] * s_im + a_im[n] * s_re + sim_scr[q, rows, :]
                sre_scr[q, rows, :] = n_re
                sim_scr[q, rows, :] = n_im
                out += [n_re, n_im]
            return tuple(out)

        init = []
        for q in qs:
            init += [stre_scr[q], stim_scr[q]]
        fin = lax.fori_loop(0, tc, step, tuple(init), unroll=2)
        for n, q in enumerate(qs):
            stre_scr[q] = fin[2 * n]
            stim_scr[q] = fin[2 * n + 1]

    def state_rows(s):
        tiles = [tile_rows(s * tiles_per_slab + j) for j in range(tiles_per_slab)]
        return jnp.concatenate([sre_scr[qr].astype(BF16) for qr in tiles]
                               + [(-sim_scr[qr]).astype(BF16) for qr in tiles], axis=-1)

    for s in range(n_slabs):
        cols = slice(s * k_slab, (s + 1) * k_slab)
        y = _dot(state_rows(s), c_ref[s])
        y_scr[:, cols] = y + d_ref[:, cols] * u_scr[:, cols]
    y = _gelu_tanh(y_scr[...])
    gate = _dot(y.astype(BF16), wglu_ref[...]) + bglu_ref[...]
    y_scr[...] = y * jax.nn.sigmoid(gate)
    for b in range(nb):
        o_ref[b] = y_scr[b * pitch:b * pitch + tc, :].astype(o_ref.dtype)


def _pack_state_rows(a, nb):
    depth = a.shape[0]
    pack = SUBLANES // nb
    n_groups = a.shape[1] // (LANES * pack)
    a = a.reshape(depth, n_groups, pack, 1, LANES)
    return jnp.broadcast_to(a, (depth, n_groups, pack, nb, LANES)).reshape(
        depth, n_groups, SUBLANES, LANES)


def ssm(u3, bb_re, bb_im, c_re, c_im, a_re, a_im, d_skip, w_glu16, b_glu, layer):
    nb, seq_len, d_ssm = u3.shape
    n_groups = a_re.shape[1]
    n_slabs, k_slab, p_dim = bb_re.shape[1:]
    sw = c_re.shape[2]
    assert SUBLANES % nb == 0
    tc = _tile(seq_len, 128)
    pitch = tc + SUBLANES // 2
    assert (nb * pitch) % SUBLANES == 0
    kern = functools.partial(_ssm_kernel, pitch=pitch, groups_per_step=min(8, n_groups))

    def full(a):
        return _layer_spec(layer, a.shape[1:])

    return pl.pallas_call(
        kern,
        grid=(seq_len // tc,),
        in_specs=[pl.BlockSpec((nb, tc, d_ssm), lambda c: (0, c, 0)),
                  full(bb_re), full(bb_im), full(c_re), full(c_im), full(a_re), full(a_im),
                  full(d_skip), full(w_glu16), full(b_glu)],
        out_specs=pl.BlockSpec((nb, tc, d_ssm), lambda c: (0, c, 0)),
        out_shape=jax.ShapeDtypeStruct((nb, seq_len, d_ssm), BF16),
        scratch_shapes=[pltpu.VMEM((n_groups, SUBLANES * pitch, LANES), F32),
                        pltpu.VMEM((n_groups, SUBLANES * pitch, LANES), F32),
                        pltpu.VMEM((n_groups, SUBLANES, LANES), F32),
                        pltpu.VMEM((n_groups, SUBLANES, LANES), F32),
                        pltpu.VMEM((nb * pitch, d_ssm), F32),
                        pltpu.VMEM((nb * pitch, d_ssm), F32),
                        pltpu.VMEM((n_slabs, k_slab, 2 * sw), BF16),
                        pltpu.VMEM((n_slabs, 2 * sw, k_slab), BF16)],
        compiler_params=_params(("arbitrary",), 56),
        name="ssm_scan",
    )(u3, bb_re, bb_im, c_re, c_im, a_re, a_im, d_skip, w_glu16, b_glu)


def _mixer_out_kernel(*refs, n_experts):
    routed = n_experts is not None
    if routed:
        (yc_ref, ys_ref, wo_ref, x_ref, gt_ref, gpost_ref, gpre_ref, sc_ref,
         sh_ref, rw_ref, rb_ref, xo_ref, h_ref, route_i_ref, route_w_ref, cnt_ref,
         y_scr, carry_scr, earlier_scr) = refs
    else:
        (yc_ref, ys_ref, wo_ref, x_ref, gt_ref, gpost_ref, gpre_ref, sc_ref,
         sh_ref, xo_ref, h_ref, y_scr) = refs
    tm = yc_ref.shape[0]
    y_scr[...] = _dot(jnp.concatenate([yc_ref[...], ys_ref[...]], axis=-1), wo_ref[...])
    post_scale = gt_ref[...] * gpost_ref[...]
    pre_scale = gpre_ref[...] * (1.0 + sc_ref[...])
    for r0 in range(0, tm, EPILOGUE_ROWS):
        rows = slice(r0, r0 + EPILOGUE_ROWS)
        x_new = x_ref[rows, :] + _rms(y_scr[rows, :]) * post_scale
        xo_ref[rows, :] = x_new
        h_ref[rows, :] = (_rms(x_new) * pre_scale + sh_ref[...]).astype(h_ref.dtype)
    if routed:
        h = h_ref[...]
        w = rw_ref[...]
        h_hi = h.astype(BF16)
        w_hi = w.astype(BF16)
        h_lo = (h - h_hi.astype(F32)).astype(BF16)
        w_lo = (w - w_hi.astype(F32)).astype(BF16)
        hi = _dot(h_hi, jnp.concatenate([w_hi, w_lo], axis=-1))
        logits = hi[:, :LANES] + (hi[:, LANES:] + _dot(h_lo, w_hi)) + rb_ref[...]
        _route(logits, route_i_ref, route_w_ref, cnt_ref, carry_scr, earlier_scr, n_experts)


def mixer_out(y_conv, y_ssm, w_out16, x, mod5, g_post, g_pre, layer, router=None, *,
              seq_len):
    t, d = x.shape
    d_conv = y_conv.shape[1]
    d_ssm = y_ssm.shape[1]
    tm = _tile(seq_len, 512)
    tiles_per_seq = seq_len // tm
    routed = router is not None

    def batch_of(i):
        return i // tiles_per_seq

    def row_spec(n):
        return pl.BlockSpec((tm, n), lambda i: (i, 0))

    def const_spec(shape):
        return pl.BlockSpec(shape, lambda i: (0, 0))

    in_specs = [row_spec(d_conv), row_spec(d_ssm),
                _layer_spec(layer, (d_conv + d_ssm, d), pipeline_mode=pl.Buffered(1)),
                row_spec(d), _mod_spec(layer, GT_MIX, d, batch_of),
                _layer_spec(layer, (1, d)), _layer_spec(layer, (1, d)),
                _mod_spec(layer, SC_FFN, d, batch_of), _mod_spec(layer, SH_FFN, d, batch_of)]
    args = [y_conv, y_ssm, w_out16, x, mod5, g_post, g_pre, mod5, mod5]
    out_specs = [row_spec(d), row_spec(d)]
    out_shape = [jax.ShapeDtypeStruct((t, d), F32),
                 jax.ShapeDtypeStruct((t, d), F32 if routed else BF16)]
    scratch_shapes = [pltpu.VMEM((tm, d), F32)]
    n_experts = None
    if routed:
        rw_pad, rb_pad, n_experts = router
        in_specs += [const_spec(rw_pad.shape), const_spec(rb_pad.shape)]
        args += [rw_pad, rb_pad]
        out_specs += [row_spec(LANES), row_spec(LANES), const_spec((SUBLANES, LANES))]
        out_shape += [jax.ShapeDtypeStruct((t, LANES), jnp.int32),
                      jax.ShapeDtypeStruct((t, LANES), F32),
                      jax.ShapeDtypeStruct((SUBLANES, LANES), jnp.int32)]
        scratch_shapes += [pltpu.VMEM((SUBLANES, LANES), F32), pltpu.VMEM((tm, tm), BF16)]
    return pl.pallas_call(
        functools.partial(_mixer_out_kernel, n_experts=n_experts),
        grid=(t // tm,),
        in_specs=in_specs, out_specs=out_specs, out_shape=out_shape,
        scratch_shapes=scratch_shapes,
        compiler_params=_params(("arbitrary",), 56),
        name="mixer_out_routed" if routed else "mixer_out",
    )(*args)


def _ffn_kernel(h_ref, wg_ref, wu_ref, wd_ref, x_hbm, gt_ref, gpost_ref, o_ref,
                wg16, wu16, wd16, x_buf, sem, *, sub_rows):
    i = pl.program_id(0)
    k = pl.program_id(1)
    tm = o_ref.shape[0]
    x_copy = pltpu.make_async_copy(x_hbm.at[pl.ds(i * tm, tm)], x_buf, sem)

    @pl.when(k == 0)
    def _():
        o_ref[...] = jnp.zeros(o_ref.shape, F32)
        x_copy.start()

    wg16[...] = wg_ref[...].astype(BF16)
    wu16[...] = wu_ref[...].astype(BF16)
    wd16[...] = wd_ref[...].astype(BF16)
    for r0 in range(0, tm, sub_rows):
        h = h_ref[r0:r0 + sub_rows, :]
        gate = _dot(h, wg16[...])
        act = (gate * jax.nn.sigmoid(gate)) * _dot(h, wu16[...])
        o_ref[r0:r0 + sub_rows, :] += _dot(act.astype(BF16), wd16[...])

    @pl.when(k == pl.num_programs(1) - 1)
    def _():
        x_copy.wait()
        scale = gt_ref[...] * gpost_ref[...]
        for r0 in range(0, tm, FFN_EPILOGUE_ROWS):
            rows = slice(r0, r0 + FFN_EPILOGUE_ROWS)
            o_ref[rows, :] = x_buf[rows, :] + _rms(o_ref[rows, :]) * scale


def ffn_dense(h16, w_gate, w_up, w_down, x, mod5, g_post, layer, ffn_index, *, seq_len):
    t, d = x.shape
    d_ff = w_gate.shape[2]
    tm = _tile(seq_len, 1024)
    sub_rows = _tile(tm, 512)
    tk = _tile(d_ff, 256)
    tiles_per_seq = seq_len // tm
    return pl.pallas_call(
        functools.partial(_ffn_kernel, sub_rows=sub_rows),
        grid=(t // tm, d_ff // tk),
        in_specs=[
            pl.BlockSpec((tm, d), lambda i, k: (i, 0)),
            pl.BlockSpec((None, d, tk), lambda i, k: (ffn_index, 0, k)),
            pl.BlockSpec((None, d, tk), lambda i, k: (ffn_index, 0, k)),
            pl.BlockSpec((None, tk, d), lambda i, k: (ffn_index, k, 0)),
            pl.BlockSpec(memory_space=pl.ANY),
            _mod_spec(layer, GT_FFN, d, lambda i: i // tiles_per_seq),
            _layer_spec(layer, (1, d)),
        ],
        out_specs=pl.BlockSpec((tm, d), lambda i, k: (i, 0)),
        out_shape=jax.ShapeDtypeStruct((t, d), F32),
        scratch_shapes=[pltpu.VMEM((d, tk), BF16), pltpu.VMEM((d, tk), BF16),
                        pltpu.VMEM((tk, d), BF16), pltpu.VMEM((tm, d), F32),
                        pltpu.SemaphoreType.DMA(())],
        compiler_params=_params(("arbitrary", "arbitrary"), 56),
        name="ffn_dense",
    )(h16, w_gate, w_up, w_down, x, mod5, g_post)


def _route(logits, ri_ref, rw_ref, cnt_ref, carry_scr, earlier_scr, n_experts):
    tm = logits.shape[0]

    @pl.when(pl.program_id(0) == 0)
    def _():
        carry_scr[...] = jnp.zeros(carry_scr.shape, F32)
        r = lax.broadcasted_iota(jnp.int32, (tm, tm), 0)
        c = lax.broadcasted_iota(jnp.int32, (tm, tm), 1)
        earlier_scr[...] = jnp.where(c < r, 1.0, 0.0).astype(BF16)

    lane = lax.broadcasted_iota(jnp.int32, (tm, LANES), 1)
    neg = jnp.float32(-jnp.inf)
    logit = jnp.where(lane < n_experts, logits, neg)
    m1 = jnp.max(logit, axis=-1, keepdims=True)
    i1 = jnp.min(jnp.where(logit == m1, lane, LANES), axis=-1, keepdims=True)
    rest = jnp.where(lane == i1, neg, logit)
    m2 = jnp.max(rest, axis=-1, keepdims=True)
    i2 = jnp.min(jnp.where(rest == m2, lane, LANES), axis=-1, keepdims=True)
    e2 = jnp.exp(m2 - m1)
    w1 = 1.0 / (1.0 + e2)
    w2 = e2 / (1.0 + e2)
    hit1 = lane == i1
    hit2 = lane == i2
    onehot = jnp.where(hit1 | hit2, 1.0, 0.0)
    carry = carry_scr[0:1, :]
    before = _dot(earlier_scr[...], onehot.astype(BF16)) + carry
    pos1 = jnp.sum(jnp.where(hit1, before, 0.0), axis=-1, keepdims=True).astype(jnp.int32)
    pos2 = jnp.sum(jnp.where(hit2, before, 0.0), axis=-1, keepdims=True).astype(jnp.int32)
    total = carry + jnp.sum(onehot, axis=0, keepdims=True)
    carry_scr[...] = jnp.broadcast_to(total, carry_scr.shape)
    cnt_ref[...] = jnp.broadcast_to(total, cnt_ref.shape).astype(jnp.int32)
    zero_i = jnp.zeros((tm, LANES), jnp.int32)
    ri_ref[...] = jnp.where(lane == 0, i1, jnp.where(lane == 1, i2,
                            jnp.where(lane == 2, pos1, jnp.where(lane == 3, pos2, zero_i))))
    rw_ref[...] = jnp.where(lane == 0, w1, jnp.where(lane == 1, w2, 0.0))


def _moe_ffn_kernel(ge_ref, nr_ref, nu_ref, tok_ref, h_hbm, wg_ref, wu_ref, wd_ref, o_ref,
                    h_scr, gbuf, wg16, wu16, wd16, sem, *, rows_per_step):
    g = pl.program_id(0)
    k = pl.program_id(1)
    n_used = nu_ref[0]
    n_sub_max, ts, _ = h_scr.shape
    group_rows = n_sub_max * ts
    nk = pl.num_programs(1)
    n_gather = gbuf.shape[0]

    def row_copy(src_row, dst_row):
        return pltpu.make_async_copy(h_hbm.at[pl.ds(src_row, 1)],
                                     gbuf.at[pl.ds(dst_row, 1)], sem)

    def for_each_sub_tile(group, fn):
        for j in range(n_sub_max):
            @pl.when(j * ts < nr_ref[group])
            def _(j=j):
                fn(j)

    def wait_rows():
        pltpu.make_async_copy(h_hbm.at[pl.ds(0, n_gather)], gbuf, sem).wait()

    @pl.when(k == 0)
    def _():
        o_ref[...] = jnp.zeros(o_ref.shape, o_ref.dtype)

    @pl.when(g < n_used)
    def _():
        @pl.when(k == 0)
        def _():
            @pl.when(g == 0)
            def _():
                def body(i, carry):
                    row_copy(tok_ref[i], i).start()
                    return carry
                lax.fori_loop(0, n_gather, body, 0, unroll=8)

            wait_rows()

            def cast(j):
                h_scr[j] = gbuf[j * ts:(j + 1) * ts, :].astype(BF16)
            for_each_sub_tile(g, cast)

        next_base = jnp.minimum(g + 1, n_used - 1) * group_rows + k * rows_per_step

        def sub_tile(j, m):
            if j == 0:
                wg16[...] = wg_ref[...].astype(BF16)
                wu16[...] = wu_ref[...].astype(BF16)
                wd16[...] = wd_ref[...].astype(BF16)
                for i in range(rows_per_step):
                    row_copy(tok_ref[next_base + i], k * rows_per_step + i).start()
            h = h_scr[j, 0:m, :]
            gate = _dot(h, wg16[...])
            act = (gate * jax.nn.sigmoid(gate)) * _dot(h, wu16[...])
            o_ref[j * ts:j * ts + m, :] += _dot(act.astype(BF16), wd16[...])

        half = ts // 2
        all_full = nr_ref[g] > group_rows - half

        @pl.when(all_full)
        def _():
            for j in range(n_sub_max):
                sub_tile(j, ts)

        @pl.when(jnp.logical_not(all_full))
        def _():
            for j in range(n_sub_max):
                rows_j = nr_ref[g] - j * ts

                @pl.when(rows_j > half)
                def _(j=j):
                    sub_tile(j, ts)

                @pl.when((rows_j > 0) & (rows_j <= half))
                def _(j=j):
                    sub_tile(j, half)

        @pl.when((g == n_used - 1) & (k == nk - 1))
        def _():
            wait_rows()


def moe_ffn(group_expert, group_rows_used, n_used, row_tok, h, w_gate, w_up, w_down, layer, *,
            sub_rows, n_sub_max):
    d = h.shape[1]
    d_ff = w_gate.shape[3]
    tk = _tile(d_ff, 256)
    nk = d_ff // tk
    group_rows = sub_rows * n_sub_max
    rows_per_step = -(-group_rows // (nk * SUBLANES)) * SUBLANES
    n_gather = rows_per_step * nk
    n_rows = group_expert.shape[0] * group_rows
    assert row_tok.shape[0] - n_rows >= n_gather - group_rows

    def ff_idx(g, k, nu):
        return jnp.where(g < nu[0], k, nk - 1)

    return pl.pallas_call(
        functools.partial(_moe_ffn_kernel, rows_per_step=rows_per_step),
        grid_spec=pltpu.PrefetchScalarGridSpec(
            num_scalar_prefetch=4,
            grid=(n_rows // group_rows, nk),
            in_specs=[
                pl.BlockSpec(memory_space=pl.ANY),
                pl.BlockSpec((None, None, d, tk),
                             lambda g, k, ge, ns, nu, tok: (layer, ge[g], 0, ff_idx(g, k, nu))),
                pl.BlockSpec((None, None, d, tk),
                             lambda g, k, ge, ns, nu, tok: (layer, ge[g], 0, ff_idx(g, k, nu))),
                pl.BlockSpec((None, None, tk, d),
                             lambda g, k, ge, ns, nu, tok: (layer, ge[g], ff_idx(g, k, nu), 0)),
            ],
            out_specs=pl.BlockSpec((group_rows, d), lambda g, k, ge, ns, nu, tok: (g, 0)),
            scratch_shapes=[pltpu.VMEM((n_sub_max, sub_rows, d), BF16),
                            pltpu.VMEM((n_gather, d), F32),
                            pltpu.VMEM((d, tk), BF16), pltpu.VMEM((d, tk), BF16),
                            pltpu.VMEM((tk, d), BF16),
                            pltpu.SemaphoreType.DMA(())],
        ),
        out_shape=jax.ShapeDtypeStruct((n_rows, d), F32),
        compiler_params=_params(("arbitrary", "arbitrary"), 56),
        name="moe_ffn",
    )(group_expert, group_rows_used, n_used, row_tok, h, w_gate, w_up, w_down)


def _moe_combine_kernel(d0_ref, d1_ref, ys_ref, rw_ref, x_ref, gt_ref, gpost_ref, o_ref,
                        buf, sems):
    i = pl.program_id(0)
    tm = x_ref.shape[0]
    slot = i % 2

    def start_tile(tile, s):
        def body(t, carry):
            for which, dest in enumerate((d0_ref, d1_ref)):
                pltpu.make_async_copy(ys_ref.at[pl.ds(dest[tile * tm + t], 1)],
                                      buf.at[s, pl.ds(which * tm + t, 1)],
                                      sems.at[s]).start()
            return carry
        lax.fori_loop(0, tm, body, 0, unroll=8)

    @pl.when(i == 0)
    def _():
        start_tile(0, 0)

    @pl.when(i + 1 < pl.num_programs(0))
    def _():
        start_tile(i + 1, 1 - slot)

    pltpu.make_async_copy(ys_ref.at[pl.ds(0, 2 * tm)], buf.at[slot], sems.at[slot]).wait()
    scale = gt_ref[...] * gpost_ref[...]
    for r0 in range(0, tm, EPILOGUE_ROWS):
        rows = slice(r0, r0 + EPILOGUE_ROWS)
        y = (buf[slot, r0:r0 + EPILOGUE_ROWS, :] * rw_ref[rows, 0:1]
             + buf[slot, tm + r0:tm + r0 + EPILOGUE_ROWS, :] * rw_ref[rows, 1:2])
        o_ref[rows, :] = x_ref[rows, :] + _rms(y) * scale


def moe_combine(dest0, dest1, ys, route_w, x, mod5, g_post, layer, *, seq_len):
    t, d = x.shape
    tm = _tile(seq_len, 256)
    tiles_per_seq = seq_len // tm
    return pl.pallas_call(
        _moe_combine_kernel,
        grid_spec=pltpu.PrefetchScalarGridSpec(
            num_scalar_prefetch=2,
            grid=(t // tm,),
            in_specs=[
                pl.BlockSpec(memory_space=pl.ANY),
                pl.BlockSpec((tm, LANES), lambda i, d0, d1: (i, 0)),
                pl.BlockSpec((tm, d), lambda i, d0, d1: (i, 0)),
                _mod_spec(layer, GT_FFN, d, lambda i: i // tiles_per_seq),
                _layer_spec(layer, (1, d)),
            ],
            out_specs=pl.BlockSpec((tm, d), lambda i, d0, d1: (i, 0)),
            scratch_shapes=[pltpu.VMEM((2, 2 * tm, d), F32), pltpu.SemaphoreType.DMA((2,))],
        ),
        out_shape=jax.ShapeDtypeStruct((t, d), F32),
        compiler_params=_params(("arbitrary",), 48),
        name="moe_combine",
    )(dest0, dest1, ys, route_w, x, mod5, g_post)


MOE_SUB_ROWS = 512
MOE_GROUP_SUBS = 2


def _row_tok_kernel(d0_ref, d1_ref, o_ref):
    def clear(r, carry):
        o_ref[r] = 0
        return carry
    lax.fori_loop(0, o_ref.shape[0], clear, 0, unroll=8)

    def place(t, carry):
        o_ref[d0_ref[t]] = t
        o_ref[d1_ref[t]] = t
        return carry
    lax.fori_loop(0, d0_ref.shape[0], place, 0, unroll=8)


def row_tokens(dest0, dest1, n_rows):
    smem = pl.BlockSpec(memory_space=pltpu.SMEM)
    return pl.pallas_call(
        _row_tok_kernel,
        in_specs=[smem, smem],
        out_specs=smem,
        out_shape=jax.ShapeDtypeStruct((n_rows,), jnp.int32),
        name="row_tokens",
    )(dest0, dest1)


def moe_layer(h, route_i, route_w, counts, x, mod5, g_post, w_gate, w_up, w_down, layer,
              moe_index, *, seq_len):
    t, d = h.shape
    n_experts = w_gate.shape[1]
    ts = min(MOE_SUB_ROWS, t)
    group_rows = ts * MOE_GROUP_SUBS
    e0, e1, pos0, pos1 = (route_i[:, k] for k in range(4))
    counts = counts[0, :n_experts]
    groups_e = (counts + group_rows - 1) // group_rows
    g_end = jnp.cumsum(groups_e)
    g_start = g_end - groups_e
    dest0 = g_start[e0] * group_rows + pos0
    dest1 = g_start[e1] * group_rows + pos1
    n_groups = -(-(2 * t) // group_rows) + n_experts
    n_rows = n_groups * group_rows
    row_tok = row_tokens(dest0, dest1, n_rows + group_rows)
    n_used = g_end[-1].astype(jnp.int32)
    gid = jnp.arange(n_groups, dtype=jnp.int32)
    group_e = jnp.clip(jnp.searchsorted(g_end, gid, side='right'), 0, n_experts - 1)
    group_e = jnp.where(gid < n_used, group_e, group_e[n_used - 1]).astype(jnp.int32)
    group_rows_used = jnp.where(
        gid < n_used,
        jnp.clip(counts[group_e] - (gid - g_start[group_e]) * group_rows, 0, group_rows),
        0).astype(jnp.int32)
    ys = moe_ffn(group_e, group_rows_used, n_used.reshape(1), row_tok, h, w_gate, w_up, w_down,
                 moe_index, sub_rows=ts, n_sub_max=MOE_GROUP_SUBS)
    return moe_combine(dest0, dest1, ys, route_w, x, mod5, g_post, layer, seq_len=seq_len)


def kernel(x, c, w_ada, b_ada, g_mix_pre, g_mix_post, g_ffn_pre, g_ffn_post, w_in, conv_w,
           ssm_a_re, ssm_a_im, ssm_log_dt, ssm_b_re, ssm_b_im, ssm_c_re, ssm_c_im, ssm_d,
           w_glu, b_glu, w_out, ffn_w_gate, ffn_w_up, ffn_w_down, router_w, router_b,
           moe_w_gate, moe_w_up, moe_w_down):
    nb, seq_len, d = x.shape
    depth = w_ada.shape[0]
    g, p, h_dim = ssm_b_re.shape[1:]
    d_ssm = g * h_dim
    n_experts = router_w.shape[2]
    t = nb * seq_len
    groups_per_slab = max(1, min(g, MXU_DIM // h_dim))

    mod_rows = 2 * SUBLANES
    c_pad = jnp.zeros((mod_rows, d), F32).at[:nb].set(c)
    mod = adaln_mod(c_pad, w_ada, b_ada)
    mod5 = mod.reshape(depth, mod_rows, 6, 1, d)
    xf = x.reshape(t, d)

    def rows(v):
        return v.reshape(v.shape[0], 1, v.shape[1])
    g_mix_pre, g_mix_post, g_ffn_pre, g_ffn_post = (
        rows(v) for v in (g_mix_pre, g_mix_post, g_ffn_pre, g_ffn_post))
    w_in16 = w_in.astype(BF16)
    w_out16 = w_out.astype(BF16)
    w_glu16 = w_glu.astype(BF16)
    conv_wt = jnp.swapaxes(conv_w, 1, 2)
    bb_re, bb_im, ab_re, ab_im = ssm_prep(ssm_a_re, ssm_a_im, ssm_log_dt, ssm_b_re, ssm_b_im)
    def stacked_t(w):
        return _group_rows(jnp.swapaxes(w, 2, 3), groups_per_slab)
    bbs_re, bbs_im = stacked_t(bb_re), stacked_t(bb_im)
    cs_re, cs_im = stacked_t(ssm_c_re), stacked_t(ssm_c_im)
    a_re = _pack_state_rows(ab_re.reshape(depth, g * p), nb)
    a_im = _pack_state_rows(ab_im.reshape(depth, g * p), nb)
    d_skip = ssm_d.reshape(depth, 1, d_ssm)
    b_glu = rows(b_glu)
    n_moe = router_w.shape[0]
    rw_pad = jnp.zeros((n_moe, d, LANES), F32).at[:, :, :n_experts].set(router_w)
    rb_pad = jnp.zeros((n_moe, 1, LANES), F32).at[:, 0, :n_experts].set(router_b)

    for l in range(depth):
        y_conv, u = mixer_in(xf, mod5, g_mix_pre, w_in16, conv_wt, l, seq_len=seq_len)
        y_ssm = ssm(u.reshape(nb, seq_len, d_ssm), bbs_re, bbs_im, cs_re, cs_im, a_re, a_im, d_skip,
                    w_glu16, b_glu, l).reshape(t, d_ssm)
        routed = l % 2 == 1
        i = l // 2
        rt = (rw_pad[i], rb_pad[i], n_experts) if routed else None
        outs = mixer_out(y_conv, y_ssm, w_out16, xf, mod5, g_mix_post, g_ffn_pre, l, rt,
                         seq_len=seq_len)
        if routed:
            xf, h, route_i, route_w, counts = outs
            xf = moe_layer(h, route_i, route_w, counts, xf, mod5, g_ffn_post,
                           moe_w_gate, moe_w_up, moe_w_down, l, i, seq_len=seq_len)
        else:
            xf, h = outs
            xf = ffn_dense(h, ffn_w_gate, ffn_w_up, ffn_w_down, xf, mod5, g_ffn_post, l, i,
                           seq_len=seq_len)
    return xf.reshape(nb, seq_len, d)
```

```python
import functools

import jax
import jax.numpy as jnp
from jax import lax
from jax.experimental import pallas as pl
from jax.experimental.pallas import tpu as pltpu

F32 = jnp.float32
BF16 = jnp.bfloat16
NORM_EPS = 1e-6
LANES = 128
SUBLANES = 8
MXU_DIM = 256
EPILOGUE_ROWS = 64
FFN_EPILOGUE_ROWS = 16
MIB = 1024 * 1024


def _params(semantics, vmem_mib):
    return pltpu.CompilerParams(dimension_semantics=semantics,
                                vmem_limit_bytes=vmem_mib * MIB)


def _dot(a, b):
    return jnp.dot(a, b, preferred_element_type=F32)


def _rms(x):
    return x * lax.rsqrt(jnp.mean(x * x, axis=-1, keepdims=True) + NORM_EPS)


def _tile(n, want):
    t = min(n, want)
    while n % t:
        t -= 1
    return t


def _adaln_kernel(c_ref, w_ref, b_ref, o_ref):
    c = c_ref[...]
    c_act = c * jax.nn.sigmoid(c)
    c_hi = c_act.astype(BF16)
    c_lo = (c_act - c_hi.astype(F32)).astype(BF16)
    w16 = w_ref[...].astype(BF16)
    o_ref[...] = (_dot(c_hi, w16) + _dot(c_lo, w16)) + b_ref[...]


def adaln_mod(c_pad, w_ada, b_ada):
    depth, d, n = w_ada.shape
    rows = c_pad.shape[0]
    tn = _tile(n, 1024)
    return pl.pallas_call(
        _adaln_kernel,
        grid=(depth, n // tn),
        in_specs=[
            pl.BlockSpec((rows, d), lambda l, j: (0, 0)),
            pl.BlockSpec((None, d, tn), lambda l, j: (l, 0, j)),
            pl.BlockSpec((None, 1, tn), lambda l, j: (l, 0, j)),
        ],
        out_specs=pl.BlockSpec((None, rows, tn), lambda l, j: (l, 0, j)),
        out_shape=jax.ShapeDtypeStruct((depth, rows, n), F32),
        compiler_params=_params(("arbitrary", "arbitrary"), 56),
        name="adaln_mod",
    )(c_pad, w_ada, b_ada.reshape(depth, 1, n))


def _mixer_in_kernel(x_ref, sc_ref, sh_ref, g_ref, w_ref, cw_ref, yconv_ref, u_ref,
                     halo_scr, h_scr, *, tiles_per_seq, col_chunk):
    tm = x_ref.shape[0]
    d_conv = yconv_ref.shape[1]
    d_ssm = u_ref.shape[1]
    pre_scale = g_ref[...] * (1.0 + sc_ref[...])
    for r0 in range(0, tm, EPILOGUE_ROWS):
        rows = slice(r0, r0 + EPILOGUE_ROWS)
        h_scr[rows, :] = (_rms(x_ref[rows, :]) * pre_scale + sh_ref[...]).astype(BF16)
    h = h_scr[...]

    @pl.when(pl.program_id(0) % tiles_per_seq == 0)
    def _():
        halo_scr[...] = jnp.zeros(halo_scr.shape, F32)

    for c0 in range(0, d_conv, col_chunk):
        cols = slice(c0, c0 + col_chunk)
        gate_b = _dot(h, w_ref[:, c0:c0 + col_chunk])
        z = (_dot(h, w_ref[:, d_conv + c0:d_conv + c0 + col_chunk])
             * _dot(h, w_ref[:, 2 * d_conv + c0:2 * d_conv + c0 + col_chunk]))
        prev = halo_scr[:, cols]
        halo_scr[:, cols] = z[tm - SUBLANES:, :]
        w0 = cw_ref[0:1, cols]
        w1 = cw_ref[1:2, cols]
        w2 = cw_ref[2:3, cols]
        z1 = pltpu.roll(z, 1, 0)
        z2 = pltpu.roll(z, 2, 0)
        yconv_ref[:, cols] = (gate_b * (z * w2 + z1 * w1 + z2 * w0)).astype(yconv_ref.dtype)
        row = lax.broadcasted_iota(jnp.int32, (SUBLANES, col_chunk), 0)
        z1t = jnp.where(row < 1, pltpu.roll(prev, 1, 0), z1[:SUBLANES, :])
        z2t = jnp.where(row < 2, pltpu.roll(prev, 2, 0), z2[:SUBLANES, :])
        top = gate_b[:SUBLANES, :] * (z[:SUBLANES, :] * w2 + z1t * w1 + z2t * w0)
        yconv_ref[0:SUBLANES, cols] = top.astype(yconv_ref.dtype)

    for c0 in range(0, d_ssm, col_chunk):
        u_ref[:, c0:c0 + col_chunk] = _dot(
            h, w_ref[:, 3 * d_conv + c0:3 * d_conv + c0 + col_chunk])


SH_MIX, SC_MIX, GT_MIX, SH_FFN, SC_FFN, GT_FFN = range(6)


def _mod_spec(layer, chunk, d, batch_of):
    return pl.BlockSpec((None, None, None, 1, d),
                        lambda i, *_: (layer, batch_of(i), chunk, 0, 0))


def _layer_spec(layer, shape, **kw):
    zeros = (0,) * len(shape)
    return pl.BlockSpec((None,) + tuple(shape), lambda *_: (layer,) + zeros, **kw)


def mixer_in(x, mod5, g_pre, w_in16, conv_wt, layer, *, seq_len):
    t, d = x.shape
    d_in = w_in16.shape[2]
    d_conv = conv_wt.shape[2]
    d_ssm = d_in - 3 * d_conv
    tm = _tile(seq_len, 512)
    col_chunk = _tile(min(d_conv, d_ssm), 512)
    assert d_conv % col_chunk == 0 and d_ssm % col_chunk == 0
    tiles_per_seq = seq_len // tm

    def batch_of(i):
        return i // tiles_per_seq

    kern = functools.partial(_mixer_in_kernel, tiles_per_seq=tiles_per_seq,
                             col_chunk=col_chunk)
    return pl.pallas_call(
        kern,
        grid=(t // tm,),
        in_specs=[
            pl.BlockSpec((tm, d), lambda i: (i, 0)),
            _mod_spec(layer, SC_MIX, d, batch_of), _mod_spec(layer, SH_MIX, d, batch_of),
            _layer_spec(layer, (1, d)),
            _layer_spec(layer, (d, d_in), pipeline_mode=pl.Buffered(1)),
            _layer_spec(layer, (3, d_conv)),
        ],
        out_specs=[
            pl.BlockSpec((tm, d_conv), lambda i: (i, 0)),
            pl.BlockSpec((tm, d_ssm), lambda i: (i, 0)),
        ],
        out_shape=[jax.ShapeDtypeStruct((t, d_conv), BF16),
                   jax.ShapeDtypeStruct((t, d_ssm), F32)],
        scratch_shapes=[pltpu.VMEM((SUBLANES, d_conv), F32), pltpu.VMEM((tm, d), BF16)],
        compiler_params=_params(("arbitrary",), 48),
        name="mixer_in",
    )(x, mod5, mod5, g_pre, w_in16, conv_wt)


def _ssm_prep_kernel(are_ref, aim_ref, ldt_ref, bre_ref, bim_ref,
                     bbre_ref, bbim_ref, abre_ref, abim_ref):
    lam_re = are_ref[...]
    lam_im = aim_ref[...]
    dt = jnp.exp(ldt_ref[...])
    mag = jnp.exp(lam_re * dt)
    ang = lam_im * dt
    ab_re = mag * jnp.cos(ang)
    ab_im = mag * jnp.sin(ang)
    den = lam_re * lam_re + lam_im * lam_im
    nr = ab_re - 1.0
    q_re = (nr * lam_re + ab_im * lam_im) / den
    q_im = (ab_im * lam_re - nr * lam_im) / den
    b_re = bre_ref[...]
    b_im = bim_ref[...]
    bbre_ref[...] = q_re * b_re - q_im * b_im
    bbim_ref[...] = q_re * b_im + q_im * b_re
    abre_ref[...] = ab_re
    abim_ref[...] = ab_im


def ssm_prep(a_re, a_im, log_dt, b_re, b_im):
    depth, g, p, h = b_re.shape
    ldt = jnp.broadcast_to(log_dt.reshape(depth, g, 1, 1), (depth, g, p, 1))

    def spec(last):
        return pl.BlockSpec((None, g, p, last), lambda l: (l, 0, 0, 0))

    return pl.pallas_call(
        _ssm_prep_kernel,
        grid=(depth,),
        in_specs=[spec(1), spec(1), spec(1), spec(h), spec(h)],
        out_specs=[spec(h), spec(h), spec(1), spec(1)],
        out_shape=[jax.ShapeDtypeStruct((depth, g, p, h), F32),
                   jax.ShapeDtypeStruct((depth, g, p, h), F32),
                   jax.ShapeDtypeStruct((depth, g, p, 1), F32),
                   jax.ShapeDtypeStruct((depth, g, p, 1), F32)],
        compiler_params=_params(("arbitrary",), 56),
        name="ssm_prep",
    )(a_re.reshape(depth, g, p, 1), a_im.reshape(depth, g, p, 1), ldt, b_re, b_im)


def _group_rows(w, groups_per_slab):
    depth, g, r, c = w.shape
    return w.reshape(depth, g // groups_per_slab, groups_per_slab * r, c)


def _block_diag(stacked, block_rows):
    n_rows, c = stacked.shape
    n_cols = (n_rows // block_rows) * c
    src = lax.broadcasted_iota(jnp.int32, (c, n_cols), 0)
    dst = lax.broadcasted_iota(jnp.int32, (c, n_cols), 1)
    repeat = jnp.where(dst % c == src, 1.0, 0.0).astype(BF16)
    tiled = _dot(stacked.astype(BF16), repeat)
    row = lax.broadcasted_iota(jnp.int32, (n_rows, n_cols), 0)
    col = lax.broadcasted_iota(jnp.int32, (n_rows, n_cols), 1)
    return jnp.where(row // block_rows == col // c, tiled, 0.0).astype(BF16)


def _gelu_tanh(x):
    return 0.5 * x * (1.0 + jnp.tanh(0.7978845608028654 * (x + 0.044715 * x * x * x)))


def _ssm_kernel(u_ref, bre_ref, bim_ref, cre_ref, cim_ref, are_ref, aim_ref, d_ref, wglu_ref,
                bglu_ref, o_ref, sre_scr, sim_scr, stre_scr, stim_scr, u_scr, y_scr,
                bbd_ref, c_ref, *, pitch, groups_per_step):
    nb, tc, d_ssm = u_ref.shape
    n_slabs, k_slab, two_sw = bbd_ref.shape
    sw = two_sw // 2
    tiles_per_slab = sw // LANES
    pack = SUBLANES // nb
    n_groups = sre_scr.shape[0]
    rows_all = nb * pitch

    def tile_rows(lane_tile):
        q, h = divmod(lane_tile, pack)
        return q, slice(h * rows_all, (h + 1) * rows_all)

    @pl.when(pl.program_id(0) == 0)
    def _():
        stre_scr[...] = jnp.zeros(stre_scr.shape, F32)
        stim_scr[...] = jnp.zeros(stim_scr.shape, F32)
        u_scr[...] = jnp.zeros(u_scr.shape, F32)
        p_dim = bre_ref.shape[2]
        h_dim = k_slab // (sw // p_dim)
        for s in range(n_slabs):
            bbd_ref[s, :, 0:sw] = _block_diag(bre_ref[s], h_dim)
            bbd_ref[s, :, sw:2 * sw] = _block_diag(bim_ref[s], h_dim)
            c_ref[s, 0:sw, :] = _block_diag(cre_ref[s], p_dim)
            c_ref[s, sw:2 * sw, :] = _block_diag(cim_ref[s], p_dim)

    for b in range(nb):
        u_scr[b * pitch:b * pitch + tc, :] = u_ref[b]

    for s in range(n_slabs):
        r = _dot(u_scr[:, s * k_slab:(s + 1) * k_slab].astype(BF16), bbd_ref[s])
        for j in range(tiles_per_slab):
            qr = tile_rows(s * tiles_per_slab + j)
            sre_scr[qr] = r[:, j * LANES:(j + 1) * LANES]
            sim_scr[qr] = r[:, sw + j * LANES:sw + (j + 1) * LANES]

    for q0 in range(0, n_groups, groups_per_step):
        qs = list(range(q0, min(q0 + groups_per_step, n_groups)))
        a_re = [are_ref[q] for q in qs]
        a_im = [aim_ref[q] for q in qs]

        def step(t, carry, qs=qs, a_re=a_re, a_im=a_im):
            rows = pl.ds(t, SUBLANES, stride=pitch)
            out = []
            for n, q in enumerate(qs):
                s_re, s_im = carry[2 * n], carry[2 * n + 1]
                n_re = a_re[n] * s_re - a_im[n] * s_im + sre_scr[q, rows, :]
                n_im = a_re[n] * s_im + a_im[n] * s_re + sim_scr[q, rows, :]
                sre_scr[q, rows, :] = n_re
                sim_scr[q, rows, :] = n_im
                out += [n_re, n_im]
            return tuple(out)

        init = []
        for q in qs:
            init += [stre_scr[q], stim_scr[q]]
        fin = lax.fori_loop(0, tc, step, tuple(init), unroll=2)
        for n, q in enumerate(qs):
            stre_scr[q] = fin[2 * n]
            stim_scr[q] = fin[2 * n + 1]

    def state_rows(s):
        tiles = [tile_rows(s * tiles_per_slab + j) for j in range(tiles_per_slab)]
        return jnp.concatenate([sre_scr[qr].astype(BF16) for qr in tiles]
                               + [(-sim_scr[qr]).astype(BF16) for qr in tiles], axis=-1)

    for s in range(n_slabs):
        cols = slice(s * k_slab, (s + 1) * k_slab)
        y = _dot(state_rows(s), c_ref[s])
        y_scr[:, cols] = y + d_ref[:, cols] * u_scr[:, cols]
    y = _gelu_tanh(y_scr[...])
    gate = _dot(y.astype(BF16), wglu_ref[...]) + bglu_ref[...]
    y_scr[...] = y * jax.nn.sigmoid(gate)
    for b in range(nb):
        o_ref[b] = y_scr[b * pitch:b * pitch + tc, :].astype(o_ref.dtype)


def _pack_state_rows(a, nb):
    depth = a.shape[0]
    pack = SUBLANES // nb
    n_groups = a.shape[1] // (LANES * pack)
    a = a.reshape(depth, n_groups, pack, 1, LANES)
    return jnp.broadcast_to(a, (depth, n_groups, pack, nb, LANES)).reshape(
        depth, n_groups, SUBLANES, LANES)


def ssm(u3, bb_re, bb_im, c_re, c_im, a_re, a_im, d_skip, w_glu16, b_glu, layer):
    nb, seq_len, d_ssm = u3.shape
    n_groups = a_re.shape[1]
    n_slabs, k_slab, p_dim = bb_re.shape[1:]
    sw = c_re.shape[2]
    assert SUBLANES % nb == 0
    tc = _tile(seq_len, 128)
    pitch = tc + SUBLANES // 2
    assert (nb * pitch) % SUBLANES == 0
    kern = functools.partial(_ssm_kernel, pitch=pitch, groups_per_step=min(8, n_groups))

    def full(a):
        return _layer_spec(layer, a.shape[1:])

    return pl.pallas_call(
        kern,
        grid=(seq_len // tc,),
        in_specs=[pl.BlockSpec((nb, tc, d_ssm), lambda c: (0, c, 0)),
                  full(bb_re), full(bb_im), full(c_re), full(c_im), full(a_re), full(a_im),
                  full(d_skip), full(w_glu16), full(b_glu)],
        out_specs=pl.BlockSpec((nb, tc, d_ssm), lambda c: (0, c, 0)),
        out_shape=jax.ShapeDtypeStruct((nb, seq_len, d_ssm), BF16),
        scratch_shapes=[pltpu.VMEM((n_groups, SUBLANES * pitch, LANES), F32),
                        pltpu.VMEM((n_groups, SUBLANES * pitch, LANES), F32),
                        pltpu.VMEM((n_groups, SUBLANES, LANES), F32),
                        pltpu.VMEM((n_groups, SUBLANES, LANES), F32),
                        pltpu.VMEM((nb * pitch, d_ssm), F32),
                        pltpu.VMEM((nb * pitch, d_ssm), F32),
                        pltpu.VMEM((n_slabs, k_slab, 2 * sw), BF16),
                        pltpu.VMEM((n_slabs, 2 * sw, k_slab), BF16)],
        compiler_params=_params(("arbitrary",), 56),
        name="ssm_scan",
    )(u3, bb_re, bb_im, c_re, c_im, a_re, a_im, d_skip, w_glu16, b_glu)


def _mixer_out_kernel(*refs, n_experts):
    routed = n_experts is not None
    if routed:
        (yc_ref, ys_ref, wo_ref, x_ref, gt_ref, gpost_ref, gpre_ref, sc_ref,
         sh_ref, rw_ref, rb_ref, xo_ref, h_ref, route_i_ref, route_w_ref, cnt_ref,
         y_scr, carry_scr) = refs
    else:
        (yc_ref, ys_ref, wo_ref, x_ref, gt_ref, gpost_ref, gpre_ref, sc_ref,
         sh_ref, xo_ref, h_ref, y_scr) = refs
    tm = yc_ref.shape[0]
    y_scr[...] = _dot(jnp.concatenate([yc_ref[...], ys_ref[...]], axis=-1), wo_ref[...])
    post_scale = gt_ref[...] * gpost_ref[...]
    pre_scale = gpre_ref[...] * (1.0 + sc_ref[...])
    for r0 in range(0, tm, EPILOGUE_ROWS):
        rows = slice(r0, r0 + EPILOGUE_ROWS)
        x_new = x_ref[rows, :] + _rms(y_scr[rows, :]) * post_scale
        xo_ref[rows, :] = x_new
        h_ref[rows, :] = (_rms(x_new) * pre_scale + sh_ref[...]).astype(h_ref.dtype)
    if routed:
        h = h_ref[...]
        w = rw_ref[...]
        h_hi = h.astype(BF16)
        w_hi = w.astype(BF16)
        h_lo = (h - h_hi.astype(F32)).astype(BF16)
        w_lo = (w - w_hi.astype(F32)).astype(BF16)
        logits = _dot(h_hi, w_hi) + (_dot(h_hi, w_lo) + _dot(h_lo, w_hi)) + rb_ref[...]
        _route(logits, route_i_ref, route_w_ref, cnt_ref, carry_scr, n_experts)


def mixer_out(y_conv, y_ssm, w_out16, x, mod5, g_post, g_pre, layer, router=None, *,
              seq_len):
    t, d = x.shape
    d_conv = y_conv.shape[1]
    d_ssm = y_ssm.shape[1]
    tm = _tile(seq_len, 512)
    tiles_per_seq = seq_len // tm
    routed = router is not None

    def batch_of(i):
        return i // tiles_per_seq

    def row_spec(n):
        return pl.BlockSpec((tm, n), lambda i: (i, 0))

    def const_spec(shape):
        return pl.BlockSpec(shape, lambda i: (0, 0))

    in_specs = [row_spec(d_conv), row_spec(d_ssm),
                _layer_spec(layer, (d_conv + d_ssm, d), pipeline_mode=pl.Buffered(1)),
                row_spec(d), _mod_spec(layer, GT_MIX, d, batch_of),
                _layer_spec(layer, (1, d)), _layer_spec(layer, (1, d)),
                _mod_spec(layer, SC_FFN, d, batch_of), _mod_spec(layer, SH_FFN, d, batch_of)]
    args = [y_conv, y_ssm, w_out16, x, mod5, g_post, g_pre, mod5, mod5]
    out_specs = [row_spec(d), row_spec(d)]
    out_shape = [jax.ShapeDtypeStruct((t, d), F32),
                 jax.ShapeDtypeStruct((t, d), F32 if routed else BF16)]
    scratch_shapes = [pltpu.VMEM((tm, d), F32)]
    n_experts = None
    if routed:
        rw_pad, rb_pad, n_experts = router
        in_specs += [const_spec(rw_pad.shape), const_spec(rb_pad.shape)]
        args += [rw_pad, rb_pad]
        out_specs += [row_spec(LANES), row_spec(LANES), const_spec((SUBLANES, LANES))]
        out_shape += [jax.ShapeDtypeStruct((t, LANES), jnp.int32),
                      jax.ShapeDtypeStruct((t, LANES), F32),
                      jax.ShapeDtypeStruct((SUBLANES, LANES), jnp.int32)]
        scratch_shapes.append(pltpu.VMEM((SUBLANES, LANES), F32))
    return pl.pallas_call(
        functools.partial(_mixer_out_kernel, n_experts=n_experts),
        grid=(t // tm,),
        in_specs=in_specs, out_specs=out_specs, out_shape=out_shape,
        scratch_shapes=scratch_shapes,
        compiler_params=_params(("arbitrary",), 56),
        name="mixer_out_routed" if routed else "mixer_out",
    )(*args)


def _ffn_kernel(h_ref, wg_ref, wu_ref, wd_ref, x_hbm, gt_ref, gpost_ref, o_ref,
                wg16, wu16, wd16, x_buf, sem, *, sub_rows):
    i = pl.program_id(0)
    k = pl.program_id(1)
    tm = o_ref.shape[0]
    x_copy = pltpu.make_async_copy(x_hbm.at[pl.ds(i * tm, tm)], x_buf, sem)

    @pl.when(k == 0)
    def _():
        o_ref[...] = jnp.zeros(o_ref.shape, F32)
        x_copy.start()

    wg16[...] = wg_ref[...].astype(BF16)
    wu16[...] = wu_ref[...].astype(BF16)
    wd16[...] = wd_ref[...].astype(BF16)
    for r0 in range(0, tm, sub_rows):
        h = h_ref[r0:r0 + sub_rows, :]
        gate = _dot(h, wg16[...])
        act = (gate * jax.nn.sigmoid(gate)) * _dot(h, wu16[...])
        o_ref[r0:r0 + sub_rows, :] += _dot(act.astype(BF16), wd16[...])

    @pl.when(k == pl.num_programs(1) - 1)
    def _():
        x_copy.wait()
        scale = gt_ref[...] * gpost_ref[...]
        for r0 in range(0, tm, FFN_EPILOGUE_ROWS):
            rows = slice(r0, r0 + FFN_EPILOGUE_ROWS)
            o_ref[rows, :] = x_buf[rows, :] + _rms(o_ref[rows, :]) * scale


def ffn_dense(h16, w_gate, w_up, w_down, x, mod5, g_post, layer, ffn_index, *, seq_len):
    t, d = x.shape
    d_ff = w_gate.shape[2]
    tm = _tile(seq_len, 1024)
    sub_rows = _tile(tm, 512)
    tk = _tile(d_ff, 256)
    tiles_per_seq = seq_len // tm
    return pl.pallas_call(
        functools.partial(_ffn_kernel, sub_rows=sub_rows),
        grid=(t // tm, d_ff // tk),
        in_specs=[
            pl.BlockSpec((tm, d), lambda i, k: (i, 0)),
            pl.BlockSpec((None, d, tk), lambda i, k: (ffn_index, 0, k)),
            pl.BlockSpec((None, d, tk), lambda i, k: (ffn_index, 0, k)),
            pl.BlockSpec((None, tk, d), lambda i, k: (ffn_index, k, 0)),
            pl.BlockSpec(memory_space=pl.ANY),
            _mod_spec(layer, GT_FFN, d, lambda i: i // tiles_per_seq),
            _layer_spec(layer, (1, d)),
        ],
        out_specs=pl.BlockSpec((tm, d), lambda i, k: (i, 0)),
        out_shape=jax.ShapeDtypeStruct((t, d), F32),
        scratch_shapes=[pltpu.VMEM((d, tk), BF16), pltpu.VMEM((d, tk), BF16),
                        pltpu.VMEM((tk, d), BF16), pltpu.VMEM((tm, d), F32),
                        pltpu.SemaphoreType.DMA(())],
        compiler_params=_params(("arbitrary", "arbitrary"), 56),
        name="ffn_dense",
    )(h16, w_gate, w_up, w_down, x, mod5, g_post)


def _route(logits, ri_ref, rw_ref, cnt_ref, carry_scr, n_experts):
    @pl.when(pl.program_id(0) == 0)
    def _():
        carry_scr[...] = jnp.zeros(carry_scr.shape, F32)

    tm = logits.shape[0]
    lane = lax.broadcasted_iota(jnp.int32, (tm, LANES), 1)
    neg = jnp.float32(-jnp.inf)
    logit = jnp.where(lane < n_experts, logits, neg)
    m1 = jnp.max(logit, axis=-1, keepdims=True)
    i1 = jnp.min(jnp.where(logit == m1, lane, LANES), axis=-1, keepdims=True)
    rest = jnp.where(lane == i1, neg, logit)
    m2 = jnp.max(rest, axis=-1, keepdims=True)
    i2 = jnp.min(jnp.where(rest == m2, lane, LANES), axis=-1, keepdims=True)
    e2 = jnp.exp(m2 - m1)
    w1 = 1.0 / (1.0 + e2)
    w2 = e2 / (1.0 + e2)
    hit1 = lane == i1
    hit2 = lane == i2
    onehot = jnp.where(hit1 | hit2, 1.0, 0.0)
    r = lax.broadcasted_iota(jnp.int32, (tm, tm), 0)
    c = lax.broadcasted_iota(jnp.int32, (tm, tm), 1)
    earlier = jnp.where(c < r, 1.0, 0.0).astype(BF16)
    carry = carry_scr[0:1, :]
    before = _dot(earlier, onehot.astype(BF16)) + carry
    pos1 = jnp.sum(jnp.where(hit1, before, 0.0), axis=-1, keepdims=True).astype(jnp.int32)
    pos2 = jnp.sum(jnp.where(hit2, before, 0.0), axis=-1, keepdims=True).astype(jnp.int32)
    total = carry + jnp.sum(onehot, axis=0, keepdims=True)
    carry_scr[...] = jnp.broadcast_to(total, carry_scr.shape)
    cnt_ref[...] = jnp.broadcast_to(total, cnt_ref.shape).astype(jnp.int32)
    zero_i = jnp.zeros((tm, LANES), jnp.int32)
    ri_ref[...] = jnp.where(lane == 0, i1, jnp.where(lane == 1, i2,
                            jnp.where(lane == 2, pos1, jnp.where(lane == 3, pos2, zero_i))))
    rw_ref[...] = jnp.where(lane == 0, w1, jnp.where(lane == 1, w2, 0.0))


def _moe_ffn_kernel(ge_ref, nr_ref, nu_ref, tok_ref, h_hbm, wg_ref, wu_ref, wd_ref, o_ref,
                    h_scr, gbuf, wg16, wu16, wd16, sem, *, rows_per_step):
    g = pl.program_id(0)
    k = pl.program_id(1)
    n_used = nu_ref[0]
    n_sub_max, ts, _ = h_scr.shape
    group_rows = n_sub_max * ts
    nk = pl.num_programs(1)
    n_gather = gbuf.shape[0]

    def row_copy(src_row, dst_row):
        return pltpu.make_async_copy(h_hbm.at[pl.ds(src_row, 1)],
                                     gbuf.at[pl.ds(dst_row, 1)], sem)

    def for_each_sub_tile(group, fn):
        for j in range(n_sub_max):
            @pl.when(j * ts < nr_ref[group])
            def _(j=j):
                fn(j)

    def wait_rows():
        pltpu.make_async_copy(h_hbm.at[pl.ds(0, n_gather)], gbuf, sem).wait()

    @pl.when(k == 0)
    def _():
        o_ref[...] = jnp.zeros(o_ref.shape, o_ref.dtype)

    @pl.when(g < n_used)
    def _():
        @pl.when(k == 0)
        def _():
            @pl.when(g == 0)
            def _():
                def body(i, carry):
                    row_copy(tok_ref[i], i).start()
                    return carry
                lax.fori_loop(0, n_gather, body, 0, unroll=8)

            wait_rows()

            def cast(j):
                h_scr[j] = gbuf[j * ts:(j + 1) * ts, :].astype(BF16)
            for_each_sub_tile(g, cast)

        next_base = jnp.minimum(g + 1, n_used - 1) * group_rows + k * rows_per_step

        def sub_tile(j, m):
            if j == 0:
                wg16[...] = wg_ref[...].astype(BF16)
                wu16[...] = wu_ref[...].astype(BF16)
                wd16[...] = wd_ref[...].astype(BF16)
                for i in range(rows_per_step):
                    row_copy(tok_ref[next_base + i], k * rows_per_step + i).start()
            h = h_scr[j, 0:m, :]
            gate = _dot(h, wg16[...])
            act = (gate * jax.nn.sigmoid(gate)) * _dot(h, wu16[...])
            o_ref[j * ts:j * ts + m, :] += _dot(act.astype(BF16), wd16[...])

        half = ts // 2
        all_full = nr_ref[g] > group_rows - half

        @pl.when(all_full)
        def _():
            for j in range(n_sub_max):
                sub_tile(j, ts)

        @pl.when(jnp.logical_not(all_full))
        def _():
            for j in range(n_sub_max):
                rows_j = nr_ref[g] - j * ts

                @pl.when(rows_j > half)
                def _(j=j):
                    sub_tile(j, ts)

                @pl.when((rows_j > 0) & (rows_j <= half))
                def _(j=j):
                    sub_tile(j, half)

        @pl.when((g == n_used - 1) & (k == nk - 1))
        def _():
            wait_rows()


def moe_ffn(group_expert, group_rows_used, n_used, row_tok, h, w_gate, w_up, w_down, layer, *,
            sub_rows, n_sub_max):
    d = h.shape[1]
    d_ff = w_gate.shape[3]
    tk = _tile(d_ff, 256)
    nk = d_ff // tk
    group_rows = sub_rows * n_sub_max
    rows_per_step = -(-group_rows // (nk * SUBLANES)) * SUBLANES
    n_gather = rows_per_step * nk
    n_rows = group_expert.shape[0] * group_rows
    assert row_tok.shape[0] - n_rows >= n_gather - group_rows

    def ff_idx(g, k, nu):
        return jnp.where(g < nu[0], k, nk - 1)

    return pl.pallas_call(
        functools.partial(_moe_ffn_kernel, rows_per_step=rows_per_step),
        grid_spec=pltpu.PrefetchScalarGridSpec(
            num_scalar_prefetch=4,
            grid=(n_rows // group_rows, nk),
            in_specs=[
                pl.BlockSpec(memory_space=pl.ANY),
                pl.BlockSpec((None, None, d, tk),
                             lambda g, k, ge, ns, nu, tok: (layer, ge[g], 0, ff_idx(g, k, nu))),
                pl.BlockSpec((None, None, d, tk),
                             lambda g, k, ge, ns, nu, tok: (layer, ge[g], 0, ff_idx(g, k, nu))),
                pl.BlockSpec((None, None, tk, d),
                             lambda g, k, ge, ns, nu, tok: (layer, ge[g], ff_idx(g, k, nu), 0)),
            ],
            out_specs=pl.BlockSpec((group_rows, d), lambda g, k, ge, ns, nu, tok: (g, 0)),
            scratch_shapes=[pltpu.VMEM((n_sub_max, sub_rows, d), BF16),
                            pltpu.VMEM((n_gather, d), F32),
                            pltpu.VMEM((d, tk), BF16), pltpu.VMEM((d, tk), BF16),
                            pltpu.VMEM((tk, d), BF16),
                            pltpu.SemaphoreType.DMA(())],
        ),
        out_shape=jax.ShapeDtypeStruct((n_rows, d), F32),
        compiler_params=_params(("arbitrary", "arbitrary"), 56),
        name="moe_ffn",
    )(group_expert, group_rows_used, n_used, row_tok, h, w_gate, w_up, w_down)


def _moe_combine_kernel(d0_ref, d1_ref, ys_ref, rw_ref, x_ref, gt_ref, gpost_ref, o_ref,
                        buf, sems):
    i = pl.program_id(0)
    tm = x_ref.shape[0]
    slot = i % 2

    def start_tile(tile, s):
        def body(t, carry):
            for which, dest in enumerate((d0_ref, d1_ref)):
                pltpu.make_async_copy(ys_ref.at[pl.ds(dest[tile * tm + t], 1)],
                                      buf.at[s, pl.ds(which * tm + t, 1)],
                                      sems.at[s]).start(priority=which)
            return carry
        lax.fori_loop(0, tm, body, 0, unroll=8)

    @pl.when(i == 0)
    def _():
        start_tile(0, 0)

    @pl.when(i + 1 < pl.num_programs(0))
    def _():
        start_tile(i + 1, 1 - slot)

    pltpu.make_async_copy(ys_ref.at[pl.ds(0, 2 * tm)], buf.at[slot], sems.at[slot]).wait()
    scale = gt_ref[...] * gpost_ref[...]
    for r0 in range(0, tm, EPILOGUE_ROWS):
        rows = slice(r0, r0 + EPILOGUE_ROWS)
        y = (buf[slot, r0:r0 + EPILOGUE_ROWS, :] * rw_ref[rows, 0:1]
             + buf[slot, tm + r0:tm + r0 + EPILOGUE_ROWS, :] * rw_ref[rows, 1:2])
        o_ref[rows, :] = x_ref[rows, :] + _rms(y) * scale


def moe_combine(dest0, dest1, ys, route_w, x, mod5, g_post, layer, *, seq_len):
    t, d = x.shape
    tm = _tile(seq_len, 256)
    tiles_per_seq = seq_len // tm
    return pl.pallas_call(
        _moe_combine_kernel,
        grid_spec=pltpu.PrefetchScalarGridSpec(
            num_scalar_prefetch=2,
            grid=(t // tm,),
            in_specs=[
                pl.BlockSpec(memory_space=pl.ANY),
                pl.BlockSpec((tm, LANES), lambda i, d0, d1: (i, 0)),
                pl.BlockSpec((tm, d), lambda i, d0, d1: (i, 0)),
                _mod_spec(layer, GT_FFN, d, lambda i: i // tiles_per_seq),
                _layer_spec(layer, (1, d)),
            ],
            out_specs=pl.BlockSpec((tm, d), lambda i, d0, d1: (i, 0)),
            scratch_shapes=[pltpu.VMEM((2, 2 * tm, d), F32), pltpu.SemaphoreType.DMA((2,))],
        ),
        out_shape=jax.ShapeDtypeStruct((t, d), F32),
        compiler_params=_params(("arbitrary",), 48),
        name="moe_combine",
    )(dest0, dest1, ys, route_w, x, mod5, g_post)


MOE_SUB_ROWS = 512
MOE_GROUP_SUBS = 2


def _row_tok_kernel(d0_ref, d1_ref, o_ref):
    def clear(r, carry):
        o_ref[r] = 0
        return carry
    lax.fori_loop(0, o_ref.shape[0], clear, 0, unroll=8)

    def place(t, carry):
        o_ref[d0_ref[t]] = t
        o_ref[d1_ref[t]] = t
        return carry
    lax.fori_loop(0, d0_ref.shape[0], place, 0, unroll=8)


def row_tokens(dest0, dest1, n_rows):
    smem = pl.BlockSpec(memory_space=pltpu.SMEM)
    return pl.pallas_call(
        _row_tok_kernel,
        in_specs=[smem, smem],
        out_specs=smem,
        out_shape=jax.ShapeDtypeStruct((n_rows,), jnp.int32),
        name="row_tokens",
    )(dest0, dest1)


def moe_layer(h, route_i, route_w, counts, x, mod5, g_post, w_gate, w_up, w_down, layer,
              moe_index, *, seq_len):
    t, d = h.shape
    n_experts = w_gate.shape[1]
    ts = min(MOE_SUB_ROWS, t)
    group_rows = ts * MOE_GROUP_SUBS
    e0, e1, pos0, pos1 = (route_i[:, k] for k in range(4))
    counts = counts[0, :n_experts]
    groups_e = (counts + group_rows - 1) // group_rows
    g_end = jnp.cumsum(groups_e)
    g_start = g_end - groups_e
    dest0 = g_start[e0] * group_rows + pos0
    dest1 = g_start[e1] * group_rows + pos1
    n_groups = -(-(2 * t) // group_rows) + n_experts
    n_rows = n_groups * group_rows
    row_tok = row_tokens(dest0, dest1, n_rows + group_rows)
    n_used = g_end[-1].astype(jnp.int32)
    gid = jnp.arange(n_groups, dtype=jnp.int32)
    group_e = jnp.clip(jnp.searchsorted(g_end, gid, side='right'), 0, n_experts - 1)
    group_e = jnp.where(gid < n_used, group_e, group_e[n_used - 1]).astype(jnp.int32)
    group_rows_used = jnp.where(
        gid < n_used,
        jnp.clip(counts[group_e] - (gid - g_start[group_e]) * group_rows, 0, group_rows),
        0).astype(jnp.int32)
    ys = moe_ffn(group_e, group_rows_used, n_used.reshape(1), row_tok, h, w_gate, w_up, w_down,
                 moe_index, sub_rows=ts, n_sub_max=MOE_GROUP_SUBS)
    return moe_combine(dest0, dest1, ys, route_w, x, mod5, g_post, layer, seq_len=seq_len)


def kernel(x, c, w_ada, b_ada, g_mix_pre, g_mix_post, g_ffn_pre, g_ffn_post, w_in, conv_w,
           ssm_a_re, ssm_a_im, ssm_log_dt, ssm_b_re, ssm_b_im, ssm_c_re, ssm_c_im, ssm_d,
           w_glu, b_glu, w_out, ffn_w_gate, ffn_w_up, ffn_w_down, router_w, router_b,
           moe_w_gate, moe_w_up, moe_w_down):
    nb, seq_len, d = x.shape
    depth = w_ada.shape[0]
    g, p, h_dim = ssm_b_re.shape[1:]
    d_ssm = g * h_dim
    n_experts = router_w.shape[2]
    t = nb * seq_len
    groups_per_slab = max(1, min(g, MXU_DIM // h_dim))

    mod_rows = 2 * SUBLANES
    c_pad = jnp.zeros((mod_rows, d), F32).at[:nb].set(c)
    mod = adaln_mod(c_pad, w_ada, b_ada)
    mod5 = mod.reshape(depth, mod_rows, 6, 1, d)
    xf = x.reshape(t, d)

    def rows(v):
        return v.reshape(v.shape[0], 1, v.shape[1])
    g_mix_pre, g_mix_post, g_ffn_pre, g_ffn_post = (
        rows(v) for v in (g_mix_pre, g_mix_post, g_ffn_pre, g_ffn_post))
    w_in16 = w_in.astype(BF16)
    w_out16 = w_out.astype(BF16)
    w_glu16 = w_glu.astype(BF16)
    conv_wt = jnp.swapaxes(conv_w, 1, 2)
    bb_re, bb_im, ab_re, ab_im = ssm_prep(ssm_a_re, ssm_a_im, ssm_log_dt, ssm_b_re, ssm_b_im)
    def stacked_t(w):
        return _group_rows(jnp.swapaxes(w, 2, 3), groups_per_slab)
    bbs_re, bbs_im = stacked_t(bb_re), stacked_t(bb_im)
    cs_re, cs_im = stacked_t(ssm_c_re), stacked_t(ssm_c_im)
    a_re = _pack_state_rows(ab_re.reshape(depth, g * p), nb)
    a_im = _pack_state_rows(ab_im.reshape(depth, g * p), nb)
    d_skip = ssm_d.reshape(depth, 1, d_ssm)
    b_glu = rows(b_glu)
    n_moe = router_w.shape[0]
    rw_pad = jnp.zeros((n_moe, d, LANES), F32).at[:, :, :n_experts].set(router_w)
    rb_pad = jnp.zeros((n_moe, 1, LANES), F32).at[:, 0, :n_experts].set(router_b)

    for l in range(depth):
        y_conv, u = mixer_in(xf, mod5, g_mix_pre, w_in16, conv_wt, l, seq_len=seq_len)
        y_ssm = ssm(u.reshape(nb, seq_len, d_ssm), bbs_re, bbs_im, cs_re, cs_im, a_re, a_im, d_skip,
                    w_glu16, b_glu, l).reshape(t, d_ssm)
        routed = l % 2 == 1
        i = l // 2
        rt = (rw_pad[i], rb_pad[i], n_experts) if routed else None
        outs = mixer_out(y_conv, y_ssm, w_out16, xf, mod5, g_mix_post, g_ffn_pre, l, rt,
                         seq_len=seq_len)
        if routed:
            xf, h, route_i, route_w, counts = outs
            xf = moe_layer(h, route_i, route_w, counts, xf, mod5, g_ffn_post,
                           moe_w_gate, moe_w_up, moe_w_down, l, i, seq_len=seq_len)
        else:
            xf, h = outs
            xf = ffn_dense(h, ffn_w_gate, ffn_w_up, ffn_w_down, xf, mod5, g_ffn_post, l, i,
                           seq_len=seq_len)
    return xf.reshape(nb, seq_len, d)
```
